```python
import math
import jax
import jax.numpy as jnp
from jax import lax
import numpy as np

D_MODEL = 2048
BATCH = 4
SEQ = 2048
DEPTH = 4
DEC_BATCH = 8
DEC_SEQ = 1
PAST_LEN = 16384
PAGE_SIZE = 128

HEAD_DIM = 64
N_HEADS = D_MODEL // HEAD_DIM
N_ATT_HEADS = N_HEADS // 2
N_RWKV_HEADS = N_HEADS - N_ATT_HEADS
D_ATT = N_ATT_HEADS * HEAD_DIM
D_RWKV = N_RWKV_HEADS * HEAD_DIM
QKV_COLS = 3 * D_ATT + 3 * D_RWKV
BRANCHES = ((128, 1), (512, 4), (2048, 16))
W_MAX = max(w for w, _ in BRANCHES)
BAND = 128
NUM_BUCKETS = 32
MAX_DISTANCE = W_MAX
ATT_SCALE = HEAD_DIM ** -0.5
D_FF = -(-8 * D_MODEL // (3 * 256)) * 256
D_DECAY_LORA = 64
D_AAA_LORA = 64
D_MV_LORA = 32
D_GATE_LORA = 160
RMS_EPS = 1e-6
GN_EPS = 64e-5

kernel_name = 'hymba_dilated_attn_rwkv7_adaln_step'


def rmsnorm(x, g):
    xf = x.astype(jnp.float32)
    y = xf * lax.rsqrt(jnp.mean(xf * xf, axis=-1, keepdims=True) + RMS_EPS)
    return (y * g.astype(jnp.float32)).astype(x.dtype)


def modulate(h, shift, scale):
    return h * (1 + scale[:, None, :]) + shift[:, None, :]


def token_shift(t, prev):
    return jnp.concatenate([prev[:, None, :].astype(t.dtype), t[:, :-1]], axis=1)


def rel_bucket(dist):
    max_exact = NUM_BUCKETS // 2
    d = jnp.maximum(dist, 0)
    df = jnp.maximum(d, 1).astype(jnp.float32)
    large = max_exact + (jnp.log(df / max_exact) / math.log(MAX_DISTANCE / max_exact)
                         * (NUM_BUCKETS - max_exact)).astype(jnp.int32)
    return jnp.where(d < max_exact, d, jnp.minimum(large, NUM_BUCKETS - 1))


def dilated_branch_prompt(q, k, v, rel_bias, window, dil):
    B, S, H, E = q.shape
    L = S // dil
    nb = -(-L // BAND)
    Lp = nb * BAND
    wsub = window // dil

    def to_res(t):
        t = t.reshape(B, L, dil, H, E).transpose(0, 2, 1, 3, 4)
        t = jnp.pad(t, ((0, 0), (0, 0), (0, Lp - L), (0, 0), (0, 0)))
        return t.reshape(B, dil, nb, BAND, H, E)

    def with_prev(t):
        prev = jnp.pad(t, ((0, 0), (0, 0), (1, 0), (0, 0), (0, 0), (0, 0)))[:, :, :-1]
        return jnp.concatenate([prev, t], axis=3)

    qb = to_res(q)
    kb = with_prev(to_res(k))
    vb = with_prev(to_res(v))
    logits = jnp.einsum('brnqhe,brnkhe->brnhqk', qb, kb) * ATT_SCALE
    iq = jnp.arange(BAND)[:, None]
    jk = jnp.arange(2 * BAND)[None, :]
    dsub = BAND + iq - jk
    kpos = (jnp.arange(nb)[:, None, None] - 1) * BAND + jk[None]
    valid = (dsub >= 0) & (dsub <= wsub) & (kpos >= 0)
    bias = rel_bias[rel_bucket(dsub * dil)].astype(jnp.float32)
    logits = logits + jnp.moveaxis(bias, -1, 0)
    logits = jnp.where(valid[:, None], logits, -jnp.inf)
    lse = jax.nn.logsumexp(logits, axis=-1)
    p = jnp.exp(logits - lse[..., None])
    o = jnp.einsum('brnhqk,brnkhe->brnqhe', p, vb)

    def from_res(t):
        t = t.reshape((B, dil, Lp) + t.shape[4:])[:, :, :L]
        t = jnp.swapaxes(t, 1, 2)
        return t.reshape((B, S) + t.shape[3:])

    return from_res(o), from_res(jnp.swapaxes(lse, 3, 4))


def dilated_branch_sample(q, kc, vc, rel_bias, window, dil, wbuf):
    T = q.shape[1]
    m = jnp.arange(window // dil + 1)
    idx = wbuf + jnp.arange(T)[:, None] - dil * m[None, :]
    valid = idx >= 0
    idxc = jnp.maximum(idx, 0)
    kg = kc[:, idxc]
    vg = vc[:, idxc]
    logits = jnp.einsum('bthe,btmhe->bhtm', q, kg) * ATT_SCALE
    bias = rel_bias[rel_bucket(dil * m)].astype(jnp.float32)
    logits = logits + bias.T[None, :, None, :]
    logits = jnp.where(valid[None, None], logits, -jnp.inf)
    lse = jax.nn.logsumexp(logits, axis=-1)
    p = jnp.exp(logits - lse[..., None])
    o = jnp.einsum('bhtm,btmhe->bthe', p, vg)
    return o, jnp.swapaxes(lse, 1, 2)


def wkv_scan(S0, r, w, k, v, a, b):
    def step(S, inp):
        r_t, w_t, k_t, v_t, a_t, b_t = inp
        sa = jnp.einsum('bhij,bhj->bhi', S, a_t)
        S = S * w_t[:, :, None, :] + sa[..., None] * b_t[:, :, None, :] + v_t[..., None] * k_t[:, :, None, :]
        return S, jnp.einsum('bhij,bhj->bhi', S, r_t)
    xs = tuple(jnp.swapaxes(t, 0, 1) for t in (r, w, k, v, a, b))
    S, ys = lax.scan(step, S0, xs)
    return S, jnp.swapaxes(ys, 0, 1)


def trunk_layer(P, l, x, c, h_last, wkv0, k_buf, v_buf, v_first):
    B, T, _ = x.shape
    f32 = jnp.float32
    mod = jnp.dot(jax.nn.silu(c), P['ada_w'][l]) + P['ada_b'][l]
    sh1, sc1, gt1, sh2, sc2, gt2 = jnp.split(mod, 6, axis=-1)
    h = modulate(rmsnorm(x, P['norm1_g'][l]), sh1, sc1)
    w_in = P['w_in'][l]
    qa, ka, va, rkv0 = jnp.split(h @ w_in, [D_ATT, 2 * D_ATT, 3 * D_ATT], axis=-1)

    q = rmsnorm(qa.reshape(B, T, N_ATT_HEADS, HEAD_DIM), P['q_norm_g'][l]).astype(f32)
    k = rmsnorm(ka.reshape(B, T, N_ATT_HEADS, HEAD_DIM), P['k_norm_g'][l])
    v = va.reshape(B, T, N_ATT_HEADS, HEAD_DIM)
    if k_buf is None:
        res = [dilated_branch_prompt(q, k.astype(f32), v.astype(f32), P['rel_bias'], w, d)
               for (w, d) in BRANCHES]
        keep = min(W_MAX, T)
        new_k, new_v = k[:, T - keep:], v[:, T - keep:]
    else:
        kc = jnp.concatenate([k_buf.astype(f32), k.astype(f32)], axis=1)
        vc = jnp.concatenate([v_buf.astype(f32), v.astype(f32)], axis=1)
        res = [dilated_branch_sample(q, kc, vc, P['rel_bias'], w, d, k_buf.shape[1])
               for (w, d) in BRANCHES]
        new_k, new_v = k, v
    outs = jnp.stack([o for o, _ in res])
    lses = jnp.stack([s for _, s in res])
    alpha = jax.nn.softmax(lses, axis=0)
    att = jnp.einsum('nbth,nbthe->bthe', alpha, outs).reshape(B, T, D_ATT).astype(x.dtype)

    H, N = N_RWKV_HEADS, HEAD_DIM
    dh = token_shift(h, h_last) - h
    mu = P['mu_wag'][l]
    xw = h + dh * mu[0]
    xa = h + dh * mu[1]
    xg = h + dh * mu[2]
    rkv_prev = token_shift(rkv0, h_last @ w_in[:, 3 * D_ATT:])
    rkv = rkv0 + (rkv_prev - rkv0) * P['mu_rkv'][l].reshape(-1)
    r, kr, vr = jnp.split(rkv.astype(f32), 3, axis=-1)
    logw = -jax.nn.softplus(-(P['decay_w0'][l] + jnp.tanh(xw @ P['decay_w1'][l]) @ P['decay_w2'][l])) - 0.5
    decay = jnp.exp(-jnp.exp(logw.astype(f32)))
    if l == 0:
        v_first = vr
    else:
        xv = h + dh * P['vres_mu'][l - 1]
        vgate = jax.nn.sigmoid((P['vres_v0'][l - 1] + (xv @ P['vres_w1'][l - 1]) @ P['vres_w2'][l - 1]).astype(f32))
        vr = vr + (v_first - vr) * vgate
    a = jax.nn.sigmoid((P['aaa_a0'][l] + (xa @ P['aaa_w1'][l]) @ P['aaa_w2'][l]).astype(f32))
    g = (jax.nn.sigmoid(xg @ P['gate_w1'][l]) @ P['gate_w2'][l]).astype(f32)
    kk = (kr * P['k_k'][l]).reshape(B, T, H, N)
    kk = kk / jnp.maximum(jnp.sqrt(jnp.sum(kk * kk, axis=-1, keepdims=True)), 1e-12)
    kr = kr * (1 + (a - 1) * P['k_a'][l])
    rh = r.reshape(B, T, H, N)
    kh = kr.reshape(B, T, H, N)
    vh = vr.reshape(B, T, H, N)
    ah = a.reshape(B, T, H, N)
    wkv, y = wkv_scan(wkv0.astype(f32), rh, decay.reshape(B, T, H, N), kh, vh, -kk, kk * ah)
    y_mu = jnp.mean(y, axis=-1, keepdims=True)
    y_var = jnp.mean(jnp.square(y - y_mu), axis=-1, keepdims=True)
    y = (y - y_mu) * lax.rsqrt(y_var + GN_EPS)
    y = y * P['lnx_g'][l].reshape(H, N) + P['lnx_b'][l].reshape(H, N)
    y = y + jnp.sum(rh * kh * P['r_k'][l], axis=-1, keepdims=True) * vh
    rw = (y.reshape(B, T, D_RWKV) * g).astype(x.dtype)

    mix = jnp.concatenate([att, rw], axis=-1) @ P['w_out'][l]
    x = x + gt1[:, None, :] * mix

    h2 = modulate(rmsnorm(x, P['norm2_g'][l]), sh2, sc2)
    gate, up = jnp.split(h2 @ P['w_gu'][l], 2, axis=-1)
    x = x + gt2[:, None, :] * ((jax.nn.silu(gate) * up) @ P['w_down'][l])
    return x, v_first, new_k, new_v, wkv.astype(x.dtype), h[:, -1]


def setup_inputs(seed: int = 0) -> dict:
    key = jax.random.key(seed)
    ks = iter(jax.random.split(key, 48))

    def nrm(shape, scale):
        return jax.random.normal(next(ks), shape, jnp.float32) * scale

    def uni(shape, lo, hi):
        return jax.random.uniform(next(ks), shape, jnp.float32, lo, hi)

    wbuf = min(W_MAX, PAST_LEN)
    nv = DEPTH - 1
    d_in = D_MODEL ** -0.5
    return {
        'x_prompt': nrm((BATCH, SEQ, D_MODEL), 1.0),
        'x_sample': nrm((DEC_BATCH, DEC_SEQ, D_MODEL), 1.0),
        'c_prompt': nrm((BATCH, D_MODEL), 1.0),
        'c_sample': nrm((DEC_BATCH, D_MODEL), 1.0),
        'cache_k': nrm((DEPTH, DEC_BATCH, wbuf, N_ATT_HEADS, HEAD_DIM), 1.0),
        'cache_v': nrm((DEPTH, DEC_BATCH, wbuf, N_ATT_HEADS, HEAD_DIM), 1.0),
        'state_wkv': nrm((DEPTH, DEC_BATCH, N_RWKV_HEADS, HEAD_DIM, HEAD_DIM), 0.3),
        'state_shift': nrm((DEPTH, DEC_BATCH, D_MODEL), 1.0),
        'rel_bias': nrm((NUM_BUCKETS, N_ATT_HEADS), 0.5),
        'ada_w': nrm((DEPTH, D_MODEL, 6 * D_MODEL), 0.5 * d_in),
        'ada_b': nrm((DEPTH, 6 * D_MODEL), 0.02),
        'norm1_g': 1.0 + nrm((DEPTH, D_MODEL), 0.02),
        'norm2_g': 1.0 + nrm((DEPTH, D_MODEL), 0.02),
        'w_in': nrm((DEPTH, D_MODEL, QKV_COLS), d_in),
        'q_norm_g': 1.0 + nrm((DEPTH, HEAD_DIM), 0.02),
        'k_norm_g': 1.0 + nrm((DEPTH, HEAD_DIM), 0.02),
        'mu_wag': uni((DEPTH, 3, D_MODEL), 0.0, 1.0),
        'mu_rkv': uni((DEPTH, 3, D_RWKV), 0.0, 1.0),
        'decay_w0': uni((DEPTH, D_RWKV), -3.0, 0.0),
        'decay_w1': nrm((DEPTH, D_MODEL, D_DECAY_LORA), d_in),
        'decay_w2': nrm((DEPTH, D_DECAY_LORA, D_RWKV), 0.1 * D_DECAY_LORA ** -0.5),
        'aaa_a0': nrm((DEPTH, D_RWKV), 0.1),
        'aaa_w1': nrm((DEPTH, D_MODEL, D_AAA_LORA), d_in),
        'aaa_w2': nrm((DEPTH, D_AAA_LORA, D_RWKV), 0.1 * D_AAA_LORA ** -0.5),
        'gate_w1': nrm((DEPTH, D_MODEL, D_GATE_LORA), d_in),
        'gate_w2': nrm((DEPTH, D_GATE_LORA, D_RWKV), D_GATE_LORA ** -0.5),
        'vres_mu': uni((nv, D_MODEL), 0.0, 1.0),
        'vres_v0': 1.0 + nrm((nv, D_RWKV), 0.1),
        'vres_w1': nrm((nv, D_MODEL, D_MV_LORA), d_in),
        'vres_w2': nrm((nv, D_MV_LORA, D_RWKV), 0.1 * D_MV_LORA ** -0.5),
        'k_k': 0.85 + nrm((DEPTH, D_RWKV), 0.02),
        'k_a': 1.0 + nrm((DEPTH, D_RWKV), 0.02),
        'r_k': nrm((DEPTH, N_RWKV_HEADS, HEAD_DIM), 0.1),
        'lnx_g': 1.0 + nrm((DEPTH, D_RWKV), 0.02),
        'lnx_b': nrm((DEPTH, D_RWKV), 0.02),
        'w_out': nrm((DEPTH, D_MODEL, D_MODEL), d_in),
        'w_gu': nrm((DEPTH, D_MODEL, 2 * D_FF), d_in),
        'w_down': nrm((DEPTH, D_FF, D_MODEL), D_FF ** -0.5),
    }


def reference(x_prompt, x_sample, c_prompt, c_sample, cache_k, cache_v, state_wkv, state_shift,
              rel_bias, ada_w, ada_b, norm1_g, norm2_g, w_in, q_norm_g, k_norm_g, mu_wag, mu_rkv,
              decay_w0, decay_w1, decay_w2, aaa_a0, aaa_w1, aaa_w2, gate_w1, gate_w2,
              vres_mu, vres_v0, vres_w1, vres_w2, k_k, k_a, r_k, lnx_g, lnx_b, w_out, w_gu, w_down):
    P = dict(rel_bias=rel_bias, ada_w=ada_w, ada_b=ada_b, norm1_g=norm1_g, norm2_g=norm2_g,
             w_in=w_in, q_norm_g=q_norm_g, k_norm_g=k_norm_g, mu_wag=mu_wag, mu_rkv=mu_rkv,
             decay_w0=decay_w0, decay_w1=decay_w1, decay_w2=decay_w2,
             aaa_a0=aaa_a0, aaa_w1=aaa_w1, aaa_w2=aaa_w2, gate_w1=gate_w1, gate_w2=gate_w2,
             vres_mu=vres_mu, vres_v0=vres_v0, vres_w1=vres_w1, vres_w2=vres_w2,
             k_k=k_k, k_a=k_a, r_k=r_k, lnx_g=lnx_g, lnx_b=lnx_b,
             w_out=w_out, w_gu=w_gu, w_down=w_down)

    B = x_prompt.shape[0]
    h0 = jnp.zeros((B, D_MODEL), x_prompt.dtype)
    s0 = jnp.zeros((B, N_RWKV_HEADS, HEAD_DIM, HEAD_DIM), jnp.float32)
    xp, vf = x_prompt, None
    pk, pv, ps, ph = [], [], [], []
    for l in range(DEPTH):
        xp, vf, nk, nvv, ns, nh = trunk_layer(P, l, xp, c_prompt, h0, s0, None, None, vf)
        pk.append(nk)
        pv.append(nvv)
        ps.append(ns)
        ph.append(nh)

    xs, vf = x_sample, None
    sk, sv, ss, sh = [], [], [], []
    for l in range(DEPTH):
        xs, vf, nk, nvv, ns, nh = trunk_layer(P, l, xs, c_sample, state_shift[l], state_wkv[l],
                                              cache_k[l], cache_v[l], vf)
        sk.append(nk)
        sv.append(nvv)
        ss.append(ns)
        sh.append(nh)

    return (xp, xs, jnp.stack(pk), jnp.stack(pv), jnp.stack(ps), jnp.stack(ph),
            jnp.stack(sk), jnp.stack(sv), jnp.stack(ss), jnp.stack(sh))
```

```python
import functools
import math

import jax
import jax.numpy as jnp
import numpy as np
from jax import lax
from jax.experimental import pallas as pl
from jax.experimental.pallas import tpu as pltpu

F32 = jnp.float32
BF16 = jnp.bfloat16

HEAD_DIM = 64
BRANCHES = ((128, 1), (512, 4), (2048, 16))
W_MAX = 2048
NUM_BUCKETS = 32
MAX_DISTANCE = W_MAX
ATT_SCALE = HEAD_DIM ** -0.5
RMS_EPS = 1e-6
GN_EPS = 64e-5
NEG = -1e30

LANES = 128
SUBLANES = 8
Q_BLOCK = 128
SCAN_CHUNK = 64
SCAN_HEADS = 2
VMEM_CAP = 56 * 1024 * 1024


def _vmem(nbytes):
    return int(min(VMEM_CAP, nbytes * 1.3 + (6 << 20)))


def _params(sem, nbytes):
    return pltpu.CompilerParams(dimension_semantics=sem, vmem_limit_bytes=_vmem(nbytes))


def _dot(a, b):
    return jnp.dot(a.astype(BF16), b.astype(BF16), preferred_element_type=F32)


def _dot_nt(a, b):
    return lax.dot_general(a.astype(BF16), b.astype(BF16), (((1,), (1,)), ((), ())), preferred_element_type=F32)


def _dot_tn(a, b):
    return lax.dot_general(a.astype(BF16), b.astype(BF16), (((0,), (0,)), ((), ())), preferred_element_type=F32)


def _split_dot(a, b01):
    hi = a.astype(BF16)
    lo = (a - hi.astype(F32)).astype(BF16)
    return jnp.dot(hi, b01, preferred_element_type=F32) + jnp.dot(lo, b01, preferred_element_type=F32)


def _sigmoid(x):
    return 1.0 / (1.0 + jnp.exp(-x))


def _ada_kernel(c_ref, w_ref, b_ref, o_ref):
    c = c_ref[...]
    s = c * _sigmoid(c)
    o_ref[0] = _dot(s, w_ref[0]) + b_ref[0]


def _ada(c_all, ada_w, ada_b, tn=1024):
    L, D, N = ada_w.shape
    R = c_all.shape[0]
    return pl.pallas_call(
        _ada_kernel,
        grid=(L, N // tn),
        in_specs=[pl.BlockSpec((R, D), lambda l, j: (0, 0)),
                  pl.BlockSpec((1, D, tn), lambda l, j: (l, 0, j)),
                  pl.BlockSpec((1, 1, tn), lambda l, j: (l, 0, j))],
        out_specs=pl.BlockSpec((1, R, tn), lambda l, j: (l, 0, j)),
        out_shape=jax.ShapeDtypeStruct((L, R, N), F32),
        compiler_params=_params(("arbitrary", "arbitrary"), 2 * D * tn * 4 + D * tn * 2),
        name="ada_mod",
    )(c_all, ada_w, ada_b.reshape(L, 1, N))


def _norm_kernel(x_ref, g_ref, sh_ref, sc_ref, *out_refs):
    x = x_ref[0]
    y = x * lax.rsqrt(jnp.mean(x * x, axis=-1, keepdims=True) + RMS_EPS) * g_ref[0]
    h = y * (1.0 + sc_ref[0]) + sh_ref[0]
    for o in out_refs:
        o[0] = h.astype(o.dtype)


def _row_spec(arr, tm, T):
    C = arr.shape[-1]
    if arr.shape[1] == 1:
        return pl.BlockSpec((1, 1, C), lambda b, i: (b, 0, 0))
    assert arr.shape[1] == T
    return pl.BlockSpec((1, tm, C), lambda b, i: (b, i, 0))


def _norm_mod(x, g_stack, l, shift, scale, out_dtypes, tm=256):
    B, T, D = x.shape
    tm = min(tm, T)
    L = g_stack.shape[0]
    outs = pl.pallas_call(
        _norm_kernel,
        grid=(B, T // tm),
        in_specs=[pl.BlockSpec((1, tm, D), lambda b, i: (b, i, 0)),
                  pl.BlockSpec((1, 1, D), lambda b, i: (l, 0, 0)),
                  _row_spec(shift, tm, T), _row_spec(scale, tm, T)],
        out_specs=[pl.BlockSpec((1, tm, D), lambda b, i: (b, i, 0)) for _ in out_dtypes],
        out_shape=[jax.ShapeDtypeStruct((B, T, D), dt) for dt in out_dtypes],
        compiler_params=_params(("arbitrary", "arbitrary"), 2 * tm * D * 4 * (2 + len(out_dtypes))),
        name="norm_mod",
    )(x, g_stack.reshape(L, 1, D), shift, scale)
    return outs


def _first_inner_step():
    return (pl.program_id(1) == 0) & (pl.program_id(2) == 0)


def _head_rmsnorm(acc, g):
    tn = acc.shape[1]
    r = lax.broadcasted_iota(jnp.int32, (LANES, LANES), 0) // HEAD_DIM
    c = lax.broadcasted_iota(jnp.int32, (LANES, LANES), 1) // HEAD_DIM
    bd = (r == c).astype(BF16)
    cols = []
    for j in range(tn // LANES):
        a = acc[:, j * LANES:(j + 1) * LANES]
        ss = _split_dot(a * a, bd)
        cols.append(a * lax.rsqrt(ss * (1.0 / HEAD_DIM) + RMS_EPS))
    return jnp.concatenate(cols, axis=-1) * g


def _mm_kernel(*refs, headnorm, scale, n_out):
    if headnorm:
        x_ref, w_ref, g_ref = refs[:3]
        rest = refs[3:]
    else:
        x_ref, w_ref = refs[:2]
        rest = refs[2:]
    outs, wb_ref = rest[:n_out], rest[n_out]

    @pl.when(_first_inner_step())
    def _():
        wb_ref[...] = w_ref[0].astype(BF16)

    acc = jnp.dot(x_ref[0], wb_ref[...], preferred_element_type=F32)
    if headnorm:
        acc = _head_rmsnorm(acc, g_ref[0])
        if scale != 1.0:
            acc = acc * scale
    for o in outs:
        o[0] = acc.astype(o.dtype)


def _proj(x, w_stack, l, col0, ncols, out_dtypes, *, head_g=None, scale=1.0, tm=1024, tn=1024):
    B, T, K = x.shape
    tm = min(tm, T)
    tn = min(tn, ncols)
    assert T % tm == 0 and ncols % tn == 0 and col0 % tn == 0
    cb = col0 // tn
    in_specs = [pl.BlockSpec((1, tm, K), lambda n, b, i: (b, i, 0)),
                pl.BlockSpec((1, K, tn), lambda n, b, i: (l, 0, cb + n))]
    args = [x, w_stack]
    if head_g is not None:
        in_specs.append(pl.BlockSpec((1, 1, tn), lambda n, b, i: (0, 0, 0)))
        args.append(jnp.tile(head_g, tn // HEAD_DIM).reshape(1, 1, tn))
    nbytes = 2 * (tm * K * 2 + K * tn * 4 + tm * tn * 4 * len(out_dtypes)) + K * tn * 2 + 2 * tm * tn * 4
    return pl.pallas_call(
        functools.partial(_mm_kernel, headnorm=head_g is not None, scale=scale, n_out=len(out_dtypes)),
        grid=(ncols // tn, B, T // tm),
        in_specs=in_specs,
        out_specs=[pl.BlockSpec((1, tm, tn), lambda n, b, i: (b, i, n)) for _ in out_dtypes],
        out_shape=[jax.ShapeDtypeStruct((B, T, ncols), dt) for dt in out_dtypes],
        scratch_shapes=[pltpu.VMEM((K, tn), BF16)],
        compiler_params=_params(("arbitrary",) * 3, nbytes),
        name="proj",
    )(*args)


def _swiglu_kernel(x_ref, wg_ref, wu_ref, o_ref, wgb_ref, wub_ref):
    @pl.when(_first_inner_step())
    def _():
        wgb_ref[...] = wg_ref[0].astype(BF16)
        wub_ref[...] = wu_ref[0].astype(BF16)

    x = x_ref[0]
    gate = jnp.dot(x, wgb_ref[...], preferred_element_type=F32)
    up = jnp.dot(x, wub_ref[...], preferred_element_type=F32)
    o_ref[0] = (gate * _sigmoid(gate) * up).astype(o_ref.dtype)


def _proj_swiglu(x, w_gu, l, *, tm=1024, tn=512):
    B, T, K = x.shape
    F = w_gu.shape[2] // 2
    tm = min(tm, T)
    assert T % tm == 0 and F % tn == 0
    nb = F // tn
    nbytes = 2 * (tm * K * 2 + 2 * K * tn * 4 + tm * tn * 2) + 2 * K * tn * 2 + 3 * tm * tn * 4
    return pl.pallas_call(
        _swiglu_kernel,
        grid=(nb, B, T // tm),
        in_specs=[pl.BlockSpec((1, tm, K), lambda n, b, i: (b, i, 0)),
                  pl.BlockSpec((1, K, tn), lambda n, b, i: (l, 0, n)),
                  pl.BlockSpec((1, K, tn), lambda n, b, i: (l, 0, nb + n))],
        out_specs=pl.BlockSpec((1, tm, tn), lambda n, b, i: (b, i, n)),
        out_shape=jax.ShapeDtypeStruct((B, T, F), BF16),
        scratch_shapes=[pltpu.VMEM((K, tn), BF16), pltpu.VMEM((K, tn), BF16)],
        compiler_params=_params(("arbitrary",) * 3, nbytes),
        name="proj_swiglu",
    )(x, w_gu, w_gu)


def _resid_kernel(*refs, k_sizes):
    n_x = len(k_sizes)
    x_refs = refs[:n_x]
    w_ref, res_ref, gate_ref, o_ref, wb_ref = refs[n_x:]

    @pl.when(_first_inner_step())
    def _():
        wb_ref[...] = w_ref[0].astype(BF16)

    acc = None
    k0 = 0
    for x_ref, ks in zip(x_refs, k_sizes):
        part = jnp.dot(x_ref[0], wb_ref[k0:k0 + ks, :], preferred_element_type=F32)
        acc = part if acc is None else acc + part
        k0 += ks
    o_ref[0] = res_ref[0] + gate_ref[0] * acc


def _proj_resid(xs, w_stack, l, resid, gate, *, tm=512, tn=512):
    B, T, N = resid.shape
    k_sizes = tuple(x.shape[2] for x in xs)
    K = sum(k_sizes)
    tm = min(tm, T)
    assert T % tm == 0 and N % tn == 0
    if gate.shape[1] == 1:
        gate_spec = pl.BlockSpec((1, 1, tn), lambda n, b, i: (b, 0, n))
    else:
        gate_spec = pl.BlockSpec((1, tm, tn), lambda n, b, i: (b, i, n))
    in_specs = [pl.BlockSpec((1, tm, ks), lambda n, b, i: (b, i, 0)) for ks in k_sizes]
    in_specs += [pl.BlockSpec((1, K, tn), lambda n, b, i: (l, 0, n)),
                 pl.BlockSpec((1, tm, tn), lambda n, b, i: (b, i, n)),
                 gate_spec]
    nbytes = 2 * (tm * K * 2 + K * tn * 4 + 2 * tm * tn * 4) + K * tn * 2 + 2 * tm * tn * 4
    return pl.pallas_call(
        functools.partial(_resid_kernel, k_sizes=k_sizes),
        grid=(N // tn, B, T // tm),
        in_specs=in_specs,
        out_specs=pl.BlockSpec((1, tm, tn), lambda n, b, i: (b, i, n)),
        out_shape=jax.ShapeDtypeStruct((B, T, N), F32),
        scratch_shapes=[pltpu.VMEM((K, tn), BF16)],
        compiler_params=_params(("arbitrary",) * 3, nbytes),
        name="proj_resid",
    )(*xs, w_stack, resid, gate)


def _rel_bucket(dist):
    max_exact = NUM_BUCKETS // 2
    d = jnp.maximum(dist, 0)
    df = jnp.maximum(d, 1).astype(F32)
    large = max_exact + (jnp.log(df / max_exact) / math.log(MAX_DISTANCE / max_exact)
                         * (NUM_BUCKETS - max_exact)).astype(jnp.int32)
    return jnp.where(d < max_exact, d, jnp.minimum(large, NUM_BUCKETS - 1))


def _distance_logits(rel_bias, nd):
    d = jnp.arange(nd)
    mult = sum(((d % dil == 0) & (d // dil <= w // dil)).astype(F32) for (w, dil) in BRANCHES)
    bias = rel_bias[_rel_bucket(d)].astype(F32).T
    return jnp.where(mult > 0, bias + jnp.log(jnp.maximum(mult, 1.0)), NEG)


def _toeplitz_tiles(rel_bias, T):
    nq = T // Q_BLOCK
    g = _distance_logits(rel_bias, T + Q_BLOCK)
    i = jnp.arange(Q_BLOCK)
    idx = (jnp.arange(nq) * Q_BLOCK)[:, None, None] + i[None, :, None] - i[None, None, :]
    tiles = g[:, jnp.maximum(idx, 0)]
    return jnp.where(idx[None] >= 0, tiles, NEG)


def _attn_kernel(q_ref, k_ref, v_ref, tab_ref, o_ref):
    qi = pl.program_id(2)
    outs = []
    for hh in range(LANES // HEAD_DIM):
        lo = hh * HEAD_DIM
        q = q_ref[0, :, lo:lo + HEAD_DIM]

        def body(j, carry, lo=lo, q=q, hh=hh):
            m, den, acc = carry
            start = pl.multiple_of((qi - j) * Q_BLOCK, Q_BLOCK)
            k = k_ref[0, pl.ds(start, Q_BLOCK), lo:lo + HEAD_DIM]
            v = v_ref[0, pl.ds(start, Q_BLOCK), lo:lo + HEAD_DIM]
            s = lax.dot_general(q, k, (((1,), (1,)), ((), ())), preferred_element_type=F32) + tab_ref[hh, j]
            m_new = jnp.maximum(m, jnp.max(s, axis=-1, keepdims=True))
            p = jnp.exp(s - m_new)
            alpha = jnp.exp(m - m_new)
            den = alpha * den + jnp.sum(p, axis=-1, keepdims=True)
            acc = alpha * acc + jnp.dot(p.astype(BF16), v, preferred_element_type=F32)
            return m_new, den, acc

        init = (jnp.full((Q_BLOCK, 1), NEG, F32), jnp.zeros((Q_BLOCK, 1), F32),
                jnp.zeros((Q_BLOCK, HEAD_DIM), F32))
        m, den, acc = lax.fori_loop(0, qi + 1, body, init)
        outs.append(acc / den)
    o_ref[0] = jnp.concatenate(outs, axis=-1).astype(o_ref.dtype)


def _attention_prompt(q, k, v, tiles):
    B, T, DA = q.shape
    hpb = LANES // HEAD_DIM
    nq = T // Q_BLOCK
    nbytes = 2 * (2 * T * LANES * 2 + hpb * nq * Q_BLOCK * Q_BLOCK * 4 + 2 * Q_BLOCK * LANES * 2)
    return pl.pallas_call(
        _attn_kernel,
        grid=(DA // LANES, B, nq),
        in_specs=[pl.BlockSpec((1, Q_BLOCK, LANES), lambda p, b, i: (b, i, p)),
                  pl.BlockSpec((1, T, LANES), lambda p, b, i: (b, 0, p)),
                  pl.BlockSpec((1, T, LANES), lambda p, b, i: (b, 0, p)),
                  pl.BlockSpec((hpb, nq, Q_BLOCK, Q_BLOCK), lambda p, b, i: (p, 0, 0, 0))],
        out_specs=pl.BlockSpec((1, Q_BLOCK, LANES), lambda p, b, i: (b, i, p)),
        out_shape=jax.ShapeDtypeStruct((B, T, DA), BF16),
        compiler_params=_params(("arbitrary",) * 3, nbytes),
        name="attn_prompt",
    )(q, k, v, tiles)


def _sattn_kernel(q_ref, kn_ref, vn_ref, k1_ref, k2_ref, k3_ref, v1_ref, v2_ref, v3_ref,
                  sb_ref, b0_ref, o_ref, *, n_heads):
    DA = q_ref.shape[-1]
    row = lax.broadcasted_iota(jnp.int32, (n_heads, DA), 0)
    lane_head = lax.broadcasted_iota(jnp.int32, (n_heads, DA), 1) // HEAD_DIM
    own = row == lane_head
    q = jnp.where(own, q_ref[0], 0.0).astype(BF16)
    kn = kn_ref[0].astype(BF16).astype(F32)
    vn = vn_ref[0].astype(BF16).astype(F32)
    s_self = jnp.sum(q.astype(F32) * kn, axis=-1, keepdims=True) + b0_ref[...]
    s_blk = [_dot_nt(q, kr[0, 0]) + sb_ref[i] for i, kr in enumerate((k1_ref, k2_ref, k3_ref))]
    m = s_self
    for s in s_blk:
        m = jnp.maximum(m, jnp.max(s, axis=-1, keepdims=True))
    n_br = len(s_blk)
    p_self = jnp.exp(s_self - m) * float(n_br)
    den = p_self
    acc = p_self * vn
    for s, vr in zip(s_blk, (v1_ref, v2_ref, v3_ref)):
        p = jnp.exp(s - m)
        den = den + jnp.sum(p, axis=-1, keepdims=True)
        acc = acc + _dot(p, vr[0, 0])
    o = jnp.where(own, acc / den, 0.0)
    o_ref[0] = jnp.sum(o, axis=0, keepdims=True).astype(o_ref.dtype)


def _attention_sample(q, k_new, v_new, cache_k, cache_v, l, rel_bias):
    L, DB, W, H, E = cache_k.shape
    DA = H * E
    nk = BRANCHES[0][0] // BRANCHES[0][1]
    assert W == W_MAX and all(w // dil == nk and W % dil == 0 and (W // dil) % nk == 0 for w, dil in BRANCHES)
    m_of_row = nk - jnp.arange(nk)
    sb = jnp.stack([rel_bias[_rel_bucket(dil * m_of_row)].astype(F32).T for (_, dil) in BRANCHES])
    b0 = rel_bias[_rel_bucket(jnp.zeros((1,), jnp.int32))].astype(F32).T
    views, specs = [], []
    for cache in (cache_k, cache_v):
        for (_, dil) in BRANCHES:
            views.append(cache.reshape(L, DB, W // dil, dil * DA))
            specs.append(pl.BlockSpec((1, 1, nk, DA), lambda b, dil=dil: (l, b, W // dil // nk - 1, 0)))
    vec = pl.BlockSpec((1, 1, DA), lambda b: (b, 0, 0))
    return pl.pallas_call(
        functools.partial(_sattn_kernel, n_heads=H),
        grid=(DB,),
        in_specs=[vec, vec, vec] + specs + [pl.BlockSpec((len(BRANCHES), H, nk), lambda b: (0, 0, 0)),
                                            pl.BlockSpec((H, 1), lambda b: (0, 0))],
        out_specs=vec,
        out_shape=jax.ShapeDtypeStruct((DB, 1, DA), BF16),
        compiler_params=_params(("arbitrary",), 2 * 6 * nk * DA * 4),
        name="attn_sample",
    )(q, k_new, v_new, *views, sb, b0)


def _prep_kernel(*refs, shift_rows, has_vres):
    it = iter(refs)
    h_ref = next(it)
    if shift_rows:
        hp8_ref, hlast_ref = next(it), next(it)
    else:
        hprev_ref = next(it)
    rkv_ref = next(it)
    if shift_rows:
        rp8_ref, rlast_ref = next(it), next(it)
    else:
        rprev_ref = next(it)
    if has_vres:
        vfirst_ref = next(it)
    mu_ref, murkv_ref, w0_ref, dw1_ref, dw2_ref, a0_ref, aw1_ref, aw2_ref, gw1_ref, gw2_ref = (next(it) for _ in range(10))
    if has_vres:
        vmu_ref, v0_ref, vw1_ref, vw2_ref = (next(it) for _ in range(4))
    kk_ref, ka_ref = next(it), next(it)
    r_out, lw_out, k_out, v_out, kk_out, a_out, g_out = (next(it) for _ in range(7))

    h = h_ref[0]
    rkv0 = rkv_ref[0]
    tm = h.shape[0]
    if shift_rows:
        first = pl.program_id(1) == 0
        row0 = lax.broadcasted_iota(jnp.int32, (tm, 1), 0) == 0
        h_edge = jnp.where(first, hlast_ref[0], hp8_ref[0, SUBLANES - 1:SUBLANES, :])
        r_edge = jnp.where(first, rlast_ref[0], rp8_ref[0, SUBLANES - 1:SUBLANES, :])
        hprev = jnp.where(row0, h_edge, pltpu.roll(h, 1, 0))
        rprev = jnp.where(row0, r_edge, pltpu.roll(rkv0, 1, 0))
    else:
        hprev = hprev_ref[0]
        rprev = rprev_ref[0]

    dh = hprev - h
    mu = mu_ref[0]
    xw = h + dh * mu[0:1]
    xa = h + dh * mu[1:2]
    xg = h + dh * mu[2:3]

    z = w0_ref[0] + _dot(jnp.tanh(_dot(xw, dw1_ref[0])), dw2_ref[0])
    softplus = jnp.maximum(-z, 0.0) + jnp.log(1.0 + jnp.exp(-jnp.abs(z)))
    lw_out[0] = -jnp.exp(-softplus - 0.5)

    a = _sigmoid(a0_ref[0] + _dot(_dot(xa, aw1_ref[0]), aw2_ref[0]))
    a_out[0] = a
    g_out[0] = _dot(_sigmoid(_dot(xg, gw1_ref[0])), gw2_ref[0])

    murkv = murkv_ref[0]
    DR = kk_ref.shape[-1]
    r0, k0, v0 = (rkv0[:, j * DR:(j + 1) * DR] for j in range(3))
    rp, kp, vp = (rprev[:, j * DR:(j + 1) * DR] for j in range(3))
    r_out[0] = r0 + (rp - r0) * murkv[0:1]
    kr = k0 + (kp - k0) * murkv[1:2]
    vr = v0 + (vp - v0) * murkv[2:3]
    if has_vres:
        xv = h + dh * vmu_ref[0]
        vgate = _sigmoid(v0_ref[0] + _dot(_dot(xv, vw1_ref[0]), vw2_ref[0]))
        vr = vr + (vfirst_ref[0] - vr) * vgate
    v_out[0] = vr
    kk_out[0] = kr * kk_ref[0]
    k_out[0] = kr * (1.0 + (a - 1.0) * ka_ref[0])


def _rwkv_prep(h, h_prev, rkv0, rkv_prev, v_first, P, l, *, tm=256):
    B, T, D = h.shape
    DR = rkv0.shape[2] // 3
    tm = min(tm, T)
    shift_rows = h_prev.shape[1] == 1 and T > 1
    has_vres = v_first is not None
    L = P['mu_wag'].shape[0]

    def tile(C):
        return pl.BlockSpec((1, tm, C), lambda b, i: (b, i, 0))

    def prev8(C):
        return pl.BlockSpec((1, SUBLANES, C), lambda b, i: (b, jnp.maximum(i * (tm // SUBLANES) - 1, 0), 0))

    def seq_row(C):
        return pl.BlockSpec((1, 1, C), lambda b, i: (b, 0, 0))

    def layer(shape, ll=l):
        return pl.BlockSpec((1,) + shape, lambda b, i: (ll,) + (0,) * len(shape))

    args, specs = [h], [tile(D)]
    if shift_rows:
        args += [h, h_prev]
        specs += [prev8(D), seq_row(D)]
    else:
        args += [jnp.broadcast_to(h_prev, h.shape)]
        specs += [tile(D)]
    args.append(rkv0)
    specs.append(tile(3 * DR))
    if shift_rows:
        args += [rkv0, rkv_prev]
        specs += [prev8(3 * DR), seq_row(3 * DR)]
    else:
        args += [jnp.broadcast_to(rkv_prev, rkv0.shape)]
        specs += [tile(3 * DR)]
    if has_vres:
        args.append(v_first)
        specs.append(tile(DR))
    r1 = lambda a: a.reshape(a.shape[0], 1, a.shape[-1])
    for name in ('mu_wag', 'mu_rkv'):
        args.append(P[name]); specs.append(layer(P[name].shape[1:]))
    args.append(r1(P['decay_w0'])); specs.append(layer((1, DR)))
    for name in ('decay_w1', 'decay_w2'):
        args.append(P[name]); specs.append(layer(P[name].shape[1:]))
    args.append(r1(P['aaa_a0'])); specs.append(layer((1, DR)))
    for name in ('aaa_w1', 'aaa_w2', 'gate_w1', 'gate_w2'):
        args.append(P[name]); specs.append(layer(P[name].shape[1:]))
    if has_vres:
        args.append(r1(P['vres_mu'])); specs.append(layer((1, D), l - 1))
        args.append(r1(P['vres_v0'])); specs.append(layer((1, DR), l - 1))
        for name in ('vres_w1', 'vres_w2'):
            args.append(P[name]); specs.append(layer(P[name].shape[1:], l - 1))
    args.append(r1(P['k_k'])); specs.append(layer((1, DR)))
    args.append(r1(P['k_a'])); specs.append(layer((1, DR)))
    nbytes = 2 * tm * 4 * (2 * D + 7 * DR + 7 * DR + DR) + 8 * tm * D * 4 + 4 * D * 512 * 4
    return pl.pallas_call(
        functools.partial(_prep_kernel, shift_rows=shift_rows, has_vres=has_vres),
        grid=(B, T // tm),
        in_specs=specs,
        out_specs=[tile(DR) for _ in range(7)],
        out_shape=[jax.ShapeDtypeStruct((B, T, DR), F32) for _ in range(7)],
        compiler_params=_params(("arbitrary", "arbitrary"), nbytes),
        name="rwkv_prep",
    )(*args)


def _prefix_sum_rows(x):
    n = x.shape[0]
    row = lax.broadcasted_iota(jnp.int32, (n, 1), 0)
    s = 1
    while s < n:
        x = x + jnp.where(row >= s, pltpu.roll(x, s, 0), 0.0)
        s *= 2
    return x


def _unit_lower_inverse(a_strict, ti, si):
    n = a_strict.shape[0]
    d = (ti == si).astype(F32)
    s = 1
    while s < n:
        lower_left = ((ti // (2 * s)) == (si // (2 * s))) & ((ti % (2 * s)) >= s) & ((si % (2 * s)) < s)
        d = d + _dot(_dot(d, jnp.where(lower_left, a_strict, 0.0)), d)
        s *= 2
    return d


def _chunk_terms(r, lw_cum, lw, k, v, kk, a_sig, ti, si):
    C = r.shape[0]
    g_in = jnp.exp(lw_cum)
    g_ex = jnp.exp(lw_cum - lw)
    g_inv = jnp.exp(-lw_cum)
    g_end = g_in[C - 1:C]
    kkn = kk / jnp.maximum(jnp.sqrt(jnp.sum(kk * kk, axis=-1, keepdims=True)), 1e-12)
    a_t = -kkn * g_ex
    r_t = r * g_in
    b_t = kkn * a_sig * g_inv
    k_t = k * g_inv
    strict = si < ti
    incl = si <= ti
    ar = jnp.concatenate([a_t, r_t], axis=0)
    xb = _dot_nt(ar, b_t)
    xk = _dot_nt(ar, k_t)
    a_ab = jnp.where(strict, xb[:C], 0.0)
    a_ak = jnp.where(strict, xk[:C], 0.0)
    m_rb = jnp.where(incl, xb[C:], 0.0)
    m_rk = jnp.where(incl, xk[C:], 0.0)
    tinv = _unit_lower_inverse(a_ab, ti, si)
    p = _dot(tinv, a_t)
    q = _dot(tinv, _dot(a_ak, v))
    r_eff = r_t + _dot(m_rb, p)
    y0 = _dot(m_rb, q) + _dot(m_rk, v)
    bg = b_t * g_end
    kg = k_t * g_end
    g_mat = _dot_tn(p, bg)
    h_mat = _dot_tn(jnp.concatenate([q, v], axis=0), jnp.concatenate([bg, kg], axis=0))
    return r_eff, y0, g_mat, h_mat, g_end


def _scan_kernel(r_ref, lw_ref, k_ref, v_ref, kk_ref, a_ref, g_ref, lng_ref, lnb_ref, rk_ref,
                 o_ref, s_ref, reff_s, y0_s, gm_s, hm_s, ge_s, *, n_heads, chunk):
    C = chunk
    T = r_ref.shape[1]
    NC = T // C
    E = HEAD_DIM
    ti = lax.broadcasted_iota(jnp.int32, (C, C), 0)
    si = lax.broadcasted_iota(jnp.int32, (C, C), 1)

    def phase1(c, _):
        rows = pl.ds(pl.multiple_of(c * C, C), C)
        lw = lw_ref[0, rows, :]
        cum = _prefix_sum_rows(lw)
        r, k, v, kk, a = (ref[0, rows, :] for ref in (r_ref, k_ref, v_ref, kk_ref, a_ref))
        for hh in range(n_heads):
            sl = slice(hh * E, (hh + 1) * E)
            r_eff, y0, g_mat, h_mat, g_end = _chunk_terms(r[:, sl], cum[:, sl], lw[:, sl], k[:, sl], v[:, sl],
                                                          kk[:, sl], a[:, sl], ti, si)
            reff_s[hh, rows, :] = r_eff
            y0_s[hh, rows, :] = y0
            gm_s[hh, c] = g_mat
            hm_s[hh, c] = h_mat
            ge_s[hh, c] = jnp.broadcast_to(g_end, (SUBLANES, E))
        return 0

    lax.fori_loop(0, NC, phase1, 0)

    def phase2(c, states):
        rows = pl.ds(pl.multiple_of(c * C, C), C)
        r, k, v, g = (ref[0, rows, :] for ref in (r_ref, k_ref, v_ref, g_ref))
        new_states, outs = [], []
        for hh in range(n_heads):
            sl = slice(hh * E, (hh + 1) * E)
            S = states[hh]
            y = _dot_nt(reff_s[hh, rows, :], S) + y0_s[hh, rows, :]
            new_states.append(S * ge_s[hh, c][0:1] + _dot(S, gm_s[hh, c]) + hm_s[hh, c])
            mu = jnp.mean(y, axis=-1, keepdims=True)
            yc = y - mu
            var = jnp.mean(yc * yc, axis=-1, keepdims=True)
            yn = yc * lax.rsqrt(var + GN_EPS) * lng_ref[0, :, sl] + lnb_ref[0, :, sl]
            bonus = jnp.sum(r[:, sl] * k[:, sl] * rk_ref[0, :, sl], axis=-1, keepdims=True)
            outs.append((yn + bonus * v[:, sl]) * g[:, sl])
        o_ref[0, rows, :] = jnp.concatenate(outs, axis=-1).astype(o_ref.dtype)
        return tuple(new_states)

    states = lax.fori_loop(0, NC, phase2, tuple(jnp.zeros((E, E), F32) for _ in range(n_heads)))
    for hh in range(n_heads):
        s_ref[0, hh] = states[hh]


def _rwkv_scan(r, lw, k, v, kk, a, g, lnx_g, lnx_b, r_k, l):
    B, T, DR = r.shape
    H = DR // HEAD_DIM
    L = lnx_g.shape[0]
    C = min(SCAN_CHUNK, T)
    assert T % C == 0
    nh = SCAN_HEADS
    lw_ = nh * HEAD_DIM
    seq = pl.BlockSpec((1, T, lw_), lambda p, b: (b, 0, p))
    par = pl.BlockSpec((1, 1, lw_), lambda p, b: (l, 0, p))
    nc = T // C
    scratch = [pltpu.VMEM((nh, T, HEAD_DIM), F32), pltpu.VMEM((nh, T, HEAD_DIM), F32),
               pltpu.VMEM((nh, nc, HEAD_DIM, HEAD_DIM), F32), pltpu.VMEM((nh, nc, HEAD_DIM, HEAD_DIM), F32),
               pltpu.VMEM((nh, nc, SUBLANES, HEAD_DIM), F32)]
    nbytes = 2 * 8 * T * lw_ * 4 + 2 * nh * T * LANES * 4 + 2 * nh * nc * HEAD_DIM * LANES * 4 + nh * nc * 8 * LANES * 4
    return pl.pallas_call(
        functools.partial(_scan_kernel, n_heads=nh, chunk=C),
        grid=(DR // lw_, B),
        in_specs=[seq] * 7 + [par] * 3,
        out_specs=[seq, pl.BlockSpec((1, nh, HEAD_DIM, HEAD_DIM), lambda p, b: (b, p, 0, 0))],
        out_shape=[jax.ShapeDtypeStruct((B, T, DR), BF16), jax.ShapeDtypeStruct((B, H, HEAD_DIM, HEAD_DIM), F32)],
        scratch_shapes=scratch,
        compiler_params=_params(("arbitrary", "arbitrary"), nbytes),
        name="rwkv_scan",
    )(r, lw, k, v, kk, a, g, lnx_g.reshape(L, 1, DR), lnx_b.reshape(L, 1, DR), r_k.reshape(L, 1, DR))


def _step_kernel(s_ref, r_ref, lw_ref, k_ref, v_ref, kk_ref, a_ref, g_ref, lng_ref, lnb_ref, rk_ref,
                 o_ref, so_ref, *, n_heads):
    E = HEAD_DIM
    eye = (lax.broadcasted_iota(jnp.int32, (E, E), 0) == lax.broadcasted_iota(jnp.int32, (E, E), 1)).astype(F32)
    outs = []
    for hh in range(n_heads):
        sl = slice(hh * E, (hh + 1) * E)
        S = s_ref[0, hh]
        r, k, v, kk, a_sig, g = (ref[0, :, sl] for ref in (r_ref, k_ref, v_ref, kk_ref, a_ref, g_ref))
        w = jnp.exp(lw_ref[0, :, sl])
        kkn = kk / jnp.maximum(jnp.sqrt(jnp.sum(kk * kk, axis=-1, keepdims=True)), 1e-12)
        v_col = jnp.sum(eye * v, axis=-1, keepdims=True)
        sa = jnp.sum(S * (-kkn), axis=-1, keepdims=True)
        S = S * w + sa * (kkn * a_sig) + v_col * k
        so_ref[0, hh] = S
        y_col = jnp.sum(S * r, axis=-1, keepdims=True)
        y = jnp.sum(eye * y_col, axis=0, keepdims=True)
        mu = jnp.mean(y, axis=-1, keepdims=True)
        yc = y - mu
        var = jnp.mean(yc * yc, axis=-1, keepdims=True)
        yn = yc * lax.rsqrt(var + GN_EPS) * lng_ref[0, :, sl] + lnb_ref[0, :, sl]
        bonus = jnp.sum(r * k * rk_ref[0, :, sl], axis=-1, keepdims=True)
        outs.append((yn + bonus * v) * g)
    o_ref[0] = jnp.concatenate(outs, axis=-1).astype(o_ref.dtype)


def _rwkv_step(state, r, lw, k, v, kk, a, g, lnx_g, lnx_b, r_k, l):
    DB, H = state.shape[:2]
    DR = H * HEAD_DIM
    L = lnx_g.shape[0]
    vec = pl.BlockSpec((1, 1, DR), lambda b: (b, 0, 0))
    par = pl.BlockSpec((1, 1, DR), lambda b: (l, 0, 0))
    st = pl.BlockSpec((1, H, HEAD_DIM, HEAD_DIM), lambda b: (b, 0, 0, 0))
    return pl.pallas_call(
        functools.partial(_step_kernel, n_heads=H),
        grid=(DB,),
        in_specs=[st] + [vec] * 7 + [par] * 3,
        out_specs=[vec, st],
        out_shape=[jax.ShapeDtypeStruct((DB, 1, DR), BF16), jax.ShapeDtypeStruct(state.shape, F32)],
        compiler_params=_params(("arbitrary",), 4 * H * HEAD_DIM * LANES * 4),
        name="rwkv_step",
    )(state, r, lw, k, v, kk, a, g, lnx_g.reshape(L, 1, DR), lnx_b.reshape(L, 1, DR), r_k.reshape(L, 1, DR))


def _mixer_tail(P, l, x, att, rw, mods):
    x = _proj_resid([att, rw], P['w_out'], l, x, mods[2])
    (h2,) = _norm_mod(x, P['norm2_g'], l, mods[3], mods[4], [BF16])
    act = _proj_swiglu(h2, P['w_gu'], l)
    return _proj_resid([act], P['w_down'], l, x, mods[5])


def _prompt_layer(P, l, x, mods, tiles, v_first):
    B, T, D = x.shape
    DA = P['rel_bias'].shape[1] * HEAD_DIM
    h, hb = _norm_mod(x, P['norm1_g'], l, mods[0], mods[1], [F32, BF16])
    (q,) = _proj(hb, P['w_in'], l, 0, DA, [BF16], head_g=P['q_norm_g'][l], scale=ATT_SCALE)
    k, kb = _proj(hb, P['w_in'], l, DA, DA, [F32, BF16], head_g=P['k_norm_g'][l])
    v, vb = _proj(hb, P['w_in'], l, 2 * DA, DA, [F32, BF16])
    (rkv0,) = _proj(hb, P['w_in'], l, 3 * DA, P['w_in'].shape[2] - 3 * DA, [F32])
    att = _attention_prompt(q, kb, vb, tiles)
    zero_h = jnp.zeros((B, 1, D), F32)
    zero_r = jnp.zeros((B, 1, rkv0.shape[2]), F32)
    r, lw, kr, vr, kk, a, g = _rwkv_prep(h, zero_h, rkv0, zero_r, v_first, P, l)
    if v_first is None:
        v_first = vr
    rw, state = _rwkv_scan(r, lw, kr, vr, kk, a, g, P['lnx_g'], P['lnx_b'], P['r_k'], l)
    x = _mixer_tail(P, l, x, att, rw, mods)
    return x, v_first, k, v, state, h[:, -1]


def _sample_layer(P, l, x, mods, cache_k, cache_v, state, h_last, v_first):
    _, DB, D = x.shape
    DA = P['rel_bias'].shape[1] * HEAD_DIM
    h, hb = _norm_mod(x, P['norm1_g'], l, mods[0], mods[1], [F32, BF16])
    (q,) = _proj(hb, P['w_in'], l, 0, DA, [F32], head_g=P['q_norm_g'][l], scale=ATT_SCALE)
    (k,) = _proj(hb, P['w_in'], l, DA, DA, [F32], head_g=P['k_norm_g'][l])
    (v,) = _proj(hb, P['w_in'], l, 2 * DA, DA, [F32])
    both = jnp.concatenate([hb, h_last[None].astype(BF16)], axis=1)
    (rkv2,) = _proj(both, P['w_in'], l, 3 * DA, P['w_in'].shape[2] - 3 * DA, [F32])
    rkv0, rkv_prev = rkv2[:, :DB], rkv2[:, DB:]
    per_seq = lambda t: t.reshape(DB, 1, t.shape[-1])
    att = _attention_sample(per_seq(q), per_seq(k), per_seq(v), cache_k, cache_v, l, P['rel_bias'])
    r, lw, kr, vr, kk, a, g = _rwkv_prep(h, h_last[None], rkv0, rkv_prev, v_first, P, l)
    if v_first is None:
        v_first = vr
    rw, new_state = _rwkv_step(state, *(per_seq(t) for t in (r, lw, kr, vr, kk, a, g)),
                               P['lnx_g'], P['lnx_b'], P['r_k'], l)
    x = _mixer_tail(P, l, x, att.reshape(1, DB, DA), rw.reshape(1, DB, DA), mods)
    return x, v_first, k[0], v[0], new_state, h[0]


def kernel(x_prompt, x_sample, c_prompt, c_sample, cache_k, cache_v, state_wkv, state_shift, rel_bias, ada_w, ada_b, norm1_g, norm2_g, w_in, q_norm_g, k_norm_g, mu_wag, mu_rkv, decay_w0, decay_w1, decay_w2, aaa_a0, aaa_w1, aaa_w2, gate_w1, gate_w2, vres_mu, vres_v0, vres_w1, vres_w2, k_k, k_a, r_k, lnx_g, lnx_b, w_out, w_gu, w_down):
    P = dict(rel_bias=rel_bias, norm1_g=norm1_g, norm2_g=norm2_g, w_in=w_in, q_norm_g=q_norm_g,
             k_norm_g=k_norm_g, mu_wag=mu_wag, mu_rkv=mu_rkv, decay_w0=decay_w0, decay_w1=decay_w1,
             decay_w2=decay_w2, aaa_a0=aaa_a0, aaa_w1=aaa_w1, aaa_w2=aaa_w2, gate_w1=gate_w1,
             gate_w2=gate_w2, vres_mu=vres_mu, vres_v0=vres_v0, vres_w1=vres_w1, vres_w2=vres_w2,
             k_k=k_k, k_a=k_a, r_k=r_k.reshape(r_k.shape[0], -1), lnx_g=lnx_g, lnx_b=lnx_b,
             w_out=w_out, w_gu=w_gu, w_down=w_down)
    L = ada_w.shape[0]
    B, T, D = x_prompt.shape
    DB = x_sample.shape[0]
    H_ATT = rel_bias.shape[1]
    assert x_sample.shape[1] == 1 and T % Q_BLOCK == 0

    rows = -(-(B + DB) // SUBLANES) * SUBLANES
    c_all = jnp.concatenate([c_prompt, c_sample, jnp.zeros((rows - B - DB, D), F32)], axis=0)
    mod = _ada(c_all, ada_w, ada_b).reshape(L, rows, 6, D)
    tiles = _toeplitz_tiles(rel_bias, T)

    xp, vf = x_prompt, None
    pk, pv, ps, ph = [], [], [], []
    for l in range(L):
        mods = [mod[l, :B, j][:, None, :] for j in range(6)]
        xp, vf, nk, nv, ns, nh = _prompt_layer(P, l, xp, mods, tiles, vf)
        pk.append(nk.reshape(B, T, H_ATT, HEAD_DIM)[:, T - min(W_MAX, T):])
        pv.append(nv.reshape(B, T, H_ATT, HEAD_DIM)[:, T - min(W_MAX, T):])
        ps.append(ns)
        ph.append(nh)

    xs, vf = x_sample.reshape(1, DB, D), None
    sk, sv, ss, sh = [], [], [], []
    for l in range(L):
        mods = [mod[l, B:B + DB, j][None] for j in range(6)]
        xs, vf, nk, nv, ns, nh = _sample_layer(P, l, xs, mods, cache_k, cache_v, state_wkv[l], state_shift[l], vf)
        sk.append(nk.reshape(DB, 1, H_ATT, HEAD_DIM))
        sv.append(nv.reshape(DB, 1, H_ATT, HEAD_DIM))
        ss.append(ns)
        sh.append(nh)

    return (xp, xs.reshape(DB, 1, D), jnp.stack(pk), jnp.stack(pv), jnp.stack(ps), jnp.stack(ph),
            jnp.stack(sk), jnp.stack(sv), jnp.stack(ss), jnp.stack(sh))
```

```python
import functools
import math

import jax
import jax.numpy as jnp
from jax import lax
from jax.experimental import pallas as pl
from jax.experimental.pallas import tpu as pltpu

F32 = jnp.float32
BF16 = jnp.bfloat16

HEAD_DIM = 64
BRANCHES = ((128, 1), (512, 4), (2048, 16))
W_MAX = 2048
NUM_BUCKETS = 32
MAX_DISTANCE = W_MAX
ATT_SCALE = HEAD_DIM ** -0.5
RMS_EPS = 1e-6
GN_EPS = 64e-5
NEG = -1e30

LANES = 128
SUBLANES = 8
Q_BLOCK = 128
SCAN_CHUNK = 64
SCAN_UNROLL = 4
VMEM_CAP = 56 * 1024 * 1024

assert all(w // dil == Q_BLOCK for w, dil in BRANCHES) and 2 * HEAD_DIM == LANES


def _vmem(nbytes):
    return int(min(VMEM_CAP, nbytes * 1.3 + (6 << 20)))


def _params(sem, nbytes):
    return pltpu.CompilerParams(dimension_semantics=sem, vmem_limit_bytes=_vmem(nbytes))


def _dot(a, b):
    return jnp.dot(a.astype(BF16), b.astype(BF16), preferred_element_type=F32)


def _dot_nt(a, b):
    return lax.dot_general(a.astype(BF16), b.astype(BF16), (((1,), (1,)), ((), ())), preferred_element_type=F32)


def _dot_tn(a, b):
    return lax.dot_general(a.astype(BF16), b.astype(BF16), (((0,), (0,)), ((), ())), preferred_element_type=F32)


def _split_dot(a, b01):
    hi = a.astype(BF16)
    lo = (a - hi.astype(F32)).astype(BF16)
    return jnp.dot(hi, b01, preferred_element_type=F32) + jnp.dot(lo, b01, preferred_element_type=F32)


def _sigmoid(x):
    return 1.0 / (1.0 + jnp.exp(-x))


def _head_sums(x):
    r = lax.broadcasted_iota(jnp.int32, (LANES, LANES), 0) // HEAD_DIM
    c = lax.broadcasted_iota(jnp.int32, (LANES, LANES), 1) // HEAD_DIM
    bd = (r == c).astype(BF16)
    cols = [_split_dot(x[:, j * LANES:(j + 1) * LANES], bd) for j in range(x.shape[1] // LANES)]
    return cols[0] if len(cols) == 1 else jnp.concatenate(cols, axis=-1)


def _ada_kernel(c_ref, w_ref, b_ref, o_ref):
    c = c_ref[...]
    s = c * _sigmoid(c)
    o_ref[0] = _dot(s, w_ref[0]) + b_ref[0]


def _ada(c_all, ada_w, ada_b, tn=1024):
    L, D, N = ada_w.shape
    R = c_all.shape[0]
    return pl.pallas_call(
        _ada_kernel,
        grid=(L, N // tn),
        in_specs=[pl.BlockSpec((R, D), lambda l, j: (0, 0)),
                  pl.BlockSpec((1, D, tn), lambda l, j: (l, 0, j)),
                  pl.BlockSpec((1, 1, tn), lambda l, j: (l, 0, j))],
        out_specs=pl.BlockSpec((1, R, tn), lambda l, j: (l, 0, j)),
        out_shape=jax.ShapeDtypeStruct((L, R, N), F32),
        compiler_params=_params(("arbitrary", "arbitrary"), 2 * D * tn * 4 + D * tn * 2),
        name="ada_mod",
    )(c_all, ada_w, ada_b.reshape(L, 1, N))


def _norm_kernel(x_ref, g_ref, sh_ref, sc_ref, *out_refs):
    x = x_ref[0]
    y = x * lax.rsqrt(jnp.mean(x * x, axis=-1, keepdims=True) + RMS_EPS) * g_ref[0]
    h = y * (1.0 + sc_ref[0]) + sh_ref[0]
    for o in out_refs:
        o[0] = h.astype(o.dtype)


def _row_spec(arr, tm, T):
    C = arr.shape[-1]
    if arr.shape[1] == 1:
        return pl.BlockSpec((1, 1, C), lambda b, i: (b, 0, 0))
    assert arr.shape[1] == T
    return pl.BlockSpec((1, tm, C), lambda b, i: (b, i, 0))


def _norm_mod(x, g_stack, l, shift, scale, out_dtypes, tm=256):
    B, T, D = x.shape
    tm = min(tm, T)
    L = g_stack.shape[0]
    outs = pl.pallas_call(
        _norm_kernel,
        grid=(B, T // tm),
        in_specs=[pl.BlockSpec((1, tm, D), lambda b, i: (b, i, 0)),
                  pl.BlockSpec((1, 1, D), lambda b, i: (l, 0, 0)),
                  _row_spec(shift, tm, T), _row_spec(scale, tm, T)],
        out_specs=[pl.BlockSpec((1, tm, D), lambda b, i: (b, i, 0)) for _ in out_dtypes],
        out_shape=[jax.ShapeDtypeStruct((B, T, D), dt) for dt in out_dtypes],
        compiler_params=_params(("arbitrary", "arbitrary"), 2 * tm * D * 4 * (2 + len(out_dtypes))),
        name="norm_mod",
    )(x, g_stack.reshape(L, 1, D), shift, scale)
    return outs


def _first_inner_step():
    return (pl.program_id(1) == 0) & (pl.program_id(2) == 0)


def _head_rmsnorm(acc, g):
    return acc * lax.rsqrt(_head_sums(acc * acc) * (1.0 / HEAD_DIM) + RMS_EPS) * g


def _mm_kernel(*refs, headnorm, scale, aliased):
    it = iter(refs)
    x_ref, w_ref = next(it), next(it)
    g_ref = next(it) if headnorm else None
    if aliased:
        next(it)
    o_ref, wb_ref = next(it), next(it)

    @pl.when(_first_inner_step())
    def _():
        wb_ref[...] = w_ref[0].astype(BF16)

    acc = jnp.dot(x_ref[0], wb_ref[...], preferred_element_type=F32)
    if headnorm:
        acc = _head_rmsnorm(acc, g_ref[0])
        if scale != 1.0:
            acc = acc * scale
    if len(o_ref.shape) == 4:
        o_ref[0, 0] = acc
    else:
        o_ref[0] = acc


def _proj(x, w_stack, l, col0, ncols, *, head_g=None, scale=1.0, stack=None, tm=1024, tn=1024):
    B, T, K = x.shape
    tm = min(tm, T)
    tn = min(tn, ncols)
    assert T % tm == 0 and ncols % tn == 0 and col0 % tn == 0
    cb = col0 // tn
    in_specs = [pl.BlockSpec((1, tm, K), lambda n, b, i: (b, i, 0)),
                pl.BlockSpec((1, K, tn), lambda n, b, i: (l, 0, cb + n))]
    args = [x, w_stack]
    if head_g is not None:
        in_specs.append(pl.BlockSpec((1, 1, tn), lambda n, b, i: (0, 0, 0)))
        args.append(jnp.tile(head_g, tn // HEAD_DIM).reshape(1, 1, tn))
    aliases = {}
    if stack is None:
        out_spec = pl.BlockSpec((1, tm, tn), lambda n, b, i: (b, i, n))
        out_shape = jax.ShapeDtypeStruct((B, T, ncols), F32)
    else:
        L, prev = stack
        out_spec = pl.BlockSpec((1, 1, tm, tn), lambda n, b, i: (l, b, i, n))
        out_shape = jax.ShapeDtypeStruct((L, B, T, ncols), F32)
        if prev is not None:
            in_specs.append(pl.BlockSpec(memory_space=pl.ANY))
            args.append(prev)
            aliases = {len(args) - 1: 0}
    nbytes = 2 * (tm * K * 2 + K * tn * 4 + tm * tn * 4) + K * tn * 2 + 2 * tm * tn * 4
    return pl.pallas_call(
        functools.partial(_mm_kernel, headnorm=head_g is not None, scale=scale, aliased=bool(aliases)),
        grid=(ncols // tn, B, T // tm),
        in_specs=in_specs,
        out_specs=out_spec,
        out_shape=out_shape,
        scratch_shapes=[pltpu.VMEM((K, tn), BF16)],
        input_output_aliases=aliases,
        compiler_params=_params(("arbitrary",) * 3, nbytes),
        name="proj",
    )(*args)


def _swiglu_kernel(x_ref, wg_ref, wu_ref, o_ref, wgb_ref, wub_ref):
    @pl.when(_first_inner_step())
    def _():
        wgb_ref[...] = wg_ref[0].astype(BF16)
        wub_ref[...] = wu_ref[0].astype(BF16)

    x = x_ref[0]
    gate = jnp.dot(x, wgb_ref[...], preferred_element_type=F32)
    up = jnp.dot(x, wub_ref[...], preferred_element_type=F32)
    o_ref[0] = (gate * _sigmoid(gate) * up).astype(o_ref.dtype)


def _proj_swiglu(x, w_gu, l, *, tm=1024, tn=512):
    B, T, K = x.shape
    F = w_gu.shape[2] // 2
    tm = min(tm, T)
    assert T % tm == 0 and F % tn == 0
    nb = F // tn
    nbytes = 2 * (tm * K * 2 + 2 * K * tn * 4 + tm * tn * 2) + 2 * K * tn * 2 + 3 * tm * tn * 4
    return pl.pallas_call(
        _swiglu_kernel,
        grid=(nb, B, T // tm),
        in_specs=[pl.BlockSpec((1, tm, K), lambda n, b, i: (b, i, 0)),
                  pl.BlockSpec((1, K, tn), lambda n, b, i: (l, 0, n)),
                  pl.BlockSpec((1, K, tn), lambda n, b, i: (l, 0, nb + n))],
        out_specs=pl.BlockSpec((1, tm, tn), lambda n, b, i: (b, i, n)),
        out_shape=jax.ShapeDtypeStruct((B, T, F), BF16),
        scratch_shapes=[pltpu.VMEM((K, tn), BF16), pltpu.VMEM((K, tn), BF16)],
        compiler_params=_params(("arbitrary",) * 3, nbytes),
        name="proj_swiglu",
    )(x, w_gu, w_gu)


def _resid_kernel(*refs, k_sizes):
    n_x = len(k_sizes)
    x_refs = refs[:n_x]
    w_ref, res_ref, gate_ref, o_ref, wb_ref = refs[n_x:]

    @pl.when(_first_inner_step())
    def _():
        wb_ref[...] = w_ref[0].astype(BF16)

    acc = None
    k0 = 0
    for x_ref, ks in zip(x_refs, k_sizes):
        part = jnp.dot(x_ref[0], wb_ref[k0:k0 + ks, :], preferred_element_type=F32)
        acc = part if acc is None else acc + part
        k0 += ks
    o_ref[0] = res_ref[0] + gate_ref[0] * acc


def _proj_resid(xs, w_stack, l, resid, gate, *, tm=512, tn=512):
    B, T, N = resid.shape
    k_sizes = tuple(x.shape[2] for x in xs)
    K = sum(k_sizes)
    tm = min(tm, T)
    assert T % tm == 0 and N % tn == 0
    if gate.shape[1] == 1:
        gate_spec = pl.BlockSpec((1, 1, tn), lambda n, b, i: (b, 0, n))
    else:
        gate_spec = pl.BlockSpec((1, tm, tn), lambda n, b, i: (b, i, n))
    in_specs = [pl.BlockSpec((1, tm, ks), lambda n, b, i: (b, i, 0)) for ks in k_sizes]
    in_specs += [pl.BlockSpec((1, K, tn), lambda n, b, i: (l, 0, n)),
                 pl.BlockSpec((1, tm, tn), lambda n, b, i: (b, i, n)),
                 gate_spec]
    nbytes = 2 * (tm * K * 2 + K * tn * 4 + 2 * tm * tn * 4) + K * tn * 2 + 2 * tm * tn * 4
    return pl.pallas_call(
        functools.partial(_resid_kernel, k_sizes=k_sizes),
        grid=(N // tn, B, T // tm),
        in_specs=in_specs,
        out_specs=pl.BlockSpec((1, tm, tn), lambda n, b, i: (b, i, n)),
        out_shape=jax.ShapeDtypeStruct((B, T, N), F32),
        scratch_shapes=[pltpu.VMEM((K, tn), BF16)],
        compiler_params=_params(("arbitrary",) * 3, nbytes),
        name="proj_resid",
    )(*xs, w_stack, resid, gate)


def _rel_bucket(dist):
    max_exact = NUM_BUCKETS // 2
    d = jnp.maximum(dist, 0)
    df = jnp.maximum(d, 1).astype(F32)
    large = max_exact + (jnp.log(df / max_exact) / math.log(MAX_DISTANCE / max_exact)
                         * (NUM_BUCKETS - max_exact)).astype(jnp.int32)
    return jnp.where(d < max_exact, d, jnp.minimum(large, NUM_BUCKETS - 1))


def _band_bias_rows(rel_bias):
    dsub = Q_BLOCK - jnp.arange(2 * Q_BLOCK)
    rows = []
    for (w, dil) in BRANCHES:
        valid = (dsub >= 0) & (dsub <= w // dil)
        bias = rel_bias[_rel_bucket(jnp.maximum(dsub, 0) * dil)].astype(F32).T
        rows.append(jnp.where(valid[None], bias, NEG))
    return jnp.stack(rows, axis=1)


def _attn_kernel(q_ref, k_ref, v_ref, base_ref, o_ref, qs, ks, vs, os_, ls):
    T = q_ref.shape[2]
    QB = Q_BLOCK
    low = lax.broadcasted_iota(jnp.int32, (1, LANES), 1) < HEAD_DIM
    own = (low, jnp.logical_not(low))
    prev_cols = lax.broadcasted_iota(jnp.int32, (1, 2 * QB), 1) < QB

    for bi, (_, dil) in enumerate(BRANCHES):
        L = T // dil
        ks[bi, 0:QB, :] = jnp.zeros((QB, LANES), BF16)
        vs[bi, 0:QB, :] = jnp.zeros((QB, LANES), BF16)
        for r in range(dil):
            rows = pl.ds(r, L, stride=dil)
            q = q_ref[0, 0, rows, :]
            for hh in range(2):
                qs[hh, bi, r * L:(r + 1) * L, :] = jnp.where(own[hh], q, 0.0).astype(BF16)
            ks[bi, QB + r * L:QB + (r + 1) * L, :] = k_ref[0, 0, rows, :].astype(BF16)
            vs[bi, QB + r * L:QB + (r + 1) * L, :] = v_ref[0, 0, rows, :].astype(BF16)

    for bi, (_, dil) in enumerate(BRANCHES):
        nb = T // dil // QB
        tiles = [pltpu.roll(jnp.broadcast_to(base_ref[hh, bi:bi + 1, :], (QB, 2 * QB)), 0, 1, stride=1, stride_axis=0)
                 for hh in range(2)]

        def block(g, _, bi=bi, dil=dil, nb=nb, tiles=tiles):
            r = g // nb
            n = g - r * nb
            at = pl.multiple_of(g * QB, QB)
            kw = ks[bi, pl.ds(at, 2 * QB), :]
            vw = vs[bi, pl.ds(at, 2 * QB), :]
            no_prev = jnp.logical_and(n == 0, prev_cols)
            o_pair, l_pair = None, None
            for hh in range(2):
                s = lax.dot_general(qs[hh, bi, pl.ds(at, QB), :], kw, (((1,), (1,)), ((), ())),
                                    preferred_element_type=F32) + tiles[hh]
                s = jnp.where(no_prev, NEG, s)
                m = jnp.max(s, axis=-1, keepdims=True)
                p = jnp.exp(s - m)
                den = jnp.sum(p, axis=-1, keepdims=True)
                o = jnp.dot(p.astype(BF16), vw, preferred_element_type=F32) / den
                lse = jnp.broadcast_to(m + jnp.log(den), (QB, LANES))
                o_pair = o if hh == 0 else jnp.where(low, o_pair, o)
                l_pair = lse if hh == 0 else jnp.where(low, l_pair, lse)
            tok = pl.ds(n * (QB * dil) + r, QB, stride=dil)
            os_[bi, tok, :] = o_pair
            ls[bi, tok, :] = l_pair
            return 0

        lax.fori_loop(0, T // QB, block, 0)

    def merge(i, _):
        rows = pl.ds(pl.multiple_of(i * QB, QB), QB)
        lses = [ls[bi, rows, :] for bi in range(len(BRANCHES))]
        m = functools.reduce(jnp.maximum, lses)
        ws = [jnp.exp(x - m) for x in lses]
        num = sum(w * os_[bi, rows, :] for bi, w in enumerate(ws))
        o_ref[0, rows, :] = (num / sum(ws)).astype(o_ref.dtype)
        return 0

    lax.fori_loop(0, T // QB, merge, 0)


def _attention_prompt(q, k_stack, v_stack, l, base):
    B, T, DA = q.shape
    nbr = len(BRANCHES)
    assert T % (Q_BLOCK * max(d for _, d in BRANCHES)) == 0
    qkv = pl.BlockSpec((1, 1, T, LANES), lambda p, b: (l, b, 0, p))
    scratch = [pltpu.VMEM((2, nbr, T, LANES), BF16), pltpu.VMEM((nbr, T + Q_BLOCK, LANES), BF16),
               pltpu.VMEM((nbr, T + Q_BLOCK, LANES), BF16), pltpu.VMEM((nbr, T, LANES), F32),
               pltpu.VMEM((nbr, T, LANES), F32)]
    nbytes = 2 * 3 * T * LANES * 4 + 3 * nbr * (T + Q_BLOCK) * LANES * 2 + 2 * nbr * T * LANES * 4 + 2 * T * LANES * 2
    return pl.pallas_call(
        _attn_kernel,
        grid=(DA // LANES, B),
        in_specs=[pl.BlockSpec((1, 1, T, LANES), lambda p, b: (0, b, 0, p)), qkv, qkv,
                  pl.BlockSpec((2, nbr, 2 * Q_BLOCK), lambda p, b: (p, 0, 0))],
        out_specs=pl.BlockSpec((1, T, LANES), lambda p, b: (b, 0, p)),
        out_shape=jax.ShapeDtypeStruct((B, T, DA), BF16),
        scratch_shapes=scratch,
        compiler_params=_params(("arbitrary", "arbitrary"), nbytes),
        name="attn_prompt",
    )(q[None], k_stack, v_stack, base)


def _sattn_kernel(q_ref, kn_ref, vn_ref, k1_ref, k2_ref, k3_ref, v1_ref, v2_ref, v3_ref, tab_ref, b0_ref, o_ref):
    H, E = q_ref.shape[1:]
    eye = (lax.broadcasted_iota(jnp.int32, (H, H), 0) == lax.broadcasted_iota(jnp.int32, (H, H), 1)).astype(F32)
    to_row = lambda col: jnp.sum(eye * col, axis=0, keepdims=True)
    to_col = lambda row: jnp.sum(eye * row, axis=1, keepdims=True)
    q = q_ref[0].astype(BF16)
    kn = kn_ref[0].astype(BF16).astype(F32)
    vn = vn_ref[0].astype(BF16).astype(F32)
    s_self = jnp.sum(q.astype(F32) * kn, axis=-1, keepdims=True) + b0_ref[...]
    blocks = []
    for bi, kr in enumerate((k1_ref, k2_ref, k3_ref)):
        kf = kr[0, 0, :, 0].reshape(-1, E)
        blocks.append(_dot_nt(kf, q) + tab_ref[bi])
    m_row = to_row(s_self)
    for s in blocks:
        m_row = jnp.maximum(m_row, jnp.max(s, axis=0, keepdims=True))
    m_col = to_col(m_row)
    n_br = len(blocks)
    p_self = jnp.exp(s_self - m_col) * float(n_br)
    den = to_row(p_self)
    acc = p_self * vn
    for s, vr in zip(blocks, (v1_ref, v2_ref, v3_ref)):
        p = jnp.exp(s - m_row)
        den = den + jnp.sum(p, axis=0, keepdims=True)
        acc = acc + _dot_tn(p, vr[0, 0, :, 0].reshape(-1, E))
    o_ref[0] = (acc / to_col(den)).astype(o_ref.dtype)


def _attention_sample(q, k_new, v_new, cache_k, cache_v, l, rel_bias):
    L, DB, W, H, E = cache_k.shape
    nk = Q_BLOCK
    assert W == W_MAX and all(W % (dil * nk) == 0 for _, dil in BRANCHES)
    m_of_row = nk - jnp.arange(nk)
    head_eq = jnp.arange(H)[:, None] == jnp.arange(H)[None, :]
    tabs = []
    for (_, dil) in BRANCHES:
        bias = rel_bias[_rel_bucket(dil * m_of_row)].astype(F32)
        tabs.append(jnp.where(head_eq[None], bias[:, None, :], NEG).reshape(nk * H, H))
    tab = jnp.stack(tabs)
    b0 = rel_bias[_rel_bucket(jnp.zeros((1,), jnp.int32))].astype(F32).T
    views, specs = [], []
    for cache in (cache_k, cache_v):
        for (_, dil) in BRANCHES:
            views.append(cache.reshape(L, DB, W // dil, dil, H, E))
            specs.append(pl.BlockSpec((1, 1, nk, 1, H, E), lambda b, dil=dil: (l, b, W // dil // nk - 1, 0, 0, 0)))
    vec = pl.BlockSpec((1, H, E), lambda b: (b, 0, 0))
    return pl.pallas_call(
        _sattn_kernel,
        grid=(DB,),
        in_specs=[vec, vec, vec] + specs + [pl.BlockSpec((len(BRANCHES), nk * H, H), lambda b: (0, 0, 0)),
                                            pl.BlockSpec((H, 1), lambda b: (0, 0))],
        out_specs=vec,
        out_shape=jax.ShapeDtypeStruct((DB, H, E), BF16),
        compiler_params=_params(("arbitrary",), 2 * 6 * nk * H * LANES * 4 + 2 * 3 * nk * H * LANES * 4),
        name="attn_sample",
    )(q, k_new, v_new, *views, tab, b0)


def _prep_kernel(*refs, shift_rows, has_vres):
    it = iter(refs)
    h_ref = next(it)
    if shift_rows:
        hp8_ref, hlast_ref = next(it), next(it)
    else:
        hprev_ref = next(it)
    rkv_ref = next(it)
    if shift_rows:
        rp8_ref, rlast_ref = next(it), next(it)
    else:
        rprev_ref = next(it)
    if has_vres:
        vfirst_ref = next(it)
    mu_ref, murkv_ref, w0_ref, dw1_ref, dw2_ref, a0_ref, aw1_ref, aw2_ref, gw1_ref, gw2_ref = (next(it) for _ in range(10))
    if has_vres:
        vmu_ref, v0_ref, vw1_ref, vw2_ref = (next(it) for _ in range(4))
    kk_ref, ka_ref = next(it), next(it)
    r_out, lw_out, k_out, v_out, kk_out, b_out, g_out = (next(it) for _ in range(7))

    h = h_ref[0]
    rkv0 = rkv_ref[0]
    tm = h.shape[0]
    if shift_rows:
        first = pl.program_id(1) == 0
        row0 = lax.broadcasted_iota(jnp.int32, (tm, 1), 0) == 0
        h_edge = jnp.where(first, hlast_ref[0], hp8_ref[0, SUBLANES - 1:SUBLANES, :])
        r_edge = jnp.where(first, rlast_ref[0], rp8_ref[0, SUBLANES - 1:SUBLANES, :])
        hprev = jnp.where(row0, h_edge, pltpu.roll(h, 1, 0))
        rprev = jnp.where(row0, r_edge, pltpu.roll(rkv0, 1, 0))
    else:
        hprev = hprev_ref[0]
        rprev = rprev_ref[0]

    dh = hprev - h
    mu = mu_ref[0]
    xw = h + dh * mu[0:1]
    xa = h + dh * mu[1:2]
    xg = h + dh * mu[2:3]

    z = w0_ref[0] + _dot(jnp.tanh(_dot(xw, dw1_ref[0])), dw2_ref[0])
    softplus = jnp.maximum(-z, 0.0) + jnp.log(1.0 + jnp.exp(-jnp.abs(z)))
    lw_out[0] = -jnp.exp(-softplus - 0.5)

    a = _sigmoid(a0_ref[0] + _dot(_dot(xa, aw1_ref[0]), aw2_ref[0]))
    g_out[0] = _dot(_sigmoid(_dot(xg, gw1_ref[0])), gw2_ref[0])

    murkv = murkv_ref[0]
    DR = kk_ref.shape[-1]
    r0, k0, v0 = (rkv0[:, j * DR:(j + 1) * DR] for j in range(3))
    rp, kp, vp = (rprev[:, j * DR:(j + 1) * DR] for j in range(3))
    r_out[0] = r0 + (rp - r0) * murkv[0:1]
    kr = k0 + (kp - k0) * murkv[1:2]
    vr = v0 + (vp - v0) * murkv[2:3]
    if has_vres:
        xv = h + dh * vmu_ref[0]
        vgate = _sigmoid(v0_ref[0] + _dot(_dot(xv, vw1_ref[0]), vw2_ref[0]))
        vr = vr + (vfirst_ref[0] - vr) * vgate
    v_out[0] = vr
    kk = kr * kk_ref[0]
    kkn = kk / jnp.maximum(jnp.sqrt(_head_sums(kk * kk)), 1e-12)
    kk_out[0] = kkn
    b_out[0] = kkn * a
    k_out[0] = kr * (1.0 + (a - 1.0) * ka_ref[0])


def _rwkv_prep(h, h_prev, rkv0, rkv_prev, v_first, P, l, *, tm=256):
    B, T, D = h.shape
    DR = rkv0.shape[2] // 3
    tm = min(tm, T)
    shift_rows = h_prev.shape[1] == 1 and T > 1
    has_vres = v_first is not None

    def tile(C):
        return pl.BlockSpec((1, tm, C), lambda b, i: (b, i, 0))

    def prev8(C):
        return pl.BlockSpec((1, SUBLANES, C), lambda b, i: (b, jnp.maximum(i * (tm // SUBLANES) - 1, 0), 0))

    def seq_row(C):
        return pl.BlockSpec((1, 1, C), lambda b, i: (b, 0, 0))

    def layer(shape, ll=l):
        return pl.BlockSpec((1,) + shape, lambda b, i: (ll,) + (0,) * len(shape))

    args, specs = [h], [tile(D)]
    if shift_rows:
        args += [h, h_prev]
        specs += [prev8(D), seq_row(D)]
    else:
        args += [jnp.broadcast_to(h_prev, h.shape)]
        specs += [tile(D)]
    args.append(rkv0)
    specs.append(tile(3 * DR))
    if shift_rows:
        args += [rkv0, rkv_prev]
        specs += [prev8(3 * DR), seq_row(3 * DR)]
    else:
        args += [jnp.broadcast_to(rkv_prev, rkv0.shape)]
        specs += [tile(3 * DR)]
    if has_vres:
        args.append(v_first)
        specs.append(tile(DR))
    r1 = lambda a: a.reshape(a.shape[0], 1, a.shape[-1])
    for name in ('mu_wag', 'mu_rkv'):
        args.append(P[name]); specs.append(layer(P[name].shape[1:]))
    args.append(r1(P['decay_w0'])); specs.append(layer((1, DR)))
    for name in ('decay_w1', 'decay_w2'):
        args.append(P[name]); specs.append(layer(P[name].shape[1:]))
    args.append(r1(P['aaa_a0'])); specs.append(layer((1, DR)))
    for name in ('aaa_w1', 'aaa_w2', 'gate_w1', 'gate_w2'):
        args.append(P[name]); specs.append(layer(P[name].shape[1:]))
    if has_vres:
        args.append(r1(P['vres_mu'])); specs.append(layer((1, D), l - 1))
        args.append(r1(P['vres_v0'])); specs.append(layer((1, DR), l - 1))
        for name in ('vres_w1', 'vres_w2'):
            args.append(P[name]); specs.append(layer(P[name].shape[1:], l - 1))
    args.append(r1(P['k_k'])); specs.append(layer((1, DR)))
    args.append(r1(P['k_a'])); specs.append(layer((1, DR)))
    nbytes = 2 * tm * 4 * (2 * D + 7 * DR + 7 * DR + DR) + 8 * tm * D * 4 + 4 * D * 512 * 4
    return pl.pallas_call(
        functools.partial(_prep_kernel, shift_rows=shift_rows, has_vres=has_vres),
        grid=(B, T // tm),
        in_specs=specs,
        out_specs=[tile(DR) for _ in range(7)],
        out_shape=[jax.ShapeDtypeStruct((B, T, DR), F32) for _ in range(7)],
        compiler_params=_params(("arbitrary", "arbitrary"), nbytes),
        name="rwkv_prep",
    )(*args)


def _prefix_sum_rows(x):
    n = x.shape[0]
    row = lax.broadcasted_iota(jnp.int32, (n, 1), 0)
    s = 1
    while s < n:
        x = x + jnp.where(row >= s, pltpu.roll(x, s, 0), 0.0)
        s *= 2
    return x


def _unit_lower_inverse(a_strict, blk):
    n = a_strict.shape[0]
    ti = lax.broadcasted_iota(jnp.int32, (n, n), 0)
    si = lax.broadcasted_iota(jnp.int32, (n, n), 1)

    def lower_left(s):
        return ((ti // (2 * s)) == (si // (2 * s))) & ((ti % (2 * s)) >= s) & ((si % (2 * s)) < s)

    d = (ti == si).astype(F32) + jnp.where(lower_left(1), a_strict, 0.0)
    s = 2
    while s < blk:
        d = d + _dot(_dot(d, jnp.where(lower_left(s), a_strict, 0.0)), d)
        s *= 2
    return d


def _pair_chunk_terms(r, lw, k, v, kk, b, low, tri):
    C = r.shape[0]
    cum = _prefix_sum_rows(lw)
    g_in = jnp.exp(cum)
    g_inv = jnp.exp(-cum)
    g_end = g_in[C - 1:C]
    a_t = -kk * jnp.exp(cum - lw)
    r_t = r * g_in
    b_t = b * g_inv
    k_t = k * g_inv
    bg = b_t * g_end
    kg = k_t * g_end
    zc = jnp.zeros((C, LANES), F32)
    h0 = lambda x: jnp.where(low, x, 0.0)
    h1 = lambda x: jnp.where(low, 0.0, x)
    v0, v1 = h0(v), h1(v)
    ar = jnp.concatenate([a_t, r_t], axis=0)
    m0 = jnp.where(tri, _dot_nt(h0(ar), jnp.concatenate([b_t, k_t], axis=0)), 0.0)
    m1 = jnp.where(tri, _dot_nt(h1(ar), jnp.concatenate([k_t, b_t], axis=0)), 0.0)
    top0, bot0, top1, bot1 = m0[:C], m0[C:], m1[:C], m1[C:]
    tinv = _unit_lower_inverse(jnp.concatenate([h0(top0), h1(top1)], axis=0), C)
    stack2 = lambda x0, x1: jnp.concatenate([jnp.concatenate([x0, zc], axis=0),
                                             jnp.concatenate([zc, x1], axis=0)], axis=1)
    akv = _dot(stack2(top0, top1), jnp.concatenate([zc, v0, v1, zc], axis=0))
    a_sw = pltpu.roll(a_t, HEAD_DIM, 1)
    x = _dot(tinv, akv + jnp.concatenate([h1(a_sw), h0(a_sw)], axis=0))
    z = jnp.concatenate([x[:C], v0, v1, x[C:]], axis=0)
    e = _dot(stack2(bot0, bot1), z)
    y0 = jnp.where(low, e[:C], e[C:])
    r_eff = r_t + pltpu.roll(jnp.where(low, e[C:], e[:C]), HEAD_DIM, 1)
    gh = _dot_tn(z, jnp.concatenate([h0(bg), h0(kg), h1(kg), h1(bg)], axis=0))
    h_mat = jnp.concatenate([h0(gh[:C]), h1(gh[C:])], axis=0)
    g_mat = jnp.concatenate([h0(gh[C:]), h1(gh[:C])], axis=0)
    return r_eff, y0, g_mat, h_mat, g_end


def _scan_kernel(r_ref, lw_ref, k_ref, v_ref, kk_ref, b_ref, g_ref, lng_ref, lnb_ref, rk_ref,
                 o_ref, s_ref, reff_s, y0_s, gm_s, hm_s, ge_s, *, chunk, unroll):
    C = chunk
    T = r_ref.shape[1]
    NC = T // C
    E = HEAD_DIM
    low = lax.broadcasted_iota(jnp.int32, (1, LANES), 1) < E
    ti = lax.broadcasted_iota(jnp.int32, (2 * C, 2 * C), 0)
    si = lax.broadcasted_iota(jnp.int32, (2 * C, 2 * C), 1)
    tri = (si % C) <= jnp.where(ti < C, ti - 1, ti - C)

    def phase1(i, _):
        for u in range(unroll):
            c = i * unroll + u
            rows = pl.ds(pl.multiple_of(c * C, C), C)
            args = (ref[0, rows, :] for ref in (r_ref, lw_ref, k_ref, v_ref, kk_ref, b_ref))
            r_eff, y0, g_mat, h_mat, g_end = _pair_chunk_terms(*args, low, tri)
            reff_s[rows, :] = r_eff
            y0_s[rows, :] = y0
            gm_s[c] = g_mat
            hm_s[c] = h_mat
            ge_s[c] = jnp.broadcast_to(g_end, (SUBLANES, LANES))
        return 0

    lax.fori_loop(0, NC // unroll, phase1, 0)

    def phase2(c, S):
        rows = pl.ds(pl.multiple_of(c * C, C), C)
        r, k, v, g = (ref[0, rows, :] for ref in (r_ref, k_ref, v_ref, g_ref))
        y = _dot_nt(reff_s[rows, :], S) + y0_s[rows, :]
        S_new = S * ge_s[c][0:1] + _dot(S, gm_s[c]) + hm_s[c]
        mu = _head_sums(y) * (1.0 / E)
        yc = y - mu
        var = _head_sums(yc * yc) * (1.0 / E)
        yn = yc * lax.rsqrt(var + GN_EPS) * lng_ref[0] + lnb_ref[0]
        bonus = _head_sums(r * k * rk_ref[0])
        o_ref[0, rows, :] = ((yn + bonus * v) * g).astype(o_ref.dtype)
        return S_new

    S = lax.fori_loop(0, NC, phase2, jnp.zeros((LANES, LANES), F32))
    s_ref[0, 0] = S[:E, :E]
    s_ref[0, 1] = S[E:, E:]


def _rwkv_scan(r, lw, k, v, kk, b, g, lnx_g, lnx_b, r_k, l):
    B, T, DR = r.shape
    H = DR // HEAD_DIM
    L = lnx_g.shape[0]
    C = min(SCAN_CHUNK, T)
    nc = T // C
    unroll = math.gcd(SCAN_UNROLL, nc)
    assert T % C == 0
    seq = pl.BlockSpec((1, T, LANES), lambda p, b_: (b_, 0, p))
    par = pl.BlockSpec((1, 1, LANES), lambda p, b_: (l, 0, p))
    scratch = [pltpu.VMEM((T, LANES), F32), pltpu.VMEM((T, LANES), F32),
               pltpu.VMEM((nc, LANES, LANES), F32), pltpu.VMEM((nc, LANES, LANES), F32),
               pltpu.VMEM((nc, SUBLANES, LANES), F32)]
    nbytes = 2 * 8 * T * LANES * 4 + 2 * T * LANES * 4 + 2 * nc * LANES * LANES * 4 + nc * SUBLANES * LANES * 4
    return pl.pallas_call(
        functools.partial(_scan_kernel, chunk=C, unroll=unroll),
        grid=(DR // LANES, B),
        in_specs=[seq] * 7 + [par] * 3,
        out_specs=[seq, pl.BlockSpec((1, 2, HEAD_DIM, HEAD_DIM), lambda p, b_: (b_, p, 0, 0))],
        out_shape=[jax.ShapeDtypeStruct((B, T, DR), BF16), jax.ShapeDtypeStruct((B, H, HEAD_DIM, HEAD_DIM), F32)],
        scratch_shapes=scratch,
        compiler_params=_params(("arbitrary", "arbitrary"), nbytes),
        name="rwkv_scan",
    )(r, lw, k, v, kk, b, g, lnx_g.reshape(L, 1, DR), lnx_b.reshape(L, 1, DR), r_k.reshape(L, 1, DR))


def _step_kernel(s_ref, r_ref, lw_ref, k_ref, v_ref, kk_ref, b_ref, g_ref, lng_ref, lnb_ref, rk_ref,
                 o_ref, so_ref, *, n_heads):
    E = HEAD_DIM
    eye = (lax.broadcasted_iota(jnp.int32, (E, E), 0) == lax.broadcasted_iota(jnp.int32, (E, E), 1)).astype(F32)
    outs = []
    for hh in range(n_heads):
        sl = slice(hh * E, (hh + 1) * E)
        S = s_ref[0, hh]
        r, k, v, kk, b, g = (ref[0, :, sl] for ref in (r_ref, k_ref, v_ref, kk_ref, b_ref, g_ref))
        w = jnp.exp(lw_ref[0, :, sl])
        v_col = jnp.sum(eye * v, axis=-1, keepdims=True)
        sa = jnp.sum(S * (-kk), axis=-1, keepdims=True)
        S = S * w + sa * b + v_col * k
        so_ref[0, hh] = S
        y_col = jnp.sum(S * r, axis=-1, keepdims=True)
        y = jnp.sum(eye * y_col, axis=0, keepdims=True)
        mu = jnp.mean(y, axis=-1, keepdims=True)
        yc = y - mu
        var = jnp.mean(yc * yc, axis=-1, keepdims=True)
        yn = yc * lax.rsqrt(var + GN_EPS) * lng_ref[0, :, sl] + lnb_ref[0, :, sl]
        bonus = jnp.sum(r * k * rk_ref[0, :, sl], axis=-1, keepdims=True)
        outs.append((yn + bonus * v) * g)
    o_ref[0] = jnp.concatenate(outs, axis=-1).astype(o_ref.dtype)


def _rwkv_step(state, r, lw, k, v, kk, b, g, lnx_g, lnx_b, r_k, l):
    DB, H = state.shape[:2]
    DR = H * HEAD_DIM
    L = lnx_g.shape[0]
    vec = pl.BlockSpec((1, 1, DR), lambda b_: (b_, 0, 0))
    par = pl.BlockSpec((1, 1, DR), lambda b_: (l, 0, 0))
    st = pl.BlockSpec((1, H, HEAD_DIM, HEAD_DIM), lambda b_: (b_, 0, 0, 0))
    return pl.pallas_call(
        functools.partial(_step_kernel, n_heads=H),
        grid=(DB,),
        in_specs=[st] + [vec] * 7 + [par] * 3,
        out_specs=[vec, st],
        out_shape=[jax.ShapeDtypeStruct((DB, 1, DR), BF16), jax.ShapeDtypeStruct(state.shape, F32)],
        compiler_params=_params(("arbitrary",), 4 * H * HEAD_DIM * LANES * 4),
        name="rwkv_step",
    )(state, r, lw, k, v, kk, b, g, lnx_g.reshape(L, 1, DR), lnx_b.reshape(L, 1, DR), r_k.reshape(L, 1, DR))


def _mixer_tail(P, l, x, att, rw, mods):
    x = _proj_resid([att, rw], P['w_out'], l, x, mods[2])
    (h2,) = _norm_mod(x, P['norm2_g'], l, mods[3], mods[4], [BF16])
    act = _proj_swiglu(h2, P['w_gu'], l)
    return _proj_resid([act], P['w_down'], l, x, mods[5])


def _prompt_layer(P, l, x, mods, base, v_first, k_stack, v_stack):
    B, T, D = x.shape
    L = P['w_in'].shape[0]
    DA = P['rel_bias'].shape[1] * HEAD_DIM
    h, hb = _norm_mod(x, P['norm1_g'], l, mods[0], mods[1], [F32, BF16])
    q = _proj(hb, P['w_in'], l, 0, DA, head_g=P['q_norm_g'][l], scale=ATT_SCALE)
    k_stack = _proj(hb, P['w_in'], l, DA, DA, head_g=P['k_norm_g'][l], stack=(L, k_stack))
    v_stack = _proj(hb, P['w_in'], l, 2 * DA, DA, stack=(L, v_stack))
    rkv0 = _proj(hb, P['w_in'], l, 3 * DA, P['w_in'].shape[2] - 3 * DA)
    att = _attention_prompt(q, k_stack, v_stack, l, base)
    zero_h = jnp.zeros((B, 1, D), F32)
    zero_r = jnp.zeros((B, 1, rkv0.shape[2]), F32)
    r, lw, kr, vr, kk, b, g = _rwkv_prep(h, zero_h, rkv0, zero_r, v_first, P, l)
    if v_first is None:
        v_first = vr
    rw, state = _rwkv_scan(r, lw, kr, vr, kk, b, g, P['lnx_g'], P['lnx_b'], P['r_k'], l)
    x = _mixer_tail(P, l, x, att, rw, mods)
    return x, v_first, k_stack, v_stack, state, h[:, -1]


def _sample_layer(P, l, x, mods, cache_k, cache_v, state, h_last, v_first):
    _, DB, D = x.shape
    H = P['rel_bias'].shape[1]
    DA = H * HEAD_DIM
    h, hb = _norm_mod(x, P['norm1_g'], l, mods[0], mods[1], [F32, BF16])
    q = _proj(hb, P['w_in'], l, 0, DA, head_g=P['q_norm_g'][l], scale=ATT_SCALE)
    k = _proj(hb, P['w_in'], l, DA, DA, head_g=P['k_norm_g'][l])
    v = _proj(hb, P['w_in'], l, 2 * DA, DA)
    both = jnp.concatenate([hb, h_last[None].astype(BF16)], axis=1)
    rkv2 = _proj(both, P['w_in'], l, 3 * DA, P['w_in'].shape[2] - 3 * DA)
    rkv0, rkv_prev = rkv2[:, :DB], rkv2[:, DB:]
    heads = lambda t: t.reshape(DB, H, HEAD_DIM)
    att = _attention_sample(heads(q), heads(k), heads(v), cache_k, cache_v, l, P['rel_bias'])
    r, lw, kr, vr, kk, b, g = _rwkv_prep(h, h_last[None], rkv0, rkv_prev, v_first, P, l)
    if v_first is None:
        v_first = vr
    per_seq = lambda t: t.reshape(DB, 1, t.shape[-1])
    rw, new_state = _rwkv_step(state, *(per_seq(t) for t in (r, lw, kr, vr, kk, b, g)),
                               P['lnx_g'], P['lnx_b'], P['r_k'], l)
    x = _mixer_tail(P, l, x, att.reshape(1, DB, DA), rw.reshape(1, DB, DA), mods)
    return x, v_first, k[0], v[0], new_state, h[0]


def kernel(x_prompt, x_sample, c_prompt, c_sample, cache_k, cache_v, state_wkv, state_shift, rel_bias, ada_w, ada_b, norm1_g, norm2_g, w_in, q_norm_g, k_norm_g, mu_wag, mu_rkv, decay_w0, decay_w1, decay_w2, aaa_a0, aaa_w1, aaa_w2, gate_w1, gate_w2, vres_mu, vres_v0, vres_w1, vres_w2, k_k, k_a, r_k, lnx_g, lnx_b, w_out, w_gu, w_down):
    P = dict(rel_bias=rel_bias, norm1_g=norm1_g, norm2_g=norm2_g, w_in=w_in, q_norm_g=q_norm_g,
             k_norm_g=k_norm_g, mu_wag=mu_wag, mu_rkv=mu_rkv, decay_w0=decay_w0, decay_w1=decay_w1,
             decay_w2=decay_w2, aaa_a0=aaa_a0, aaa_w1=aaa_w1, aaa_w2=aaa_w2, gate_w1=gate_w1,
             gate_w2=gate_w2, vres_mu=vres_mu, vres_v0=vres_v0, vres_w1=vres_w1, vres_w2=vres_w2,
             k_k=k_k, k_a=k_a, r_k=r_k.reshape(r_k.shape[0], -1), lnx_g=lnx_g, lnx_b=lnx_b,
             w_out=w_out, w_gu=w_gu, w_down=w_down)
    L = ada_w.shape[0]
    B, T, D = x_prompt.shape
    DB = x_sample.shape[0]
    H_ATT = rel_bias.shape[1]
    assert x_sample.shape[1] == 1 and T <= W_MAX

    rows = -(-(B + DB) // SUBLANES) * SUBLANES
    c_all = jnp.concatenate([c_prompt, c_sample, jnp.zeros((rows - B - DB, D), F32)], axis=0)
    mod = _ada(c_all, ada_w, ada_b).reshape(L, rows, 6, D)
    base = _band_bias_rows(rel_bias)

    xp, vf, k_stack, v_stack = x_prompt, None, None, None
    ps, ph = [], []
    for l in range(L):
        mods = [mod[l, :B, j][:, None, :] for j in range(6)]
        xp, vf, k_stack, v_stack, ns, nh = _prompt_layer(P, l, xp, mods, base, vf, k_stack, v_stack)
        ps.append(ns)
        ph.append(nh)

    xs, vf = x_sample.reshape(1, DB, D), None
    sk, sv, ss, sh = [], [], [], []
    for l in range(L):
        mods = [mod[l, B:B + DB, j][None] for j in range(6)]
        xs, vf, nk, nv, ns, nh = _sample_layer(P, l, xs, mods, cache_k, cache_v, state_wkv[l], state_shift[l], vf)
        sk.append(nk.reshape(DB, 1, H_ATT, HEAD_DIM))
        sv.append(nv.reshape(DB, 1, H_ATT, HEAD_DIM))
        ss.append(ns)
        sh.append(nh)

    return (xp, xs.reshape(DB, 1, D), k_stack.reshape(L, B, T, H_ATT, HEAD_DIM),
            v_stack.reshape(L, B, T, H_ATT, HEAD_DIM), jnp.stack(ps), jnp.stack(ph),
            jnp.stack(sk), jnp.stack(sv), jnp.stack(ss), jnp.stack(sh))
```

```python
import functools
import math

import jax
import jax.numpy as jnp
from jax import lax
from jax.experimental import pallas as pl
from jax.experimental.pallas import tpu as pltpu

F32 = jnp.float32
BF16 = jnp.bfloat16

HEAD_DIM = 64
BRANCHES = ((128, 1), (512, 4), (2048, 16))
W_MAX = 2048
NUM_BUCKETS = 32
MAX_DISTANCE = W_MAX
ATT_SCALE = HEAD_DIM ** -0.5
RMS_EPS = 1e-6
GN_EPS = 64e-5
NEG = -1e30

LANES = 128
SUBLANES = 8
Q_BLOCK = 128
SCAN_CHUNK = 64
SCAN_SEQS = 2
SCAN_UNROLL = 16
ATT_UNROLL = 4
VMEM_CAP = 56 * 1024 * 1024

assert all(w // dil == Q_BLOCK for w, dil in BRANCHES) and 2 * HEAD_DIM == LANES


def _vmem(nbytes):
    return int(min(VMEM_CAP, nbytes * 1.3 + (6 << 20)))


def _params(sem, nbytes):
    return pltpu.CompilerParams(dimension_semantics=sem, vmem_limit_bytes=_vmem(nbytes))


def _dot(a, b):
    return jnp.dot(a.astype(BF16), b.astype(BF16), preferred_element_type=F32)


def _dot_nt(a, b):
    return lax.dot_general(a.astype(BF16), b.astype(BF16), (((1,), (1,)), ((), ())), preferred_element_type=F32)


def _dot_tn(a, b):
    return lax.dot_general(a.astype(BF16), b.astype(BF16), (((0,), (0,)), ((), ())), preferred_element_type=F32)


def _sigmoid(x):
    return 1.0 / (1.0 + jnp.exp(-x))


def _head_sums(x):
    r = lax.broadcasted_iota(jnp.int32, (LANES, LANES), 0) // HEAD_DIM
    c = lax.broadcasted_iota(jnp.int32, (LANES, LANES), 1) // HEAD_DIM
    bd = (r == c).astype(BF16)
    cols = [_dot(x[:, j * LANES:(j + 1) * LANES], bd) for j in range(x.shape[1] // LANES)]
    return cols[0] if len(cols) == 1 else jnp.concatenate(cols, axis=-1)


def _ada_kernel(c_ref, w_ref, b_ref, o_ref):
    c = c_ref[...]
    s = c * _sigmoid(c)
    o_ref[0] = _dot(s, w_ref[0]) + b_ref[0]


def _ada(c_all, ada_w, ada_b, tn=1024):
    L, D, N = ada_w.shape
    R = c_all.shape[0]
    return pl.pallas_call(
        _ada_kernel,
        grid=(L, N // tn),
        in_specs=[pl.BlockSpec((R, D), lambda l, j: (0, 0)),
                  pl.BlockSpec((1, D, tn), lambda l, j: (l, 0, j)),
                  pl.BlockSpec((1, 1, tn), lambda l, j: (l, 0, j))],
        out_specs=pl.BlockSpec((1, R, tn), lambda l, j: (l, 0, j)),
        out_shape=jax.ShapeDtypeStruct((L, R, N), F32),
        compiler_params=_params(("arbitrary", "arbitrary"), 2 * D * tn * 4 + D * tn * 2),
        name="ada_mod",
    )(c_all, ada_w, ada_b.reshape(L, 1, N))


def _norm_kernel(x_ref, g_ref, sh_ref, sc_ref, *out_refs):
    x = x_ref[0]
    y = x * lax.rsqrt(jnp.mean(x * x, axis=-1, keepdims=True) + RMS_EPS) * g_ref[0]
    h = y * (1.0 + sc_ref[0]) + sh_ref[0]
    for o in out_refs:
        o[0] = h.astype(o.dtype)


def _row_spec(arr, tm, T):
    C = arr.shape[-1]
    if arr.shape[1] == 1:
        return pl.BlockSpec((1, 1, C), lambda b, i: (b, 0, 0))
    assert arr.shape[1] == T
    return pl.BlockSpec((1, tm, C), lambda b, i: (b, i, 0))


def _norm_mod(x, g_stack, l, shift, scale, out_dtypes, tm=256):
    B, T, D = x.shape
    tm = min(tm, T)
    L = g_stack.shape[0]
    outs = pl.pallas_call(
        _norm_kernel,
        grid=(B, T // tm),
        in_specs=[pl.BlockSpec((1, tm, D), lambda b, i: (b, i, 0)),
                  pl.BlockSpec((1, 1, D), lambda b, i: (l, 0, 0)),
                  _row_spec(shift, tm, T), _row_spec(scale, tm, T)],
        out_specs=[pl.BlockSpec((1, tm, D), lambda b, i: (b, i, 0)) for _ in out_dtypes],
        out_shape=[jax.ShapeDtypeStruct((B, T, D), dt) for dt in out_dtypes],
        compiler_params=_params(("arbitrary", "arbitrary"), 2 * tm * D * 4 * (2 + len(out_dtypes))),
        name="norm_mod",
    )(x, g_stack.reshape(L, 1, D), shift, scale)
    return outs


def _first_inner_step():
    return (pl.program_id(1) == 0) & (pl.program_id(2) == 0)


def _head_rmsnorm(acc, g):
    return acc * lax.rsqrt(_head_sums(acc * acc) * (1.0 / HEAD_DIM) + RMS_EPS) * g


def _mm_kernel(*refs, headnorm, scale, aliased):
    it = iter(refs)
    x_ref, w_ref = next(it), next(it)
    g_ref = next(it) if headnorm else None
    if aliased:
        next(it)
    o_ref, wb_ref = next(it), next(it)

    @pl.when(_first_inner_step())
    def _():
        wb_ref[...] = w_ref[0].astype(BF16)

    acc = jnp.dot(x_ref[0], wb_ref[...], preferred_element_type=F32)
    if headnorm:
        acc = _head_rmsnorm(acc, g_ref[0])
        if scale != 1.0:
            acc = acc * scale
    if len(o_ref.shape) == 4:
        o_ref[0, 0] = acc
    else:
        o_ref[0] = acc


def _proj(x, w_stack, l, col0, ncols, *, head_g=None, scale=1.0, stack=None, tm=1024, tn=1024):
    B, T, K = x.shape
    tm = min(tm, T)
    tn = min(tn, ncols)
    assert T % tm == 0 and ncols % tn == 0 and col0 % tn == 0
    cb = col0 // tn
    in_specs = [pl.BlockSpec((1, tm, K), lambda n, b, i: (b, i, 0)),
                pl.BlockSpec((1, K, tn), lambda n, b, i: (l, 0, cb + n))]
    args = [x, w_stack]
    if head_g is not None:
        in_specs.append(pl.BlockSpec((1, 1, tn), lambda n, b, i: (0, 0, 0)))
        args.append(jnp.tile(head_g, tn // HEAD_DIM).reshape(1, 1, tn))
    aliases = {}
    if stack is None:
        out_spec = pl.BlockSpec((1, tm, tn), lambda n, b, i: (b, i, n))
        out_shape = jax.ShapeDtypeStruct((B, T, ncols), F32)
    else:
        out_spec = pl.BlockSpec((1, 1, tm, tn), lambda n, b, i: (l, b, i, n))
        out_shape = jax.ShapeDtypeStruct(stack.shape, F32)
        in_specs.append(pl.BlockSpec(memory_space=pl.ANY))
        args.append(stack)
        aliases = {len(args) - 1: 0}
    nbytes = 2 * (tm * K * 2 + K * tn * 4 + tm * tn * 4) + K * tn * 2 + 2 * tm * tn * 4
    return pl.pallas_call(
        functools.partial(_mm_kernel, headnorm=head_g is not None, scale=scale, aliased=bool(aliases)),
        grid=(ncols // tn, B, T // tm),
        in_specs=in_specs,
        out_specs=out_spec,
        out_shape=out_shape,
        scratch_shapes=[pltpu.VMEM((K, tn), BF16)],
        input_output_aliases=aliases,
        compiler_params=_params(("arbitrary",) * 3, nbytes),
        name="proj",
    )(*args)


def _swiglu_kernel(x_ref, wg_ref, wu_ref, o_ref, wgb_ref, wub_ref):
    @pl.when(_first_inner_step())
    def _():
        wgb_ref[...] = wg_ref[0].astype(BF16)
        wub_ref[...] = wu_ref[0].astype(BF16)

    x = x_ref[0]
    gate = jnp.dot(x, wgb_ref[...], preferred_element_type=F32)
    up = jnp.dot(x, wub_ref[...], preferred_element_type=F32)
    o_ref[0] = (gate * _sigmoid(gate) * up).astype(o_ref.dtype)


def _proj_swiglu(x, w_gu, l, *, tm=1024, tn=512):
    B, T, K = x.shape
    F = w_gu.shape[2] // 2
    tm = min(tm, T)
    assert T % tm == 0 and F % tn == 0
    nb = F // tn
    nbytes = 2 * (tm * K * 2 + 2 * K * tn * 4 + tm * tn * 2) + 2 * K * tn * 2 + 3 * tm * tn * 4
    return pl.pallas_call(
        _swiglu_kernel,
        grid=(nb, B, T // tm),
        in_specs=[pl.BlockSpec((1, tm, K), lambda n, b, i: (b, i, 0)),
                  pl.BlockSpec((1, K, tn), lambda n, b, i: (l, 0, n)),
                  pl.BlockSpec((1, K, tn), lambda n, b, i: (l, 0, nb + n))],
        out_specs=pl.BlockSpec((1, tm, tn), lambda n, b, i: (b, i, n)),
        out_shape=jax.ShapeDtypeStruct((B, T, F), BF16),
        scratch_shapes=[pltpu.VMEM((K, tn), BF16), pltpu.VMEM((K, tn), BF16)],
        compiler_params=_params(("arbitrary",) * 3, nbytes),
        name="proj_swiglu",
    )(x, w_gu, w_gu)


def _resid_kernel(*refs, k_sizes):
    n_x = len(k_sizes)
    x_refs = refs[:n_x]
    w_ref, res_ref, gate_ref, o_ref, wb_ref = refs[n_x:]

    @pl.when(_first_inner_step())
    def _():
        wb_ref[...] = w_ref[0].astype(BF16)

    acc = None
    k0 = 0
    for x_ref, ks in zip(x_refs, k_sizes):
        part = jnp.dot(x_ref[0], wb_ref[k0:k0 + ks, :], preferred_element_type=F32)
        acc = part if acc is None else acc + part
        k0 += ks
    o_ref[0] = res_ref[0] + gate_ref[0] * acc


def _proj_resid(xs, w_stack, l, resid, gate, *, tm=512, tn=512):
    B, T, N = resid.shape
    k_sizes = tuple(x.shape[2] for x in xs)
    K = sum(k_sizes)
    tm = min(tm, T)
    assert T % tm == 0 and N % tn == 0
    if gate.shape[1] == 1:
        gate_spec = pl.BlockSpec((1, 1, tn), lambda n, b, i: (b, 0, n))
    else:
        gate_spec = pl.BlockSpec((1, tm, tn), lambda n, b, i: (b, i, n))
    in_specs = [pl.BlockSpec((1, tm, ks), lambda n, b, i: (b, i, 0)) for ks in k_sizes]
    in_specs += [pl.BlockSpec((1, K, tn), lambda n, b, i: (l, 0, n)),
                 pl.BlockSpec((1, tm, tn), lambda n, b, i: (b, i, n)),
                 gate_spec]
    nbytes = 2 * (tm * K * 2 + K * tn * 4 + 2 * tm * tn * 4) + K * tn * 2 + 2 * tm * tn * 4
    return pl.pallas_call(
        functools.partial(_resid_kernel, k_sizes=k_sizes),
        grid=(N // tn, B, T // tm),
        in_specs=in_specs,
        out_specs=pl.BlockSpec((1, tm, tn), lambda n, b, i: (b, i, n)),
        out_shape=jax.ShapeDtypeStruct((B, T, N), F32),
        scratch_shapes=[pltpu.VMEM((K, tn), BF16)],
        compiler_params=_params(("arbitrary",) * 3, nbytes),
        name="proj_resid",
    )(*xs, w_stack, resid, gate)


def _rel_bucket(dist):
    max_exact = NUM_BUCKETS // 2
    d = jnp.maximum(dist, 0)
    df = jnp.maximum(d, 1).astype(F32)
    large = max_exact + (jnp.log(df / max_exact) / math.log(MAX_DISTANCE / max_exact)
                         * (NUM_BUCKETS - max_exact)).astype(jnp.int32)
    return jnp.where(d < max_exact, d, jnp.minimum(large, NUM_BUCKETS - 1))


def _band_bias_rows(rel_bias):
    dsub = Q_BLOCK - jnp.arange(2 * Q_BLOCK)
    rows = []
    for (w, dil) in BRANCHES:
        valid = (dsub >= 0) & (dsub <= w // dil)
        bias = rel_bias[_rel_bucket(jnp.maximum(dsub, 0) * dil)].astype(F32).T
        rows.append(jnp.where(valid[None], bias, NEG))
    return jnp.stack(rows, axis=1)


def _attn_kernel(q_ref, k_ref, v_ref, base_ref, o_ref, qs, ks, vs, os_, ls, stage):
    T = q_ref.shape[2]
    QB = Q_BLOCK
    low = lax.broadcasted_iota(jnp.int32, (1, LANES), 1) < HEAD_DIM
    own = (low, jnp.logical_not(low))
    prev_cols = lax.broadcasted_iota(jnp.int32, (1, 2 * QB), 1) < QB

    assert len(BRANCHES) == 3 and BRANCHES[0][1] == 1
    prev_dil = 1
    for bi, (_, dil) in enumerate(BRANCHES):
        L, Lp, ratio = T // dil, T // prev_dil, dil // prev_dil
        ks[bi, 0:QB, :] = jnp.zeros((QB, LANES), BF16)
        vs[bi, 0:QB, :] = jnp.zeros((QB, LANES), BF16)
        for r in range(dil):
            rows = pl.ds((r % prev_dil) * Lp + r // prev_dil, L, stride=ratio)
            if bi < 2:
                q, k, v = (ref[0, 0, rows, :] for ref in (q_ref, k_ref, v_ref))
            else:
                q, k, v = (stage[a, rows, :] for a in range(3))
            if bi == 1:
                for a, x in enumerate((q, k, v)):
                    stage[a, r * L:(r + 1) * L, :] = x
            for hh in range(2):
                qs[hh, bi, r * L:(r + 1) * L, :] = jnp.where(own[hh], q, 0.0).astype(BF16)
            ks[bi, QB + r * L:QB + (r + 1) * L, :] = k.astype(BF16)
            vs[bi, QB + r * L:QB + (r + 1) * L, :] = v.astype(BF16)
        prev_dil = dil

    for bi, (_, dil) in enumerate(BRANCHES):
        nb = T // dil // QB
        tiles = [pltpu.roll(jnp.broadcast_to(base_ref[hh, bi:bi + 1, :], (QB, 2 * QB)), 0, 1, stride=1, stride_axis=0)
                 for hh in range(2)]

        def one_block(g, q0, q1, kw, vw, nb=nb, tiles=tiles):
            no_prev = jnp.logical_and(g % nb == 0, prev_cols)
            ss = [jnp.where(no_prev, NEG, lax.dot_general(q, kw, (((1,), (1,)), ((), ())),
                                                          preferred_element_type=F32) + tiles[hh])
                  for hh, q in enumerate((q0, q1))]
            yield
            ms = [jnp.max(s, axis=-1, keepdims=True) for s in ss]
            ps = [jnp.exp(s - m) for s, m in zip(ss, ms)]
            dens = [jnp.sum(p, axis=-1, keepdims=True) for p in ps]
            os2 = [jnp.dot(p.astype(BF16), vw, preferred_element_type=F32) for p in ps]
            yield
            o_pair = jnp.where(low, os2[0] / dens[0], os2[1] / dens[1])
            l_pair = jnp.where(low, jnp.broadcast_to(ms[0] + jnp.log(dens[0]), (QB, LANES)),
                               jnp.broadcast_to(ms[1] + jnp.log(dens[1]), (QB, LANES)))
            return o_pair, l_pair

        def blocks(i, _, bi=bi, dil=dil, nb=nb):
            gs = [i * ATT_UNROLL + u for u in range(ATT_UNROLL)]
            ats = [pl.multiple_of(g * QB, QB) for g in gs]
            loaded = [(qs[0, bi, pl.ds(at, QB), :], qs[1, bi, pl.ds(at, QB), :],
                       ks[bi, pl.ds(at, 2 * QB), :], vs[bi, pl.ds(at, 2 * QB), :]) for at in ats]
            results = _round_robin([one_block(g, *ld) for g, ld in zip(gs, loaded)])
            for g, (o_pair, l_pair) in zip(gs, results):
                r = g // nb
                tok = pl.ds((g - r * nb) * (QB * dil) + r, QB, stride=dil)
                os_[bi, tok, :] = o_pair
                ls[bi, tok, :] = l_pair
            return 0

        lax.fori_loop(0, T // QB // ATT_UNROLL, blocks, 0)

    def merge(i, _):
        rows = pl.ds(pl.multiple_of(i * QB, QB), QB)
        lses = [ls[bi, rows, :] for bi in range(len(BRANCHES))]
        m = functools.reduce(jnp.maximum, lses)
        ws = [jnp.exp(x - m) for x in lses]
        num = sum(w * os_[bi, rows, :] for bi, w in enumerate(ws))
        o_ref[0, rows, :] = (num / sum(ws)).astype(o_ref.dtype)
        return 0

    lax.fori_loop(0, T // QB, merge, 0)


def _attention_prompt(q, k_stack, v_stack, l, base):
    B, T, DA = q.shape
    nbr = len(BRANCHES)
    assert T % (Q_BLOCK * max(d for _, d in BRANCHES)) == 0
    qkv = pl.BlockSpec((1, 1, T, LANES), lambda p, b: (l, b, 0, p))
    scratch = [pltpu.VMEM((2, nbr, T, LANES), BF16), pltpu.VMEM((nbr, T + Q_BLOCK, LANES), BF16),
               pltpu.VMEM((nbr, T + Q_BLOCK, LANES), BF16), pltpu.VMEM((nbr, T, LANES), F32),
               pltpu.VMEM((nbr, T, LANES), F32), pltpu.VMEM((3, T, LANES), F32)]
    nbytes = (2 * 3 * T * LANES * 4 + 4 * nbr * (T + Q_BLOCK) * LANES * 2 + 2 * nbr * T * LANES * 4
              + 3 * T * LANES * 4 + 2 * T * LANES * 2)
    return pl.pallas_call(
        _attn_kernel,
        grid=(DA // LANES, B),
        in_specs=[pl.BlockSpec((1, 1, T, LANES), lambda p, b: (0, b, 0, p)), qkv, qkv,
                  pl.BlockSpec((2, nbr, 2 * Q_BLOCK), lambda p, b: (p, 0, 0))],
        out_specs=pl.BlockSpec((1, T, LANES), lambda p, b: (b, 0, p)),
        out_shape=jax.ShapeDtypeStruct((B, T, DA), BF16),
        scratch_shapes=scratch,
        compiler_params=_params(("arbitrary", "arbitrary"), nbytes),
        name="attn_prompt",
    )(q[None], k_stack, v_stack, base)


def _distance_logits(rel_bias, nd):
    d = jnp.arange(nd)
    mult = sum(((d % dil == 0) & (d // dil <= w // dil)).astype(F32) for (w, dil) in BRANCHES)
    bias = rel_bias[_rel_bucket(d)].astype(F32).T
    return jnp.where(mult > 0, bias + jnp.log(jnp.maximum(mult, 1.0)), NEG)


def _sattn_kernel(q_ref, kn_ref, vn_ref, kt_ref, vt_ref, tab_ref, o_ref):
    row = lax.broadcasted_iota(jnp.int32, (SUBLANES, LANES), 0)
    low = lax.broadcasted_iota(jnp.int32, (SUBLANES, LANES), 1) < HEAD_DIM
    own = ((row == 0) & low) | ((row == 1) & jnp.logical_not(low))
    q = jnp.where(own, q_ref[0], 0.0).astype(BF16)
    kn = kn_ref[0].astype(BF16).astype(F32)
    vn = vn_ref[0].astype(BF16).astype(F32)
    tab = tab_ref[0]
    W = kt_ref.shape[3]
    s_self = jnp.sum(q.astype(F32) * kn, axis=-1, keepdims=True) + tab[:, W:W + 1]
    s = _dot(q, kt_ref[0, 0]) + tab[:, :W]
    m = jnp.maximum(s_self, jnp.max(s, axis=-1, keepdims=True))
    p_self = jnp.exp(s_self - m)
    p = jnp.exp(s - m)
    den = p_self + jnp.sum(p, axis=-1, keepdims=True)
    o = (p_self * vn + _dot_nt(p, vt_ref[0, 0])) / den
    o_ref[0] = jnp.where(low[0:1], o[0:1], o[1:2]).astype(o_ref.dtype)


def _attention_sample(q, k_new, v_new, cache_k, cache_v, l, rel_bias):
    L, DB, W, H, E = cache_k.shape
    DA = H * E
    assert W >= max(w for w, _ in BRANCHES)
    tab = _distance_logits(rel_bias, W + 1)[:, ::-1].reshape(H // 2, 2, W + 1)
    tab = jnp.concatenate([tab, jnp.zeros((H // 2, SUBLANES - 2, W + 1), F32)], axis=1)
    kt = jnp.transpose(cache_k, (0, 1, 3, 4, 2)).reshape(L, DB, DA, W)
    vt = jnp.transpose(cache_v, (0, 1, 3, 4, 2)).reshape(L, DB, DA, W)
    vec = pl.BlockSpec((1, 1, LANES), lambda p, b: (b, 0, p))
    buf = pl.BlockSpec((1, 1, LANES, W), lambda p, b: (l, b, p, 0))
    return pl.pallas_call(
        _sattn_kernel,
        grid=(DA // LANES, DB),
        in_specs=[vec, vec, vec, buf, buf, pl.BlockSpec((1, SUBLANES, W + 1), lambda p, b: (p, 0, 0))],
        out_specs=vec,
        out_shape=jax.ShapeDtypeStruct((DB, 1, DA), BF16),
        compiler_params=_params(("arbitrary", "arbitrary"), 2 * 2 * LANES * W * 4 + 4 * SUBLANES * W * 4),
        name="attn_sample",
    )(q, k_new, v_new, kt, vt, tab)


def _prep_kernel(*refs, shift_rows, has_vres):
    it = iter(refs)
    h_ref = next(it)
    if shift_rows:
        hp8_ref, hlast_ref = next(it), next(it)
    else:
        hprev_ref = next(it)
    rkv_ref = next(it)
    if shift_rows:
        rp8_ref, rlast_ref = next(it), next(it)
    else:
        rprev_ref = next(it)
    if has_vres:
        vfirst_ref = next(it)
    mu_ref, murkv_ref, w0_ref, dw1_ref, dw2_ref, a0_ref, aw1_ref, aw2_ref, gw1_ref, gw2_ref = (next(it) for _ in range(10))
    if has_vres:
        vmu_ref, v0_ref, vw1_ref, vw2_ref = (next(it) for _ in range(4))
    kk_ref, ka_ref = next(it), next(it)
    r_out, lw_out, k_out, v_out, kk_out, b_out, g_out = (next(it) for _ in range(7))

    h = h_ref[0]
    rkv0 = rkv_ref[0]
    tm = h.shape[0]
    if shift_rows:
        first = pl.program_id(1) == 0
        row0 = lax.broadcasted_iota(jnp.int32, (tm, 1), 0) == 0
        h_edge = jnp.where(first, hlast_ref[0], hp8_ref[0, SUBLANES - 1:SUBLANES, :])
        r_edge = jnp.where(first, rlast_ref[0], rp8_ref[0, SUBLANES - 1:SUBLANES, :])
        hprev = jnp.where(row0, h_edge, pltpu.roll(h, 1, 0))
        rprev = jnp.where(row0, r_edge, pltpu.roll(rkv0, 1, 0))
    else:
        hprev = hprev_ref[0]
        rprev = rprev_ref[0]

    dh = hprev - h
    mu = mu_ref[0]
    xw = h + dh * mu[0:1]
    xa = h + dh * mu[1:2]
    xg = h + dh * mu[2:3]

    z = w0_ref[0] + _dot(jnp.tanh(_dot(xw, dw1_ref[0])), dw2_ref[0])
    softplus = jnp.maximum(-z, 0.0) + jnp.log(1.0 + jnp.exp(-jnp.abs(z)))
    lw_out[0] = -jnp.exp(-softplus - 0.5)

    a = _sigmoid(a0_ref[0] + _dot(_dot(xa, aw1_ref[0]), aw2_ref[0]))
    g_out[0] = _dot(_sigmoid(_dot(xg, gw1_ref[0])), gw2_ref[0])

    murkv = murkv_ref[0]
    DR = kk_ref.shape[-1]
    r0, k0, v0 = (rkv0[:, j * DR:(j + 1) * DR] for j in range(3))
    rp, kp, vp = (rprev[:, j * DR:(j + 1) * DR] for j in range(3))
    r_out[0] = r0 + (rp - r0) * murkv[0:1]
    kr = k0 + (kp - k0) * murkv[1:2]
    vr = v0 + (vp - v0) * murkv[2:3]
    if has_vres:
        xv = h + dh * vmu_ref[0]
        vgate = _sigmoid(v0_ref[0] + _dot(_dot(xv, vw1_ref[0]), vw2_ref[0]))
        vr = vr + (vfirst_ref[0] - vr) * vgate
    v_out[0] = vr
    kk = kr * kk_ref[0]
    kkn = kk / jnp.maximum(jnp.sqrt(_head_sums(kk * kk)), 1e-12)
    kk_out[0] = kkn
    b_out[0] = kkn * a
    k_out[0] = kr * (1.0 + (a - 1.0) * ka_ref[0])


def _rwkv_prep(h, h_prev, rkv0, rkv_prev, v_first, P, l, *, tm=256):
    B, T, D = h.shape
    DR = rkv0.shape[2] // 3
    tm = min(tm, T)
    shift_rows = h_prev.shape[1] == 1 and T > 1
    has_vres = v_first is not None

    def tile(C):
        return pl.BlockSpec((1, tm, C), lambda b, i: (b, i, 0))

    def prev8(C):
        return pl.BlockSpec((1, SUBLANES, C), lambda b, i: (b, jnp.maximum(i * (tm // SUBLANES) - 1, 0), 0))

    def seq_row(C):
        return pl.BlockSpec((1, 1, C), lambda b, i: (b, 0, 0))

    def layer(shape, ll=l):
        return pl.BlockSpec((1,) + shape, lambda b, i: (ll,) + (0,) * len(shape))

    args, specs = [h], [tile(D)]
    if shift_rows:
        args += [h, h_prev]
        specs += [prev8(D), seq_row(D)]
    else:
        args += [jnp.broadcast_to(h_prev, h.shape)]
        specs += [tile(D)]
    args.append(rkv0)
    specs.append(tile(3 * DR))
    if shift_rows:
        args += [rkv0, rkv_prev]
        specs += [prev8(3 * DR), seq_row(3 * DR)]
    else:
        args += [jnp.broadcast_to(rkv_prev, rkv0.shape)]
        specs += [tile(3 * DR)]
    if has_vres:
        args.append(v_first)
        specs.append(tile(DR))
    r1 = lambda a: a.reshape(a.shape[0], 1, a.shape[-1])
    for name in ('mu_wag', 'mu_rkv'):
        args.append(P[name]); specs.append(layer(P[name].shape[1:]))
    args.append(r1(P['decay_w0'])); specs.append(layer((1, DR)))
    for name in ('decay_w1', 'decay_w2'):
        args.append(P[name]); specs.append(layer(P[name].shape[1:]))
    args.append(r1(P['aaa_a0'])); specs.append(layer((1, DR)))
    for name in ('aaa_w1', 'aaa_w2', 'gate_w1', 'gate_w2'):
        args.append(P[name]); specs.append(layer(P[name].shape[1:]))
    if has_vres:
        args.append(r1(P['vres_mu'])); specs.append(layer((1, D), l - 1))
        args.append(r1(P['vres_v0'])); specs.append(layer((1, DR), l - 1))
        for name in ('vres_w1', 'vres_w2'):
            args.append(P[name]); specs.append(layer(P[name].shape[1:], l - 1))
    args.append(r1(P['k_k'])); specs.append(layer((1, DR)))
    args.append(r1(P['k_a'])); specs.append(layer((1, DR)))
    nbytes = 2 * tm * 4 * (2 * D + 7 * DR + 7 * DR + DR) + 8 * tm * D * 4 + 4 * D * 512 * 4
    return pl.pallas_call(
        functools.partial(_prep_kernel, shift_rows=shift_rows, has_vres=has_vres),
        grid=(B, T // tm),
        in_specs=specs,
        out_specs=[tile(DR) for _ in range(7)],
        out_shape=[jax.ShapeDtypeStruct((B, T, DR), F32) for _ in range(7)],
        compiler_params=_params(("arbitrary", "arbitrary"), nbytes),
        name="rwkv_prep",
    )(*args)


def _prefix_sum_rows(x):
    n = x.shape[0]
    row = lax.broadcasted_iota(jnp.int32, (n, 1), 0)
    s = 1
    while s < n:
        x = x + jnp.where(row >= s, pltpu.roll(x, s, 0), 0.0)
        s *= 2
    return x


def _unit_lower_inverse(a_strict, blk):
    n = a_strict.shape[0]
    ti = lax.broadcasted_iota(jnp.int32, (n, n), 0)
    si = lax.broadcasted_iota(jnp.int32, (n, n), 1)

    def lower_left(s):
        return ((ti // (2 * s)) == (si // (2 * s))) & ((ti % (2 * s)) >= s) & ((si % (2 * s)) < s)

    d = (ti == si).astype(F32) + jnp.where(lower_left(1), a_strict, 0.0)
    s = 2
    while s < blk:
        t = _dot(d, jnp.where(lower_left(s), a_strict, 0.0))
        yield
        d = d + _dot(t, d)
        yield
        s *= 2
    return d


def _round_robin(generators):
    results = [None] * len(generators)
    live = list(range(len(generators)))
    while live:
        for u in list(live):
            try:
                next(generators[u])
            except StopIteration as done:
                results[u] = done.value
                live.remove(u)
    return results


def _pair_chunk_terms(r, lw, k, v, kk, b, low, tri):
    C = r.shape[0]
    cum = _prefix_sum_rows(lw)
    g_in = jnp.exp(cum)
    g_inv = jnp.exp(-cum)
    g_end = g_in[C - 1:C]
    a_t = -kk * jnp.exp(cum - lw)
    r_t = r * g_in
    b_t = b * g_inv
    k_t = k * g_inv
    bg = b_t * g_end
    kg = k_t * g_end
    zc = jnp.zeros((C, LANES), F32)
    h0 = lambda x: jnp.where(low, x, 0.0)
    h1 = lambda x: jnp.where(low, 0.0, x)
    v0, v1 = h0(v), h1(v)
    ar = jnp.concatenate([a_t, r_t], axis=0)
    m0 = jnp.where(tri, _dot_nt(h0(ar), jnp.concatenate([b_t, k_t], axis=0)), 0.0)
    m1 = jnp.where(tri, _dot_nt(h1(ar), jnp.concatenate([k_t, b_t], axis=0)), 0.0)
    yield
    top0, bot0, top1, bot1 = m0[:C], m0[C:], m1[:C], m1[C:]
    stack2 = lambda x0, x1: jnp.concatenate([jnp.concatenate([x0, zc], axis=0),
                                             jnp.concatenate([zc, x1], axis=0)], axis=1)
    akv = _dot(stack2(top0, top1), jnp.concatenate([zc, v0, v1, zc], axis=0))
    a_sw = pltpu.roll(a_t, HEAD_DIM, 1)
    tinv = yield from _unit_lower_inverse(jnp.concatenate([h0(top0), h1(top1)], axis=0), C)
    x = _dot(tinv, akv + jnp.concatenate([h1(a_sw), h0(a_sw)], axis=0))
    yield
    z = jnp.concatenate([x[:C], v0, v1, x[C:]], axis=0)
    e = _dot(stack2(bot0, bot1), z)
    gh = _dot_tn(z, jnp.concatenate([h0(bg), h0(kg), h1(kg), h1(bg)], axis=0))
    yield
    y0 = jnp.where(low, e[:C], e[C:])
    r_eff = r_t + pltpu.roll(jnp.where(low, e[C:], e[:C]), HEAD_DIM, 1)
    h_mat = jnp.concatenate([h0(gh[:C]), h1(gh[C:])], axis=0)
    g_mat = jnp.concatenate([h0(gh[C:]), h1(gh[:C])], axis=0)
    return r_eff, y0, g_mat, h_mat, g_end


def _scan_kernel(r_ref, lw_ref, k_ref, v_ref, kk_ref, b_ref, g_ref, lng_ref, lnb_ref, rk_ref,
                 o_ref, s_ref, reff_s, y0_s, gm_s, hm_s, ge_s, *, chunk, unroll):
    C = chunk
    NS, T = r_ref.shape[:2]
    NC = T // C
    E = HEAD_DIM
    low = lax.broadcasted_iota(jnp.int32, (1, LANES), 1) < E
    ti = lax.broadcasted_iota(jnp.int32, (2 * C, 2 * C), 0)
    si = lax.broadcasted_iota(jnp.int32, (2 * C, 2 * C), 1)
    tri = (si % C) <= jnp.where(ti < C, ti - 1, ti - C)

    def phase1(i, _):
        sq = i // (NC // unroll)
        c0 = (i - sq * (NC // unroll)) * unroll
        cs = [c0 + u for u in range(unroll)]
        rows = [pl.ds(pl.multiple_of(c * C, C), C) for c in cs]
        loaded = [[ref[sq, rw, :] for ref in (r_ref, lw_ref, k_ref, v_ref, kk_ref, b_ref)] for rw in rows]
        terms = _round_robin([_pair_chunk_terms(*args, low, tri) for args in loaded])
        for c, rw, (r_eff, y0, g_mat, h_mat, g_end) in zip(cs, rows, terms):
            reff_s[sq, rw, :] = r_eff
            y0_s[sq, rw, :] = y0
            gm_s[sq, c] = g_mat
            hm_s[sq, c] = h_mat
            ge_s[sq, c] = jnp.broadcast_to(g_end, (SUBLANES, LANES))
        return 0

    lax.fori_loop(0, NS * (NC // unroll), phase1, 0)

    def phase2(c, states):
        rows = pl.ds(pl.multiple_of(c * C, C), C)
        ys = [_dot_nt(reff_s[sq, rows, :], S) + y0_s[sq, rows, :] for sq, S in enumerate(states)]
        new = tuple(S * ge_s[sq, c][0:1] + _dot(S, gm_s[sq, c]) + hm_s[sq, c] for sq, S in enumerate(states))
        for sq, y in enumerate(ys):
            y0_s[sq, rows, :] = y
        return new

    states = lax.fori_loop(0, NC, phase2, tuple(jnp.zeros((LANES, LANES), F32) for _ in range(NS)))
    for sq, S in enumerate(states):
        s_ref[sq, 0] = S[:E, :E]
        s_ref[sq, 1] = S[E:, E:]

    def finish(y, r, k, v, g):
        mu = _head_sums(y) * (1.0 / E)
        yield
        yc = y - mu
        var = _head_sums(yc * yc) * (1.0 / E)
        bonus = _head_sums(r * k * rk_ref[0])
        yield
        yn = yc * lax.rsqrt(var + GN_EPS) * lng_ref[0] + lnb_ref[0]
        return ((yn + bonus * v) * g).astype(o_ref.dtype)

    def phase3(i, _):
        sq = i // (NC // unroll)
        c0 = (i - sq * (NC // unroll)) * unroll
        rows = [pl.ds(pl.multiple_of((c0 + u) * C, C), C) for u in range(unroll)]
        loaded = [[y0_s[sq, rw, :]] + [ref[sq, rw, :] for ref in (r_ref, k_ref, v_ref, g_ref)] for rw in rows]
        outs = _round_robin([finish(*args) for args in loaded])
        for rw, out in zip(rows, outs):
            o_ref[sq, rw, :] = out
        return 0

    lax.fori_loop(0, NS * (NC // unroll), phase3, 0)


def _rwkv_scan(r, lw, k, v, kk, b, g, lnx_g, lnx_b, r_k, l):
    B, T, DR = r.shape
    H = DR // HEAD_DIM
    L = lnx_g.shape[0]
    C = min(SCAN_CHUNK, T)
    nc = T // C
    unroll = math.gcd(SCAN_UNROLL, nc)
    ns = math.gcd(SCAN_SEQS, B)
    assert T % C == 0
    seq = pl.BlockSpec((ns, T, LANES), lambda p, b_: (b_, 0, p))
    par = pl.BlockSpec((1, 1, LANES), lambda p, b_: (l, 0, p))
    scratch = [pltpu.VMEM((ns, T, LANES), F32), pltpu.VMEM((ns, T, LANES), F32),
               pltpu.VMEM((ns, nc, LANES, LANES), F32), pltpu.VMEM((ns, nc, LANES, LANES), F32),
               pltpu.VMEM((ns, nc, SUBLANES, LANES), F32)]
    nbytes = ns * (2 * 8 * T * LANES * 4 + 2 * T * LANES * 4 + 2 * nc * LANES * LANES * 4 + nc * SUBLANES * LANES * 4)
    return pl.pallas_call(
        functools.partial(_scan_kernel, chunk=C, unroll=unroll),
        grid=(DR // LANES, B // ns),
        in_specs=[seq] * 7 + [par] * 3,
        out_specs=[seq, pl.BlockSpec((ns, 2, HEAD_DIM, HEAD_DIM), lambda p, b_: (b_, p, 0, 0))],
        out_shape=[jax.ShapeDtypeStruct((B, T, DR), BF16), jax.ShapeDtypeStruct((B, H, HEAD_DIM, HEAD_DIM), F32)],
        scratch_shapes=scratch,
        compiler_params=_params(("arbitrary", "arbitrary"), nbytes),
        name="rwkv_scan",
    )(r, lw, k, v, kk, b, g, lnx_g.reshape(L, 1, DR), lnx_b.reshape(L, 1, DR), r_k.reshape(L, 1, DR))


def _step_kernel(s_ref, r_ref, lw_ref, k_ref, v_ref, kk_ref, b_ref, g_ref, lng_ref, lnb_ref, rk_ref,
                 o_ref, so_ref, *, n_heads):
    E = HEAD_DIM
    eye = (lax.broadcasted_iota(jnp.int32, (E, E), 0) == lax.broadcasted_iota(jnp.int32, (E, E), 1)).astype(F32)

    def head(hh):
        sl = slice(hh * E, (hh + 1) * E)
        S = s_ref[0, hh]
        r, k, v, kk, b, g = (ref[0, :, sl] for ref in (r_ref, k_ref, v_ref, kk_ref, b_ref, g_ref))
        w = jnp.exp(lw_ref[0, :, sl])
        v_col = jnp.sum(eye * v, axis=-1, keepdims=True)
        sa = jnp.sum(S * (-kk), axis=-1, keepdims=True)
        bonus = jnp.sum(r * k * rk_ref[0, :, sl], axis=-1, keepdims=True)
        yield
        S = S * w + sa * b + v_col * k
        y_col = jnp.sum(S * r, axis=-1, keepdims=True)
        yield
        y = jnp.sum(eye * y_col, axis=0, keepdims=True)
        mu = jnp.mean(y, axis=-1, keepdims=True)
        yield
        yc = y - mu
        var = jnp.mean(yc * yc, axis=-1, keepdims=True)
        yield
        yn = yc * lax.rsqrt(var + GN_EPS) * lng_ref[0, :, sl] + lnb_ref[0, :, sl]
        return S, (yn + bonus * v) * g

    results = _round_robin([head(hh) for hh in range(n_heads)])
    for hh, (S, _) in enumerate(results):
        so_ref[0, hh] = S
    o_ref[0] = jnp.concatenate([o for _, o in results], axis=-1).astype(o_ref.dtype)


def _rwkv_step(state, r, lw, k, v, kk, b, g, lnx_g, lnx_b, r_k, l):
    DB, H = state.shape[:2]
    DR = H * HEAD_DIM
    L = lnx_g.shape[0]
    vec = pl.BlockSpec((1, 1, DR), lambda b_: (b_, 0, 0))
    par = pl.BlockSpec((1, 1, DR), lambda b_: (l, 0, 0))
    st = pl.BlockSpec((1, H, HEAD_DIM, HEAD_DIM), lambda b_: (b_, 0, 0, 0))
    return pl.pallas_call(
        functools.partial(_step_kernel, n_heads=H),
        grid=(DB,),
        in_specs=[st] + [vec] * 7 + [par] * 3,
        out_specs=[vec, st],
        out_shape=[jax.ShapeDtypeStruct((DB, 1, DR), BF16), jax.ShapeDtypeStruct(state.shape, F32)],
        compiler_params=_params(("arbitrary",), 4 * H * HEAD_DIM * LANES * 4),
        name="rwkv_step",
    )(state, r, lw, k, v, kk, b, g, lnx_g.reshape(L, 1, DR), lnx_b.reshape(L, 1, DR), r_k.reshape(L, 1, DR))


def _mixer_tail(P, l, x, att, rw, mods):
    x = _proj_resid([att, rw], P['w_out'], l, x, mods[2], tm=1024, tn=512)
    (h2,) = _norm_mod(x, P['norm2_g'], l, mods[3], mods[4], [BF16])
    act = _proj_swiglu(h2, P['w_gu'], l)
    return _proj_resid([act], P['w_down'], l, x, mods[5], tm=512, tn=512)


def _prompt_layer(P, l, x, mods, base, v_first, k_stack, v_stack):
    B, T, D = x.shape
    DA = P['rel_bias'].shape[1] * HEAD_DIM
    h, hb = _norm_mod(x, P['norm1_g'], l, mods[0], mods[1], [F32, BF16])
    q = _proj(hb, P['w_in'], l, 0, DA, head_g=P['q_norm_g'][l], scale=ATT_SCALE)
    k_stack = _proj(hb, P['w_in'], l, DA, DA, head_g=P['k_norm_g'][l], stack=k_stack)
    v_stack = _proj(hb, P['w_in'], l, 2 * DA, DA, stack=v_stack)
    rkv0 = _proj(hb, P['w_in'], l, 3 * DA, P['w_in'].shape[2] - 3 * DA)
    att = _attention_prompt(q, k_stack, v_stack, l, base)
    zero_h = jnp.zeros((B, 1, D), F32)
    zero_r = jnp.zeros((B, 1, rkv0.shape[2]), F32)
    r, lw, kr, vr, kk, b, g = _rwkv_prep(h, zero_h, rkv0, zero_r, v_first, P, l)
    if v_first is None:
        v_first = vr
    rw, state = _rwkv_scan(r, lw, kr, vr, kk, b, g, P['lnx_g'], P['lnx_b'], P['r_k'], l)
    x = _mixer_tail(P, l, x, att, rw, mods)
    return x, v_first, k_stack, v_stack, state, h[:, -1]


def _sample_layer(P, l, x, mods, cache_k, cache_v, state, h_last, v_first):
    _, DB, D = x.shape
    H = P['rel_bias'].shape[1]
    DA = H * HEAD_DIM
    h, hb = _norm_mod(x, P['norm1_g'], l, mods[0], mods[1], [F32, BF16])
    q = _proj(hb, P['w_in'], l, 0, DA, head_g=P['q_norm_g'][l], scale=ATT_SCALE)
    k = _proj(hb, P['w_in'], l, DA, DA, head_g=P['k_norm_g'][l])
    v = _proj(hb, P['w_in'], l, 2 * DA, DA)
    both = jnp.concatenate([hb, h_last[None].astype(BF16)], axis=1)
    rkv2 = _proj(both, P['w_in'], l, 3 * DA, P['w_in'].shape[2] - 3 * DA)
    rkv0, rkv_prev = rkv2[:, :DB], rkv2[:, DB:]
    per_seq = lambda t: t.reshape(DB, 1, t.shape[-1])
    att = _attention_sample(per_seq(q), per_seq(k), per_seq(v), cache_k, cache_v, l, P['rel_bias'])
    r, lw, kr, vr, kk, b, g = _rwkv_prep(h, h_last[None], rkv0, rkv_prev, v_first, P, l)
    if v_first is None:
        v_first = vr
    rw, new_state = _rwkv_step(state, *(per_seq(t) for t in (r, lw, kr, vr, kk, b, g)),
                               P['lnx_g'], P['lnx_b'], P['r_k'], l)
    x = _mixer_tail(P, l, x, att.reshape(1, DB, DA), rw.reshape(1, DB, DA), mods)
    return x, v_first, k[0], v[0], new_state, h[0]


def kernel(x_prompt, x_sample, c_prompt, c_sample, cache_k, cache_v, state_wkv, state_shift, rel_bias, ada_w, ada_b, norm1_g, norm2_g, w_in, q_norm_g, k_norm_g, mu_wag, mu_rkv, decay_w0, decay_w1, decay_w2, aaa_a0, aaa_w1, aaa_w2, gate_w1, gate_w2, vres_mu, vres_v0, vres_w1, vres_w2, k_k, k_a, r_k, lnx_g, lnx_b, w_out, w_gu, w_down):
    P = dict(rel_bias=rel_bias, norm1_g=norm1_g, norm2_g=norm2_g, w_in=w_in, q_norm_g=q_norm_g,
             k_norm_g=k_norm_g, mu_wag=mu_wag, mu_rkv=mu_rkv, decay_w0=decay_w0, decay_w1=decay_w1,
             decay_w2=decay_w2, aaa_a0=aaa_a0, aaa_w1=aaa_w1, aaa_w2=aaa_w2, gate_w1=gate_w1,
             gate_w2=gate_w2, vres_mu=vres_mu, vres_v0=vres_v0, vres_w1=vres_w1, vres_w2=vres_w2,
             k_k=k_k, k_a=k_a, r_k=r_k.reshape(r_k.shape[0], -1), lnx_g=lnx_g, lnx_b=lnx_b,
             w_out=w_out, w_gu=w_gu, w_down=w_down)
    L = ada_w.shape[0]
    B, T, D = x_prompt.shape
    DB = x_sample.shape[0]
    H_ATT = rel_bias.shape[1]
    assert x_sample.shape[1] == 1 and T <= W_MAX

    rows = -(-(B + DB) // SUBLANES) * SUBLANES
    c_all = jnp.concatenate([c_prompt, c_sample, jnp.zeros((rows - B - DB, D), F32)], axis=0)
    mod = _ada(c_all, ada_w, ada_b).reshape(L, rows, 6, D)
    base = _band_bias_rows(rel_bias)

    k_stack = jnp.zeros((L, B, T, H_ATT * HEAD_DIM), F32)
    v_stack = jnp.zeros((L, B, T, H_ATT * HEAD_DIM), F32)
    xp, vf = x_prompt, None
    ps, ph = [], []
    for l in range(L):
        mods = [mod[l, :B, j][:, None, :] for j in range(6)]
        xp, vf, k_stack, v_stack, ns, nh = _prompt_layer(P, l, xp, mods, base, vf, k_stack, v_stack)
        ps.append(ns)
        ph.append(nh)

    xs, vf = x_sample.reshape(1, DB, D), None
    sk, sv, ss, sh = [], [], [], []
    for l in range(L):
        mods = [mod[l, B:B + DB, j][None] for j in range(6)]
        xs, vf, nk, nv, ns, nh = _sample_layer(P, l, xs, mods, cache_k, cache_v, state_wkv[l], state_shift[l], vf)
        sk.append(nk.reshape(DB, 1, H_ATT, HEAD_DIM))
        sv.append(nv.reshape(DB, 1, H_ATT, HEAD_DIM))
        ss.append(ns)
        sh.append(nh)

    return (xp, xs.reshape(DB, 1, D), k_stack.reshape(L, B, T, H_ATT, HEAD_DIM),
            v_stack.reshape(L, B, T, H_ATT, HEAD_DIM), jnp.stack(ps), jnp.stack(ph),
            jnp.stack(sk), jnp.stack(sv), jnp.stack(ss), jnp.stack(sh))
```

```python
import functools
import math

import jax
import jax.numpy as jnp
from jax import lax
from jax.experimental import pallas as pl
from jax.experimental.pallas import tpu as pltpu

F32 = jnp.float32
BF16 = jnp.bfloat16

HEAD_DIM = 64
BRANCHES = ((128, 1), (512, 4), (2048, 16))
W_MAX = 2048
NUM_BUCKETS = 32
MAX_DISTANCE = W_MAX
ATT_SCALE = HEAD_DIM ** -0.5
RMS_EPS = 1e-6
GN_EPS = 64e-5
NEG = -1e30

LANES = 128
SUBLANES = 8
Q_BLOCK = 128
SCAN_CHUNK = 64
SCAN_SEQS = 2
SCAN_UNROLL = 16
ATT_UNROLL = 4
VMEM_CAP = 56 * 1024 * 1024

assert all(w // dil == Q_BLOCK for w, dil in BRANCHES) and 2 * HEAD_DIM == LANES


def _vmem(nbytes):
    return int(min(VMEM_CAP, nbytes * 1.3 + (6 << 20)))


def _params(sem, nbytes):
    return pltpu.CompilerParams(dimension_semantics=sem, vmem_limit_bytes=_vmem(nbytes))


def _dot(a, b):
    return jnp.dot(a.astype(BF16), b.astype(BF16), preferred_element_type=F32)


def _dot_nt(a, b):
    return lax.dot_general(a.astype(BF16), b.astype(BF16), (((1,), (1,)), ((), ())), preferred_element_type=F32)


def _dot_tn(a, b):
    return lax.dot_general(a.astype(BF16), b.astype(BF16), (((0,), (0,)), ((), ())), preferred_element_type=F32)


def _sigmoid(x):
    return 1.0 / (1.0 + jnp.exp(-x))


def _head_sums(x):
    r = lax.broadcasted_iota(jnp.int32, (LANES, LANES), 0) // HEAD_DIM
    c = lax.broadcasted_iota(jnp.int32, (LANES, LANES), 1) // HEAD_DIM
    bd = (r == c).astype(BF16)
    cols = [_dot(x[:, j * LANES:(j + 1) * LANES], bd) for j in range(x.shape[1] // LANES)]
    return cols[0] if len(cols) == 1 else jnp.concatenate(cols, axis=-1)


def _ada_kernel(c_ref, w_ref, b_ref, o_ref):
    c = c_ref[...]
    s = c * _sigmoid(c)
    o_ref[0] = _dot(s, w_ref[0]) + b_ref[0]


def _ada(c_all, ada_w, ada_b, tn=1024):
    L, D, N = ada_w.shape
    R = c_all.shape[0]
    return pl.pallas_call(
        _ada_kernel,
        grid=(L, N // tn),
        in_specs=[pl.BlockSpec((R, D), lambda l, j: (0, 0)),
                  pl.BlockSpec((1, D, tn), lambda l, j: (l, 0, j)),
                  pl.BlockSpec((1, 1, tn), lambda l, j: (l, 0, j))],
        out_specs=pl.BlockSpec((1, R, tn), lambda l, j: (l, 0, j)),
        out_shape=jax.ShapeDtypeStruct((L, R, N), F32),
        compiler_params=_params(("arbitrary", "arbitrary"), 2 * D * tn * 4 + D * tn * 2),
        name="ada_mod",
    )(c_all, ada_w, ada_b.reshape(L, 1, N))


def _norm_kernel(x_ref, g_ref, sh_ref, sc_ref, *out_refs):
    x = x_ref[0]
    y = x * lax.rsqrt(jnp.mean(x * x, axis=-1, keepdims=True) + RMS_EPS) * g_ref[0]
    h = y * (1.0 + sc_ref[0]) + sh_ref[0]
    for o in out_refs:
        o[0] = h.astype(o.dtype)


def _row_spec(arr, tm, T):
    C = arr.shape[-1]
    if arr.shape[1] == 1:
        return pl.BlockSpec((1, 1, C), lambda b, i: (b, 0, 0))
    assert arr.shape[1] == T
    return pl.BlockSpec((1, tm, C), lambda b, i: (b, i, 0))


def _norm_mod(x, g_stack, l, shift, scale, out_dtypes, tm=512):
    B, T, D = x.shape
    tm = min(tm, T)
    L = g_stack.shape[0]
    outs = pl.pallas_call(
        _norm_kernel,
        grid=(B, T // tm),
        in_specs=[pl.BlockSpec((1, tm, D), lambda b, i: (b, i, 0)),
                  pl.BlockSpec((1, 1, D), lambda b, i: (l, 0, 0)),
                  _row_spec(shift, tm, T), _row_spec(scale, tm, T)],
        out_specs=[pl.BlockSpec((1, tm, D), lambda b, i: (b, i, 0)) for _ in out_dtypes],
        out_shape=[jax.ShapeDtypeStruct((B, T, D), dt) for dt in out_dtypes],
        compiler_params=_params(("arbitrary", "arbitrary"), 2 * tm * D * 4 * (2 + len(out_dtypes))),
        name="norm_mod",
    )(x, g_stack.reshape(L, 1, D), shift, scale)
    return outs


def _first_inner_step():
    return (pl.program_id(1) == 0) & (pl.program_id(2) == 0)


def _head_rmsnorm(acc, g):
    return acc * lax.rsqrt(_head_sums(acc * acc) * (1.0 / HEAD_DIM) + RMS_EPS) * g


def _rider_specs(R, k_sizes, tn):
    ins = [pl.BlockSpec((1, R, ks), lambda n, b, i: (0, 0, 0)) for ks in k_sizes]
    return ins, pl.BlockSpec((1, R, tn), lambda n, b, i: (0, 0, n))


def _mm_kernel(*refs, headnorm, scale, aliased):
    it = iter(refs)
    x_ref, w_ref = next(it), next(it)
    g_ref = next(it) if headnorm else None
    if aliased:
        next(it)
    xr_ref, o_ref, or_ref, wb_ref = next(it), next(it), next(it), next(it)

    def result(x):
        acc = jnp.dot(x, wb_ref[...], preferred_element_type=F32)
        if headnorm:
            acc = _head_rmsnorm(acc, g_ref[0])
            if scale != 1.0:
                acc = acc * scale
        return acc

    @pl.when(_first_inner_step())
    def _():
        wb_ref[...] = w_ref[0].astype(BF16)
        or_ref[0] = result(xr_ref[0])

    if len(o_ref.shape) == 4:
        o_ref[0, 0] = result(x_ref[0])
    else:
        o_ref[0] = result(x_ref[0])


def _proj(x, rider, w_stack, l, col0, ncols, *, head_g=None, scale=1.0, stack=None, tm=1024, tn=1024):
    B, T, K = x.shape
    R = rider.shape[1]
    tm = min(tm, T)
    tn = min(tn, ncols)
    assert T % tm == 0 and ncols % tn == 0 and col0 % tn == 0
    cb = col0 // tn
    in_specs = [pl.BlockSpec((1, tm, K), lambda n, b, i: (b, i, 0)),
                pl.BlockSpec((1, K, tn), lambda n, b, i: (l, 0, cb + n))]
    args = [x, w_stack]
    if head_g is not None:
        in_specs.append(pl.BlockSpec((1, 1, tn), lambda n, b, i: (0, 0, 0)))
        args.append(jnp.tile(head_g, tn // HEAD_DIM).reshape(1, 1, tn))
    aliases = {}
    if stack is None:
        out_spec = pl.BlockSpec((1, tm, tn), lambda n, b, i: (b, i, n))
        out_shape = jax.ShapeDtypeStruct((B, T, ncols), F32)
    else:
        out_spec = pl.BlockSpec((1, 1, tm, tn), lambda n, b, i: (l, b, i, n))
        out_shape = jax.ShapeDtypeStruct(stack.shape, F32)
        in_specs.append(pl.BlockSpec(memory_space=pl.ANY))
        args.append(stack)
        aliases = {len(args) - 1: 0}
    r_ins, r_out = _rider_specs(R, (K,), tn)
    nbytes = 2 * (tm * K * 2 + K * tn * 4 + tm * tn * 4) + K * tn * 2 + 2 * tm * tn * 4
    return pl.pallas_call(
        functools.partial(_mm_kernel, headnorm=head_g is not None, scale=scale, aliased=bool(aliases)),
        grid=(ncols // tn, B, T // tm),
        in_specs=in_specs + r_ins,
        out_specs=[out_spec, r_out],
        out_shape=[out_shape, jax.ShapeDtypeStruct((1, R, ncols), F32)],
        scratch_shapes=[pltpu.VMEM((K, tn), BF16)],
        input_output_aliases=aliases,
        compiler_params=_params(("arbitrary",) * 3, nbytes),
        name="proj",
    )(*args, rider)


def _swiglu_kernel(x_ref, wg_ref, wu_ref, xr_ref, o_ref, or_ref, wgb_ref, wub_ref):
    def result(x):
        gate = jnp.dot(x, wgb_ref[...], preferred_element_type=F32)
        up = jnp.dot(x, wub_ref[...], preferred_element_type=F32)
        return (gate * _sigmoid(gate) * up).astype(o_ref.dtype)

    @pl.when(_first_inner_step())
    def _():
        wgb_ref[...] = wg_ref[0].astype(BF16)
        wub_ref[...] = wu_ref[0].astype(BF16)
        or_ref[0] = result(xr_ref[0])

    o_ref[0] = result(x_ref[0])


def _proj_swiglu(x, rider, w_gu, l, *, tm=1024, tn=512):
    B, T, K = x.shape
    R = rider.shape[1]
    F = w_gu.shape[2] // 2
    tm = min(tm, T)
    assert T % tm == 0 and F % tn == 0
    nb = F // tn
    r_ins, r_out = _rider_specs(R, (K,), tn)
    nbytes = 2 * (tm * K * 2 + 2 * K * tn * 4 + tm * tn * 2) + 2 * K * tn * 2 + 3 * tm * tn * 4
    return pl.pallas_call(
        _swiglu_kernel,
        grid=(nb, B, T // tm),
        in_specs=[pl.BlockSpec((1, tm, K), lambda n, b, i: (b, i, 0)),
                  pl.BlockSpec((1, K, tn), lambda n, b, i: (l, 0, n)),
                  pl.BlockSpec((1, K, tn), lambda n, b, i: (l, 0, nb + n))] + r_ins,
        out_specs=[pl.BlockSpec((1, tm, tn), lambda n, b, i: (b, i, n)), r_out],
        out_shape=[jax.ShapeDtypeStruct((B, T, F), BF16), jax.ShapeDtypeStruct((1, R, F), BF16)],
        scratch_shapes=[pltpu.VMEM((K, tn), BF16), pltpu.VMEM((K, tn), BF16)],
        compiler_params=_params(("arbitrary",) * 3, nbytes),
        name="proj_swiglu",
    )(x, w_gu, w_gu, rider)


def _resid_kernel(*refs, k_sizes):
    n_x = len(k_sizes)
    x_refs, refs = refs[:n_x], refs[n_x:]
    w_ref, res_ref, gate_ref = refs[:3]
    xr_refs, refs = refs[3:3 + n_x], refs[3 + n_x:]
    resr_ref, gater_ref, o_ref, or_ref, wb_ref = refs

    def result(xs, res, gate):
        acc = None
        k0 = 0
        for x_ref, ks in zip(xs, k_sizes):
            part = jnp.dot(x_ref[0], wb_ref[k0:k0 + ks, :], preferred_element_type=F32)
            acc = part if acc is None else acc + part
            k0 += ks
        return res[0] + gate[0] * acc

    @pl.when(_first_inner_step())
    def _():
        wb_ref[...] = w_ref[0].astype(BF16)
        or_ref[0] = result(xr_refs, resr_ref, gater_ref)

    o_ref[0] = result(x_refs, res_ref, gate_ref)


def _proj_resid(xs, rider_xs, w_stack, l, resid, gate, rider_resid, rider_gate, *, tm=512, tn=512):
    B, T, N = resid.shape
    R = rider_resid.shape[1]
    k_sizes = tuple(x.shape[2] for x in xs)
    K = sum(k_sizes)
    tm = min(tm, T)
    assert T % tm == 0 and N % tn == 0 and gate.shape[1] == 1
    in_specs = [pl.BlockSpec((1, tm, ks), lambda n, b, i: (b, i, 0)) for ks in k_sizes]
    in_specs += [pl.BlockSpec((1, K, tn), lambda n, b, i: (l, 0, n)),
                 pl.BlockSpec((1, tm, tn), lambda n, b, i: (b, i, n)),
                 pl.BlockSpec((1, 1, tn), lambda n, b, i: (b, 0, n))]
    r_ins, r_out = _rider_specs(R, k_sizes, tn)
    nbytes = 2 * (tm * K * 2 + K * tn * 4 + 2 * tm * tn * 4) + K * tn * 2 + 2 * tm * tn * 4
    return pl.pallas_call(
        functools.partial(_resid_kernel, k_sizes=k_sizes),
        grid=(N // tn, B, T // tm),
        in_specs=in_specs + r_ins + [r_out, r_out],
        out_specs=[pl.BlockSpec((1, tm, tn), lambda n, b, i: (b, i, n)), r_out],
        out_shape=[jax.ShapeDtypeStruct((B, T, N), F32), jax.ShapeDtypeStruct((1, R, N), F32)],
        scratch_shapes=[pltpu.VMEM((K, tn), BF16)],
        compiler_params=_params(("arbitrary",) * 3, nbytes),
        name="proj_resid",
    )(*xs, w_stack, resid, gate, *rider_xs, rider_resid, rider_gate)


def _rel_bucket(dist):
    max_exact = NUM_BUCKETS // 2
    d = jnp.maximum(dist, 0)
    df = jnp.maximum(d, 1).astype(F32)
    large = max_exact + (jnp.log(df / max_exact) / math.log(MAX_DISTANCE / max_exact)
                         * (NUM_BUCKETS - max_exact)).astype(jnp.int32)
    return jnp.where(d < max_exact, d, jnp.minimum(large, NUM_BUCKETS - 1))


def _band_bias_rows(rel_bias):
    dsub = Q_BLOCK - jnp.arange(2 * Q_BLOCK)
    rows = []
    for (w, dil) in BRANCHES:
        valid = (dsub >= 0) & (dsub <= w // dil)
        bias = rel_bias[_rel_bucket(jnp.maximum(dsub, 0) * dil)].astype(F32).T
        rows.append(jnp.where(valid[None], bias, NEG))
    return jnp.stack(rows, axis=1)


def _attn_kernel(q_ref, k_ref, v_ref, base_ref, o_ref, qs, ks, vs, os_, ms_, ds_, stage):
    T = q_ref.shape[2]
    QB = Q_BLOCK
    low = lax.broadcasted_iota(jnp.int32, (1, LANES), 1) < HEAD_DIM
    own = (low, jnp.logical_not(low))
    prev_cols = lax.broadcasted_iota(jnp.int32, (1, 2 * QB), 1) < QB

    assert len(BRANCHES) == 3 and BRANCHES[0][1] == 1
    prev_dil = 1
    for bi, (_, dil) in enumerate(BRANCHES):
        L, Lp, ratio = T // dil, T // prev_dil, dil // prev_dil
        ks[bi, 0:QB, :] = jnp.zeros((QB, LANES), BF16)
        vs[bi, 0:QB, :] = jnp.zeros((QB, LANES), BF16)
        for r in range(dil):
            rows = pl.ds((r % prev_dil) * Lp + r // prev_dil, L, stride=ratio)
            if bi < 2:
                q, k, v = (ref[0, 0, rows, :] for ref in (q_ref, k_ref, v_ref))
            else:
                q, k, v = (stage[a, rows, :] for a in range(3))
            if bi == 1:
                for a, x in enumerate((q, k, v)):
                    stage[a, r * L:(r + 1) * L, :] = x
            for hh in range(2):
                qs[hh, bi, r * L:(r + 1) * L, :] = jnp.where(own[hh], q, 0.0).astype(BF16)
            ks[bi, QB + r * L:QB + (r + 1) * L, :] = k.astype(BF16)
            vs[bi, QB + r * L:QB + (r + 1) * L, :] = v.astype(BF16)
        prev_dil = dil

    for bi, (_, dil) in enumerate(BRANCHES):
        nb = T // dil // QB
        tiles = [pltpu.roll(jnp.broadcast_to(base_ref[hh, bi:bi + 1, :], (QB, 2 * QB)), 0, 1, stride=1, stride_axis=0)
                 for hh in range(2)]

        def one_block(g, q0, q1, kw, vw, nb=nb, tiles=tiles):
            no_prev = jnp.logical_and(g % nb == 0, prev_cols)
            ss = [jnp.where(no_prev, NEG, lax.dot_general(q, kw, (((1,), (1,)), ((), ())),
                                                          preferred_element_type=F32) + tiles[hh])
                  for hh, q in enumerate((q0, q1))]
            yield
            ms = [jnp.max(s, axis=-1, keepdims=True) for s in ss]
            ps = [jnp.exp(s - m) for s, m in zip(ss, ms)]
            dens = [jnp.sum(p, axis=-1, keepdims=True) for p in ps]
            os2 = [jnp.dot(p.astype(BF16), vw, preferred_element_type=F32) for p in ps]
            yield
            pair = lambda x0, x1: jnp.where(low, jnp.broadcast_to(x0, (QB, LANES)), jnp.broadcast_to(x1, (QB, LANES)))
            return pair(*os2), pair(*ms), pair(*dens)

        def blocks(i, _, bi=bi, dil=dil, nb=nb):
            gs = [i * ATT_UNROLL + u for u in range(ATT_UNROLL)]
            ats = [pl.multiple_of(g * QB, QB) for g in gs]
            loaded = [(qs[0, bi, pl.ds(at, QB), :], qs[1, bi, pl.ds(at, QB), :],
                       ks[bi, pl.ds(at, 2 * QB), :], vs[bi, pl.ds(at, 2 * QB), :]) for at in ats]
            results = _round_robin([one_block(g, *ld) for g, ld in zip(gs, loaded)])
            for g, (acc, m, den) in zip(gs, results):
                r = g // nb
                tok = pl.ds((g - r * nb) * (QB * dil) + r, QB, stride=dil)
                os_[bi, tok, :] = acc
                ms_[bi, tok, :] = m
                ds_[bi, tok, :] = den
            return 0

        lax.fori_loop(0, T // QB // ATT_UNROLL, blocks, 0)

    def merge(i, _):
        rows = pl.ds(pl.multiple_of(i * QB, QB), QB)
        maxes = [ms_[bi, rows, :] for bi in range(len(BRANCHES))]
        m = functools.reduce(jnp.maximum, maxes)
        ws = [jnp.exp(x - m) for x in maxes]
        num = sum(w * os_[bi, rows, :] for bi, w in enumerate(ws))
        den = sum(w * ds_[bi, rows, :] for bi, w in enumerate(ws))
        o_ref[0, rows, :] = (num / den).astype(o_ref.dtype)
        return 0

    lax.fori_loop(0, T // QB, merge, 0)


def _attention_prompt(q, k_stack, v_stack, l, base):
    B, T, DA = q.shape
    nbr = len(BRANCHES)
    assert T % (Q_BLOCK * max(d for _, d in BRANCHES)) == 0
    qkv = pl.BlockSpec((1, 1, T, LANES), lambda p, b: (l, b, 0, p))
    scratch = [pltpu.VMEM((2, nbr, T, LANES), BF16), pltpu.VMEM((nbr, T + Q_BLOCK, LANES), BF16),
               pltpu.VMEM((nbr, T + Q_BLOCK, LANES), BF16), pltpu.VMEM((nbr, T, LANES), F32),
               pltpu.VMEM((nbr, T, LANES), F32), pltpu.VMEM((nbr, T, LANES), F32), pltpu.VMEM((3, T, LANES), F32)]
    nbytes = (2 * 3 * T * LANES * 4 + 4 * nbr * (T + Q_BLOCK) * LANES * 2 + 3 * nbr * T * LANES * 4
              + 3 * T * LANES * 4 + 2 * T * LANES * 2)
    return pl.pallas_call(
        _attn_kernel,
        grid=(DA // LANES, B),
        in_specs=[pl.BlockSpec((1, 1, T, LANES), lambda p, b: (0, b, 0, p)), qkv, qkv,
                  pl.BlockSpec((2, nbr, 2 * Q_BLOCK), lambda p, b: (p, 0, 0))],
        out_specs=pl.BlockSpec((1, T, LANES), lambda p, b: (b, 0, p)),
        out_shape=jax.ShapeDtypeStruct((B, T, DA), BF16),
        scratch_shapes=scratch,
        compiler_params=_params(("arbitrary", "arbitrary"), nbytes),
        name="attn_prompt",
    )(q[None], k_stack, v_stack, base)


def _distance_logits(rel_bias, nd):
    d = jnp.arange(nd)
    mult = sum(((d % dil == 0) & (d // dil <= w // dil)).astype(F32) for (w, dil) in BRANCHES)
    bias = rel_bias[_rel_bucket(d)].astype(F32).T
    return jnp.where(mult > 0, bias + jnp.log(jnp.maximum(mult, 1.0)), NEG)


def _sattn_kernel(q_ref, kn_ref, vn_ref, kt_ref, vt_ref, tab_ref, o_ref):
    row = lax.broadcasted_iota(jnp.int32, (SUBLANES, LANES), 0)
    low = lax.broadcasted_iota(jnp.int32, (SUBLANES, LANES), 1) < HEAD_DIM
    own = ((row == 0) & low) | ((row == 1) & jnp.logical_not(low))
    q = jnp.where(own, q_ref[0], 0.0).astype(BF16)
    kn = kn_ref[0].astype(BF16).astype(F32)
    vn = vn_ref[0].astype(BF16).astype(F32)
    tab = tab_ref[0]
    W = kt_ref.shape[3]
    s_self = jnp.sum(q.astype(F32) * kn, axis=-1, keepdims=True) + tab[:, W:W + 1]
    s = _dot(q, kt_ref[0, 0]) + tab[:, :W]
    m = jnp.maximum(s_self, jnp.max(s, axis=-1, keepdims=True))
    p_self = jnp.exp(s_self - m)
    p = jnp.exp(s - m)
    den = p_self + jnp.sum(p, axis=-1, keepdims=True)
    o = (p_self * vn + _dot_nt(p, vt_ref[0, 0])) / den
    o_ref[0] = jnp.where(low[0:1], o[0:1], o[1:2]).astype(o_ref.dtype)


def _attention_sample(q, k_new, v_new, cache_k, cache_v, l, rel_bias):
    L, DB, W, H, E = cache_k.shape
    DA = H * E
    assert W >= max(w for w, _ in BRANCHES)
    tab = _distance_logits(rel_bias, W + 1)[:, ::-1].reshape(H // 2, 2, W + 1)
    tab = jnp.concatenate([tab, jnp.zeros((H // 2, SUBLANES - 2, W + 1), F32)], axis=1)
    kt = jnp.transpose(cache_k, (0, 1, 3, 4, 2)).reshape(L, DB, DA, W)
    vt = jnp.transpose(cache_v, (0, 1, 3, 4, 2)).reshape(L, DB, DA, W)
    vec = pl.BlockSpec((1, 1, LANES), lambda p, b: (b, 0, p))
    buf = pl.BlockSpec((1, 1, LANES, W), lambda p, b: (l, b, p, 0))
    return pl.pallas_call(
        _sattn_kernel,
        grid=(DA // LANES, DB),
        in_specs=[vec, vec, vec, buf, buf, pl.BlockSpec((1, SUBLANES, W + 1), lambda p, b: (p, 0, 0))],
        out_specs=vec,
        out_shape=jax.ShapeDtypeStruct((DB, 1, DA), BF16),
        compiler_params=_params(("arbitrary", "arbitrary"), 2 * 2 * LANES * W * 4 + 4 * SUBLANES * W * 4),
        name="attn_sample",
    )(q, k_new, v_new, kt, vt, tab)


def _prep_kernel(*refs, shift_rows, has_vres):
    it = iter(refs)
    h_ref = next(it)
    if shift_rows:
        hp8_ref, hlast_ref = next(it), next(it)
    else:
        hprev_ref = next(it)
    rkv_ref = next(it)
    if shift_rows:
        rp8_ref, rlast_ref = next(it), next(it)
    else:
        rprev_ref = next(it)
    if has_vres:
        vfirst_ref = next(it)
    mu_ref, murkv_ref, w0_ref, dw1_ref, dw2_ref, a0_ref, aw1_ref, aw2_ref, gw1_ref, gw2_ref = (next(it) for _ in range(10))
    if has_vres:
        vmu_ref, v0_ref, vw1_ref, vw2_ref = (next(it) for _ in range(4))
    kk_ref, ka_ref = next(it), next(it)
    r_out, lw_out, k_out, v_out, kk_out, b_out, g_out = (next(it) for _ in range(7))

    h = h_ref[0]
    rkv0 = rkv_ref[0]
    tm = h.shape[0]
    if shift_rows:
        first = pl.program_id(1) == 0
        row0 = lax.broadcasted_iota(jnp.int32, (tm, 1), 0) == 0
        h_edge = jnp.where(first, hlast_ref[0], hp8_ref[0, SUBLANES - 1:SUBLANES, :])
        r_edge = jnp.where(first, rlast_ref[0], rp8_ref[0, SUBLANES - 1:SUBLANES, :])
        hprev = jnp.where(row0, h_edge, pltpu.roll(h, 1, 0))
        rprev = jnp.where(row0, r_edge, pltpu.roll(rkv0, 1, 0))
    else:
        hprev = hprev_ref[0]
        rprev = rprev_ref[0]

    dh = hprev - h
    mu = mu_ref[0]
    xw = h + dh * mu[0:1]
    xa = h + dh * mu[1:2]
    xg = h + dh * mu[2:3]

    z = w0_ref[0] + _dot(jnp.tanh(_dot(xw, dw1_ref[0])), dw2_ref[0])
    softplus = jnp.maximum(-z, 0.0) + jnp.log(1.0 + jnp.exp(-jnp.abs(z)))
    lw_out[0] = -jnp.exp(-softplus - 0.5)

    a = _sigmoid(a0_ref[0] + _dot(_dot(xa, aw1_ref[0]), aw2_ref[0]))
    g_out[0] = _dot(_sigmoid(_dot(xg, gw1_ref[0])), gw2_ref[0])

    murkv = murkv_ref[0]
    DR = kk_ref.shape[-1]
    r0, k0, v0 = (rkv0[:, j * DR:(j + 1) * DR] for j in range(3))
    rp, kp, vp = (rprev[:, j * DR:(j + 1) * DR] for j in range(3))
    r_out[0] = r0 + (rp - r0) * murkv[0:1]
    kr = k0 + (kp - k0) * murkv[1:2]
    vr = v0 + (vp - v0) * murkv[2:3]
    if has_vres:
        xv = h + dh * vmu_ref[0]
        vgate = _sigmoid(v0_ref[0] + _dot(_dot(xv, vw1_ref[0]), vw2_ref[0]))
        vr = vr + (vfirst_ref[0] - vr) * vgate
    v_out[0] = vr
    kk = kr * kk_ref[0]
    kkn = kk / jnp.maximum(jnp.sqrt(_head_sums(kk * kk)), 1e-12)
    kk_out[0] = kkn
    b_out[0] = kkn * a
    k_out[0] = kr * (1.0 + (a - 1.0) * ka_ref[0])


def _rwkv_prep(h, h_prev, rkv0, rkv_prev, v_first, P, l, *, tm=256):
    B, T, D = h.shape
    DR = rkv0.shape[2] // 3
    tm = min(tm, T)
    shift_rows = h_prev.shape[1] == 1 and T > 1
    has_vres = v_first is not None

    def tile(C):
        return pl.BlockSpec((1, tm, C), lambda b, i: (b, i, 0))

    def prev8(C):
        return pl.BlockSpec((1, SUBLANES, C), lambda b, i: (b, jnp.maximum(i * (tm // SUBLANES) - 1, 0), 0))

    def seq_row(C):
        return pl.BlockSpec((1, 1, C), lambda b, i: (b, 0, 0))

    def layer(shape, ll=l):
        return pl.BlockSpec((1,) + shape, lambda b, i: (ll,) + (0,) * len(shape))

    args, specs = [h], [tile(D)]
    if shift_rows:
        args += [h, h_prev]
        specs += [prev8(D), seq_row(D)]
    else:
        args += [jnp.broadcast_to(h_prev, h.shape)]
        specs += [tile(D)]
    args.append(rkv0)
    specs.append(tile(3 * DR))
    if shift_rows:
        args += [rkv0, rkv_prev]
        specs += [prev8(3 * DR), seq_row(3 * DR)]
    else:
        args += [jnp.broadcast_to(rkv_prev, rkv0.shape)]
        specs += [tile(3 * DR)]
    if has_vres:
        args.append(v_first)
        specs.append(tile(DR))
    r1 = lambda a: a.reshape(a.shape[0], 1, a.shape[-1])
    for name in ('mu_wag', 'mu_rkv'):
        args.append(P[name]); specs.append(layer(P[name].shape[1:]))
    args.append(r1(P['decay_w0'])); specs.append(layer((1, DR)))
    for name in ('decay_w1', 'decay_w2'):
        args.append(P[name]); specs.append(layer(P[name].shape[1:]))
    args.append(r1(P['aaa_a0'])); specs.append(layer((1, DR)))
    for name in ('aaa_w1', 'aaa_w2', 'gate_w1', 'gate_w2'):
        args.append(P[name]); specs.append(layer(P[name].shape[1:]))
    if has_vres:
        args.append(r1(P['vres_mu'])); specs.append(layer((1, D), l - 1))
        args.append(r1(P['vres_v0'])); specs.append(layer((1, DR), l - 1))
        for name in ('vres_w1', 'vres_w2'):
            args.append(P[name]); specs.append(layer(P[name].shape[1:], l - 1))
    args.append(r1(P['k_k'])); specs.append(layer((1, DR)))
    args.append(r1(P['k_a'])); specs.append(layer((1, DR)))
    nbytes = 2 * tm * 4 * (2 * D + 7 * DR + 7 * DR + DR) + 8 * tm * D * 4 + 4 * D * 512 * 4
    return pl.pallas_call(
        functools.partial(_prep_kernel, shift_rows=shift_rows, has_vres=has_vres),
        grid=(B, T // tm),
        in_specs=specs,
        out_specs=[tile(DR) for _ in range(7)],
        out_shape=[jax.ShapeDtypeStruct((B, T, DR), F32) for _ in range(7)],
        compiler_params=_params(("arbitrary", "arbitrary"), nbytes),
        name="rwkv_prep",
    )(*args)


def _prefix_sum_rows(x):
    n = x.shape[0]
    row = lax.broadcasted_iota(jnp.int32, (n, 1), 0)
    s = 1
    while s < n:
        x = x + jnp.where(row >= s, pltpu.roll(x, s, 0), 0.0)
        s *= 2
    return x


def _unit_lower_inverse(a_strict, blk):
    n = a_strict.shape[0]
    ti = lax.broadcasted_iota(jnp.int32, (n, n), 0)
    si = lax.broadcasted_iota(jnp.int32, (n, n), 1)

    def lower_left(s):
        return ((ti // (2 * s)) == (si // (2 * s))) & ((ti % (2 * s)) >= s) & ((si % (2 * s)) < s)

    d = (ti == si).astype(F32) + jnp.where(lower_left(1), a_strict, 0.0)
    s = 2
    while s < blk:
        t = _dot(d, jnp.where(lower_left(s), a_strict, 0.0))
        yield
        d = d + _dot(t, d)
        yield
        s *= 2
    return d


def _round_robin(generators):
    results = [None] * len(generators)
    live = list(range(len(generators)))
    while live:
        for u in list(live):
            try:
                next(generators[u])
            except StopIteration as done:
                results[u] = done.value
                live.remove(u)
    return results


def _pair_chunk_terms(r, lw, k, v, kk, b, low, tri):
    C = r.shape[0]
    cum = _prefix_sum_rows(lw)
    g_in = jnp.exp(cum)
    g_inv = jnp.exp(-cum)
    g_end = g_in[C - 1:C]
    a_t = -kk * jnp.exp(cum - lw)
    r_t = r * g_in
    b_t = b * g_inv
    k_t = k * g_inv
    bg = b_t * g_end
    kg = k_t * g_end
    zc = jnp.zeros((C, LANES), F32)
    h0 = lambda x: jnp.where(low, x, 0.0)
    h1 = lambda x: jnp.where(low, 0.0, x)
    v0, v1 = h0(v), h1(v)
    ar = jnp.concatenate([a_t, r_t], axis=0)
    m0 = jnp.where(tri, _dot_nt(h0(ar), jnp.concatenate([b_t, k_t], axis=0)), 0.0)
    m1 = jnp.where(tri, _dot_nt(h1(ar), jnp.concatenate([k_t, b_t], axis=0)), 0.0)
    yield
    top0, bot0, top1, bot1 = m0[:C], m0[C:], m1[:C], m1[C:]
    stack2 = lambda x0, x1: jnp.concatenate([jnp.concatenate([x0, zc], axis=0),
                                             jnp.concatenate([zc, x1], axis=0)], axis=1)
    akv = _dot(stack2(top0, top1), jnp.concatenate([zc, v0, v1, zc], axis=0))
    a_sw = pltpu.roll(a_t, HEAD_DIM, 1)
    tinv = yield from _unit_lower_inverse(jnp.concatenate([h0(top0), h1(top1)], axis=0), C)
    x = _dot(tinv, akv + jnp.concatenate([h1(a_sw), h0(a_sw)], axis=0))
    yield
    z = jnp.concatenate([x[:C], v0, v1, x[C:]], axis=0)
    e = _dot(stack2(bot0, bot1), z)
    gh = _dot_tn(z, jnp.concatenate([h0(bg), h0(kg), h1(kg), h1(bg)], axis=0))
    yield
    y0 = jnp.where(low, e[:C], e[C:])
    r_eff = r_t + pltpu.roll(jnp.where(low, e[C:], e[:C]), HEAD_DIM, 1)
    h_mat = jnp.concatenate([h0(gh[:C]), h1(gh[C:])], axis=0)
    g_mat = jnp.concatenate([h0(gh[C:]), h1(gh[:C])], axis=0)
    return r_eff, y0, g_mat, h_mat, g_end


def _scan_kernel(r_ref, lw_ref, k_ref, v_ref, kk_ref, b_ref, g_ref, lng_ref, lnb_ref, rk_ref,
                 o_ref, s_ref, reff_s, y0_s, gm_s, hm_s, ge_s, *, chunk, unroll):
    C = chunk
    NS, T = r_ref.shape[:2]
    NC = T // C
    E = HEAD_DIM
    low = lax.broadcasted_iota(jnp.int32, (1, LANES), 1) < E
    ti = lax.broadcasted_iota(jnp.int32, (2 * C, 2 * C), 0)
    si = lax.broadcasted_iota(jnp.int32, (2 * C, 2 * C), 1)
    tri = (si % C) <= jnp.where(ti < C, ti - 1, ti - C)

    def phase1(i, _):
        sq = i // (NC // unroll)
        c0 = (i - sq * (NC // unroll)) * unroll
        cs = [c0 + u for u in range(unroll)]
        rows = [pl.ds(pl.multiple_of(c * C, C), C) for c in cs]
        loaded = [[ref[sq, rw, :] for ref in (r_ref, lw_ref, k_ref, v_ref, kk_ref, b_ref)] for rw in rows]
        terms = _round_robin([_pair_chunk_terms(*args, low, tri) for args in loaded])
        for c, rw, (r_eff, y0, g_mat, h_mat, g_end) in zip(cs, rows, terms):
            reff_s[sq, rw, :] = r_eff
            y0_s[sq, rw, :] = y0
            gm_s[sq, c] = g_mat
            hm_s[sq, c] = h_mat
            ge_s[sq, c] = jnp.broadcast_to(g_end, (SUBLANES, LANES))
        return 0

    lax.fori_loop(0, NS * (NC // unroll), phase1, 0)

    def phase2(c, states):
        rows = pl.ds(pl.multiple_of(c * C, C), C)
        ys = [_dot_nt(reff_s[sq, rows, :], S) + y0_s[sq, rows, :] for sq, S in enumerate(states)]
        new = tuple(S * ge_s[sq, c][0:1] + _dot(S, gm_s[sq, c]) + hm_s[sq, c] for sq, S in enumerate(states))
        for sq, y in enumerate(ys):
            y0_s[sq, rows, :] = y
        return new

    states = lax.fori_loop(0, NC, phase2, tuple(jnp.zeros((LANES, LANES), F32) for _ in range(NS)))
    for sq, S in enumerate(states):
        s_ref[sq, 0] = S[:E, :E]
        s_ref[sq, 1] = S[E:, E:]

    def finish(y, r, k, v, g):
        mu = _head_sums(y) * (1.0 / E)
        yield
        yc = y - mu
        var = _head_sums(yc * yc) * (1.0 / E)
        bonus = _head_sums(r * k * rk_ref[0])
        yield
        yn = yc * lax.rsqrt(var + GN_EPS) * lng_ref[0] + lnb_ref[0]
        return ((yn + bonus * v) * g).astype(o_ref.dtype)

    def phase3(i, _):
        sq = i // (NC // unroll)
        c0 = (i - sq * (NC // unroll)) * unroll
        rows = [pl.ds(pl.multiple_of((c0 + u) * C, C), C) for u in range(unroll)]
        loaded = [[y0_s[sq, rw, :]] + [ref[sq, rw, :] for ref in (r_ref, k_ref, v_ref, g_ref)] for rw in rows]
        outs = _round_robin([finish(*args) for args in loaded])
        for rw, out in zip(rows, outs):
            o_ref[sq, rw, :] = out
        return 0

    lax.fori_loop(0, NS * (NC // unroll), phase3, 0)


def _rwkv_scan(r, lw, k, v, kk, b, g, lnx_g, lnx_b, r_k, l):
    B, T, DR = r.shape
    H = DR // HEAD_DIM
    L = lnx_g.shape[0]
    C = min(SCAN_CHUNK, T)
    nc = T // C
    unroll = math.gcd(SCAN_UNROLL, nc)
    ns = math.gcd(SCAN_SEQS, B)
    assert T % C == 0
    seq = pl.BlockSpec((ns, T, LANES), lambda p, b_: (b_, 0, p))
    par = pl.BlockSpec((1, 1, LANES), lambda p, b_: (l, 0, p))
    scratch = [pltpu.VMEM((ns, T, LANES), F32), pltpu.VMEM((ns, T, LANES), F32),
               pltpu.VMEM((ns, nc, LANES, LANES), F32), pltpu.VMEM((ns, nc, LANES, LANES), F32),
               pltpu.VMEM((ns, nc, SUBLANES, LANES), F32)]
    nbytes = ns * (2 * 8 * T * LANES * 4 + 2 * T * LANES * 4 + 2 * nc * LANES * LANES * 4 + nc * SUBLANES * LANES * 4)
    return pl.pallas_call(
        functools.partial(_scan_kernel, chunk=C, unroll=unroll),
        grid=(DR // LANES, B // ns),
        in_specs=[seq] * 7 + [par] * 3,
        out_specs=[seq, pl.BlockSpec((ns, 2, HEAD_DIM, HEAD_DIM), lambda p, b_: (b_, p, 0, 0))],
        out_shape=[jax.ShapeDtypeStruct((B, T, DR), BF16), jax.ShapeDtypeStruct((B, H, HEAD_DIM, HEAD_DIM), F32)],
        scratch_shapes=scratch,
        compiler_params=_params(("arbitrary", "arbitrary"), nbytes),
        name="rwkv_scan",
    )(r, lw, k, v, kk, b, g, lnx_g.reshape(L, 1, DR), lnx_b.reshape(L, 1, DR), r_k.reshape(L, 1, DR))


def _step_kernel(s_ref, r_ref, lw_ref, k_ref, v_ref, kk_ref, b_ref, g_ref, lng_ref, lnb_ref, rk_ref,
                 o_ref, so_ref, *, n_heads):
    E = HEAD_DIM
    eye = (lax.broadcasted_iota(jnp.int32, (E, E), 0) == lax.broadcasted_iota(jnp.int32, (E, E), 1)).astype(F32)

    def head(hh):
        sl = slice(hh * E, (hh + 1) * E)
        S = s_ref[0, 0, hh]
        r, k, v, kk, b, g = (ref[0, :, sl] for ref in (r_ref, k_ref, v_ref, kk_ref, b_ref, g_ref))
        w = jnp.exp(lw_ref[0, :, sl])
        v_col = jnp.sum(eye * v, axis=-1, keepdims=True)
        sa = jnp.sum(S * (-kk), axis=-1, keepdims=True)
        bonus = jnp.sum(r * k * rk_ref[0, :, sl], axis=-1, keepdims=True)
        yield
        S = S * w + sa * b + v_col * k
        y_col = jnp.sum(S * r, axis=-1, keepdims=True)
        yield
        y = jnp.sum(eye * y_col, axis=0, keepdims=True)
        mu = jnp.mean(y, axis=-1, keepdims=True)
        yield
        yc = y - mu
        var = jnp.mean(yc * yc, axis=-1, keepdims=True)
        yield
        yn = yc * lax.rsqrt(var + GN_EPS) * lng_ref[0, :, sl] + lnb_ref[0, :, sl]
        return S, (yn + bonus * v) * g

    results = _round_robin([head(hh) for hh in range(n_heads)])
    for hh, (S, _) in enumerate(results):
        so_ref[0, hh] = S
    o_ref[0] = jnp.concatenate([o for _, o in results], axis=-1).astype(o_ref.dtype)


def _rwkv_step(state, r, lw, k, v, kk, b, g, lnx_g, lnx_b, r_k, l):
    DB, H = state.shape[1:3]
    DR = H * HEAD_DIM
    L = lnx_g.shape[0]
    vec = pl.BlockSpec((1, 1, DR), lambda b_: (b_, 0, 0))
    par = pl.BlockSpec((1, 1, DR), lambda b_: (l, 0, 0))
    st = pl.BlockSpec((1, H, HEAD_DIM, HEAD_DIM), lambda b_: (b_, 0, 0, 0))
    return pl.pallas_call(
        functools.partial(_step_kernel, n_heads=H),
        grid=(DB,),
        in_specs=[pl.BlockSpec((1, 1, H, HEAD_DIM, HEAD_DIM), lambda b_: (l, b_, 0, 0, 0))] + [vec] * 7 + [par] * 3,
        out_specs=[vec, st],
        out_shape=[jax.ShapeDtypeStruct((DB, 1, DR), BF16), jax.ShapeDtypeStruct(state.shape[1:], F32)],
        compiler_params=_params(("arbitrary",), 4 * H * HEAD_DIM * LANES * 4),
        name="rwkv_step",
    )(state, r, lw, k, v, kk, b, g, lnx_g.reshape(L, 1, DR), lnx_b.reshape(L, 1, DR), r_k.reshape(L, 1, DR))


def _layer(P, l, xp, xs, mods_p, mods_s, base, vf_p, vf_s, k_stack, v_stack, cache_k, cache_v, state_wkv, h_last_s):
    B, T, D = xp.shape
    DB = xs.shape[1]
    DA = P['rel_bias'].shape[1] * HEAD_DIM
    n_rkv = P['w_in'].shape[2] - 3 * DA
    w_in = P['w_in']
    h, hb = _norm_mod(xp, P['norm1_g'], l, mods_p[0], mods_p[1], [F32, BF16])
    hs, hbs = _norm_mod(xs, P['norm1_g'], l, mods_s[0], mods_s[1], [F32, BF16])
    q, q_s = _proj(hb, hbs, w_in, l, 0, DA, head_g=P['q_norm_g'][l], scale=ATT_SCALE)
    k_stack, k_s = _proj(hb, hbs, w_in, l, DA, DA, head_g=P['k_norm_g'][l], stack=k_stack)
    v_stack, v_s = _proj(hb, hbs, w_in, l, 2 * DA, DA, stack=v_stack)
    both = jnp.concatenate([hbs, h_last_s[None].astype(BF16)], axis=1)
    rkv0, rkv2_s = _proj(hb, both, w_in, l, 3 * DA, n_rkv)

    att = _attention_prompt(q, k_stack, v_stack, l, base)
    r, lw, kr, vr, kk, b, g = _rwkv_prep(h, jnp.zeros((B, 1, D), F32), rkv0, jnp.zeros((B, 1, n_rkv), F32), vf_p, P, l)
    if vf_p is None:
        vf_p = vr
    rw, state_p = _rwkv_scan(r, lw, kr, vr, kk, b, g, P['lnx_g'], P['lnx_b'], P['r_k'], l)

    per_seq = lambda t: t.reshape(DB, 1, t.shape[-1])
    att_s = _attention_sample(per_seq(q_s), per_seq(k_s), per_seq(v_s), cache_k, cache_v, l, P['rel_bias'])
    r, lw, kr, vr, kk, b, g = _rwkv_prep(hs, h_last_s[None], rkv2_s[:, :DB], rkv2_s[:, DB:], vf_s, P, l)
    if vf_s is None:
        vf_s = vr
    rw_s, state_s = _rwkv_step(state_wkv, *(per_seq(t) for t in (r, lw, kr, vr, kk, b, g)),
                                P['lnx_g'], P['lnx_b'], P['r_k'], l)

    xp, xs = _proj_resid([att, rw], [att_s.reshape(1, DB, DA), rw_s.reshape(1, DB, DA)], P['w_out'], l,
                         xp, mods_p[2], xs, mods_s[2], tm=1024, tn=512)
    (h2,) = _norm_mod(xp, P['norm2_g'], l, mods_p[3], mods_p[4], [BF16])
    (h2s,) = _norm_mod(xs, P['norm2_g'], l, mods_s[3], mods_s[4], [BF16])
    act, act_s = _proj_swiglu(h2, h2s, P['w_gu'], l)
    xp, xs = _proj_resid([act], [act_s], P['w_down'], l, xp, mods_p[5], xs, mods_s[5], tm=512, tn=512)
    return xp, xs, vf_p, vf_s, k_stack, v_stack, state_p, h[:, -1], k_s[0], v_s[0], state_s, hs[0]


def kernel(x_prompt, x_sample, c_prompt, c_sample, cache_k, cache_v, state_wkv, state_shift, rel_bias, ada_w, ada_b, norm1_g, norm2_g, w_in, q_norm_g, k_norm_g, mu_wag, mu_rkv, decay_w0, decay_w1, decay_w2, aaa_a0, aaa_w1, aaa_w2, gate_w1, gate_w2, vres_mu, vres_v0, vres_w1, vres_w2, k_k, k_a, r_k, lnx_g, lnx_b, w_out, w_gu, w_down):
    P = dict(rel_bias=rel_bias, norm1_g=norm1_g, norm2_g=norm2_g, w_in=w_in, q_norm_g=q_norm_g,
             k_norm_g=k_norm_g, mu_wag=mu_wag, mu_rkv=mu_rkv, decay_w0=decay_w0, decay_w1=decay_w1,
             decay_w2=decay_w2, aaa_a0=aaa_a0, aaa_w1=aaa_w1, aaa_w2=aaa_w2, gate_w1=gate_w1,
             gate_w2=gate_w2, vres_mu=vres_mu, vres_v0=vres_v0, vres_w1=vres_w1, vres_w2=vres_w2,
             k_k=k_k, k_a=k_a, r_k=r_k.reshape(r_k.shape[0], -1), lnx_g=lnx_g, lnx_b=lnx_b,
             w_out=w_out, w_gu=w_gu, w_down=w_down)
    L = ada_w.shape[0]
    B, T, D = x_prompt.shape
    DB = x_sample.shape[0]
    H_ATT = rel_bias.shape[1]
    assert x_sample.shape[1] == 1 and T <= W_MAX

    rows = -(-(B + DB) // SUBLANES) * SUBLANES
    c_all = jnp.concatenate([c_prompt, c_sample, jnp.zeros((rows - B - DB, D), F32)], axis=0)
    mod = _ada(c_all, ada_w, ada_b).reshape(L, rows, 6, D)
    base = _band_bias_rows(rel_bias)

    k_stack = jnp.zeros((L, B, T, H_ATT * HEAD_DIM), F32)
    v_stack = jnp.zeros((L, B, T, H_ATT * HEAD_DIM), F32)
    xp, xs, vf_p, vf_s = x_prompt, x_sample.reshape(1, DB, D), None, None
    ps, ph, sk, sv, ss, sh = [], [], [], [], [], []
    for l in range(L):
        mods_p = [mod[l, :B, j][:, None, :] for j in range(6)]
        mods_s = [mod[l, B:B + DB, j][None] for j in range(6)]
        (xp, xs, vf_p, vf_s, k_stack, v_stack, state_p, shift_p, k_s, v_s, state_s, shift_s) = _layer(
            P, l, xp, xs, mods_p, mods_s, base, vf_p, vf_s, k_stack, v_stack, cache_k, cache_v, state_wkv,
            state_shift[l])
        ps.append(state_p)
        ph.append(shift_p)
        sk.append(k_s.reshape(DB, 1, H_ATT, HEAD_DIM))
        sv.append(v_s.reshape(DB, 1, H_ATT, HEAD_DIM))
        ss.append(state_s)
        sh.append(shift_s)

    return (xp, xs.reshape(DB, 1, D), k_stack.reshape(L, B, T, H_ATT, HEAD_DIM),
            v_stack.reshape(L, B, T, H_ATT, HEAD_DIM), jnp.stack(ps), jnp.stack(ph),
            jnp.stack(sk), jnp.stack(sv), jnp.stack(ss), jnp.stack(sh))
```

```python
import functools
import math

import jax
import jax.numpy as jnp
from jax import lax
from jax.experimental import pallas as pl
from jax.experimental.pallas import tpu as pltpu

F32 = jnp.float32
BF16 = jnp.bfloat16

HEAD_DIM = 64
BRANCHES = ((128, 1), (512, 4), (2048, 16))
W_MAX = 2048
NUM_BUCKETS = 32
MAX_DISTANCE = W_MAX
ATT_SCALE = HEAD_DIM ** -0.5
RMS_EPS = 1e-6
GN_EPS = 64e-5
NEG = -1e30

LANES = 128
SUBLANES = 8
Q_BLOCK = 128
SCAN_CHUNK = 64
SCAN_SEQS = 2
SCAN_UNROLL = 16
ATT_UNROLL = 4
VMEM_CAP = 56 * 1024 * 1024

assert all(w // dil == Q_BLOCK for w, dil in BRANCHES) and 2 * HEAD_DIM == LANES


def _vmem(nbytes):
    return int(min(VMEM_CAP, nbytes * 1.3 + (6 << 20)))


def _params(sem, nbytes):
    return pltpu.CompilerParams(dimension_semantics=sem, vmem_limit_bytes=_vmem(nbytes))


def _dot(a, b):
    return jnp.dot(a.astype(BF16), b.astype(BF16), preferred_element_type=F32)


def _dot_nt(a, b):
    return lax.dot_general(a.astype(BF16), b.astype(BF16), (((1,), (1,)), ((), ())), preferred_element_type=F32)


def _dot_tn(a, b):
    return lax.dot_general(a.astype(BF16), b.astype(BF16), (((0,), (0,)), ((), ())), preferred_element_type=F32)


def _sigmoid(x):
    return 1.0 / (1.0 + jnp.exp(-x))


def _head_sums(x):
    r = lax.broadcasted_iota(jnp.int32, (LANES, LANES), 0) // HEAD_DIM
    c = lax.broadcasted_iota(jnp.int32, (LANES, LANES), 1) // HEAD_DIM
    bd = (r == c).astype(BF16)
    cols = [_dot(x[:, j * LANES:(j + 1) * LANES], bd) for j in range(x.shape[1] // LANES)]
    return cols[0] if len(cols) == 1 else jnp.concatenate(cols, axis=-1)


def _ada_kernel(c_ref, w_ref, b_ref, o_ref):
    c = c_ref[...]
    s = c * _sigmoid(c)
    o_ref[0] = _dot(s, w_ref[0]) + b_ref[0]


def _ada(c_all, ada_w, ada_b, tn=1024):
    L, D, N = ada_w.shape
    R = c_all.shape[0]
    return pl.pallas_call(
        _ada_kernel,
        grid=(L, N // tn),
        in_specs=[pl.BlockSpec((R, D), lambda l, j: (0, 0)),
                  pl.BlockSpec((1, D, tn), lambda l, j: (l, 0, j)),
                  pl.BlockSpec((1, 1, tn), lambda l, j: (l, 0, j))],
        out_specs=pl.BlockSpec((1, R, tn), lambda l, j: (l, 0, j)),
        out_shape=jax.ShapeDtypeStruct((L, R, N), F32),
        compiler_params=_params(("arbitrary", "arbitrary"), 2 * D * tn * 4 + D * tn * 2),
        name="ada_mod",
    )(c_all, ada_w, ada_b.reshape(L, 1, N))


def _norm_kernel(x_ref, g_ref, sh_ref, sc_ref, *out_refs):
    x = x_ref[0]
    y = x * lax.rsqrt(jnp.mean(x * x, axis=-1, keepdims=True) + RMS_EPS) * g_ref[0]
    h = y * (1.0 + sc_ref[0]) + sh_ref[0]
    for o in out_refs:
        o[0] = h.astype(o.dtype)


def _row_spec(arr, tm, T):
    C = arr.shape[-1]
    if arr.shape[1] == 1:
        return pl.BlockSpec((1, 1, C), lambda b, i: (b, 0, 0))
    assert arr.shape[1] == T
    return pl.BlockSpec((1, tm, C), lambda b, i: (b, i, 0))


def _norm_mod(x, g_stack, l, shift, scale, out_dtypes, tm=512):
    B, T, D = x.shape
    tm = min(tm, T)
    L = g_stack.shape[0]
    outs = pl.pallas_call(
        _norm_kernel,
        grid=(B, T // tm),
        in_specs=[pl.BlockSpec((1, tm, D), lambda b, i: (b, i, 0)),
                  pl.BlockSpec((1, 1, D), lambda b, i: (l, 0, 0)),
                  _row_spec(shift, tm, T), _row_spec(scale, tm, T)],
        out_specs=[pl.BlockSpec((1, tm, D), lambda b, i: (b, i, 0)) for _ in out_dtypes],
        out_shape=[jax.ShapeDtypeStruct((B, T, D), dt) for dt in out_dtypes],
        compiler_params=_params(("arbitrary", "arbitrary"), 2 * tm * D * 4 * (2 + len(out_dtypes))),
        name="norm_mod",
    )(x, g_stack.reshape(L, 1, D), shift, scale)
    return outs


def _first_inner_step():
    return (pl.program_id(1) == 0) & (pl.program_id(2) == 0)


def _head_rmsnorm(acc, g):
    return acc * lax.rsqrt(_head_sums(acc * acc) * (1.0 / HEAD_DIM) + RMS_EPS) * g


def _rider_specs(R, k_sizes, tn):
    ins = [pl.BlockSpec((1, R, ks), lambda n, b, i: (0, 0, 0)) for ks in k_sizes]
    return ins, pl.BlockSpec((1, R, tn), lambda n, b, i: (0, 0, n))


def _mm_kernel(*refs, headnorm, scale, aliased):
    it = iter(refs)
    x_ref, w_ref = next(it), next(it)
    g_ref = next(it) if headnorm else None
    if aliased:
        next(it)
    xr_ref, o_ref, or_ref, wb_ref = next(it), next(it), next(it), next(it)

    def result(x):
        acc = jnp.dot(x, wb_ref[...], preferred_element_type=F32)
        if headnorm:
            acc = _head_rmsnorm(acc, g_ref[0])
            if scale != 1.0:
                acc = acc * scale
        return acc

    @pl.when(_first_inner_step())
    def _():
        wb_ref[...] = w_ref[0].astype(BF16)
        or_ref[0] = result(xr_ref[0])

    if len(o_ref.shape) == 4:
        o_ref[0, 0] = result(x_ref[0])
    else:
        o_ref[0] = result(x_ref[0])


def _proj(x, rider, w_stack, l, col0, ncols, *, head_g=None, scale=1.0, stack=None, tm=1024, tn=1024):
    B, T, K = x.shape
    R = rider.shape[1]
    tm = min(tm, T)
    tn = min(tn, ncols)
    assert T % tm == 0 and ncols % tn == 0 and col0 % tn == 0
    cb = col0 // tn
    in_specs = [pl.BlockSpec((1, tm, K), lambda n, b, i: (b, i, 0)),
                pl.BlockSpec((1, K, tn), lambda n, b, i: (l, 0, cb + n))]
    args = [x, w_stack]
    if head_g is not None:
        in_specs.append(pl.BlockSpec((1, 1, tn), lambda n, b, i: (0, 0, 0)))
        args.append(jnp.tile(head_g, tn // HEAD_DIM).reshape(1, 1, tn))
    aliases = {}
    if stack is None:
        out_spec = pl.BlockSpec((1, tm, tn), lambda n, b, i: (b, i, n))
        out_shape = jax.ShapeDtypeStruct((B, T, ncols), F32)
    else:
        out_spec = pl.BlockSpec((1, 1, tm, tn), lambda n, b, i: (l, b, i, n))
        out_shape = jax.ShapeDtypeStruct(stack.shape, F32)
        in_specs.append(pl.BlockSpec(memory_space=pl.ANY))
        args.append(stack)
        aliases = {len(args) - 1: 0}
    r_ins, r_out = _rider_specs(R, (K,), tn)
    nbytes = 2 * (tm * K * 2 + K * tn * 4 + tm * tn * 4) + K * tn * 2 + 2 * tm * tn * 4
    return pl.pallas_call(
        functools.partial(_mm_kernel, headnorm=head_g is not None, scale=scale, aliased=bool(aliases)),
        grid=(ncols // tn, B, T // tm),
        in_specs=in_specs + r_ins,
        out_specs=[out_spec, r_out],
        out_shape=[out_shape, jax.ShapeDtypeStruct((1, R, ncols), F32)],
        scratch_shapes=[pltpu.VMEM((K, tn), BF16)],
        input_output_aliases=aliases,
        compiler_params=_params(("arbitrary",) * 3, nbytes),
        name="proj",
    )(*args, rider)


def _swiglu_kernel(x_ref, wg_ref, wu_ref, xr_ref, o_ref, or_ref, wgb_ref, wub_ref):
    def result(x):
        gate = jnp.dot(x, wgb_ref[...], preferred_element_type=F32)
        up = jnp.dot(x, wub_ref[...], preferred_element_type=F32)
        return (gate * _sigmoid(gate) * up).astype(o_ref.dtype)

    @pl.when(_first_inner_step())
    def _():
        wgb_ref[...] = wg_ref[0].astype(BF16)
        wub_ref[...] = wu_ref[0].astype(BF16)
        or_ref[0] = result(xr_ref[0])

    o_ref[0] = result(x_ref[0])


def _proj_swiglu(x, rider, w_gu, l, *, tm=1024, tn=512):
    B, T, K = x.shape
    R = rider.shape[1]
    F = w_gu.shape[2] // 2
    tm = min(tm, T)
    assert T % tm == 0 and F % tn == 0
    nb = F // tn
    r_ins, r_out = _rider_specs(R, (K,), tn)
    nbytes = 2 * (tm * K * 2 + 2 * K * tn * 4 + tm * tn * 2) + 2 * K * tn * 2 + 3 * tm * tn * 4
    return pl.pallas_call(
        _swiglu_kernel,
        grid=(nb, B, T // tm),
        in_specs=[pl.BlockSpec((1, tm, K), lambda n, b, i: (b, i, 0)),
                  pl.BlockSpec((1, K, tn), lambda n, b, i: (l, 0, n)),
                  pl.BlockSpec((1, K, tn), lambda n, b, i: (l, 0, nb + n))] + r_ins,
        out_specs=[pl.BlockSpec((1, tm, tn), lambda n, b, i: (b, i, n)), r_out],
        out_shape=[jax.ShapeDtypeStruct((B, T, F), BF16), jax.ShapeDtypeStruct((1, R, F), BF16)],
        scratch_shapes=[pltpu.VMEM((K, tn), BF16), pltpu.VMEM((K, tn), BF16)],
        compiler_params=_params(("arbitrary",) * 3, nbytes),
        name="proj_swiglu",
    )(x, w_gu, w_gu, rider)


def _resid_kernel(*refs, k_sizes):
    n_x = len(k_sizes)
    x_refs, refs = refs[:n_x], refs[n_x:]
    w_ref, res_ref, gate_ref = refs[:3]
    xr_refs, refs = refs[3:3 + n_x], refs[3 + n_x:]
    resr_ref, gater_ref, o_ref, or_ref, wb_ref = refs

    def result(xs, res, gate):
        acc = None
        k0 = 0
        for x_ref, ks in zip(xs, k_sizes):
            part = jnp.dot(x_ref[0], wb_ref[k0:k0 + ks, :], preferred_element_type=F32)
            acc = part if acc is None else acc + part
            k0 += ks
        return res[0] + gate[0] * acc

    @pl.when(_first_inner_step())
    def _():
        wb_ref[...] = w_ref[0].astype(BF16)
        or_ref[0] = result(xr_refs, resr_ref, gater_ref)

    o_ref[0] = result(x_refs, res_ref, gate_ref)


def _proj_resid(xs, rider_xs, w_stack, l, resid, gate, rider_resid, rider_gate, *, tm=512, tn=512):
    B, T, N = resid.shape
    R = rider_resid.shape[1]
    k_sizes = tuple(x.shape[2] for x in xs)
    K = sum(k_sizes)
    tm = min(tm, T)
    assert T % tm == 0 and N % tn == 0 and gate.shape[1] == 1
    in_specs = [pl.BlockSpec((1, tm, ks), lambda n, b, i: (b, i, 0)) for ks in k_sizes]
    in_specs += [pl.BlockSpec((1, K, tn), lambda n, b, i: (l, 0, n)),
                 pl.BlockSpec((1, tm, tn), lambda n, b, i: (b, i, n)),
                 pl.BlockSpec((1, 1, tn), lambda n, b, i: (b, 0, n))]
    r_ins, r_out = _rider_specs(R, k_sizes, tn)
    nbytes = 2 * (tm * K * 2 + K * tn * 4 + 2 * tm * tn * 4) + K * tn * 2 + 2 * tm * tn * 4
    return pl.pallas_call(
        functools.partial(_resid_kernel, k_sizes=k_sizes),
        grid=(N // tn, B, T // tm),
        in_specs=in_specs + r_ins + [r_out, r_out],
        out_specs=[pl.BlockSpec((1, tm, tn), lambda n, b, i: (b, i, n)), r_out],
        out_shape=[jax.ShapeDtypeStruct((B, T, N), F32), jax.ShapeDtypeStruct((1, R, N), F32)],
        scratch_shapes=[pltpu.VMEM((K, tn), BF16)],
        compiler_params=_params(("arbitrary",) * 3, nbytes),
        name="proj_resid",
    )(*xs, w_stack, resid, gate, *rider_xs, rider_resid, rider_gate)


def _rel_bucket(dist):
    max_exact = NUM_BUCKETS // 2
    d = jnp.maximum(dist, 0)
    df = jnp.maximum(d, 1).astype(F32)
    large = max_exact + (jnp.log(df / max_exact) / math.log(MAX_DISTANCE / max_exact)
                         * (NUM_BUCKETS - max_exact)).astype(jnp.int32)
    return jnp.where(d < max_exact, d, jnp.minimum(large, NUM_BUCKETS - 1))


def _band_bias_rows(rel_bias):
    dsub = Q_BLOCK - jnp.arange(2 * Q_BLOCK)
    rows = []
    for (w, dil) in BRANCHES:
        valid = (dsub >= 0) & (dsub <= w // dil)
        bias = rel_bias[_rel_bucket(jnp.maximum(dsub, 0) * dil)].astype(F32).T
        rows.append(jnp.where(valid[None], bias, NEG))
    return jnp.stack(rows, axis=1)


def _attn_kernel(q_ref, k_ref, v_ref, base_ref, o_ref, qs, ks, vs, os_, ms_, ds_, stage):
    T = q_ref.shape[2]
    QB = Q_BLOCK
    low = lax.broadcasted_iota(jnp.int32, (1, LANES), 1) < HEAD_DIM
    own = (low, jnp.logical_not(low))
    prev_cols = lax.broadcasted_iota(jnp.int32, (1, 2 * QB), 1) < QB

    assert len(BRANCHES) == 3 and BRANCHES[0][1] == 1
    prev_dil = 1
    for bi, (_, dil) in enumerate(BRANCHES):
        L, Lp, ratio = T // dil, T // prev_dil, dil // prev_dil
        ks[bi, 0:QB, :] = jnp.zeros((QB, LANES), BF16)
        vs[bi, 0:QB, :] = jnp.zeros((QB, LANES), BF16)
        for r in range(dil):
            rows = pl.ds((r % prev_dil) * Lp + r // prev_dil, L, stride=ratio)
            if bi < 2:
                q, k, v = (ref[0, 0, rows, :] for ref in (q_ref, k_ref, v_ref))
            else:
                q, k, v = (stage[a, rows, :] for a in range(3))
            if bi == 1:
                for a, x in enumerate((q, k, v)):
                    stage[a, r * L:(r + 1) * L, :] = x
            for hh in range(2):
                qs[hh, bi, r * L:(r + 1) * L, :] = jnp.where(own[hh], q, 0.0).astype(BF16)
            ks[bi, QB + r * L:QB + (r + 1) * L, :] = k.astype(BF16)
            vs[bi, QB + r * L:QB + (r + 1) * L, :] = v.astype(BF16)
        prev_dil = dil

    for bi, (_, dil) in enumerate(BRANCHES):
        nb = T // dil // QB
        tiles = [pltpu.roll(jnp.broadcast_to(base_ref[hh, bi:bi + 1, :], (QB, 2 * QB)), 0, 1, stride=1, stride_axis=0)
                 for hh in range(2)]

        def one_block(g, q0, q1, kw, vw, nb=nb, tiles=tiles):
            no_prev = jnp.logical_and(g % nb == 0, prev_cols)
            ss = [jnp.where(no_prev, NEG, lax.dot_general(q, kw, (((1,), (1,)), ((), ())),
                                                          preferred_element_type=F32) + tiles[hh])
                  for hh, q in enumerate((q0, q1))]
            yield
            ms = [jnp.max(s, axis=-1, keepdims=True) for s in ss]
            ps = [jnp.exp(s - m) for s, m in zip(ss, ms)]
            dens = [jnp.sum(p, axis=-1, keepdims=True) for p in ps]
            os2 = [jnp.dot(p.astype(BF16), vw, preferred_element_type=F32) for p in ps]
            yield
            pair = lambda x0, x1: jnp.where(low, jnp.broadcast_to(x0, (QB, LANES)), jnp.broadcast_to(x1, (QB, LANES)))
            return pair(*os2), pair(*ms), pair(*dens)

        def blocks(i, _, bi=bi, dil=dil, nb=nb):
            gs = [i * ATT_UNROLL + u for u in range(ATT_UNROLL)]
            ats = [pl.multiple_of(g * QB, QB) for g in gs]
            loaded = [(qs[0, bi, pl.ds(at, QB), :], qs[1, bi, pl.ds(at, QB), :],
                       ks[bi, pl.ds(at, 2 * QB), :], vs[bi, pl.ds(at, 2 * QB), :]) for at in ats]
            results = _round_robin([one_block(g, *ld) for g, ld in zip(gs, loaded)])
            for g, (acc, m, den) in zip(gs, results):
                r = g // nb
                tok = pl.ds((g - r * nb) * (QB * dil) + r, QB, stride=dil)
                os_[bi, tok, :] = acc
                ms_[bi, tok, :] = m
                ds_[bi, tok, :] = den
            return 0

        lax.fori_loop(0, T // QB // ATT_UNROLL, blocks, 0)

    def merge(i, _):
        rows = pl.ds(pl.multiple_of(i * QB, QB), QB)
        maxes = [ms_[bi, rows, :] for bi in range(len(BRANCHES))]
        m = functools.reduce(jnp.maximum, maxes)
        ws = [jnp.exp(x - m) for x in maxes]
        num = sum(w * os_[bi, rows, :] for bi, w in enumerate(ws))
        den = sum(w * ds_[bi, rows, :] for bi, w in enumerate(ws))
        o_ref[0, rows, :] = (num / den).astype(o_ref.dtype)
        return 0

    lax.fori_loop(0, T // QB, merge, 0)


def _attention_prompt(q, k_stack, v_stack, l, base):
    B, T, DA = q.shape
    nbr = len(BRANCHES)
    assert T % (Q_BLOCK * max(d for _, d in BRANCHES)) == 0
    qkv = pl.BlockSpec((1, 1, T, LANES), lambda p, b: (l, b, 0, p))
    scratch = [pltpu.VMEM((2, nbr, T, LANES), BF16), pltpu.VMEM((nbr, T + Q_BLOCK, LANES), BF16),
               pltpu.VMEM((nbr, T + Q_BLOCK, LANES), BF16), pltpu.VMEM((nbr, T, LANES), F32),
               pltpu.VMEM((nbr, T, LANES), F32), pltpu.VMEM((nbr, T, LANES), F32), pltpu.VMEM((3, T, LANES), F32)]
    nbytes = (2 * 3 * T * LANES * 4 + 4 * nbr * (T + Q_BLOCK) * LANES * 2 + 3 * nbr * T * LANES * 4
              + 3 * T * LANES * 4 + 2 * T * LANES * 2)
    return pl.pallas_call(
        _attn_kernel,
        grid=(DA // LANES, B),
        in_specs=[pl.BlockSpec((1, 1, T, LANES), lambda p, b: (0, b, 0, p)), qkv, qkv,
                  pl.BlockSpec((2, nbr, 2 * Q_BLOCK), lambda p, b: (p, 0, 0))],
        out_specs=pl.BlockSpec((1, T, LANES), lambda p, b: (b, 0, p)),
        out_shape=jax.ShapeDtypeStruct((B, T, DA), BF16),
        scratch_shapes=scratch,
        compiler_params=_params(("arbitrary", "arbitrary"), nbytes),
        name="attn_prompt",
    )(q[None], k_stack, v_stack, base)


def _distance_logits(rel_bias, nd):
    d = jnp.arange(nd)
    mult = sum(((d % dil == 0) & (d // dil <= w // dil)).astype(F32) for (w, dil) in BRANCHES)
    bias = rel_bias[_rel_bucket(d)].astype(F32).T
    return jnp.where(mult > 0, bias + jnp.log(jnp.maximum(mult, 1.0)), NEG)


def _sattn_kernel(q_ref, kn_ref, vn_ref, kt_ref, vt_ref, tab_ref, o_ref):
    row = lax.broadcasted_iota(jnp.int32, (SUBLANES, LANES), 0)
    low = lax.broadcasted_iota(jnp.int32, (SUBLANES, LANES), 1) < HEAD_DIM
    own = ((row == 0) & low) | ((row == 1) & jnp.logical_not(low))
    q = jnp.where(own, q_ref[0], 0.0).astype(BF16)
    kn = kn_ref[0].astype(BF16).astype(F32)
    vn = vn_ref[0].astype(BF16).astype(F32)
    tab = tab_ref[0]
    W = kt_ref.shape[3]
    s_self = jnp.sum(q.astype(F32) * kn, axis=-1, keepdims=True) + tab[:, W:W + 1]
    s = _dot(q, kt_ref[0, 0]) + tab[:, :W]
    m = jnp.maximum(s_self, jnp.max(s, axis=-1, keepdims=True))
    p_self = jnp.exp(s_self - m)
    p = jnp.exp(s - m)
    den = p_self + jnp.sum(p, axis=-1, keepdims=True)
    o = (p_self * vn + _dot_nt(p, vt_ref[0, 0])) / den
    o_ref[0] = jnp.where(low[0:1], o[0:1], o[1:2]).astype(o_ref.dtype)


def _attention_sample(q, k_new, v_new, cache_k, cache_v, l, rel_bias):
    L, DB, W, H, E = cache_k.shape
    DA = H * E
    assert W >= max(w for w, _ in BRANCHES)
    tab = _distance_logits(rel_bias, W + 1)[:, ::-1].reshape(H // 2, 2, W + 1)
    tab = jnp.concatenate([tab, jnp.zeros((H // 2, SUBLANES - 2, W + 1), F32)], axis=1)
    kt = jnp.transpose(cache_k, (0, 1, 3, 4, 2)).reshape(L, DB, DA, W)
    vt = jnp.transpose(cache_v, (0, 1, 3, 4, 2)).reshape(L, DB, DA, W)
    vec = pl.BlockSpec((1, 1, LANES), lambda p, b: (b, 0, p))
    buf = pl.BlockSpec((1, 1, LANES, W), lambda p, b: (l, b, p, 0))
    return pl.pallas_call(
        _sattn_kernel,
        grid=(DA // LANES, DB),
        in_specs=[vec, vec, vec, buf, buf, pl.BlockSpec((1, SUBLANES, W + 1), lambda p, b: (p, 0, 0))],
        out_specs=vec,
        out_shape=jax.ShapeDtypeStruct((DB, 1, DA), BF16),
        compiler_params=_params(("arbitrary", "arbitrary"), 2 * 2 * LANES * W * 4 + 4 * SUBLANES * W * 4),
        name="attn_sample",
    )(q, k_new, v_new, kt, vt, tab)


def _prep_kernel(*refs, shift_rows, has_vres):
    it = iter(refs)
    h_ref = next(it)
    if shift_rows:
        hp8_ref, hlast_ref = next(it), next(it)
    else:
        hprev_ref = next(it)
    rkv_ref = next(it)
    if shift_rows:
        rp8_ref, rlast_ref = next(it), next(it)
    else:
        rprev_ref = next(it)
    if has_vres:
        vfirst_ref = next(it)
    mu_ref, murkv_ref, w0_ref, dw1_ref, dw2_ref, a0_ref, aw1_ref, aw2_ref, gw1_ref, gw2_ref = (next(it) for _ in range(10))
    if has_vres:
        vmu_ref, v0_ref, vw1_ref, vw2_ref = (next(it) for _ in range(4))
    kk_ref, ka_ref = next(it), next(it)
    r_out, lw_out, k_out, v_out, kk_out, b_out, g_out = (next(it) for _ in range(7))

    h = h_ref[0]
    rkv0 = rkv_ref[0]
    tm = h.shape[0]
    if shift_rows:
        first = pl.program_id(1) == 0
        row0 = lax.broadcasted_iota(jnp.int32, (tm, 1), 0) == 0
        h_edge = jnp.where(first, hlast_ref[0], hp8_ref[0, SUBLANES - 1:SUBLANES, :])
        r_edge = jnp.where(first, rlast_ref[0], rp8_ref[0, SUBLANES - 1:SUBLANES, :])
        hprev = jnp.where(row0, h_edge, pltpu.roll(h, 1, 0))
        rprev = jnp.where(row0, r_edge, pltpu.roll(rkv0, 1, 0))
    else:
        hprev = hprev_ref[0]
        rprev = rprev_ref[0]

    dh = hprev - h
    mu = mu_ref[0]
    xw = h + dh * mu[0:1]
    xa = h + dh * mu[1:2]
    xg = h + dh * mu[2:3]

    z = w0_ref[0] + _dot(jnp.tanh(_dot(xw, dw1_ref[0])), dw2_ref[0])
    lw_out[0] = -math.exp(-0.5) * _sigmoid(z)

    a = _sigmoid(a0_ref[0] + _dot(_dot(xa, aw1_ref[0]), aw2_ref[0]))
    g_out[0] = _dot(_sigmoid(_dot(xg, gw1_ref[0])), gw2_ref[0])

    murkv = murkv_ref[0]
    DR = kk_ref.shape[-1]
    r0, k0, v0 = (rkv0[:, j * DR:(j + 1) * DR] for j in range(3))
    rp, kp, vp = (rprev[:, j * DR:(j + 1) * DR] for j in range(3))
    r_out[0] = r0 + (rp - r0) * murkv[0:1]
    kr = k0 + (kp - k0) * murkv[1:2]
    vr = v0 + (vp - v0) * murkv[2:3]
    if has_vres:
        xv = h + dh * vmu_ref[0]
        vgate = _sigmoid(v0_ref[0] + _dot(_dot(xv, vw1_ref[0]), vw2_ref[0]))
        vr = vr + (vfirst_ref[0] - vr) * vgate
    v_out[0] = vr
    kk = kr * kk_ref[0]
    kkn = kk * lax.rsqrt(jnp.maximum(_head_sums(kk * kk), 1e-24))
    kk_out[0] = kkn
    b_out[0] = kkn * a
    k_out[0] = kr * (1.0 + (a - 1.0) * ka_ref[0])


def _rwkv_prep(h, h_prev, rkv0, rkv_prev, v_first, P, l, *, tm=256):
    B, T, D = h.shape
    DR = rkv0.shape[2] // 3
    tm = min(tm, T)
    shift_rows = h_prev.shape[1] == 1 and T > 1
    has_vres = v_first is not None

    def tile(C):
        return pl.BlockSpec((1, tm, C), lambda b, i: (b, i, 0))

    def prev8(C):
        return pl.BlockSpec((1, SUBLANES, C), lambda b, i: (b, jnp.maximum(i * (tm // SUBLANES) - 1, 0), 0))

    def seq_row(C):
        return pl.BlockSpec((1, 1, C), lambda b, i: (b, 0, 0))

    def layer(shape, ll=l):
        return pl.BlockSpec((1,) + shape, lambda b, i: (ll,) + (0,) * len(shape))

    args, specs = [h], [tile(D)]
    if shift_rows:
        args += [h, h_prev]
        specs += [prev8(D), seq_row(D)]
    else:
        args += [jnp.broadcast_to(h_prev, h.shape)]
        specs += [tile(D)]
    args.append(rkv0)
    specs.append(tile(3 * DR))
    if shift_rows:
        args += [rkv0, rkv_prev]
        specs += [prev8(3 * DR), seq_row(3 * DR)]
    else:
        args += [jnp.broadcast_to(rkv_prev, rkv0.shape)]
        specs += [tile(3 * DR)]
    if has_vres:
        args.append(v_first)
        specs.append(tile(DR))
    r1 = lambda a: a.reshape(a.shape[0], 1, a.shape[-1])
    for name in ('mu_wag', 'mu_rkv'):
        args.append(P[name]); specs.append(layer(P[name].shape[1:]))
    args.append(r1(P['decay_w0'])); specs.append(layer((1, DR)))
    for name in ('decay_w1', 'decay_w2'):
        args.append(P[name]); specs.append(layer(P[name].shape[1:]))
    args.append(r1(P['aaa_a0'])); specs.append(layer((1, DR)))
    for name in ('aaa_w1', 'aaa_w2', 'gate_w1', 'gate_w2'):
        args.append(P[name]); specs.append(layer(P[name].shape[1:]))
    if has_vres:
        args.append(r1(P['vres_mu'])); specs.append(layer((1, D), l - 1))
        args.append(r1(P['vres_v0'])); specs.append(layer((1, DR), l - 1))
        for name in ('vres_w1', 'vres_w2'):
            args.append(P[name]); specs.append(layer(P[name].shape[1:], l - 1))
    args.append(r1(P['k_k'])); specs.append(layer((1, DR)))
    args.append(r1(P['k_a'])); specs.append(layer((1, DR)))
    nbytes = 2 * tm * 4 * (2 * D + 7 * DR + 7 * DR + DR) + 8 * tm * D * 4 + 4 * D * 512 * 4
    return pl.pallas_call(
        functools.partial(_prep_kernel, shift_rows=shift_rows, has_vres=has_vres),
        grid=(B, T // tm),
        in_specs=specs,
        out_specs=[tile(DR) for _ in range(7)],
        out_shape=[jax.ShapeDtypeStruct((B, T, DR), F32) for _ in range(7)],
        compiler_params=_params(("arbitrary", "arbitrary"), nbytes),
        name="rwkv_prep",
    )(*args)


def _prefix_sum_rows(x):
    n = x.shape[0]
    row = lax.broadcasted_iota(jnp.int32, (n, 1), 0)
    s = 1
    while s < n:
        x = x + jnp.where(row >= s, pltpu.roll(x, s, 0), 0.0)
        s *= 2
    return x


def _unit_lower_inverse(a_strict, blk):
    n = a_strict.shape[0]
    ti = lax.broadcasted_iota(jnp.int32, (n, n), 0)
    si = lax.broadcasted_iota(jnp.int32, (n, n), 1)

    def lower_left(s):
        return ((ti // (2 * s)) == (si // (2 * s))) & ((ti % (2 * s)) >= s) & ((si % (2 * s)) < s)

    d = (ti == si).astype(F32) + jnp.where(lower_left(1), a_strict, 0.0)
    s = 2
    while s < blk:
        t = _dot(d, jnp.where(lower_left(s), a_strict, 0.0))
        yield
        d = d + _dot(t, d)
        yield
        s *= 2
    return d


def _round_robin(generators):
    results = [None] * len(generators)
    live = list(range(len(generators)))
    while live:
        for u in list(live):
            try:
                next(generators[u])
            except StopIteration as done:
                results[u] = done.value
                live.remove(u)
    return results


def _pair_chunk_terms(r, lw, k, v, kk, b, low, tri):
    C = r.shape[0]
    cum = _prefix_sum_rows(lw)
    g_in = jnp.exp(cum)
    g_inv = jnp.exp(-cum)
    g_end = g_in[C - 1:C]
    a_t = -kk * jnp.exp(cum - lw)
    r_t = r * g_in
    b_t = b * g_inv
    k_t = k * g_inv
    bg = b_t * g_end
    kg = k_t * g_end
    zc = jnp.zeros((C, LANES), F32)
    h0 = lambda x: jnp.where(low, x, 0.0)
    h1 = lambda x: jnp.where(low, 0.0, x)
    v0, v1 = h0(v), h1(v)
    ar = jnp.concatenate([a_t, r_t], axis=0)
    m0 = jnp.where(tri, _dot_nt(h0(ar), jnp.concatenate([b_t, k_t], axis=0)), 0.0)
    m1 = jnp.where(tri, _dot_nt(h1(ar), jnp.concatenate([k_t, b_t], axis=0)), 0.0)
    yield
    top0, bot0, top1, bot1 = m0[:C], m0[C:], m1[:C], m1[C:]
    stack2 = lambda x0, x1: jnp.concatenate([jnp.concatenate([x0, zc], axis=0),
                                             jnp.concatenate([zc, x1], axis=0)], axis=1)
    akv = _dot(stack2(top0, top1), jnp.concatenate([zc, v0, v1, zc], axis=0))
    a_sw = pltpu.roll(a_t, HEAD_DIM, 1)
    tinv = yield from _unit_lower_inverse(jnp.concatenate([h0(top0), h1(top1)], axis=0), C)
    x = _dot(tinv, akv + jnp.concatenate([h1(a_sw), h0(a_sw)], axis=0))
    yield
    z = jnp.concatenate([x[:C], v0, v1, x[C:]], axis=0)
    e = _dot(stack2(bot0, bot1), z)
    gh = _dot_tn(z, jnp.concatenate([h0(bg), h0(kg), h1(kg), h1(bg)], axis=0))
    yield
    y0 = jnp.where(low, e[:C], e[C:])
    r_eff = r_t + pltpu.roll(jnp.where(low, e[C:], e[:C]), HEAD_DIM, 1)
    h_mat = jnp.concatenate([h0(gh[:C]), h1(gh[C:])], axis=0)
    g_mat = jnp.concatenate([h0(gh[C:]), h1(gh[:C])], axis=0)
    return r_eff, y0, g_mat, h_mat, g_end


def _scan_kernel(r_ref, lw_ref, k_ref, v_ref, kk_ref, b_ref, g_ref, lng_ref, lnb_ref, rk_ref,
                 o_ref, s_ref, reff_s, y0_s, gm_s, hm_s, ge_s, *, chunk, unroll):
    C = chunk
    NS, T = r_ref.shape[:2]
    NC = T // C
    E = HEAD_DIM
    low = lax.broadcasted_iota(jnp.int32, (1, LANES), 1) < E
    ti = lax.broadcasted_iota(jnp.int32, (2 * C, 2 * C), 0)
    si = lax.broadcasted_iota(jnp.int32, (2 * C, 2 * C), 1)
    tri = (si % C) <= jnp.where(ti < C, ti - 1, ti - C)

    def phase1(i, _):
        sq = i // (NC // unroll)
        c0 = (i - sq * (NC // unroll)) * unroll
        cs = [c0 + u for u in range(unroll)]
        rows = [pl.ds(pl.multiple_of(c * C, C), C) for c in cs]
        loaded = [[ref[sq, rw, :] for ref in (r_ref, lw_ref, k_ref, v_ref, kk_ref, b_ref)] for rw in rows]
        terms = _round_robin([_pair_chunk_terms(*args, low, tri) for args in loaded])
        for c, rw, (r_eff, y0, g_mat, h_mat, g_end) in zip(cs, rows, terms):
            reff_s[sq, rw, :] = r_eff
            y0_s[sq, rw, :] = y0
            gm_s[sq, c] = g_mat
            hm_s[sq, c] = h_mat
            ge_s[sq, c] = jnp.broadcast_to(g_end, (SUBLANES, LANES))
        return 0

    lax.fori_loop(0, NS * (NC // unroll), phase1, 0)

    def phase2(c, states):
        rows = pl.ds(pl.multiple_of(c * C, C), C)
        ys = [_dot_nt(reff_s[sq, rows, :], S) + y0_s[sq, rows, :] for sq, S in enumerate(states)]
        new = tuple(S * ge_s[sq, c][0:1] + _dot(S, gm_s[sq, c]) + hm_s[sq, c] for sq, S in enumerate(states))
        for sq, y in enumerate(ys):
            y0_s[sq, rows, :] = y
        return new

    states = lax.fori_loop(0, NC, phase2, tuple(jnp.zeros((LANES, LANES), F32) for _ in range(NS)))
    for sq, S in enumerate(states):
        s_ref[sq, 0] = S[:E, :E]
        s_ref[sq, 1] = S[E:, E:]

    def finish(y, r, k, v, g):
        mu = _head_sums(y) * (1.0 / E)
        yield
        yc = y - mu
        var = _head_sums(yc * yc) * (1.0 / E)
        bonus = _head_sums(r * k * rk_ref[0])
        yield
        yn = yc * lax.rsqrt(var + GN_EPS) * lng_ref[0] + lnb_ref[0]
        return ((yn + bonus * v) * g).astype(o_ref.dtype)

    def phase3(i, _):
        sq = i // (NC // unroll)
        c0 = (i - sq * (NC // unroll)) * unroll
        rows = [pl.ds(pl.multiple_of((c0 + u) * C, C), C) for u in range(unroll)]
        loaded = [[y0_s[sq, rw, :]] + [ref[sq, rw, :] for ref in (r_ref, k_ref, v_ref, g_ref)] for rw in rows]
        outs = _round_robin([finish(*args) for args in loaded])
        for rw, out in zip(rows, outs):
            o_ref[sq, rw, :] = out
        return 0

    lax.fori_loop(0, NS * (NC // unroll), phase3, 0)


def _rwkv_scan(r, lw, k, v, kk, b, g, lnx_g, lnx_b, r_k, l):
    B, T, DR = r.shape
    H = DR // HEAD_DIM
    L = lnx_g.shape[0]
    C = min(SCAN_CHUNK, T)
    nc = T // C
    unroll = math.gcd(SCAN_UNROLL, nc)
    ns = math.gcd(SCAN_SEQS, B)
    assert T % C == 0
    seq = pl.BlockSpec((ns, T, LANES), lambda p, b_: (b_, 0, p))
    par = pl.BlockSpec((1, 1, LANES), lambda p, b_: (l, 0, p))
    scratch = [pltpu.VMEM((ns, T, LANES), F32), pltpu.VMEM((ns, T, LANES), F32),
               pltpu.VMEM((ns, nc, LANES, LANES), F32), pltpu.VMEM((ns, nc, LANES, LANES), F32),
               pltpu.VMEM((ns, nc, SUBLANES, LANES), F32)]
    nbytes = ns * (2 * 8 * T * LANES * 4 + 2 * T * LANES * 4 + 2 * nc * LANES * LANES * 4 + nc * SUBLANES * LANES * 4)
    return pl.pallas_call(
        functools.partial(_scan_kernel, chunk=C, unroll=unroll),
        grid=(DR // LANES, B // ns),
        in_specs=[seq] * 7 + [par] * 3,
        out_specs=[seq, pl.BlockSpec((ns, 2, HEAD_DIM, HEAD_DIM), lambda p, b_: (b_, p, 0, 0))],
        out_shape=[jax.ShapeDtypeStruct((B, T, DR), BF16), jax.ShapeDtypeStruct((B, H, HEAD_DIM, HEAD_DIM), F32)],
        scratch_shapes=scratch,
        compiler_params=_params(("arbitrary", "arbitrary"), nbytes),
        name="rwkv_scan",
    )(r, lw, k, v, kk, b, g, lnx_g.reshape(L, 1, DR), lnx_b.reshape(L, 1, DR), r_k.reshape(L, 1, DR))


def _step_kernel(s_ref, r_ref, lw_ref, k_ref, v_ref, kk_ref, b_ref, g_ref, lng_ref, lnb_ref, rk_ref,
                 o_ref, so_ref, *, n_heads):
    E = HEAD_DIM
    eye = (lax.broadcasted_iota(jnp.int32, (E, E), 0) == lax.broadcasted_iota(jnp.int32, (E, E), 1)).astype(F32)

    def head(hh):
        sl = slice(hh * E, (hh + 1) * E)
        S = s_ref[0, 0, hh]
        r, k, v, kk, b, g = (ref[0, :, sl] for ref in (r_ref, k_ref, v_ref, kk_ref, b_ref, g_ref))
        w = jnp.exp(lw_ref[0, :, sl])
        v_col = jnp.sum(eye * v, axis=-1, keepdims=True)
        sa = jnp.sum(S * (-kk), axis=-1, keepdims=True)
        bonus = jnp.sum(r * k * rk_ref[0, :, sl], axis=-1, keepdims=True)
        yield
        S = S * w + sa * b + v_col * k
        y_col = jnp.sum(S * r, axis=-1, keepdims=True)
        yield
        y = jnp.sum(eye * y_col, axis=0, keepdims=True)
        mu = jnp.mean(y, axis=-1, keepdims=True)
        yield
        yc = y - mu
        var = jnp.mean(yc * yc, axis=-1, keepdims=True)
        yield
        yn = yc * lax.rsqrt(var + GN_EPS) * lng_ref[0, :, sl] + lnb_ref[0, :, sl]
        return S, (yn + bonus * v) * g

    results = _round_robin([head(hh) for hh in range(n_heads)])
    for hh, (S, _) in enumerate(results):
        so_ref[0, hh] = S
    o_ref[0] = jnp.concatenate([o for _, o in results], axis=-1).astype(o_ref.dtype)


def _rwkv_step(state, r, lw, k, v, kk, b, g, lnx_g, lnx_b, r_k, l):
    DB, H = state.shape[1:3]
    DR = H * HEAD_DIM
    L = lnx_g.shape[0]
    vec = pl.BlockSpec((1, 1, DR), lambda b_: (b_, 0, 0))
    par = pl.BlockSpec((1, 1, DR), lambda b_: (l, 0, 0))
    st = pl.BlockSpec((1, H, HEAD_DIM, HEAD_DIM), lambda b_: (b_, 0, 0, 0))
    return pl.pallas_call(
        functools.partial(_step_kernel, n_heads=H),
        grid=(DB,),
        in_specs=[pl.BlockSpec((1, 1, H, HEAD_DIM, HEAD_DIM), lambda b_: (l, b_, 0, 0, 0))] + [vec] * 7 + [par] * 3,
        out_specs=[vec, st],
        out_shape=[jax.ShapeDtypeStruct((DB, 1, DR), BF16), jax.ShapeDtypeStruct(state.shape[1:], F32)],
        compiler_params=_params(("arbitrary",), 4 * H * HEAD_DIM * LANES * 4),
        name="rwkv_step",
    )(state, r, lw, k, v, kk, b, g, lnx_g.reshape(L, 1, DR), lnx_b.reshape(L, 1, DR), r_k.reshape(L, 1, DR))


def _layer(P, l, xp, xs, mods_p, mods_s, base, vf_p, vf_s, k_stack, v_stack, cache_k, cache_v, state_wkv, h_last_s):
    B, T, D = xp.shape
    DB = xs.shape[1]
    DA = P['rel_bias'].shape[1] * HEAD_DIM
    n_rkv = P['w_in'].shape[2] - 3 * DA
    w_in = P['w_in']
    h, hb = _norm_mod(xp, P['norm1_g'], l, mods_p[0], mods_p[1], [F32, BF16])
    hs, hbs = _norm_mod(xs, P['norm1_g'], l, mods_s[0], mods_s[1], [F32, BF16])
    q, q_s = _proj(hb, hbs, w_in, l, 0, DA, head_g=P['q_norm_g'][l], scale=ATT_SCALE)
    k_stack, k_s = _proj(hb, hbs, w_in, l, DA, DA, head_g=P['k_norm_g'][l], stack=k_stack)
    v_stack, v_s = _proj(hb, hbs, w_in, l, 2 * DA, DA, stack=v_stack)
    both = jnp.concatenate([hbs, h_last_s[None].astype(BF16)], axis=1)
    rkv0, rkv2_s = _proj(hb, both, w_in, l, 3 * DA, n_rkv)

    att = _attention_prompt(q, k_stack, v_stack, l, base)
    r, lw, kr, vr, kk, b, g = _rwkv_prep(h, jnp.zeros((B, 1, D), F32), rkv0, jnp.zeros((B, 1, n_rkv), F32), vf_p, P, l)
    if vf_p is None:
        vf_p = vr
    rw, state_p = _rwkv_scan(r, lw, kr, vr, kk, b, g, P['lnx_g'], P['lnx_b'], P['r_k'], l)

    per_seq = lambda t: t.reshape(DB, 1, t.shape[-1])
    att_s = _attention_sample(per_seq(q_s), per_seq(k_s), per_seq(v_s), cache_k, cache_v, l, P['rel_bias'])
    r, lw, kr, vr, kk, b, g = _rwkv_prep(hs, h_last_s[None], rkv2_s[:, :DB], rkv2_s[:, DB:], vf_s, P, l)
    if vf_s is None:
        vf_s = vr
    rw_s, state_s = _rwkv_step(state_wkv, *(per_seq(t) for t in (r, lw, kr, vr, kk, b, g)),
                                P['lnx_g'], P['lnx_b'], P['r_k'], l)

    xp, xs = _proj_resid([att, rw], [att_s.reshape(1, DB, DA), rw_s.reshape(1, DB, DA)], P['w_out'], l,
                         xp, mods_p[2], xs, mods_s[2], tm=1024, tn=1024)
    (h2,) = _norm_mod(xp, P['norm2_g'], l, mods_p[3], mods_p[4], [BF16])
    (h2s,) = _norm_mod(xs, P['norm2_g'], l, mods_s[3], mods_s[4], [BF16])
    act, act_s = _proj_swiglu(h2, h2s, P['w_gu'], l)
    xp, xs = _proj_resid([act], [act_s], P['w_down'], l, xp, mods_p[5], xs, mods_s[5], tm=512, tn=512)
    return xp, xs, vf_p, vf_s, k_stack, v_stack, state_p, h[:, -1], k_s[0], v_s[0], state_s, hs[0]


def kernel(x_prompt, x_sample, c_prompt, c_sample, cache_k, cache_v, state_wkv, state_shift, rel_bias, ada_w, ada_b, norm1_g, norm2_g, w_in, q_norm_g, k_norm_g, mu_wag, mu_rkv, decay_w0, decay_w1, decay_w2, aaa_a0, aaa_w1, aaa_w2, gate_w1, gate_w2, vres_mu, vres_v0, vres_w1, vres_w2, k_k, k_a, r_k, lnx_g, lnx_b, w_out, w_gu, w_down):
    P = dict(rel_bias=rel_bias, norm1_g=norm1_g, norm2_g=norm2_g, w_in=w_in, q_norm_g=q_norm_g,
             k_norm_g=k_norm_g, mu_wag=mu_wag, mu_rkv=mu_rkv, decay_w0=decay_w0, decay_w1=decay_w1,
             decay_w2=decay_w2, aaa_a0=aaa_a0, aaa_w1=aaa_w1, aaa_w2=aaa_w2, gate_w1=gate_w1,
             gate_w2=gate_w2, vres_mu=vres_mu, vres_v0=vres_v0, vres_w1=vres_w1, vres_w2=vres_w2,
             k_k=k_k, k_a=k_a, r_k=r_k.reshape(r_k.shape[0], -1), lnx_g=lnx_g, lnx_b=lnx_b,
             w_out=w_out, w_gu=w_gu, w_down=w_down)
    L = ada_w.shape[0]
    B, T, D = x_prompt.shape
    DB = x_sample.shape[0]
    H_ATT = rel_bias.shape[1]
    assert x_sample.shape[1] == 1 and T <= W_MAX

    rows = -(-(B + DB) // SUBLANES) * SUBLANES
    c_all = jnp.concatenate([c_prompt, c_sample, jnp.zeros((rows - B - DB, D), F32)], axis=0)
    mod = _ada(c_all, ada_w, ada_b).reshape(L, rows, 6, D)
    base = _band_bias_rows(rel_bias)

    k_stack = jnp.zeros((L, B, T, H_ATT * HEAD_DIM), F32)
    v_stack = jnp.zeros((L, B, T, H_ATT * HEAD_DIM), F32)
    xp, xs, vf_p, vf_s = x_prompt, x_sample.reshape(1, DB, D), None, None
    ps, ph, sk, sv, ss, sh = [], [], [], [], [], []
    for l in range(L):
        mods_p = [mod[l, :B, j][:, None, :] for j in range(6)]
        mods_s = [mod[l, B:B + DB, j][None] for j in range(6)]
        (xp, xs, vf_p, vf_s, k_stack, v_stack, state_p, shift_p, k_s, v_s, state_s, shift_s) = _layer(
            P, l, xp, xs, mods_p, mods_s, base, vf_p, vf_s, k_stack, v_stack, cache_k, cache_v, state_wkv,
            state_shift[l])
        ps.append(state_p)
        ph.append(shift_p)
        sk.append(k_s.reshape(DB, 1, H_ATT, HEAD_DIM))
        sv.append(v_s.reshape(DB, 1, H_ATT, HEAD_DIM))
        ss.append(state_s)
        sh.append(shift_s)

    return (xp, xs.reshape(DB, 1, D), k_stack.reshape(L, B, T, H_ATT, HEAD_DIM),
            v_stack.reshape(L, B, T, H_ATT, HEAD_DIM), jnp.stack(ps), jnp.stack(ph),
            jnp.stack(sk), jnp.stack(sv), jnp.stack(ss), jnp.stack(sh))
```

```python
import functools
import math

import jax
import jax.numpy as jnp
from jax import lax
from jax.experimental import pallas as pl
from jax.experimental.pallas import tpu as pltpu

F32 = jnp.float32
BF16 = jnp.bfloat16

HEAD_DIM = 64
BRANCHES = ((128, 1), (512, 4), (2048, 16))
W_MAX = 2048
NUM_BUCKETS = 32
MAX_DISTANCE = W_MAX
ATT_SCALE = HEAD_DIM ** -0.5
RMS_EPS = 1e-6
GN_EPS = 64e-5
NEG = -1e30

LANES = 128
SUBLANES = 8
Q_BLOCK = 128
SCAN_CHUNK = 64
SCAN_SEQS = 2
SCAN_UNROLL = 16
ATT_UNROLL = 4
VMEM_CAP = 56 * 1024 * 1024

assert all(w // dil == Q_BLOCK for w, dil in BRANCHES) and 2 * HEAD_DIM == LANES


def _vmem(nbytes):
    return int(min(VMEM_CAP, nbytes * 1.3 + (6 << 20)))


def _params(sem, nbytes):
    return pltpu.CompilerParams(dimension_semantics=sem, vmem_limit_bytes=_vmem(nbytes))


def _dot(a, b):
    return jnp.dot(a.astype(BF16), b.astype(BF16), preferred_element_type=F32)


def _dot_nt(a, b):
    return lax.dot_general(a.astype(BF16), b.astype(BF16), (((1,), (1,)), ((), ())), preferred_element_type=F32)


def _dot_tn(a, b):
    return lax.dot_general(a.astype(BF16), b.astype(BF16), (((0,), (0,)), ((), ())), preferred_element_type=F32)


def _sigmoid(x):
    return 1.0 / (1.0 + jnp.exp(-x))


def _head_sums(x):
    r = lax.broadcasted_iota(jnp.int32, (LANES, LANES), 0) // HEAD_DIM
    c = lax.broadcasted_iota(jnp.int32, (LANES, LANES), 1) // HEAD_DIM
    bd = (r == c).astype(BF16)
    cols = [_dot(x[:, j * LANES:(j + 1) * LANES], bd) for j in range(x.shape[1] // LANES)]
    return cols[0] if len(cols) == 1 else jnp.concatenate(cols, axis=-1)


def _ada_kernel(c_ref, w_ref, b_ref, o_ref):
    c = c_ref[...]
    s = c * _sigmoid(c)
    o_ref[0] = _dot(s, w_ref[0]) + b_ref[0]


def _ada(c_all, ada_w, ada_b, tn=1024):
    L, D, N = ada_w.shape
    R = c_all.shape[0]
    return pl.pallas_call(
        _ada_kernel,
        grid=(L, N // tn),
        in_specs=[pl.BlockSpec((R, D), lambda l, j: (0, 0)),
                  pl.BlockSpec((1, D, tn), lambda l, j: (l, 0, j)),
                  pl.BlockSpec((1, 1, tn), lambda l, j: (l, 0, j))],
        out_specs=pl.BlockSpec((1, R, tn), lambda l, j: (l, 0, j)),
        out_shape=jax.ShapeDtypeStruct((L, R, N), F32),
        compiler_params=_params(("arbitrary", "arbitrary"), 2 * D * tn * 4 + D * tn * 2),
        name="ada_mod",
    )(c_all, ada_w, ada_b.reshape(L, 1, N))


def _norm_kernel(x_ref, g_ref, sh_ref, sc_ref, *out_refs):
    x = x_ref[0]
    y = x * lax.rsqrt(jnp.mean(x * x, axis=-1, keepdims=True) + RMS_EPS) * g_ref[0]
    h = y * (1.0 + sc_ref[0]) + sh_ref[0]
    for o in out_refs:
        o[0] = h.astype(o.dtype)


def _row_spec(arr, tm, T):
    C = arr.shape[-1]
    if arr.shape[1] == 1:
        return pl.BlockSpec((1, 1, C), lambda b, i: (b, 0, 0))
    assert arr.shape[1] == T
    return pl.BlockSpec((1, tm, C), lambda b, i: (b, i, 0))


def _norm_mod(x, g_stack, l, shift, scale, out_dtypes, tm=512):
    B, T, D = x.shape
    tm = min(tm, T)
    L = g_stack.shape[0]
    outs = pl.pallas_call(
        _norm_kernel,
        grid=(B, T // tm),
        in_specs=[pl.BlockSpec((1, tm, D), lambda b, i: (b, i, 0)),
                  pl.BlockSpec((1, 1, D), lambda b, i: (l, 0, 0)),
                  _row_spec(shift, tm, T), _row_spec(scale, tm, T)],
        out_specs=[pl.BlockSpec((1, tm, D), lambda b, i: (b, i, 0)) for _ in out_dtypes],
        out_shape=[jax.ShapeDtypeStruct((B, T, D), dt) for dt in out_dtypes],
        compiler_params=_params(("arbitrary", "arbitrary"), 2 * tm * D * 4 * (2 + len(out_dtypes))),
        name="norm_mod",
    )(x, g_stack.reshape(L, 1, D), shift, scale)
    return outs


def _first_inner_step():
    return (pl.program_id(1) == 0) & (pl.program_id(2) == 0)


def _head_rmsnorm(acc, g):
    return acc * lax.rsqrt(_head_sums(acc * acc) * (1.0 / HEAD_DIM) + RMS_EPS) * g


def _rider_specs(R, k_sizes, tn):
    ins = [pl.BlockSpec((1, R, ks), lambda n, b, i: (0, 0, 0)) for ks in k_sizes]
    return ins, pl.BlockSpec((1, R, tn), lambda n, b, i: (0, 0, n))


def _mm_kernel(*refs, headnorm, scale, aliased):
    it = iter(refs)
    x_ref, w_ref = next(it), next(it)
    g_ref = next(it) if headnorm else None
    if aliased:
        next(it)
    xr_ref, o_ref, or_ref, wb_ref = next(it), next(it), next(it), next(it)

    def result(x):
        acc = jnp.dot(x, wb_ref[...], preferred_element_type=F32)
        if headnorm:
            acc = _head_rmsnorm(acc, g_ref[0])
            if scale != 1.0:
                acc = acc * scale
        return acc

    @pl.when(_first_inner_step())
    def _():
        wb_ref[...] = w_ref[0].astype(BF16)
        or_ref[0] = result(xr_ref[0])

    if len(o_ref.shape) == 4:
        o_ref[0, 0] = result(x_ref[0])
    else:
        o_ref[0] = result(x_ref[0])


def _proj(x, rider, w_stack, l, col0, ncols, *, head_g=None, scale=1.0, stack=None, tm=1024, tn=1024):
    B, T, K = x.shape
    R = rider.shape[1]
    tm = min(tm, T)
    tn = min(tn, ncols)
    assert T % tm == 0 and ncols % tn == 0 and col0 % tn == 0
    cb = col0 // tn
    in_specs = [pl.BlockSpec((1, tm, K), lambda n, b, i: (b, i, 0)),
                pl.BlockSpec((1, K, tn), lambda n, b, i: (l, 0, cb + n))]
    args = [x, w_stack]
    if head_g is not None:
        in_specs.append(pl.BlockSpec((1, 1, tn), lambda n, b, i: (0, 0, 0)))
        args.append(jnp.tile(head_g, tn // HEAD_DIM).reshape(1, 1, tn))
    aliases = {}
    if stack is None:
        out_spec = pl.BlockSpec((1, tm, tn), lambda n, b, i: (b, i, n))
        out_shape = jax.ShapeDtypeStruct((B, T, ncols), F32)
    else:
        out_spec = pl.BlockSpec((1, 1, tm, tn), lambda n, b, i: (l, b, i, n))
        out_shape = jax.ShapeDtypeStruct(stack.shape, F32)
        in_specs.append(pl.BlockSpec(memory_space=pl.ANY))
        args.append(stack)
        aliases = {len(args) - 1: 0}
    r_ins, r_out = _rider_specs(R, (K,), tn)
    nbytes = 2 * (tm * K * 2 + K * tn * 4 + tm * tn * 4) + K * tn * 2 + 2 * tm * tn * 4
    return pl.pallas_call(
        functools.partial(_mm_kernel, headnorm=head_g is not None, scale=scale, aliased=bool(aliases)),
        grid=(ncols // tn, B, T // tm),
        in_specs=in_specs + r_ins,
        out_specs=[out_spec, r_out],
        out_shape=[out_shape, jax.ShapeDtypeStruct((1, R, ncols), F32)],
        scratch_shapes=[pltpu.VMEM((K, tn), BF16)],
        input_output_aliases=aliases,
        compiler_params=_params(("arbitrary",) * 3, nbytes),
        name="proj",
    )(*args, rider)


def _swiglu_kernel(x_ref, wg_ref, wu_ref, xr_ref, o_ref, or_ref, wgb_ref, wub_ref):
    def result(x):
        gate = jnp.dot(x, wgb_ref[...], preferred_element_type=F32)
        up = jnp.dot(x, wub_ref[...], preferred_element_type=F32)
        return (gate * _sigmoid(gate) * up).astype(o_ref.dtype)

    @pl.when(_first_inner_step())
    def _():
        wgb_ref[...] = wg_ref[0].astype(BF16)
        wub_ref[...] = wu_ref[0].astype(BF16)
        or_ref[0] = result(xr_ref[0])

    o_ref[0] = result(x_ref[0])


def _proj_swiglu(x, rider, w_gu, l, *, tm=1024, tn=512):
    B, T, K = x.shape
    R = rider.shape[1]
    F = w_gu.shape[2] // 2
    tm = min(tm, T)
    assert T % tm == 0 and F % tn == 0
    nb = F // tn
    r_ins, r_out = _rider_specs(R, (K,), tn)
    nbytes = 2 * (tm * K * 2 + 2 * K * tn * 4 + tm * tn * 2) + 2 * K * tn * 2 + 3 * tm * tn * 4
    return pl.pallas_call(
        _swiglu_kernel,
        grid=(nb, B, T // tm),
        in_specs=[pl.BlockSpec((1, tm, K), lambda n, b, i: (b, i, 0)),
                  pl.BlockSpec((1, K, tn), lambda n, b, i: (l, 0, n)),
                  pl.BlockSpec((1, K, tn), lambda n, b, i: (l, 0, nb + n))] + r_ins,
        out_specs=[pl.BlockSpec((1, tm, tn), lambda n, b, i: (b, i, n)), r_out],
        out_shape=[jax.ShapeDtypeStruct((B, T, F), BF16), jax.ShapeDtypeStruct((1, R, F), BF16)],
        scratch_shapes=[pltpu.VMEM((K, tn), BF16), pltpu.VMEM((K, tn), BF16)],
        compiler_params=_params(("arbitrary",) * 3, nbytes),
        name="proj_swiglu",
    )(x, w_gu, w_gu, rider)


def _resid_kernel(*refs, k_sizes):
    n_x = len(k_sizes)
    x_refs, refs = refs[:n_x], refs[n_x:]
    w_ref, res_ref, gate_ref = refs[:3]
    xr_refs, refs = refs[3:3 + n_x], refs[3 + n_x:]
    resr_ref, gater_ref, o_ref, or_ref, wb_ref = refs

    def result(xs, res, gate):
        acc = None
        k0 = 0
        for x_ref, ks in zip(xs, k_sizes):
            part = jnp.dot(x_ref[0], wb_ref[k0:k0 + ks, :], preferred_element_type=F32)
            acc = part if acc is None else acc + part
            k0 += ks
        return res[0] + gate[0] * acc

    @pl.when(_first_inner_step())
    def _():
        wb_ref[...] = w_ref[0].astype(BF16)
        or_ref[0] = result(xr_refs, resr_ref, gater_ref)

    o_ref[0] = result(x_refs, res_ref, gate_ref)


def _proj_resid(xs, rider_xs, w_stack, l, resid, gate, rider_resid, rider_gate, *, tm=512, tn=512):
    B, T, N = resid.shape
    R = rider_resid.shape[1]
    k_sizes = tuple(x.shape[2] for x in xs)
    K = sum(k_sizes)
    tm = min(tm, T)
    assert T % tm == 0 and N % tn == 0 and gate.shape[1] == 1
    in_specs = [pl.BlockSpec((1, tm, ks), lambda n, b, i: (b, i, 0)) for ks in k_sizes]
    in_specs += [pl.BlockSpec((1, K, tn), lambda n, b, i: (l, 0, n)),
                 pl.BlockSpec((1, tm, tn), lambda n, b, i: (b, i, n)),
                 pl.BlockSpec((1, 1, tn), lambda n, b, i: (b, 0, n))]
    r_ins, r_out = _rider_specs(R, k_sizes, tn)
    nbytes = 2 * (tm * K * 2 + K * tn * 4 + 2 * tm * tn * 4) + K * tn * 2 + 2 * tm * tn * 4
    return pl.pallas_call(
        functools.partial(_resid_kernel, k_sizes=k_sizes),
        grid=(N // tn, B, T // tm),
        in_specs=in_specs + r_ins + [r_out, r_out],
        out_specs=[pl.BlockSpec((1, tm, tn), lambda n, b, i: (b, i, n)), r_out],
        out_shape=[jax.ShapeDtypeStruct((B, T, N), F32), jax.ShapeDtypeStruct((1, R, N), F32)],
        scratch_shapes=[pltpu.VMEM((K, tn), BF16)],
        compiler_params=_params(("arbitrary",) * 3, nbytes),
        name="proj_resid",
    )(*xs, w_stack, resid, gate, *rider_xs, rider_resid, rider_gate)


def _rel_bucket(dist):
    max_exact = NUM_BUCKETS // 2
    d = jnp.maximum(dist, 0)
    df = jnp.maximum(d, 1).astype(F32)
    large = max_exact + (jnp.log(df / max_exact) / math.log(MAX_DISTANCE / max_exact)
                         * (NUM_BUCKETS - max_exact)).astype(jnp.int32)
    return jnp.where(d < max_exact, d, jnp.minimum(large, NUM_BUCKETS - 1))


def _band_bias_rows(rel_bias):
    dsub = Q_BLOCK - jnp.arange(2 * Q_BLOCK)
    rows = []
    for (w, dil) in BRANCHES:
        valid = (dsub >= 0) & (dsub <= w // dil)
        bias = rel_bias[_rel_bucket(jnp.maximum(dsub, 0) * dil)].astype(F32).T
        rows.append(jnp.where(valid[None], bias, NEG))
    return jnp.stack(rows, axis=1)


def _attn_kernel(q_ref, k_ref, v_ref, base_ref, o_ref, qs, ks, vs, os_, ms_, ds_, stage):
    T = q_ref.shape[2]
    QB = Q_BLOCK
    low = lax.broadcasted_iota(jnp.int32, (1, LANES), 1) < HEAD_DIM
    own = (low, jnp.logical_not(low))
    prev_cols = lax.broadcasted_iota(jnp.int32, (1, 2 * QB), 1) < QB

    assert len(BRANCHES) == 3 and BRANCHES[0][1] == 1
    prev_dil = 1
    for bi, (_, dil) in enumerate(BRANCHES):
        L, Lp, ratio = T // dil, T // prev_dil, dil // prev_dil
        ks[bi, 0:QB, :] = jnp.zeros((QB, LANES), BF16)
        vs[bi, 0:QB, :] = jnp.zeros((QB, LANES), BF16)
        for r in range(dil):
            rows = pl.ds((r % prev_dil) * Lp + r // prev_dil, L, stride=ratio)
            if bi < 2:
                q, k, v = (ref[0, 0, rows, :] for ref in (q_ref, k_ref, v_ref))
            else:
                q, k, v = (stage[a, rows, :] for a in range(3))
            if bi == 1:
                for a, x in enumerate((q, k, v)):
                    stage[a, r * L:(r + 1) * L, :] = x
            for hh in range(2):
                qs[hh, bi, r * L:(r + 1) * L, :] = jnp.where(own[hh], q, 0.0).astype(BF16)
            ks[bi, QB + r * L:QB + (r + 1) * L, :] = k.astype(BF16)
            vs[bi, QB + r * L:QB + (r + 1) * L, :] = v.astype(BF16)
        prev_dil = dil

    for bi, (_, dil) in enumerate(BRANCHES):
        nb = T // dil // QB
        tiles = [pltpu.roll(jnp.broadcast_to(base_ref[hh, bi:bi + 1, :], (QB, 2 * QB)), 0, 1, stride=1, stride_axis=0)
                 for hh in range(2)]
        first_only = nb == 1
        if first_only:
            tiles = [t[:, QB:] for t in tiles]
        k0, kn = (QB, QB) if first_only else (0, 2 * QB)

        def one_block(g, q0, q1, kw, vw, nb=nb, tiles=tiles, first_only=first_only):
            ss = [lax.dot_general(q, kw, (((1,), (1,)), ((), ())), preferred_element_type=F32) + tiles[hh]
                  for hh, q in enumerate((q0, q1))]
            if not first_only:
                no_prev = jnp.logical_and(g % nb == 0, prev_cols)
                ss = [jnp.where(no_prev, NEG, s) for s in ss]
            yield
            ms = [jnp.max(s, axis=-1, keepdims=True) for s in ss]
            ps = [jnp.exp(s - m) for s, m in zip(ss, ms)]
            dens = [jnp.sum(p, axis=-1, keepdims=True) for p in ps]
            os2 = [jnp.dot(p.astype(BF16), vw, preferred_element_type=F32) for p in ps]
            yield
            pair = lambda x0, x1: jnp.where(low, jnp.broadcast_to(x0, (QB, LANES)), jnp.broadcast_to(x1, (QB, LANES)))
            return pair(*os2), pair(*ms), pair(*dens)

        def blocks(i, _, bi=bi, dil=dil, nb=nb, k0=k0, kn=kn):
            gs = [i * ATT_UNROLL + u for u in range(ATT_UNROLL)]
            ats = [pl.multiple_of(g * QB, QB) for g in gs]
            loaded = [(qs[0, bi, pl.ds(at, QB), :], qs[1, bi, pl.ds(at, QB), :],
                       ks[bi, pl.ds(at + k0, kn), :], vs[bi, pl.ds(at + k0, kn), :]) for at in ats]
            results = _round_robin([one_block(g, *ld) for g, ld in zip(gs, loaded)])
            for g, (acc, m, den) in zip(gs, results):
                r = g // nb
                tok = pl.ds((g - r * nb) * (QB * dil) + r, QB, stride=dil)
                os_[bi, tok, :] = acc
                ms_[bi, tok, :] = m
                ds_[bi, tok, :] = den
            return 0

        lax.fori_loop(0, T // QB // ATT_UNROLL, blocks, 0)

    def merge(i, _):
        rows = pl.ds(pl.multiple_of(i * QB, QB), QB)
        maxes = [ms_[bi, rows, :] for bi in range(len(BRANCHES))]
        m = functools.reduce(jnp.maximum, maxes)
        ws = [jnp.exp(x - m) for x in maxes]
        num = sum(w * os_[bi, rows, :] for bi, w in enumerate(ws))
        den = sum(w * ds_[bi, rows, :] for bi, w in enumerate(ws))
        o_ref[0, rows, :] = (num / den).astype(o_ref.dtype)
        return 0

    lax.fori_loop(0, T // QB, merge, 0)


def _attention_prompt(q, k_stack, v_stack, l, base):
    B, T, DA = q.shape
    nbr = len(BRANCHES)
    assert T % (Q_BLOCK * max(d for _, d in BRANCHES)) == 0
    qkv = pl.BlockSpec((1, 1, T, LANES), lambda p, b: (l, b, 0, p))
    scratch = [pltpu.VMEM((2, nbr, T, LANES), BF16), pltpu.VMEM((nbr, T + Q_BLOCK, LANES), BF16),
               pltpu.VMEM((nbr, T + Q_BLOCK, LANES), BF16), pltpu.VMEM((nbr, T, LANES), F32),
               pltpu.VMEM((nbr, T, LANES), F32), pltpu.VMEM((nbr, T, LANES), F32), pltpu.VMEM((3, T, LANES), F32)]
    nbytes = (2 * 3 * T * LANES * 4 + 4 * nbr * (T + Q_BLOCK) * LANES * 2 + 3 * nbr * T * LANES * 4
              + 3 * T * LANES * 4 + 2 * T * LANES * 2)
    return pl.pallas_call(
        _attn_kernel,
        grid=(DA // LANES, B),
        in_specs=[pl.BlockSpec((1, 1, T, LANES), lambda p, b: (0, b, 0, p)), qkv, qkv,
                  pl.BlockSpec((2, nbr, 2 * Q_BLOCK), lambda p, b: (p, 0, 0))],
        out_specs=pl.BlockSpec((1, T, LANES), lambda p, b: (b, 0, p)),
        out_shape=jax.ShapeDtypeStruct((B, T, DA), BF16),
        scratch_shapes=scratch,
        compiler_params=_params(("arbitrary", "arbitrary"), nbytes),
        name="attn_prompt",
    )(q[None], k_stack, v_stack, base)


def _distance_logits(rel_bias, nd):
    d = jnp.arange(nd)
    mult = sum(((d % dil == 0) & (d // dil <= w // dil)).astype(F32) for (w, dil) in BRANCHES)
    bias = rel_bias[_rel_bucket(d)].astype(F32).T
    return jnp.where(mult > 0, bias + jnp.log(jnp.maximum(mult, 1.0)), NEG)


def _sattn_kernel(q_ref, kn_ref, vn_ref, kt_ref, vt_ref, tab_ref, o_ref):
    row = lax.broadcasted_iota(jnp.int32, (SUBLANES, LANES), 0)
    low = lax.broadcasted_iota(jnp.int32, (SUBLANES, LANES), 1) < HEAD_DIM
    own = ((row == 0) & low) | ((row == 1) & jnp.logical_not(low))
    q = jnp.where(own, q_ref[0], 0.0).astype(BF16)
    kn = kn_ref[0].astype(BF16).astype(F32)
    vn = vn_ref[0].astype(BF16).astype(F32)
    tab = tab_ref[0]
    W = kt_ref.shape[3]
    s_self = jnp.sum(q.astype(F32) * kn, axis=-1, keepdims=True) + tab[:, W:W + 1]
    s = _dot(q, kt_ref[0, 0]) + tab[:, :W]
    m = jnp.maximum(s_self, jnp.max(s, axis=-1, keepdims=True))
    p_self = jnp.exp(s_self - m)
    p = jnp.exp(s - m)
    den = p_self + jnp.sum(p, axis=-1, keepdims=True)
    o = (p_self * vn + _dot_nt(p, vt_ref[0, 0])) / den
    o_ref[0] = jnp.where(low[0:1], o[0:1], o[1:2]).astype(o_ref.dtype)


def _attention_sample(q, k_new, v_new, cache_k, cache_v, l, rel_bias):
    L, DB, W, H, E = cache_k.shape
    DA = H * E
    assert W >= max(w for w, _ in BRANCHES)
    tab = _distance_logits(rel_bias, W + 1)[:, ::-1].reshape(H // 2, 2, W + 1)
    tab = jnp.concatenate([tab, jnp.zeros((H // 2, SUBLANES - 2, W + 1), F32)], axis=1)
    kt = jnp.transpose(cache_k, (0, 1, 3, 4, 2)).reshape(L, DB, DA, W)
    vt = jnp.transpose(cache_v, (0, 1, 3, 4, 2)).reshape(L, DB, DA, W)
    vec = pl.BlockSpec((1, 1, LANES), lambda p, b: (b, 0, p))
    buf = pl.BlockSpec((1, 1, LANES, W), lambda p, b: (l, b, p, 0))
    return pl.pallas_call(
        _sattn_kernel,
        grid=(DA // LANES, DB),
        in_specs=[vec, vec, vec, buf, buf, pl.BlockSpec((1, SUBLANES, W + 1), lambda p, b: (p, 0, 0))],
        out_specs=vec,
        out_shape=jax.ShapeDtypeStruct((DB, 1, DA), BF16),
        compiler_params=_params(("arbitrary", "arbitrary"), 2 * 2 * LANES * W * 4 + 4 * SUBLANES * W * 4),
        name="attn_sample",
    )(q, k_new, v_new, kt, vt, tab)


def _prep_kernel(*refs, shift_rows, has_vres):
    it = iter(refs)
    h_ref = next(it)
    if shift_rows:
        hp8_ref, hlast_ref = next(it), next(it)
    else:
        hprev_ref = next(it)
    rkv_ref = next(it)
    if shift_rows:
        rp8_ref, rlast_ref = next(it), next(it)
    else:
        rprev_ref = next(it)
    if has_vres:
        vfirst_ref = next(it)
    mu_ref, murkv_ref, w0_ref, dw1_ref, dw2_ref, a0_ref, aw1_ref, aw2_ref, gw1_ref, gw2_ref = (next(it) for _ in range(10))
    if has_vres:
        vmu_ref, v0_ref, vw1_ref, vw2_ref = (next(it) for _ in range(4))
    kk_ref, ka_ref = next(it), next(it)
    r_out, lw_out, k_out, v_out, kk_out, b_out, g_out = (next(it) for _ in range(7))

    h = h_ref[0]
    rkv0 = rkv_ref[0]
    tm = h.shape[0]
    if shift_rows:
        first = pl.program_id(1) == 0
        row0 = lax.broadcasted_iota(jnp.int32, (tm, 1), 0) == 0
        h_edge = jnp.where(first, hlast_ref[0], hp8_ref[0, SUBLANES - 1:SUBLANES, :])
        r_edge = jnp.where(first, rlast_ref[0], rp8_ref[0, SUBLANES - 1:SUBLANES, :])
        hprev = jnp.where(row0, h_edge, pltpu.roll(h, 1, 0))
        rprev = jnp.where(row0, r_edge, pltpu.roll(rkv0, 1, 0))
    else:
        hprev = hprev_ref[0]
        rprev = rprev_ref[0]

    dh = hprev - h
    mu = mu_ref[0]
    xw = h + dh * mu[0:1]
    xa = h + dh * mu[1:2]
    xg = h + dh * mu[2:3]

    z = w0_ref[0] + _dot(jnp.tanh(_dot(xw, dw1_ref[0])), dw2_ref[0])
    lw_out[0] = -math.exp(-0.5) * _sigmoid(z)

    a = _sigmoid(a0_ref[0] + _dot(_dot(xa, aw1_ref[0]), aw2_ref[0]))
    g_out[0] = _dot(_sigmoid(_dot(xg, gw1_ref[0])), gw2_ref[0])

    murkv = murkv_ref[0]
    DR = kk_ref.shape[-1]
    r0, k0, v0 = (rkv0[:, j * DR:(j + 1) * DR] for j in range(3))
    rp, kp, vp = (rprev[:, j * DR:(j + 1) * DR] for j in range(3))
    r_out[0] = r0 + (rp - r0) * murkv[0:1]
    kr = k0 + (kp - k0) * murkv[1:2]
    vr = v0 + (vp - v0) * murkv[2:3]
    if has_vres:
        xv = h + dh * vmu_ref[0]
        vgate = _sigmoid(v0_ref[0] + _dot(_dot(xv, vw1_ref[0]), vw2_ref[0]))
        vr = vr + (vfirst_ref[0] - vr) * vgate
    v_out[0] = vr
    kk = kr * kk_ref[0]
    kkn = kk * lax.rsqrt(jnp.maximum(_head_sums(kk * kk), 1e-24))
    kk_out[0] = kkn
    b_out[0] = kkn * a
    k_out[0] = kr * (1.0 + (a - 1.0) * ka_ref[0])


def _rwkv_prep(h, h_prev, rkv0, rkv_prev, v_first, P, l, *, tm=256):
    B, T, D = h.shape
    DR = rkv0.shape[2] // 3
    tm = min(tm, T)
    shift_rows = h_prev.shape[1] == 1 and T > 1
    has_vres = v_first is not None

    def tile(C):
        return pl.BlockSpec((1, tm, C), lambda b, i: (b, i, 0))

    def prev8(C):
        return pl.BlockSpec((1, SUBLANES, C), lambda b, i: (b, jnp.maximum(i * (tm // SUBLANES) - 1, 0), 0))

    def seq_row(C):
        return pl.BlockSpec((1, 1, C), lambda b, i: (b, 0, 0))

    def layer(shape, ll=l):
        return pl.BlockSpec((1,) + shape, lambda b, i: (ll,) + (0,) * len(shape))

    args, specs = [h], [tile(D)]
    if shift_rows:
        args += [h, h_prev]
        specs += [prev8(D), seq_row(D)]
    else:
        args += [jnp.broadcast_to(h_prev, h.shape)]
        specs += [tile(D)]
    args.append(rkv0)
    specs.append(tile(3 * DR))
    if shift_rows:
        args += [rkv0, rkv_prev]
        specs += [prev8(3 * DR), seq_row(3 * DR)]
    else:
        args += [jnp.broadcast_to(rkv_prev, rkv0.shape)]
        specs += [tile(3 * DR)]
    if has_vres:
        args.append(v_first)
        specs.append(tile(DR))
    r1 = lambda a: a.reshape(a.shape[0], 1, a.shape[-1])
    for name in ('mu_wag', 'mu_rkv'):
        args.append(P[name]); specs.append(layer(P[name].shape[1:]))
    args.append(r1(P['decay_w0'])); specs.append(layer((1, DR)))
    for name in ('decay_w1', 'decay_w2'):
        args.append(P[name]); specs.append(layer(P[name].shape[1:]))
    args.append(r1(P['aaa_a0'])); specs.append(layer((1, DR)))
    for name in ('aaa_w1', 'aaa_w2', 'gate_w1', 'gate_w2'):
        args.append(P[name]); specs.append(layer(P[name].shape[1:]))
    if has_vres:
        args.append(r1(P['vres_mu'])); specs.append(layer((1, D), l - 1))
        args.append(r1(P['vres_v0'])); specs.append(layer((1, DR), l - 1))
        for name in ('vres_w1', 'vres_w2'):
            args.append(P[name]); specs.append(layer(P[name].shape[1:], l - 1))
    args.append(r1(P['k_k'])); specs.append(layer((1, DR)))
    args.append(r1(P['k_a'])); specs.append(layer((1, DR)))
    nbytes = 2 * tm * 4 * (2 * D + 7 * DR + 7 * DR + DR) + 8 * tm * D * 4 + 4 * D * 512 * 4
    return pl.pallas_call(
        functools.partial(_prep_kernel, shift_rows=shift_rows, has_vres=has_vres),
        grid=(B, T // tm),
        in_specs=specs,
        out_specs=[tile(DR) for _ in range(7)],
        out_shape=[jax.ShapeDtypeStruct((B, T, DR), F32) for _ in range(7)],
        compiler_params=_params(("arbitrary", "arbitrary"), nbytes),
        name="rwkv_prep",
    )(*args)


def _prefix_sum_rows(x):
    n = x.shape[0]
    row = lax.broadcasted_iota(jnp.int32, (n, 1), 0)
    s = 1
    while s < n:
        x = x + jnp.where(row >= s, pltpu.roll(x, s, 0), 0.0)
        s *= 2
    return x


def _unit_lower_inverse(a_strict, blk):
    n = a_strict.shape[0]
    ti = lax.broadcasted_iota(jnp.int32, (n, n), 0)
    si = lax.broadcasted_iota(jnp.int32, (n, n), 1)

    def lower_left(s):
        return ((ti // (2 * s)) == (si // (2 * s))) & ((ti % (2 * s)) >= s) & ((si % (2 * s)) < s)

    d = (ti == si).astype(F32) + jnp.where(lower_left(1), a_strict, 0.0)
    s = 2
    while s < blk:
        t = _dot(d, jnp.where(lower_left(s), a_strict, 0.0))
        yield
        d = d + _dot(t, d)
        yield
        s *= 2
    return d


def _round_robin(generators):
    results = [None] * len(generators)
    live = list(range(len(generators)))
    while live:
        for u in list(live):
            try:
                next(generators[u])
            except StopIteration as done:
                results[u] = done.value
                live.remove(u)
    return results


def _pair_chunk_terms(r, lw, k, v, kk, b, low, tri):
    C = r.shape[0]
    cum = _prefix_sum_rows(lw)
    g_in = jnp.exp(cum)
    g_inv = jnp.exp(-cum)
    g_end = g_in[C - 1:C]
    a_t = -kk * jnp.exp(cum - lw)
    r_t = r * g_in
    b_t = b * g_inv
    k_t = k * g_inv
    bg = b_t * g_end
    kg = k_t * g_end
    zc = jnp.zeros((C, LANES), F32)
    h0 = lambda x: jnp.where(low, x, 0.0)
    h1 = lambda x: jnp.where(low, 0.0, x)
    v0, v1 = h0(v), h1(v)
    ar = jnp.concatenate([a_t, r_t], axis=0)
    m0 = jnp.where(tri, _dot_nt(h0(ar), jnp.concatenate([b_t, k_t], axis=0)), 0.0)
    m1 = jnp.where(tri, _dot_nt(h1(ar), jnp.concatenate([k_t, b_t], axis=0)), 0.0)
    yield
    top0, bot0, top1, bot1 = m0[:C], m0[C:], m1[:C], m1[C:]
    stack2 = lambda x0, x1: jnp.concatenate([jnp.concatenate([x0, zc], axis=0),
                                             jnp.concatenate([zc, x1], axis=0)], axis=1)
    akv = _dot(stack2(top0, top1), jnp.concatenate([zc, v0, v1, zc], axis=0))
    a_sw = pltpu.roll(a_t, HEAD_DIM, 1)
    tinv = yield from _unit_lower_inverse(jnp.concatenate([h0(top0), h1(top1)], axis=0), C)
    x = _dot(tinv, akv + jnp.concatenate([h1(a_sw), h0(a_sw)], axis=0))
    yield
    z = jnp.concatenate([x[:C], v0, v1, x[C:]], axis=0)
    e = _dot(stack2(bot0, bot1), z)
    gh = _dot_tn(z, jnp.concatenate([h0(bg), h0(kg), h1(kg), h1(bg)], axis=0))
    yield
    y0 = jnp.where(low, e[:C], e[C:])
    r_eff = r_t + pltpu.roll(jnp.where(low, e[C:], e[:C]), HEAD_DIM, 1)
    h_mat = jnp.concatenate([h0(gh[:C]), h1(gh[C:])], axis=0)
    g_mat = jnp.concatenate([h0(gh[C:]), h1(gh[:C])], axis=0)
    return r_eff, y0, g_mat, h_mat, g_end


def _scan_kernel(r_ref, lw_ref, k_ref, v_ref, kk_ref, b_ref, g_ref, lng_ref, lnb_ref, rk_ref,
                 o_ref, s_ref, reff_s, y0_s, gm_s, hm_s, ge_s, *, chunk, unroll):
    C = chunk
    NS, T = r_ref.shape[:2]
    NC = T // C
    E = HEAD_DIM
    low = lax.broadcasted_iota(jnp.int32, (1, LANES), 1) < E
    ti = lax.broadcasted_iota(jnp.int32, (2 * C, 2 * C), 0)
    si = lax.broadcasted_iota(jnp.int32, (2 * C, 2 * C), 1)
    tri = (si % C) <= jnp.where(ti < C, ti - 1, ti - C)

    def phase1(i, _):
        sq = i // (NC // unroll)
        c0 = (i - sq * (NC // unroll)) * unroll
        cs = [c0 + u for u in range(unroll)]
        rows = [pl.ds(pl.multiple_of(c * C, C), C) for c in cs]
        loaded = [[ref[sq, rw, :] for ref in (r_ref, lw_ref, k_ref, v_ref, kk_ref, b_ref)] for rw in rows]
        terms = _round_robin([_pair_chunk_terms(*args, low, tri) for args in loaded])
        for c, rw, (r_eff, y0, g_mat, h_mat, g_end) in zip(cs, rows, terms):
            reff_s[sq, rw, :] = r_eff
            y0_s[sq, rw, :] = y0
            gm_s[sq, c] = g_mat
            hm_s[sq, c] = h_mat
            ge_s[sq, c] = jnp.broadcast_to(g_end, (SUBLANES, LANES))
        return 0

    lax.fori_loop(0, NS * (NC // unroll), phase1, 0)

    def phase2(c, states):
        rows = pl.ds(pl.multiple_of(c * C, C), C)
        ys = [_dot_nt(reff_s[sq, rows, :], S) + y0_s[sq, rows, :] for sq, S in enumerate(states)]
        new = tuple(S * ge_s[sq, c][0:1] + _dot(S, gm_s[sq, c]) + hm_s[sq, c] for sq, S in enumerate(states))
        for sq, y in enumerate(ys):
            y0_s[sq, rows, :] = y
        return new

    states = lax.fori_loop(0, NC, phase2, tuple(jnp.zeros((LANES, LANES), F32) for _ in range(NS)))
    for sq, S in enumerate(states):
        s_ref[sq, 0] = S[:E, :E]
        s_ref[sq, 1] = S[E:, E:]

    def finish(y, r, k, v, g):
        mu = _head_sums(y) * (1.0 / E)
        yield
        yc = y - mu
        var = _head_sums(yc * yc) * (1.0 / E)
        bonus = _head_sums(r * k * rk_ref[0])
        yield
        yn = yc * lax.rsqrt(var + GN_EPS) * lng_ref[0] + lnb_ref[0]
        return ((yn + bonus * v) * g).astype(o_ref.dtype)

    def phase3(i, _):
        sq = i // (NC // unroll)
        c0 = (i - sq * (NC // unroll)) * unroll
        rows = [pl.ds(pl.multiple_of((c0 + u) * C, C), C) for u in range(unroll)]
        loaded = [[y0_s[sq, rw, :]] + [ref[sq, rw, :] for ref in (r_ref, k_ref, v_ref, g_ref)] for rw in rows]
        outs = _round_robin([finish(*args) for args in loaded])
        for rw, out in zip(rows, outs):
            o_ref[sq, rw, :] = out
        return 0

    lax.fori_loop(0, NS * (NC // unroll), phase3, 0)


def _rwkv_scan(r, lw, k, v, kk, b, g, lnx_g, lnx_b, r_k, l):
    B, T, DR = r.shape
    H = DR // HEAD_DIM
    L = lnx_g.shape[0]
    C = min(SCAN_CHUNK, T)
    nc = T // C
    unroll = math.gcd(SCAN_UNROLL, nc)
    ns = math.gcd(SCAN_SEQS, B)
    assert T % C == 0
    seq = pl.BlockSpec((ns, T, LANES), lambda p, b_: (b_, 0, p))
    par = pl.BlockSpec((1, 1, LANES), lambda p, b_: (l, 0, p))
    scratch = [pltpu.VMEM((ns, T, LANES), F32), pltpu.VMEM((ns, T, LANES), F32),
               pltpu.VMEM((ns, nc, LANES, LANES), F32), pltpu.VMEM((ns, nc, LANES, LANES), F32),
               pltpu.VMEM((ns, nc, SUBLANES, LANES), F32)]
    nbytes = ns * (2 * 8 * T * LANES * 4 + 2 * T * LANES * 4 + 2 * nc * LANES * LANES * 4 + nc * SUBLANES * LANES * 4)
    return pl.pallas_call(
        functools.partial(_scan_kernel, chunk=C, unroll=unroll),
        grid=(DR // LANES, B // ns),
        in_specs=[seq] * 7 + [par] * 3,
        out_specs=[seq, pl.BlockSpec((ns, 2, HEAD_DIM, HEAD_DIM), lambda p, b_: (b_, p, 0, 0))],
        out_shape=[jax.ShapeDtypeStruct((B, T, DR), BF16), jax.ShapeDtypeStruct((B, H, HEAD_DIM, HEAD_DIM), F32)],
        scratch_shapes=scratch,
        compiler_params=_params(("arbitrary", "arbitrary"), nbytes),
        name="rwkv_scan",
    )(r, lw, k, v, kk, b, g, lnx_g.reshape(L, 1, DR), lnx_b.reshape(L, 1, DR), r_k.reshape(L, 1, DR))


def _step_kernel(s_ref, r_ref, lw_ref, k_ref, v_ref, kk_ref, b_ref, g_ref, lng_ref, lnb_ref, rk_ref,
                 o_ref, so_ref, *, n_heads):
    E = HEAD_DIM
    eye = (lax.broadcasted_iota(jnp.int32, (E, E), 0) == lax.broadcasted_iota(jnp.int32, (E, E), 1)).astype(F32)

    def head(hh):
        sl = slice(hh * E, (hh + 1) * E)
        S = s_ref[0, 0, hh]
        r, k, v, kk, b, g = (ref[0, :, sl] for ref in (r_ref, k_ref, v_ref, kk_ref, b_ref, g_ref))
        w = jnp.exp(lw_ref[0, :, sl])
        v_col = jnp.sum(eye * v, axis=-1, keepdims=True)
        sa = jnp.sum(S * (-kk), axis=-1, keepdims=True)
        bonus = jnp.sum(r * k * rk_ref[0, :, sl], axis=-1, keepdims=True)
        yield
        S = S * w + sa * b + v_col * k
        y_col = jnp.sum(S * r, axis=-1, keepdims=True)
        yield
        y = jnp.sum(eye * y_col, axis=0, keepdims=True)
        mu = jnp.mean(y, axis=-1, keepdims=True)
        yield
        yc = y - mu
        var = jnp.mean(yc * yc, axis=-1, keepdims=True)
        yield
        yn = yc * lax.rsqrt(var + GN_EPS) * lng_ref[0, :, sl] + lnb_ref[0, :, sl]
        return S, (yn + bonus * v) * g

    results = _round_robin([head(hh) for hh in range(n_heads)])
    for hh, (S, _) in enumerate(results):
        so_ref[0, hh] = S
    o_ref[0] = jnp.concatenate([o for _, o in results], axis=-1).astype(o_ref.dtype)


def _rwkv_step(state, r, lw, k, v, kk, b, g, lnx_g, lnx_b, r_k, l):
    DB, H = state.shape[1:3]
    DR = H * HEAD_DIM
    L = lnx_g.shape[0]
    vec = pl.BlockSpec((1, 1, DR), lambda b_: (b_, 0, 0))
    par = pl.BlockSpec((1, 1, DR), lambda b_: (l, 0, 0))
    st = pl.BlockSpec((1, H, HEAD_DIM, HEAD_DIM), lambda b_: (b_, 0, 0, 0))
    return pl.pallas_call(
        functools.partial(_step_kernel, n_heads=H),
        grid=(DB,),
        in_specs=[pl.BlockSpec((1, 1, H, HEAD_DIM, HEAD_DIM), lambda b_: (l, b_, 0, 0, 0))] + [vec] * 7 + [par] * 3,
        out_specs=[vec, st],
        out_shape=[jax.ShapeDtypeStruct((DB, 1, DR), BF16), jax.ShapeDtypeStruct(state.shape[1:], F32)],
        compiler_params=_params(("arbitrary",), 4 * H * HEAD_DIM * LANES * 4),
        name="rwkv_step",
    )(state, r, lw, k, v, kk, b, g, lnx_g.reshape(L, 1, DR), lnx_b.reshape(L, 1, DR), r_k.reshape(L, 1, DR))


def _layer(P, l, xp, xs, mods_p, mods_s, base, vf_p, vf_s, k_stack, v_stack, cache_k, cache_v, state_wkv, h_last_s):
    B, T, D = xp.shape
    DB = xs.shape[1]
    DA = P['rel_bias'].shape[1] * HEAD_DIM
    n_rkv = P['w_in'].shape[2] - 3 * DA
    w_in = P['w_in']
    h, hb = _norm_mod(xp, P['norm1_g'], l, mods_p[0], mods_p[1], [F32, BF16])
    hs, hbs = _norm_mod(xs, P['norm1_g'], l, mods_s[0], mods_s[1], [F32, BF16])
    q, q_s = _proj(hb, hbs, w_in, l, 0, DA, head_g=P['q_norm_g'][l], scale=ATT_SCALE)
    k_stack, k_s = _proj(hb, hbs, w_in, l, DA, DA, head_g=P['k_norm_g'][l], stack=k_stack)
    v_stack, v_s = _proj(hb, hbs, w_in, l, 2 * DA, DA, stack=v_stack)
    both = jnp.concatenate([hbs, h_last_s[None].astype(BF16)], axis=1)
    rkv0, rkv2_s = _proj(hb, both, w_in, l, 3 * DA, n_rkv)

    att = _attention_prompt(q, k_stack, v_stack, l, base)
    r, lw, kr, vr, kk, b, g = _rwkv_prep(h, jnp.zeros((B, 1, D), F32), rkv0, jnp.zeros((B, 1, n_rkv), F32), vf_p, P, l)
    if vf_p is None:
        vf_p = vr
    rw, state_p = _rwkv_scan(r, lw, kr, vr, kk, b, g, P['lnx_g'], P['lnx_b'], P['r_k'], l)

    per_seq = lambda t: t.reshape(DB, 1, t.shape[-1])
    att_s = _attention_sample(per_seq(q_s), per_seq(k_s), per_seq(v_s), cache_k, cache_v, l, P['rel_bias'])
    r, lw, kr, vr, kk, b, g = _rwkv_prep(hs, h_last_s[None], rkv2_s[:, :DB], rkv2_s[:, DB:], vf_s, P, l)
    if vf_s is None:
        vf_s = vr
    rw_s, state_s = _rwkv_step(state_wkv, *(per_seq(t) for t in (r, lw, kr, vr, kk, b, g)),
                                P['lnx_g'], P['lnx_b'], P['r_k'], l)

    xp, xs = _proj_resid([att, rw], [att_s.reshape(1, DB, DA), rw_s.reshape(1, DB, DA)], P['w_out'], l,
                         xp, mods_p[2], xs, mods_s[2], tm=1024, tn=1024)
    (h2,) = _norm_mod(xp, P['norm2_g'], l, mods_p[3], mods_p[4], [BF16])
    (h2s,) = _norm_mod(xs, P['norm2_g'], l, mods_s[3], mods_s[4], [BF16])
    act, act_s = _proj_swiglu(h2, h2s, P['w_gu'], l)
    xp, xs = _proj_resid([act], [act_s], P['w_down'], l, xp, mods_p[5], xs, mods_s[5], tm=512, tn=512)
    return xp, xs, vf_p, vf_s, k_stack, v_stack, state_p, h[:, -1], k_s[0], v_s[0], state_s, hs[0]


def kernel(x_prompt, x_sample, c_prompt, c_sample, cache_k, cache_v, state_wkv, state_shift, rel_bias, ada_w, ada_b, norm1_g, norm2_g, w_in, q_norm_g, k_norm_g, mu_wag, mu_rkv, decay_w0, decay_w1, decay_w2, aaa_a0, aaa_w1, aaa_w2, gate_w1, gate_w2, vres_mu, vres_v0, vres_w1, vres_w2, k_k, k_a, r_k, lnx_g, lnx_b, w_out, w_gu, w_down):
    P = dict(rel_bias=rel_bias, norm1_g=norm1_g, norm2_g=norm2_g, w_in=w_in, q_norm_g=q_norm_g,
             k_norm_g=k_norm_g, mu_wag=mu_wag, mu_rkv=mu_rkv, decay_w0=decay_w0, decay_w1=decay_w1,
             decay_w2=decay_w2, aaa_a0=aaa_a0, aaa_w1=aaa_w1, aaa_w2=aaa_w2, gate_w1=gate_w1,
             gate_w2=gate_w2, vres_mu=vres_mu, vres_v0=vres_v0, vres_w1=vres_w1, vres_w2=vres_w2,
             k_k=k_k, k_a=k_a, r_k=r_k.reshape(r_k.shape[0], -1), lnx_g=lnx_g, lnx_b=lnx_b,
             w_out=w_out, w_gu=w_gu, w_down=w_down)
    L = ada_w.shape[0]
    B, T, D = x_prompt.shape
    DB = x_sample.shape[0]
    H_ATT = rel_bias.shape[1]
    assert x_sample.shape[1] == 1 and T <= W_MAX

    rows = -(-(B + DB) // SUBLANES) * SUBLANES
    c_all = jnp.concatenate([c_prompt, c_sample, jnp.zeros((rows - B - DB, D), F32)], axis=0)
    mod = _ada(c_all, ada_w, ada_b).reshape(L, rows, 6, D)
    base = _band_bias_rows(rel_bias)

    k_stack = jnp.zeros((L, B, T, H_ATT * HEAD_DIM), F32)
    v_stack = jnp.zeros((L, B, T, H_ATT * HEAD_DIM), F32)
    xp, xs, vf_p, vf_s = x_prompt, x_sample.reshape(1, DB, D), None, None
    ps, ph, sk, sv, ss, sh = [], [], [], [], [], []
    for l in range(L):
        mods_p = [mod[l, :B, j][:, None, :] for j in range(6)]
        mods_s = [mod[l, B:B + DB, j][None] for j in range(6)]
        (xp, xs, vf_p, vf_s, k_stack, v_stack, state_p, shift_p, k_s, v_s, state_s, shift_s) = _layer(
            P, l, xp, xs, mods_p, mods_s, base, vf_p, vf_s, k_stack, v_stack, cache_k, cache_v, state_wkv,
            state_shift[l])
        ps.append(state_p)
        ph.append(shift_p)
        sk.append(k_s.reshape(DB, 1, H_ATT, HEAD_DIM))
        sv.append(v_s.reshape(DB, 1, H_ATT, HEAD_DIM))
        ss.append(state_s)
        sh.append(shift_s)

    return (xp, xs.reshape(DB, 1, D), k_stack.reshape(L, B, T, H_ATT, HEAD_DIM),
            v_stack.reshape(L, B, T, H_ATT, HEAD_DIM), jnp.stack(ps), jnp.stack(ph),
            jnp.stack(sk), jnp.stack(sv), jnp.stack(ss), jnp.stack(sh))
```

```python
import functools
import math

import jax
import jax.numpy as jnp
from jax import lax
from jax.experimental import pallas as pl
from jax.experimental.pallas import tpu as pltpu

F32 = jnp.float32
BF16 = jnp.bfloat16

HEAD_DIM = 64
BRANCHES = ((128, 1), (512, 4), (2048, 16))
W_MAX = 2048
NUM_BUCKETS = 32
MAX_DISTANCE = W_MAX
ATT_SCALE = HEAD_DIM ** -0.5
RMS_EPS = 1e-6
GN_EPS = 64e-5
NEG = -1e30

LANES = 128
SUBLANES = 8
Q_BLOCK = 128
SCAN_CHUNK = 64
SCAN_SEQS = 2
SCAN_UNROLL = 16
ATT_UNROLL = 4
VMEM_CAP = 56 * 1024 * 1024

assert all(w // dil == Q_BLOCK for w, dil in BRANCHES) and 2 * HEAD_DIM == LANES


def _vmem(nbytes):
    return int(min(VMEM_CAP, nbytes * 1.3 + (6 << 20)))


def _params(sem, nbytes):
    return pltpu.CompilerParams(dimension_semantics=sem, vmem_limit_bytes=_vmem(nbytes))


def _dot(a, b):
    return jnp.dot(a.astype(BF16), b.astype(BF16), preferred_element_type=F32)


def _dot_nt(a, b):
    return lax.dot_general(a.astype(BF16), b.astype(BF16), (((1,), (1,)), ((), ())), preferred_element_type=F32)


def _dot_tn(a, b):
    return lax.dot_general(a.astype(BF16), b.astype(BF16), (((0,), (0,)), ((), ())), preferred_element_type=F32)


def _sigmoid(x):
    return 1.0 / (1.0 + jnp.exp(-x))


def _head_sums(x):
    r = lax.broadcasted_iota(jnp.int32, (LANES, LANES), 0) // HEAD_DIM
    c = lax.broadcasted_iota(jnp.int32, (LANES, LANES), 1) // HEAD_DIM
    bd = (r == c).astype(BF16)
    cols = [_dot(x[:, j * LANES:(j + 1) * LANES], bd) for j in range(x.shape[1] // LANES)]
    return cols[0] if len(cols) == 1 else jnp.concatenate(cols, axis=-1)


def _ada_kernel(c_ref, w_ref, b_ref, o_ref):
    c = c_ref[...]
    s = c * _sigmoid(c)
    o_ref[0] = _dot(s, w_ref[0]) + b_ref[0]


def _ada(c_all, ada_w, ada_b, tn=1024):
    L, D, N = ada_w.shape
    R = c_all.shape[0]
    return pl.pallas_call(
        _ada_kernel,
        grid=(L, N // tn),
        in_specs=[pl.BlockSpec((R, D), lambda l, j: (0, 0)),
                  pl.BlockSpec((1, D, tn), lambda l, j: (l, 0, j)),
                  pl.BlockSpec((1, 1, tn), lambda l, j: (l, 0, j))],
        out_specs=pl.BlockSpec((1, R, tn), lambda l, j: (l, 0, j)),
        out_shape=jax.ShapeDtypeStruct((L, R, N), F32),
        compiler_params=_params(("arbitrary", "arbitrary"), 2 * D * tn * 4 + D * tn * 2),
        name="ada_mod",
    )(c_all, ada_w, ada_b.reshape(L, 1, N))


def _norm_kernel(x_ref, g_ref, sh_ref, sc_ref, *out_refs):
    x = x_ref[0]
    y = x * lax.rsqrt(jnp.mean(x * x, axis=-1, keepdims=True) + RMS_EPS) * g_ref[0]
    h = y * (1.0 + sc_ref[0]) + sh_ref[0]
    for o in out_refs:
        o[0] = h.astype(o.dtype)


def _row_spec(arr, tm, T):
    C = arr.shape[-1]
    if arr.shape[1] == 1:
        return pl.BlockSpec((1, 1, C), lambda b, i: (b, 0, 0))
    assert arr.shape[1] == T
    return pl.BlockSpec((1, tm, C), lambda b, i: (b, i, 0))


def _norm_mod(x, g_stack, l, shift, scale, out_dtypes, tm=512):
    B, T, D = x.shape
    tm = min(tm, T)
    L = g_stack.shape[0]
    outs = pl.pallas_call(
        _norm_kernel,
        grid=(B, T // tm),
        in_specs=[pl.BlockSpec((1, tm, D), lambda b, i: (b, i, 0)),
                  pl.BlockSpec((1, 1, D), lambda b, i: (l, 0, 0)),
                  _row_spec(shift, tm, T), _row_spec(scale, tm, T)],
        out_specs=[pl.BlockSpec((1, tm, D), lambda b, i: (b, i, 0)) for _ in out_dtypes],
        out_shape=[jax.ShapeDtypeStruct((B, T, D), dt) for dt in out_dtypes],
        compiler_params=_params(("arbitrary", "arbitrary"), 2 * tm * D * 4 * (2 + len(out_dtypes))),
        name="norm_mod",
    )(x, g_stack.reshape(L, 1, D), shift, scale)
    return outs


def _first_inner_step():
    return (pl.program_id(1) == 0) & (pl.program_id(2) == 0)


def _head_rmsnorm(acc, g):
    return acc * lax.rsqrt(_head_sums(acc * acc) * (1.0 / HEAD_DIM) + RMS_EPS) * g


def _rider_specs(R, k_sizes, tn):
    ins = [pl.BlockSpec((1, R, ks), lambda n, b, i: (0, 0, 0)) for ks in k_sizes]
    return ins, pl.BlockSpec((1, R, tn), lambda n, b, i: (0, 0, n))


def _mm_kernel(*refs, headnorm, scale, aliased):
    it = iter(refs)
    x_ref, w_ref = next(it), next(it)
    g_ref = next(it) if headnorm else None
    if aliased:
        next(it)
    xr_ref, o_ref, or_ref, wb_ref = next(it), next(it), next(it), next(it)

    def result(x):
        acc = jnp.dot(x, wb_ref[...], preferred_element_type=F32)
        if headnorm:
            acc = _head_rmsnorm(acc, g_ref[0])
            if scale != 1.0:
                acc = acc * scale
        return acc

    @pl.when(_first_inner_step())
    def _():
        wb_ref[...] = w_ref[0].astype(BF16)
        or_ref[0] = result(xr_ref[0])

    if len(o_ref.shape) == 4:
        o_ref[0, 0] = result(x_ref[0])
    else:
        o_ref[0] = result(x_ref[0])


def _proj(x, rider, w_stack, l, col0, ncols, *, head_g=None, scale=1.0, stack=None, tm=1024, tn=1024):
    B, T, K = x.shape
    R = rider.shape[1]
    tm = min(tm, T)
    tn = min(tn, ncols)
    assert T % tm == 0 and ncols % tn == 0 and col0 % tn == 0
    cb = col0 // tn
    in_specs = [pl.BlockSpec((1, tm, K), lambda n, b, i: (b, i, 0)),
                pl.BlockSpec((1, K, tn), lambda n, b, i: (l, 0, cb + n))]
    args = [x, w_stack]
    if head_g is not None:
        in_specs.append(pl.BlockSpec((1, 1, tn), lambda n, b, i: (0, 0, 0)))
        args.append(jnp.tile(head_g, tn // HEAD_DIM).reshape(1, 1, tn))
    aliases = {}
    if stack is None:
        out_spec = pl.BlockSpec((1, tm, tn), lambda n, b, i: (b, i, n))
        out_shape = jax.ShapeDtypeStruct((B, T, ncols), F32)
    else:
        out_spec = pl.BlockSpec((1, 1, tm, tn), lambda n, b, i: (l, b, i, n))
        out_shape = jax.ShapeDtypeStruct(stack.shape, F32)
        in_specs.append(pl.BlockSpec(memory_space=pl.ANY))
        args.append(stack)
        aliases = {len(args) - 1: 0}
    r_ins, r_out = _rider_specs(R, (K,), tn)
    nbytes = 2 * (tm * K * 2 + K * tn * 4 + tm * tn * 4) + K * tn * 2 + 2 * tm * tn * 4
    return pl.pallas_call(
        functools.partial(_mm_kernel, headnorm=head_g is not None, scale=scale, aliased=bool(aliases)),
        grid=(ncols // tn, B, T // tm),
        in_specs=in_specs + r_ins,
        out_specs=[out_spec, r_out],
        out_shape=[out_shape, jax.ShapeDtypeStruct((1, R, ncols), F32)],
        scratch_shapes=[pltpu.VMEM((K, tn), BF16)],
        input_output_aliases=aliases,
        compiler_params=_params(("arbitrary",) * 3, nbytes),
        name="proj",
    )(*args, rider)


def _swiglu_kernel(x_ref, wg_ref, wu_ref, xr_ref, o_ref, or_ref, wgb_ref, wub_ref):
    def result(x):
        gate = jnp.dot(x, wgb_ref[...], preferred_element_type=F32)
        up = jnp.dot(x, wub_ref[...], preferred_element_type=F32)
        return (gate * _sigmoid(gate) * up).astype(o_ref.dtype)

    @pl.when(_first_inner_step())
    def _():
        wgb_ref[...] = wg_ref[0].astype(BF16)
        wub_ref[...] = wu_ref[0].astype(BF16)
        or_ref[0] = result(xr_ref[0])

    o_ref[0] = result(x_ref[0])


def _proj_swiglu(x, rider, w_gu, l, *, tm=1024, tn=512):
    B, T, K = x.shape
    R = rider.shape[1]
    F = w_gu.shape[2] // 2
    tm = min(tm, T)
    assert T % tm == 0 and F % tn == 0
    nb = F // tn
    r_ins, r_out = _rider_specs(R, (K,), tn)
    nbytes = 2 * (tm * K * 2 + 2 * K * tn * 4 + tm * tn * 2) + 2 * K * tn * 2 + 3 * tm * tn * 4
    return pl.pallas_call(
        _swiglu_kernel,
        grid=(nb, B, T // tm),
        in_specs=[pl.BlockSpec((1, tm, K), lambda n, b, i: (b, i, 0)),
                  pl.BlockSpec((1, K, tn), lambda n, b, i: (l, 0, n)),
                  pl.BlockSpec((1, K, tn), lambda n, b, i: (l, 0, nb + n))] + r_ins,
        out_specs=[pl.BlockSpec((1, tm, tn), lambda n, b, i: (b, i, n)), r_out],
        out_shape=[jax.ShapeDtypeStruct((B, T, F), BF16), jax.ShapeDtypeStruct((1, R, F), BF16)],
        scratch_shapes=[pltpu.VMEM((K, tn), BF16), pltpu.VMEM((K, tn), BF16)],
        compiler_params=_params(("arbitrary",) * 3, nbytes),
        name="proj_swiglu",
    )(x, w_gu, w_gu, rider)


def _resid_kernel(*refs, k_sizes):
    n_x = len(k_sizes)
    x_refs, refs = refs[:n_x], refs[n_x:]
    w_ref, res_ref, gate_ref = refs[:3]
    xr_refs, refs = refs[3:3 + n_x], refs[3 + n_x:]
    resr_ref, gater_ref, o_ref, or_ref, wb_ref = refs

    def result(xs, res, gate):
        acc = None
        k0 = 0
        for x_ref, ks in zip(xs, k_sizes):
            part = jnp.dot(x_ref[0], wb_ref[k0:k0 + ks, :], preferred_element_type=F32)
            acc = part if acc is None else acc + part
            k0 += ks
        return res[0] + gate[0] * acc

    @pl.when(_first_inner_step())
    def _():
        wb_ref[...] = w_ref[0].astype(BF16)
        or_ref[0] = result(xr_refs, resr_ref, gater_ref)

    o_ref[0] = result(x_refs, res_ref, gate_ref)


def _proj_resid(xs, rider_xs, w_stack, l, resid, gate, rider_resid, rider_gate, *, tm=512, tn=512):
    B, T, N = resid.shape
    R = rider_resid.shape[1]
    k_sizes = tuple(x.shape[2] for x in xs)
    K = sum(k_sizes)
    tm = min(tm, T)
    assert T % tm == 0 and N % tn == 0 and gate.shape[1] == 1
    in_specs = [pl.BlockSpec((1, tm, ks), lambda n, b, i: (b, i, 0)) for ks in k_sizes]
    in_specs += [pl.BlockSpec((1, K, tn), lambda n, b, i: (l, 0, n)),
                 pl.BlockSpec((1, tm, tn), lambda n, b, i: (b, i, n)),
                 pl.BlockSpec((1, 1, tn), lambda n, b, i: (b, 0, n))]
    r_ins, r_out = _rider_specs(R, k_sizes, tn)
    nbytes = 2 * (tm * K * 2 + K * tn * 4 + 2 * tm * tn * 4) + K * tn * 2 + 2 * tm * tn * 4
    return pl.pallas_call(
        functools.partial(_resid_kernel, k_sizes=k_sizes),
        grid=(N // tn, B, T // tm),
        in_specs=in_specs + r_ins + [r_out, r_out],
        out_specs=[pl.BlockSpec((1, tm, tn), lambda n, b, i: (b, i, n)), r_out],
        out_shape=[jax.ShapeDtypeStruct((B, T, N), F32), jax.ShapeDtypeStruct((1, R, N), F32)],
        scratch_shapes=[pltpu.VMEM((K, tn), BF16)],
        compiler_params=_params(("arbitrary",) * 3, nbytes),
        name="proj_resid",
    )(*xs, w_stack, resid, gate, *rider_xs, rider_resid, rider_gate)


def _rel_bucket(dist):
    max_exact = NUM_BUCKETS // 2
    d = jnp.maximum(dist, 0)
    df = jnp.maximum(d, 1).astype(F32)
    large = max_exact + (jnp.log(df / max_exact) / math.log(MAX_DISTANCE / max_exact)
                         * (NUM_BUCKETS - max_exact)).astype(jnp.int32)
    return jnp.where(d < max_exact, d, jnp.minimum(large, NUM_BUCKETS - 1))


def _band_bias_rows(rel_bias):
    dsub = Q_BLOCK - jnp.arange(2 * Q_BLOCK)
    rows = []
    for (w, dil) in BRANCHES:
        valid = (dsub >= 0) & (dsub <= w // dil)
        bias = rel_bias[_rel_bucket(jnp.maximum(dsub, 0) * dil)].astype(F32).T
        rows.append(jnp.where(valid[None], bias, NEG))
    return jnp.stack(rows, axis=1)


def _attn_kernel(q_ref, k_ref, v_ref, base_ref, o_ref, qs, ks, vs, os_, ms_, ds_, stage):
    T = q_ref.shape[2]
    QB = Q_BLOCK
    low = lax.broadcasted_iota(jnp.int32, (1, LANES), 1) < HEAD_DIM
    own = (low, jnp.logical_not(low))
    prev_cols = lax.broadcasted_iota(jnp.int32, (1, 2 * QB), 1) < QB

    assert len(BRANCHES) == 3 and BRANCHES[0][1] == 1
    prev_dil = 1
    for bi, (_, dil) in enumerate(BRANCHES):
        L, Lp, ratio = T // dil, T // prev_dil, dil // prev_dil
        ks[bi, 0:QB, :] = jnp.zeros((QB, LANES), BF16)
        vs[bi, 0:QB, :] = jnp.zeros((QB, LANES), BF16)
        for r in range(dil):
            rows = pl.ds((r % prev_dil) * Lp + r // prev_dil, L, stride=ratio)
            if bi < 2:
                q, k, v = (ref[0, 0, rows, :] for ref in (q_ref, k_ref, v_ref))
            else:
                q, k, v = (stage[a, rows, :] for a in range(3))
            if bi == 1:
                for a, x in enumerate((q, k, v)):
                    stage[a, r * L:(r + 1) * L, :] = x
            for hh in range(2):
                qs[hh, bi, r * L:(r + 1) * L, :] = jnp.where(own[hh], q, 0.0).astype(BF16)
            ks[bi, QB + r * L:QB + (r + 1) * L, :] = k.astype(BF16)
            vs[bi, QB + r * L:QB + (r + 1) * L, :] = v.astype(BF16)
        prev_dil = dil

    for bi, (_, dil) in enumerate(BRANCHES):
        nb = T // dil // QB
        tiles = [pltpu.roll(jnp.broadcast_to(base_ref[hh, bi:bi + 1, :], (QB, 2 * QB)), 0, 1, stride=1, stride_axis=0)
                 for hh in range(2)]
        first_only = nb == 1
        if first_only:
            tiles = [t[:, QB:] for t in tiles]
        k0, kn = (QB, QB) if first_only else (0, 2 * QB)

        def one_block(g, q0, q1, kw, vw, nb=nb, tiles=tiles, first_only=first_only):
            ss = [lax.dot_general(q, kw, (((1,), (1,)), ((), ())), preferred_element_type=F32) + tiles[hh]
                  for hh, q in enumerate((q0, q1))]
            if not first_only:
                no_prev = jnp.logical_and(g % nb == 0, prev_cols)
                ss = [jnp.where(no_prev, NEG, s) for s in ss]
            yield
            ms = [jnp.max(s, axis=-1, keepdims=True) for s in ss]
            ps = [jnp.exp(s - m) for s, m in zip(ss, ms)]
            dens = [jnp.sum(p, axis=-1, keepdims=True) for p in ps]
            os2 = [jnp.dot(p.astype(BF16), vw, preferred_element_type=F32) for p in ps]
            yield
            pair = lambda x0, x1: jnp.where(low, jnp.broadcast_to(x0, (QB, LANES)), jnp.broadcast_to(x1, (QB, LANES)))
            return pair(*os2), pair(*ms), pair(*dens)

        def blocks(i, _, bi=bi, dil=dil, nb=nb, k0=k0, kn=kn):
            gs = [i * ATT_UNROLL + u for u in range(ATT_UNROLL)]
            ats = [pl.multiple_of(g * QB, QB) for g in gs]
            loaded = [(qs[0, bi, pl.ds(at, QB), :], qs[1, bi, pl.ds(at, QB), :],
                       ks[bi, pl.ds(at + k0, kn), :], vs[bi, pl.ds(at + k0, kn), :]) for at in ats]
            results = _round_robin([one_block(g, *ld) for g, ld in zip(gs, loaded)])
            for g, (acc, m, den) in zip(gs, results):
                r = g // nb
                tok = pl.ds((g - r * nb) * (QB * dil) + r, QB, stride=dil)
                os_[bi, tok, :] = acc
                ms_[bi, tok, :] = m
                ds_[bi, tok, :] = den
            return 0

        lax.fori_loop(0, T // QB // ATT_UNROLL, blocks, 0)

    def merge(i, _):
        rows = pl.ds(pl.multiple_of(i * QB, QB), QB)
        maxes = [ms_[bi, rows, :] for bi in range(len(BRANCHES))]
        m = functools.reduce(jnp.maximum, maxes)
        ws = [jnp.exp(x - m) for x in maxes]
        num = sum(w * os_[bi, rows, :] for bi, w in enumerate(ws))
        den = sum(w * ds_[bi, rows, :] for bi, w in enumerate(ws))
        o_ref[0, rows, :] = (num / den).astype(o_ref.dtype)
        return 0

    lax.fori_loop(0, T // QB, merge, 0)


def _attention_prompt(q, k_stack, v_stack, l, base):
    B, T, DA = q.shape
    nbr = len(BRANCHES)
    assert T % (Q_BLOCK * max(d for _, d in BRANCHES)) == 0
    qkv = pl.BlockSpec((1, 1, T, LANES), lambda p, b: (l, b, 0, p))
    scratch = [pltpu.VMEM((2, nbr, T, LANES), BF16), pltpu.VMEM((nbr, T + Q_BLOCK, LANES), BF16),
               pltpu.VMEM((nbr, T + Q_BLOCK, LANES), BF16), pltpu.VMEM((nbr, T, LANES), F32),
               pltpu.VMEM((nbr, T, LANES), F32), pltpu.VMEM((nbr, T, LANES), F32), pltpu.VMEM((3, T, LANES), F32)]
    nbytes = (2 * 3 * T * LANES * 4 + 4 * nbr * (T + Q_BLOCK) * LANES * 2 + 3 * nbr * T * LANES * 4
              + 3 * T * LANES * 4 + 2 * T * LANES * 2)
    return pl.pallas_call(
        _attn_kernel,
        grid=(DA // LANES, B),
        in_specs=[pl.BlockSpec((1, 1, T, LANES), lambda p, b: (0, b, 0, p)), qkv, qkv,
                  pl.BlockSpec((2, nbr, 2 * Q_BLOCK), lambda p, b: (p, 0, 0))],
        out_specs=pl.BlockSpec((1, T, LANES), lambda p, b: (b, 0, p)),
        out_shape=jax.ShapeDtypeStruct((B, T, DA), BF16),
        scratch_shapes=scratch,
        compiler_params=_params(("arbitrary", "arbitrary"), nbytes),
        name="attn_prompt",
    )(q[None], k_stack, v_stack, base)


def _distance_logits(rel_bias, nd):
    d = jnp.arange(nd)
    mult = sum(((d % dil == 0) & (d // dil <= w // dil)).astype(F32) for (w, dil) in BRANCHES)
    bias = rel_bias[_rel_bucket(d)].astype(F32).T
    return jnp.where(mult > 0, bias + jnp.log(jnp.maximum(mult, 1.0)), NEG)


def _sattn_kernel(q_ref, kn_ref, vn_ref, kt_ref, vt_ref, tab_ref, o_ref):
    row = lax.broadcasted_iota(jnp.int32, (SUBLANES, LANES), 0)
    low = lax.broadcasted_iota(jnp.int32, (SUBLANES, LANES), 1) < HEAD_DIM
    own = ((row == 0) & low) | ((row == 1) & jnp.logical_not(low))
    q = jnp.where(own, q_ref[0], 0.0).astype(BF16)
    kn = kn_ref[0].astype(BF16).astype(F32)
    vn = vn_ref[0].astype(BF16).astype(F32)
    tab = tab_ref[0]
    W = kt_ref.shape[3]
    s_self = jnp.sum(q.astype(F32) * kn, axis=-1, keepdims=True) + tab[:, W:W + 1]
    s = _dot(q, kt_ref[0, 0]) + tab[:, :W]
    m = jnp.maximum(s_self, jnp.max(s, axis=-1, keepdims=True))
    p_self = jnp.exp(s_self - m)
    p = jnp.exp(s - m)
    den = p_self + jnp.sum(p, axis=-1, keepdims=True)
    o = (p_self * vn + _dot_nt(p, vt_ref[0, 0])) / den
    o_ref[0] = jnp.where(low[0:1], o[0:1], o[1:2]).astype(o_ref.dtype)


def _attention_sample(q, k_new, v_new, cache_k, cache_v, l, rel_bias):
    L, DB, W, H, E = cache_k.shape
    DA = H * E
    assert W >= max(w for w, _ in BRANCHES)
    tab = _distance_logits(rel_bias, W + 1)[:, ::-1].reshape(H // 2, 2, W + 1)
    tab = jnp.concatenate([tab, jnp.zeros((H // 2, SUBLANES - 2, W + 1), F32)], axis=1)
    kt = jnp.transpose(cache_k, (0, 1, 3, 4, 2)).reshape(L, DB, DA, W)
    vt = jnp.transpose(cache_v, (0, 1, 3, 4, 2)).reshape(L, DB, DA, W)
    vec = pl.BlockSpec((1, 1, LANES), lambda p, b: (b, 0, p))
    buf = pl.BlockSpec((1, 1, LANES, W), lambda p, b: (l, b, p, 0))
    return pl.pallas_call(
        _sattn_kernel,
        grid=(DA // LANES, DB),
        in_specs=[vec, vec, vec, buf, buf, pl.BlockSpec((1, SUBLANES, W + 1), lambda p, b: (p, 0, 0))],
        out_specs=vec,
        out_shape=jax.ShapeDtypeStruct((DB, 1, DA), BF16),
        compiler_params=_params(("arbitrary", "arbitrary"), 2 * 2 * LANES * W * 4 + 4 * SUBLANES * W * 4),
        name="attn_sample",
    )(q, k_new, v_new, kt, vt, tab)


def _prep_kernel(*refs, shift_rows, has_vres):
    it = iter(refs)
    h_ref = next(it)
    if shift_rows:
        hp8_ref, hlast_ref = next(it), next(it)
    else:
        hprev_ref = next(it)
    rkv_ref = next(it)
    if shift_rows:
        rp8_ref, rlast_ref = next(it), next(it)
    else:
        rprev_ref = next(it)
    if has_vres:
        vfirst_ref = next(it)
    mu_ref, murkv_ref, w0_ref, dw1_ref, dw2_ref, a0_ref, aw1_ref, aw2_ref, gw1_ref, gw2_ref = (next(it) for _ in range(10))
    if has_vres:
        vmu_ref, v0_ref, vw1_ref, vw2_ref = (next(it) for _ in range(4))
    kk_ref, ka_ref = next(it), next(it)
    r_out, lw_out, k_out, v_out, kk_out, b_out, g_out = (next(it) for _ in range(7))

    h = h_ref[0]
    rkv0 = rkv_ref[0]
    tm = h.shape[0]
    if shift_rows:
        first = pl.program_id(1) == 0
        row0 = lax.broadcasted_iota(jnp.int32, (tm, 1), 0) == 0
        h_edge = jnp.where(first, hlast_ref[0], hp8_ref[0, SUBLANES - 1:SUBLANES, :])
        r_edge = jnp.where(first, rlast_ref[0], rp8_ref[0, SUBLANES - 1:SUBLANES, :])
        hprev = jnp.where(row0, h_edge, pltpu.roll(h, 1, 0))
        rprev = jnp.where(row0, r_edge, pltpu.roll(rkv0, 1, 0))
    else:
        hprev = hprev_ref[0]
        rprev = rprev_ref[0]

    hb = h.astype(BF16)
    dhb = (hprev - h).astype(BF16)
    mu = mu_ref[0].astype(BF16)
    xw = hb + dhb * mu[0:1]
    xa = hb + dhb * mu[1:2]
    xg = hb + dhb * mu[2:3]

    z = w0_ref[0] + _dot(jnp.tanh(_dot(xw, dw1_ref[0])), dw2_ref[0])
    lw_out[0] = -math.exp(-0.5) * _sigmoid(z)

    a = _sigmoid(a0_ref[0] + _dot(_dot(xa, aw1_ref[0]), aw2_ref[0]))
    g_out[0] = _dot(_sigmoid(_dot(xg, gw1_ref[0])), gw2_ref[0])

    murkv = murkv_ref[0]
    DR = kk_ref.shape[-1]
    r0, k0, v0 = (rkv0[:, j * DR:(j + 1) * DR] for j in range(3))
    rp, kp, vp = (rprev[:, j * DR:(j + 1) * DR] for j in range(3))
    r_out[0] = r0 + (rp - r0) * murkv[0:1]
    kr = k0 + (kp - k0) * murkv[1:2]
    vr = v0 + (vp - v0) * murkv[2:3]
    if has_vres:
        xv = hb + dhb * vmu_ref[0].astype(BF16)
        vgate = _sigmoid(v0_ref[0] + _dot(_dot(xv, vw1_ref[0]), vw2_ref[0]))
        vr = vr + (vfirst_ref[0] - vr) * vgate
    v_out[0] = vr
    kk = kr * kk_ref[0]
    kkn = kk * lax.rsqrt(jnp.maximum(_head_sums(kk * kk), 1e-24))
    kk_out[0] = kkn
    b_out[0] = kkn * a
    k_out[0] = kr * (1.0 + (a - 1.0) * ka_ref[0])


def _rwkv_prep(h, h_prev, rkv0, rkv_prev, v_first, P, l, *, tm=256):
    B, T, D = h.shape
    DR = rkv0.shape[2] // 3
    tm = min(tm, T)
    shift_rows = h_prev.shape[1] == 1 and T > 1
    has_vres = v_first is not None

    def tile(C):
        return pl.BlockSpec((1, tm, C), lambda b, i: (b, i, 0))

    def prev8(C):
        return pl.BlockSpec((1, SUBLANES, C), lambda b, i: (b, jnp.maximum(i * (tm // SUBLANES) - 1, 0), 0))

    def seq_row(C):
        return pl.BlockSpec((1, 1, C), lambda b, i: (b, 0, 0))

    def layer(shape, ll=l):
        return pl.BlockSpec((1,) + shape, lambda b, i: (ll,) + (0,) * len(shape))

    args, specs = [h], [tile(D)]
    if shift_rows:
        args += [h, h_prev]
        specs += [prev8(D), seq_row(D)]
    else:
        args += [jnp.broadcast_to(h_prev, h.shape)]
        specs += [tile(D)]
    args.append(rkv0)
    specs.append(tile(3 * DR))
    if shift_rows:
        args += [rkv0, rkv_prev]
        specs += [prev8(3 * DR), seq_row(3 * DR)]
    else:
        args += [jnp.broadcast_to(rkv_prev, rkv0.shape)]
        specs += [tile(3 * DR)]
    if has_vres:
        args.append(v_first)
        specs.append(tile(DR))
    r1 = lambda a: a.reshape(a.shape[0], 1, a.shape[-1])
    for name in ('mu_wag', 'mu_rkv'):
        args.append(P[name]); specs.append(layer(P[name].shape[1:]))
    args.append(r1(P['decay_w0'])); specs.append(layer((1, DR)))
    for name in ('decay_w1', 'decay_w2'):
        args.append(P[name]); specs.append(layer(P[name].shape[1:]))
    args.append(r1(P['aaa_a0'])); specs.append(layer((1, DR)))
    for name in ('aaa_w1', 'aaa_w2', 'gate_w1', 'gate_w2'):
        args.append(P[name]); specs.append(layer(P[name].shape[1:]))
    if has_vres:
        args.append(r1(P['vres_mu'])); specs.append(layer((1, D), l - 1))
        args.append(r1(P['vres_v0'])); specs.append(layer((1, DR), l - 1))
        for name in ('vres_w1', 'vres_w2'):
            args.append(P[name]); specs.append(layer(P[name].shape[1:], l - 1))
    args.append(r1(P['k_k'])); specs.append(layer((1, DR)))
    args.append(r1(P['k_a'])); specs.append(layer((1, DR)))
    nbytes = 2 * tm * 4 * (2 * D + 7 * DR + 7 * DR + DR) + 8 * tm * D * 4 + 4 * D * 512 * 4
    return pl.pallas_call(
        functools.partial(_prep_kernel, shift_rows=shift_rows, has_vres=has_vres),
        grid=(B, T // tm),
        in_specs=specs,
        out_specs=[tile(DR) for _ in range(7)],
        out_shape=[jax.ShapeDtypeStruct((B, T, DR), F32) for _ in range(7)],
        compiler_params=_params(("arbitrary", "arbitrary"), nbytes),
        name="rwkv_prep",
    )(*args)


def _prefix_sum_rows(x):
    n = x.shape[0]
    row = lax.broadcasted_iota(jnp.int32, (n, 1), 0)
    s = 1
    while s < n:
        x = x + jnp.where(row >= s, pltpu.roll(x, s, 0), 0.0)
        s *= 2
    return x


def _unit_lower_inverse(a_strict, blk):
    n = a_strict.shape[0]
    ti = lax.broadcasted_iota(jnp.int32, (n, n), 0)
    si = lax.broadcasted_iota(jnp.int32, (n, n), 1)

    def lower_left(s):
        return ((ti // (2 * s)) == (si // (2 * s))) & ((ti % (2 * s)) >= s) & ((si % (2 * s)) < s)

    d = (ti == si).astype(F32) + jnp.where(lower_left(1), a_strict, 0.0)
    s = 2
    while s < blk:
        t = _dot(d, jnp.where(lower_left(s), a_strict, 0.0))
        yield
        d = d + _dot(t, d)
        yield
        s *= 2
    return d


def _round_robin(generators):
    results = [None] * len(generators)
    live = list(range(len(generators)))
    while live:
        for u in list(live):
            try:
                next(generators[u])
            except StopIteration as done:
                results[u] = done.value
                live.remove(u)
    return results


def _pair_chunk_terms(r, lw, k, v, kk, b, low, tri):
    C = r.shape[0]
    cum = _prefix_sum_rows(lw)
    g_in = jnp.exp(cum)
    g_inv = jnp.exp(-cum)
    g_end = g_in[C - 1:C]
    a_t = -kk * jnp.exp(cum - lw)
    r_t = r * g_in
    b_t = b * g_inv
    k_t = k * g_inv
    bg = b_t * g_end
    kg = k_t * g_end
    zc = jnp.zeros((C, LANES), F32)
    h0 = lambda x: jnp.where(low, x, 0.0)
    h1 = lambda x: jnp.where(low, 0.0, x)
    v0, v1 = h0(v), h1(v)
    ar = jnp.concatenate([a_t, r_t], axis=0)
    m0 = jnp.where(tri, _dot_nt(h0(ar), jnp.concatenate([b_t, k_t], axis=0)), 0.0)
    m1 = jnp.where(tri, _dot_nt(h1(ar), jnp.concatenate([k_t, b_t], axis=0)), 0.0)
    yield
    top0, bot0, top1, bot1 = m0[:C], m0[C:], m1[:C], m1[C:]
    stack2 = lambda x0, x1: jnp.concatenate([jnp.concatenate([x0, zc], axis=0),
                                             jnp.concatenate([zc, x1], axis=0)], axis=1)
    akv = _dot(stack2(top0, top1), jnp.concatenate([zc, v0, v1, zc], axis=0))
    a_sw = pltpu.roll(a_t, HEAD_DIM, 1)
    tinv = yield from _unit_lower_inverse(jnp.concatenate([h0(top0), h1(top1)], axis=0), C)
    x = _dot(tinv, akv + jnp.concatenate([h1(a_sw), h0(a_sw)], axis=0))
    yield
    z = jnp.concatenate([x[:C], v0, v1, x[C:]], axis=0)
    e = _dot(stack2(bot0, bot1), z)
    gh = _dot_tn(z, jnp.concatenate([h0(bg), h0(kg), h1(kg), h1(bg)], axis=0))
    yield
    y0 = jnp.where(low, e[:C], e[C:])
    r_eff = r_t + pltpu.roll(jnp.where(low, e[C:], e[:C]), HEAD_DIM, 1)
    h_mat = jnp.concatenate([h0(gh[:C]), h1(gh[C:])], axis=0)
    g_mat = jnp.concatenate([h0(gh[C:]), h1(gh[:C])], axis=0)
    return r_eff, y0, g_mat, h_mat, g_end


def _scan_kernel(r_ref, lw_ref, k_ref, v_ref, kk_ref, b_ref, g_ref, lng_ref, lnb_ref, rk_ref,
                 o_ref, s_ref, reff_s, y0_s, gm_s, hm_s, ge_s, *, chunk, unroll):
    C = chunk
    NS, T = r_ref.shape[:2]
    NC = T // C
    E = HEAD_DIM
    low = lax.broadcasted_iota(jnp.int32, (1, LANES), 1) < E
    ti = lax.broadcasted_iota(jnp.int32, (2 * C, 2 * C), 0)
    si = lax.broadcasted_iota(jnp.int32, (2 * C, 2 * C), 1)
    tri = (si % C) <= jnp.where(ti < C, ti - 1, ti - C)

    def phase1(i, _):
        sq = i // (NC // unroll)
        c0 = (i - sq * (NC // unroll)) * unroll
        cs = [c0 + u for u in range(unroll)]
        rows = [pl.ds(pl.multiple_of(c * C, C), C) for c in cs]
        loaded = [[ref[sq, rw, :] for ref in (r_ref, lw_ref, k_ref, v_ref, kk_ref, b_ref)] for rw in rows]
        terms = _round_robin([_pair_chunk_terms(*args, low, tri) for args in loaded])
        for c, rw, (r_eff, y0, g_mat, h_mat, g_end) in zip(cs, rows, terms):
            reff_s[sq, rw, :] = r_eff
            y0_s[sq, rw, :] = y0
            gm_s[sq, c] = g_mat
            hm_s[sq, c] = h_mat
            ge_s[sq, c] = jnp.broadcast_to(g_end, (SUBLANES, LANES))
        return 0

    lax.fori_loop(0, NS * (NC // unroll), phase1, 0)

    def phase2(c, states):
        rows = pl.ds(pl.multiple_of(c * C, C), C)
        ys = [_dot_nt(reff_s[sq, rows, :], S) + y0_s[sq, rows, :] for sq, S in enumerate(states)]
        new = tuple(S * ge_s[sq, c][0:1] + _dot(S, gm_s[sq, c]) + hm_s[sq, c] for sq, S in enumerate(states))
        for sq, y in enumerate(ys):
            y0_s[sq, rows, :] = y
        return new

    states = lax.fori_loop(0, NC, phase2, tuple(jnp.zeros((LANES, LANES), F32) for _ in range(NS)))
    for sq, S in enumerate(states):
        s_ref[sq, 0] = S[:E, :E]
        s_ref[sq, 1] = S[E:, E:]

    def finish(y, r, k, v, g):
        mu = _head_sums(y) * (1.0 / E)
        yield
        yc = y - mu
        var = _head_sums(yc * yc) * (1.0 / E)
        bonus = _head_sums(r * k * rk_ref[0])
        yield
        yn = yc * lax.rsqrt(var + GN_EPS) * lng_ref[0] + lnb_ref[0]
        return ((yn + bonus * v) * g).astype(o_ref.dtype)

    def phase3(i, _):
        sq = i // (NC // unroll)
        c0 = (i - sq * (NC // unroll)) * unroll
        rows = [pl.ds(pl.multiple_of((c0 + u) * C, C), C) for u in range(unroll)]
        loaded = [[y0_s[sq, rw, :]] + [ref[sq, rw, :] for ref in (r_ref, k_ref, v_ref, g_ref)] for rw in rows]
        outs = _round_robin([finish(*args) for args in loaded])
        for rw, out in zip(rows, outs):
            o_ref[sq, rw, :] = out
        return 0

    lax.fori_loop(0, NS * (NC // unroll), phase3, 0)


def _rwkv_scan(r, lw, k, v, kk, b, g, lnx_g, lnx_b, r_k, l):
    B, T, DR = r.shape
    H = DR // HEAD_DIM
    L = lnx_g.shape[0]
    C = min(SCAN_CHUNK, T)
    nc = T // C
    unroll = math.gcd(SCAN_UNROLL, nc)
    ns = math.gcd(SCAN_SEQS, B)
    assert T % C == 0
    seq = pl.BlockSpec((ns, T, LANES), lambda p, b_: (b_, 0, p))
    par = pl.BlockSpec((1, 1, LANES), lambda p, b_: (l, 0, p))
    scratch = [pltpu.VMEM((ns, T, LANES), F32), pltpu.VMEM((ns, T, LANES), F32),
               pltpu.VMEM((ns, nc, LANES, LANES), F32), pltpu.VMEM((ns, nc, LANES, LANES), F32),
               pltpu.VMEM((ns, nc, SUBLANES, LANES), F32)]
    nbytes = ns * (2 * 8 * T * LANES * 4 + 2 * T * LANES * 4 + 2 * nc * LANES * LANES * 4 + nc * SUBLANES * LANES * 4)
    return pl.pallas_call(
        functools.partial(_scan_kernel, chunk=C, unroll=unroll),
        grid=(DR // LANES, B // ns),
        in_specs=[seq] * 7 + [par] * 3,
        out_specs=[seq, pl.BlockSpec((ns, 2, HEAD_DIM, HEAD_DIM), lambda p, b_: (b_, p, 0, 0))],
        out_shape=[jax.ShapeDtypeStruct((B, T, DR), BF16), jax.ShapeDtypeStruct((B, H, HEAD_DIM, HEAD_DIM), F32)],
        scratch_shapes=scratch,
        compiler_params=_params(("arbitrary", "arbitrary"), nbytes),
        name="rwkv_scan",
    )(r, lw, k, v, kk, b, g, lnx_g.reshape(L, 1, DR), lnx_b.reshape(L, 1, DR), r_k.reshape(L, 1, DR))


def _step_kernel(s_ref, r_ref, lw_ref, k_ref, v_ref, kk_ref, b_ref, g_ref, lng_ref, lnb_ref, rk_ref,
                 o_ref, so_ref, *, n_heads):
    E = HEAD_DIM
    eye = (lax.broadcasted_iota(jnp.int32, (E, E), 0) == lax.broadcasted_iota(jnp.int32, (E, E), 1)).astype(F32)

    def head(hh):
        sl = slice(hh * E, (hh + 1) * E)
        S = s_ref[0, 0, hh]
        r, k, v, kk, b, g = (ref[0, :, sl] for ref in (r_ref, k_ref, v_ref, kk_ref, b_ref, g_ref))
        w = jnp.exp(lw_ref[0, :, sl])
        v_col = jnp.sum(eye * v, axis=-1, keepdims=True)
        sa = jnp.sum(S * (-kk), axis=-1, keepdims=True)
        bonus = jnp.sum(r * k * rk_ref[0, :, sl], axis=-1, keepdims=True)
        yield
        S = S * w + sa * b + v_col * k
        y_col = jnp.sum(S * r, axis=-1, keepdims=True)
        yield
        y = jnp.sum(eye * y_col, axis=0, keepdims=True)
        mu = jnp.mean(y, axis=-1, keepdims=True)
        yield
        yc = y - mu
        var = jnp.mean(yc * yc, axis=-1, keepdims=True)
        yield
        yn = yc * lax.rsqrt(var + GN_EPS) * lng_ref[0, :, sl] + lnb_ref[0, :, sl]
        return S, (yn + bonus * v) * g

    results = _round_robin([head(hh) for hh in range(n_heads)])
    for hh, (S, _) in enumerate(results):
        so_ref[0, hh] = S
    o_ref[0] = jnp.concatenate([o for _, o in results], axis=-1).astype(o_ref.dtype)


def _rwkv_step(state, r, lw, k, v, kk, b, g, lnx_g, lnx_b, r_k, l):
    DB, H = state.shape[1:3]
    DR = H * HEAD_DIM
    L = lnx_g.shape[0]
    vec = pl.BlockSpec((1, 1, DR), lambda b_: (b_, 0, 0))
    par = pl.BlockSpec((1, 1, DR), lambda b_: (l, 0, 0))
    st = pl.BlockSpec((1, H, HEAD_DIM, HEAD_DIM), lambda b_: (b_, 0, 0, 0))
    return pl.pallas_call(
        functools.partial(_step_kernel, n_heads=H),
        grid=(DB,),
        in_specs=[pl.BlockSpec((1, 1, H, HEAD_DIM, HEAD_DIM), lambda b_: (l, b_, 0, 0, 0))] + [vec] * 7 + [par] * 3,
        out_specs=[vec, st],
        out_shape=[jax.ShapeDtypeStruct((DB, 1, DR), BF16), jax.ShapeDtypeStruct(state.shape[1:], F32)],
        compiler_params=_params(("arbitrary",), 4 * H * HEAD_DIM * LANES * 4),
        name="rwkv_step",
    )(state, r, lw, k, v, kk, b, g, lnx_g.reshape(L, 1, DR), lnx_b.reshape(L, 1, DR), r_k.reshape(L, 1, DR))


def _layer(P, l, xp, xs, mods_p, mods_s, base, vf_p, vf_s, k_stack, v_stack, cache_k, cache_v, state_wkv, h_last_s):
    B, T, D = xp.shape
    DB = xs.shape[1]
    DA = P['rel_bias'].shape[1] * HEAD_DIM
    n_rkv = P['w_in'].shape[2] - 3 * DA
    w_in = P['w_in']
    h, hb = _norm_mod(xp, P['norm1_g'], l, mods_p[0], mods_p[1], [F32, BF16])
    hs, hbs = _norm_mod(xs, P['norm1_g'], l, mods_s[0], mods_s[1], [F32, BF16])
    q, q_s = _proj(hb, hbs, w_in, l, 0, DA, head_g=P['q_norm_g'][l], scale=ATT_SCALE)
    k_stack, k_s = _proj(hb, hbs, w_in, l, DA, DA, head_g=P['k_norm_g'][l], stack=k_stack)
    v_stack, v_s = _proj(hb, hbs, w_in, l, 2 * DA, DA, stack=v_stack)
    both = jnp.concatenate([hbs, h_last_s[None].astype(BF16)], axis=1)
    rkv0, rkv2_s = _proj(hb, both, w_in, l, 3 * DA, n_rkv)

    att = _attention_prompt(q, k_stack, v_stack, l, base)
    r, lw, kr, vr, kk, b, g = _rwkv_prep(h, jnp.zeros((B, 1, D), F32), rkv0, jnp.zeros((B, 1, n_rkv), F32), vf_p, P, l)
    if vf_p is None:
        vf_p = vr
    rw, state_p = _rwkv_scan(r, lw, kr, vr, kk, b, g, P['lnx_g'], P['lnx_b'], P['r_k'], l)

    per_seq = lambda t: t.reshape(DB, 1, t.shape[-1])
    att_s = _attention_sample(per_seq(q_s), per_seq(k_s), per_seq(v_s), cache_k, cache_v, l, P['rel_bias'])
    r, lw, kr, vr, kk, b, g = _rwkv_prep(hs, h_last_s[None], rkv2_s[:, :DB], rkv2_s[:, DB:], vf_s, P, l)
    if vf_s is None:
        vf_s = vr
    rw_s, state_s = _rwkv_step(state_wkv, *(per_seq(t) for t in (r, lw, kr, vr, kk, b, g)),
                                P['lnx_g'], P['lnx_b'], P['r_k'], l)

    xp, xs = _proj_resid([att, rw], [att_s.reshape(1, DB, DA), rw_s.reshape(1, DB, DA)], P['w_out'], l,
                         xp, mods_p[2], xs, mods_s[2], tm=1024, tn=1024)
    (h2,) = _norm_mod(xp, P['norm2_g'], l, mods_p[3], mods_p[4], [BF16])
    (h2s,) = _norm_mod(xs, P['norm2_g'], l, mods_s[3], mods_s[4], [BF16])
    act, act_s = _proj_swiglu(h2, h2s, P['w_gu'], l)
    xp, xs = _proj_resid([act], [act_s], P['w_down'], l, xp, mods_p[5], xs, mods_s[5], tm=512, tn=512)
    return xp, xs, vf_p, vf_s, k_stack, v_stack, state_p, h[:, -1], k_s[0], v_s[0], state_s, hs[0]


def kernel(x_prompt, x_sample, c_prompt, c_sample, cache_k, cache_v, state_wkv, state_shift, rel_bias, ada_w, ada_b, norm1_g, norm2_g, w_in, q_norm_g, k_norm_g, mu_wag, mu_rkv, decay_w0, decay_w1, decay_w2, aaa_a0, aaa_w1, aaa_w2, gate_w1, gate_w2, vres_mu, vres_v0, vres_w1, vres_w2, k_k, k_a, r_k, lnx_g, lnx_b, w_out, w_gu, w_down):
    P = dict(rel_bias=rel_bias, norm1_g=norm1_g, norm2_g=norm2_g, w_in=w_in, q_norm_g=q_norm_g,
             k_norm_g=k_norm_g, mu_wag=mu_wag, mu_rkv=mu_rkv, decay_w0=decay_w0, decay_w1=decay_w1,
             decay_w2=decay_w2, aaa_a0=aaa_a0, aaa_w1=aaa_w1, aaa_w2=aaa_w2, gate_w1=gate_w1,
             gate_w2=gate_w2, vres_mu=vres_mu, vres_v0=vres_v0, vres_w1=vres_w1, vres_w2=vres_w2,
             k_k=k_k, k_a=k_a, r_k=r_k.reshape(r_k.shape[0], -1), lnx_g=lnx_g, lnx_b=lnx_b,
             w_out=w_out, w_gu=w_gu, w_down=w_down)
    L = ada_w.shape[0]
    B, T, D = x_prompt.shape
    DB = x_sample.shape[0]
    H_ATT = rel_bias.shape[1]
    assert x_sample.shape[1] == 1 and T <= W_MAX

    rows = -(-(B + DB) // SUBLANES) * SUBLANES
    c_all = jnp.concatenate([c_prompt, c_sample, jnp.zeros((rows - B - DB, D), F32)], axis=0)
    mod = _ada(c_all, ada_w, ada_b).reshape(L, rows, 6, D)
    base = _band_bias_rows(rel_bias)

    k_stack = jnp.zeros((L, B, T, H_ATT * HEAD_DIM), F32)
    v_stack = jnp.zeros((L, B, T, H_ATT * HEAD_DIM), F32)
    xp, xs, vf_p, vf_s = x_prompt, x_sample.reshape(1, DB, D), None, None
    ps, ph, sk, sv, ss, sh = [], [], [], [], [], []
    for l in range(L):
        mods_p = [mod[l, :B, j][:, None, :] for j in range(6)]
        mods_s = [mod[l, B:B + DB, j][None] for j in range(6)]
        (xp, xs, vf_p, vf_s, k_stack, v_stack, state_p, shift_p, k_s, v_s, state_s, shift_s) = _layer(
            P, l, xp, xs, mods_p, mods_s, base, vf_p, vf_s, k_stack, v_stack, cache_k, cache_v, state_wkv,
            state_shift[l])
        ps.append(state_p)
        ph.append(shift_p)
        sk.append(k_s.reshape(DB, 1, H_ATT, HEAD_DIM))
        sv.append(v_s.reshape(DB, 1, H_ATT, HEAD_DIM))
        ss.append(state_s)
        sh.append(shift_s)

    return (xp, xs.reshape(DB, 1, D), k_stack.reshape(L, B, T, H_ATT, HEAD_DIM),
            v_stack.reshape(L, B, T, H_ATT, HEAD_DIM), jnp.stack(ps), jnp.stack(ph),
            jnp.stack(sk), jnp.stack(sv), jnp.stack(ss), jnp.stack(sh))
```

```python
import functools
import math

import jax
import jax.numpy as jnp
from jax import lax
from jax.experimental import pallas as pl
from jax.experimental.pallas import tpu as pltpu

F32 = jnp.float32
BF16 = jnp.bfloat16

HEAD_DIM = 64
BRANCHES = ((128, 1), (512, 4), (2048, 16))
W_MAX = 2048
NUM_BUCKETS = 32
MAX_DISTANCE = W_MAX
ATT_SCALE = HEAD_DIM ** -0.5
RMS_EPS = 1e-6
GN_EPS = 64e-5
NEG = -1e30

LANES = 128
SUBLANES = 8
Q_BLOCK = 128
SCAN_CHUNK = 64
SCAN_SEQS = 2
SCAN_UNROLL = 16
ATT_UNROLL = 4
VMEM_CAP = 56 * 1024 * 1024

assert all(w // dil == Q_BLOCK for w, dil in BRANCHES) and 2 * HEAD_DIM == LANES


def _vmem(nbytes):
    return int(min(VMEM_CAP, nbytes * 1.3 + (6 << 20)))


def _params(sem, nbytes):
    return pltpu.CompilerParams(dimension_semantics=sem, vmem_limit_bytes=_vmem(nbytes))


def _dot(a, b):
    return jnp.dot(a.astype(BF16), b.astype(BF16), preferred_element_type=F32)


def _dot_nt(a, b):
    return lax.dot_general(a.astype(BF16), b.astype(BF16), (((1,), (1,)), ((), ())), preferred_element_type=F32)


def _dot_tn(a, b):
    return lax.dot_general(a.astype(BF16), b.astype(BF16), (((0,), (0,)), ((), ())), preferred_element_type=F32)


def _sigmoid(x):
    return 1.0 / (1.0 + jnp.exp(-x))


def _head_sums(x):
    r = lax.broadcasted_iota(jnp.int32, (LANES, LANES), 0) // HEAD_DIM
    c = lax.broadcasted_iota(jnp.int32, (LANES, LANES), 1) // HEAD_DIM
    bd = (r == c).astype(BF16)
    cols = [_dot(x[:, j * LANES:(j + 1) * LANES], bd) for j in range(x.shape[1] // LANES)]
    return cols[0] if len(cols) == 1 else jnp.concatenate(cols, axis=-1)


def _ada_kernel(c_ref, w_ref, b_ref, o_ref):
    c = c_ref[...]
    s = c * _sigmoid(c)
    o_ref[0] = _dot(s, w_ref[0]) + b_ref[0]


def _ada(c_all, ada_w, ada_b, tn=1024):
    L, D, N = ada_w.shape
    R = c_all.shape[0]
    return pl.pallas_call(
        _ada_kernel,
        grid=(L, N // tn),
        in_specs=[pl.BlockSpec((R, D), lambda l, j: (0, 0)),
                  pl.BlockSpec((1, D, tn), lambda l, j: (l, 0, j)),
                  pl.BlockSpec((1, 1, tn), lambda l, j: (l, 0, j))],
        out_specs=pl.BlockSpec((1, R, tn), lambda l, j: (l, 0, j)),
        out_shape=jax.ShapeDtypeStruct((L, R, N), F32),
        compiler_params=_params(("arbitrary", "arbitrary"), 2 * D * tn * 4 + D * tn * 2),
        name="ada_mod",
    )(c_all, ada_w, ada_b.reshape(L, 1, N))


def _norm_kernel(x_ref, g_ref, sh_ref, sc_ref, *out_refs):
    x = x_ref[0]
    y = x * lax.rsqrt(jnp.mean(x * x, axis=-1, keepdims=True) + RMS_EPS) * g_ref[0]
    h = y * (1.0 + sc_ref[0, :, 0, :]) + sh_ref[0, :, 0, :]
    for o in out_refs:
        o[0] = h.astype(o.dtype)


def _mod_spec(l, rows, which, D):
    r0, n = rows
    if n is None:
        return pl.BlockSpec((1, 1, 1, D), lambda b, i: (l, r0 + b, 0, which))
    return pl.BlockSpec((1, n, 1, D), lambda b, i: (l, 0, 0, which))


def _norm_mod(x, g_stack, l, mod, rows, which, out_dtypes, tm=512):
    B, T, D = x.shape
    tm = min(tm, T)
    L = g_stack.shape[0]
    assert rows[1] is None or (B == 1 and rows[1] == T == tm)
    outs = pl.pallas_call(
        _norm_kernel,
        grid=(B, T // tm),
        in_specs=[pl.BlockSpec((1, tm, D), lambda b, i: (b, i, 0)),
                  pl.BlockSpec((1, 1, D), lambda b, i: (l, 0, 0)),
                  _mod_spec(l, rows, which, D), _mod_spec(l, rows, which + 1, D)],
        out_specs=[pl.BlockSpec((1, tm, D), lambda b, i: (b, i, 0)) for _ in out_dtypes],
        out_shape=[jax.ShapeDtypeStruct((B, T, D), dt) for dt in out_dtypes],
        compiler_params=_params(("arbitrary", "arbitrary"), 2 * tm * D * 4 * (2 + len(out_dtypes))),
        name="norm_mod",
    )(x, g_stack.reshape(L, 1, D), mod, mod)
    return outs


def _first_inner_step():
    return (pl.program_id(1) == 0) & (pl.program_id(2) == 0)


def _head_rmsnorm(acc, g):
    return acc * lax.rsqrt(_head_sums(acc * acc) * (1.0 / HEAD_DIM) + RMS_EPS) * g


def _rider_specs(R, k_sizes, tn):
    ins = [pl.BlockSpec((1, R, ks), lambda n, b, i: (0, 0, 0)) for ks in k_sizes]
    return ins, pl.BlockSpec((1, R, tn), lambda n, b, i: (0, 0, n))


def _mm_kernel(*refs, headnorm, scale, aliased):
    it = iter(refs)
    x_ref, w_ref = next(it), next(it)
    g_ref = next(it) if headnorm else None
    if aliased:
        next(it)
    xr_ref, o_ref, or_ref, wb_ref = next(it), next(it), next(it), next(it)

    def result(x):
        acc = jnp.dot(x, wb_ref[...], preferred_element_type=F32)
        if headnorm:
            acc = _head_rmsnorm(acc, g_ref[0])
            if scale != 1.0:
                acc = acc * scale
        return acc

    @pl.when(_first_inner_step())
    def _():
        wb_ref[...] = w_ref[0].astype(BF16)
        or_ref[0] = result(xr_ref[0])

    if len(o_ref.shape) == 4:
        o_ref[0, 0] = result(x_ref[0])
    else:
        o_ref[0] = result(x_ref[0])


def _proj(x, rider, w_stack, l, col0, ncols, *, head_g=None, scale=1.0, stack=None, tm=1024, tn=1024):
    B, T, K = x.shape
    R = rider.shape[1]
    tm = min(tm, T)
    tn = min(tn, ncols)
    assert T % tm == 0 and ncols % tn == 0 and col0 % tn == 0
    cb = col0 // tn
    in_specs = [pl.BlockSpec((1, tm, K), lambda n, b, i: (b, i, 0)),
                pl.BlockSpec((1, K, tn), lambda n, b, i: (l, 0, cb + n))]
    args = [x, w_stack]
    if head_g is not None:
        in_specs.append(pl.BlockSpec((1, 1, tn), lambda n, b, i: (l, 0, 0)))
        args.append(jnp.tile(head_g, (1, tn // HEAD_DIM)).reshape(head_g.shape[0], 1, tn))
    aliases = {}
    if stack is None:
        out_spec = pl.BlockSpec((1, tm, tn), lambda n, b, i: (b, i, n))
        out_shape = jax.ShapeDtypeStruct((B, T, ncols), F32)
    else:
        out_spec = pl.BlockSpec((1, 1, tm, tn), lambda n, b, i: (l, b, i, n))
        out_shape = jax.ShapeDtypeStruct(stack.shape, F32)
        in_specs.append(pl.BlockSpec(memory_space=pl.ANY))
        args.append(stack)
        aliases = {len(args) - 1: 0}
    r_ins, r_out = _rider_specs(R, (K,), tn)
    nbytes = 2 * (tm * K * 2 + K * tn * 4 + tm * tn * 4) + K * tn * 2 + 2 * tm * tn * 4
    return pl.pallas_call(
        functools.partial(_mm_kernel, headnorm=head_g is not None, scale=scale, aliased=bool(aliases)),
        grid=(ncols // tn, B, T // tm),
        in_specs=in_specs + r_ins,
        out_specs=[out_spec, r_out],
        out_shape=[out_shape, jax.ShapeDtypeStruct((1, R, ncols), F32)],
        scratch_shapes=[pltpu.VMEM((K, tn), BF16)],
        input_output_aliases=aliases,
        compiler_params=_params(("arbitrary",) * 3, nbytes),
        name="proj",
    )(*args, rider)


def _swiglu_kernel(x_ref, wg_ref, wu_ref, xr_ref, o_ref, or_ref, wgb_ref, wub_ref):
    def result(x):
        gate = jnp.dot(x, wgb_ref[...], preferred_element_type=F32)
        up = jnp.dot(x, wub_ref[...], preferred_element_type=F32)
        return (gate * _sigmoid(gate) * up).astype(o_ref.dtype)

    @pl.when(_first_inner_step())
    def _():
        wgb_ref[...] = wg_ref[0].astype(BF16)
        wub_ref[...] = wu_ref[0].astype(BF16)
        or_ref[0] = result(xr_ref[0])

    o_ref[0] = result(x_ref[0])


def _proj_swiglu(x, rider, w_gu, l, *, tm=1024, tn=512):
    B, T, K = x.shape
    R = rider.shape[1]
    F = w_gu.shape[2] // 2
    tm = min(tm, T)
    assert T % tm == 0 and F % tn == 0
    nb = F // tn
    r_ins, r_out = _rider_specs(R, (K,), tn)
    nbytes = 2 * (tm * K * 2 + 2 * K * tn * 4 + tm * tn * 2) + 2 * K * tn * 2 + 3 * tm * tn * 4
    return pl.pallas_call(
        _swiglu_kernel,
        grid=(nb, B, T // tm),
        in_specs=[pl.BlockSpec((1, tm, K), lambda n, b, i: (b, i, 0)),
                  pl.BlockSpec((1, K, tn), lambda n, b, i: (l, 0, n)),
                  pl.BlockSpec((1, K, tn), lambda n, b, i: (l, 0, nb + n))] + r_ins,
        out_specs=[pl.BlockSpec((1, tm, tn), lambda n, b, i: (b, i, n)), r_out],
        out_shape=[jax.ShapeDtypeStruct((B, T, F), BF16), jax.ShapeDtypeStruct((1, R, F), BF16)],
        scratch_shapes=[pltpu.VMEM((K, tn), BF16), pltpu.VMEM((K, tn), BF16)],
        compiler_params=_params(("arbitrary",) * 3, nbytes),
        name="proj_swiglu",
    )(x, w_gu, w_gu, rider)


def _resid_kernel(*refs, k_sizes):
    n_x = len(k_sizes)
    x_refs, refs = refs[:n_x], refs[n_x:]
    w_ref, res_ref, gate_ref = refs[:3]
    xr_refs, refs = refs[3:3 + n_x], refs[3 + n_x:]
    resr_ref, gater_ref, o_ref, or_ref, wb_ref = refs

    def result(xs, res, gate):
        acc = None
        k0 = 0
        for x_ref, ks in zip(xs, k_sizes):
            part = jnp.dot(x_ref[0], wb_ref[k0:k0 + ks, :], preferred_element_type=F32)
            acc = part if acc is None else acc + part
            k0 += ks
        return res[0] + gate[0, :, 0, :] * acc

    @pl.when(_first_inner_step())
    def _():
        wb_ref[...] = w_ref[0].astype(BF16)
        or_ref[0] = result(xr_refs, resr_ref, gater_ref)

    o_ref[0] = result(x_refs, res_ref, gate_ref)


def _proj_resid(xs, rider_xs, w_stack, l, resid, rider_resid, mod, row0, which, *, tm=512, tn=512):
    B, T, N = resid.shape
    R = rider_resid.shape[1]
    k_sizes = tuple(x.shape[2] for x in xs)
    K = sum(k_sizes)
    tm = min(tm, T)
    assert T % tm == 0 and N % tn == 0
    c0 = which * (N // tn)
    in_specs = [pl.BlockSpec((1, tm, ks), lambda n, b, i: (b, i, 0)) for ks in k_sizes]
    in_specs += [pl.BlockSpec((1, K, tn), lambda n, b, i: (l, 0, n)),
                 pl.BlockSpec((1, tm, tn), lambda n, b, i: (b, i, n)),
                 pl.BlockSpec((1, 1, 1, tn), lambda n, b, i: (l, row0 + b, 0, c0 + n))]
    r_ins, r_out = _rider_specs(R, k_sizes, tn)
    r_gate = pl.BlockSpec((1, R, 1, tn), lambda n, b, i: (l, 0, 0, c0 + n))
    nbytes = 2 * (tm * K * 2 + K * tn * 4 + 2 * tm * tn * 4) + K * tn * 2 + 2 * tm * tn * 4
    return pl.pallas_call(
        functools.partial(_resid_kernel, k_sizes=k_sizes),
        grid=(N // tn, B, T // tm),
        in_specs=in_specs + r_ins + [r_out, r_gate],
        out_specs=[pl.BlockSpec((1, tm, tn), lambda n, b, i: (b, i, n)), r_out],
        out_shape=[jax.ShapeDtypeStruct((B, T, N), F32), jax.ShapeDtypeStruct((1, R, N), F32)],
        scratch_shapes=[pltpu.VMEM((K, tn), BF16)],
        compiler_params=_params(("arbitrary",) * 3, nbytes),
        name="proj_resid",
    )(*xs, w_stack, resid, mod, *rider_xs, rider_resid, mod)


def _rel_bucket(dist):
    max_exact = NUM_BUCKETS // 2
    d = jnp.maximum(dist, 0)
    df = jnp.maximum(d, 1).astype(F32)
    large = max_exact + (jnp.log(df / max_exact) / math.log(MAX_DISTANCE / max_exact)
                         * (NUM_BUCKETS - max_exact)).astype(jnp.int32)
    return jnp.where(d < max_exact, d, jnp.minimum(large, NUM_BUCKETS - 1))


def _band_bias_rows(rel_bias):
    dsub = Q_BLOCK - jnp.arange(2 * Q_BLOCK)
    rows = []
    for (w, dil) in BRANCHES:
        valid = (dsub >= 0) & (dsub <= w // dil)
        bias = rel_bias[_rel_bucket(jnp.maximum(dsub, 0) * dil)].astype(F32).T
        rows.append(jnp.where(valid[None], bias, NEG))
    return jnp.stack(rows, axis=1)


def _attn_kernel(q_ref, k_ref, v_ref, base_ref, o_ref, qs, ks, vs, os_, ms_, ds_, stage):
    T = q_ref.shape[2]
    QB = Q_BLOCK
    low = lax.broadcasted_iota(jnp.int32, (1, LANES), 1) < HEAD_DIM
    own = (low, jnp.logical_not(low))
    prev_cols = lax.broadcasted_iota(jnp.int32, (1, 2 * QB), 1) < QB

    assert len(BRANCHES) == 3 and BRANCHES[0][1] == 1
    prev_dil = 1
    for bi, (_, dil) in enumerate(BRANCHES):
        L, Lp, ratio = T // dil, T // prev_dil, dil // prev_dil
        ks[bi, 0:QB, :] = jnp.zeros((QB, LANES), BF16)
        vs[bi, 0:QB, :] = jnp.zeros((QB, LANES), BF16)
        for r in range(dil):
            rows = pl.ds((r % prev_dil) * Lp + r // prev_dil, L, stride=ratio)
            if bi < 2:
                q, k, v = (ref[0, 0, rows, :] for ref in (q_ref, k_ref, v_ref))
            else:
                q, k, v = (stage[a, rows, :] for a in range(3))
            if bi == 1:
                for a, x in enumerate((q, k, v)):
                    stage[a, r * L:(r + 1) * L, :] = x
            for hh in range(2):
                qs[hh, bi, r * L:(r + 1) * L, :] = jnp.where(own[hh], q, 0.0).astype(BF16)
            ks[bi, QB + r * L:QB + (r + 1) * L, :] = k.astype(BF16)
            vs[bi, QB + r * L:QB + (r + 1) * L, :] = v.astype(BF16)
        prev_dil = dil

    for bi, (_, dil) in enumerate(BRANCHES):
        nb = T // dil // QB
        tiles = [pltpu.roll(jnp.broadcast_to(base_ref[hh, bi:bi + 1, :], (QB, 2 * QB)), 0, 1, stride=1, stride_axis=0)
                 for hh in range(2)]
        first_only = nb == 1
        if first_only:
            tiles = [t[:, QB:] for t in tiles]
        k0, kn = (QB, QB) if first_only else (0, 2 * QB)

        def one_block(g, q0, q1, kw, vw, nb=nb, tiles=tiles, first_only=first_only):
            ss = [lax.dot_general(q, kw, (((1,), (1,)), ((), ())), preferred_element_type=F32) + tiles[hh]
                  for hh, q in enumerate((q0, q1))]
            if not first_only:
                no_prev = jnp.logical_and(g % nb == 0, prev_cols)
                ss = [jnp.where(no_prev, NEG, s) for s in ss]
            yield
            ms = [jnp.max(s, axis=-1, keepdims=True) for s in ss]
            ps = [jnp.exp(s - m) for s, m in zip(ss, ms)]
            dens = [jnp.sum(p, axis=-1, keepdims=True) for p in ps]
            os2 = [jnp.dot(p.astype(BF16), vw, preferred_element_type=F32) for p in ps]
            yield
            pair = lambda x0, x1: jnp.where(low, jnp.broadcast_to(x0, (QB, LANES)), jnp.broadcast_to(x1, (QB, LANES)))
            return pair(*os2), pair(*ms), pair(*dens)

        def blocks(i, _, bi=bi, dil=dil, nb=nb, k0=k0, kn=kn):
            gs = [i * ATT_UNROLL + u for u in range(ATT_UNROLL)]
            ats = [pl.multiple_of(g * QB, QB) for g in gs]
            loaded = [(qs[0, bi, pl.ds(at, QB), :], qs[1, bi, pl.ds(at, QB), :],
                       ks[bi, pl.ds(at + k0, kn), :], vs[bi, pl.ds(at + k0, kn), :]) for at in ats]
            results = _round_robin([one_block(g, *ld) for g, ld in zip(gs, loaded)])
            for g, (acc, m, den) in zip(gs, results):
                r = g // nb
                tok = pl.ds((g - r * nb) * (QB * dil) + r, QB, stride=dil)
                os_[bi, tok, :] = acc
                ms_[bi, tok, :] = m
                ds_[bi, tok, :] = den
            return 0

        lax.fori_loop(0, T // QB // ATT_UNROLL, blocks, 0)

    def merge(i, _):
        rows = pl.ds(pl.multiple_of(i * QB, QB), QB)
        maxes = [ms_[bi, rows, :] for bi in range(len(BRANCHES))]
        m = functools.reduce(jnp.maximum, maxes)
        ws = [jnp.exp(x - m) for x in maxes]
        num = sum(w * os_[bi, rows, :] for bi, w in enumerate(ws))
        den = sum(w * ds_[bi, rows, :] for bi, w in enumerate(ws))
        o_ref[0, rows, :] = (num / den).astype(o_ref.dtype)
        return 0

    lax.fori_loop(0, T // QB, merge, 0)


def _attention_prompt(q, k_stack, v_stack, l, base):
    B, T, DA = q.shape
    nbr = len(BRANCHES)
    assert T % (Q_BLOCK * max(d for _, d in BRANCHES)) == 0
    qkv = pl.BlockSpec((1, 1, T, LANES), lambda p, b: (l, b, 0, p))
    scratch = [pltpu.VMEM((2, nbr, T, LANES), BF16), pltpu.VMEM((nbr, T + Q_BLOCK, LANES), BF16),
               pltpu.VMEM((nbr, T + Q_BLOCK, LANES), BF16), pltpu.VMEM((nbr, T, LANES), F32),
               pltpu.VMEM((nbr, T, LANES), F32), pltpu.VMEM((nbr, T, LANES), F32), pltpu.VMEM((3, T, LANES), F32)]
    nbytes = (2 * 3 * T * LANES * 4 + 4 * nbr * (T + Q_BLOCK) * LANES * 2 + 3 * nbr * T * LANES * 4
              + 3 * T * LANES * 4 + 2 * T * LANES * 2)
    return pl.pallas_call(
        _attn_kernel,
        grid=(DA // LANES, B),
        in_specs=[pl.BlockSpec((1, 1, T, LANES), lambda p, b: (0, b, 0, p)), qkv, qkv,
                  pl.BlockSpec((2, nbr, 2 * Q_BLOCK), lambda p, b: (p, 0, 0))],
        out_specs=pl.BlockSpec((1, T, LANES), lambda p, b: (b, 0, p)),
        out_shape=jax.ShapeDtypeStruct((B, T, DA), BF16),
        scratch_shapes=scratch,
        compiler_params=_params(("arbitrary", "arbitrary"), nbytes),
        name="attn_prompt",
    )(q[None], k_stack, v_stack, base)


def _distance_logits(rel_bias, nd):
    d = jnp.arange(nd)
    mult = sum(((d % dil == 0) & (d // dil <= w // dil)).astype(F32) for (w, dil) in BRANCHES)
    bias = rel_bias[_rel_bucket(d)].astype(F32).T
    return jnp.where(mult > 0, bias + jnp.log(jnp.maximum(mult, 1.0)), NEG)


def _sattn_kernel(q_ref, kn_ref, vn_ref, kt_ref, vt_ref, tab_ref, o_ref):
    row = lax.broadcasted_iota(jnp.int32, (SUBLANES, LANES), 0)
    low = lax.broadcasted_iota(jnp.int32, (SUBLANES, LANES), 1) < HEAD_DIM
    own = ((row == 0) & low) | ((row == 1) & jnp.logical_not(low))
    q = jnp.where(own, q_ref[0], 0.0).astype(BF16)
    kn = kn_ref[0].astype(BF16).astype(F32)
    vn = vn_ref[0].astype(BF16).astype(F32)
    tab = tab_ref[0]
    W = kt_ref.shape[3]
    s_self = jnp.sum(q.astype(F32) * kn, axis=-1, keepdims=True) + tab[:, W:W + 1]
    s = _dot(q, kt_ref[0, 0]) + tab[:, :W]
    m = jnp.maximum(s_self, jnp.max(s, axis=-1, keepdims=True))
    p_self = jnp.exp(s_self - m)
    p = jnp.exp(s - m)
    den = p_self + jnp.sum(p, axis=-1, keepdims=True)
    o = (p_self * vn + _dot_nt(p, vt_ref[0, 0])) / den
    o_ref[0] = jnp.where(low[0:1], o[0:1], o[1:2]).astype(o_ref.dtype)


def _attention_sample(q, k_new, v_new, cache_k, cache_v, l, rel_bias):
    L, DB, W, H, E = cache_k.shape
    DA = H * E
    assert W >= max(w for w, _ in BRANCHES)
    tab = _distance_logits(rel_bias, W + 1)[:, ::-1].reshape(H // 2, 2, W + 1)
    tab = jnp.concatenate([tab, jnp.zeros((H // 2, SUBLANES - 2, W + 1), F32)], axis=1)
    kt = jnp.transpose(cache_k, (0, 1, 3, 4, 2)).reshape(L, DB, DA, W)
    vt = jnp.transpose(cache_v, (0, 1, 3, 4, 2)).reshape(L, DB, DA, W)
    vec = pl.BlockSpec((1, 1, LANES), lambda p, b: (b, 0, p))
    buf = pl.BlockSpec((1, 1, LANES, W), lambda p, b: (l, b, p, 0))
    return pl.pallas_call(
        _sattn_kernel,
        grid=(DA // LANES, DB),
        in_specs=[vec, vec, vec, buf, buf, pl.BlockSpec((1, SUBLANES, W + 1), lambda p, b: (p, 0, 0))],
        out_specs=vec,
        out_shape=jax.ShapeDtypeStruct((DB, 1, DA), BF16),
        compiler_params=_params(("arbitrary", "arbitrary"), 2 * 2 * LANES * W * 4 + 4 * SUBLANES * W * 4),
        name="attn_sample",
    )(q, k_new, v_new, kt, vt, tab)


def _prep_kernel(*refs, shift_rows, has_vres):
    it = iter(refs)
    h_ref = next(it)
    if shift_rows:
        hp8_ref, hlast_ref = next(it), next(it)
    else:
        hprev_ref = next(it)
    rkv_ref = next(it)
    if shift_rows:
        rp8_ref, rlast_ref = next(it), next(it)
    else:
        rprev_ref = next(it)
    if has_vres:
        vfirst_ref = next(it)
    mu_ref, murkv_ref, w0_ref, dw1_ref, dw2_ref, a0_ref, aw1_ref, aw2_ref, gw1_ref, gw2_ref = (next(it) for _ in range(10))
    if has_vres:
        vmu_ref, v0_ref, vw1_ref, vw2_ref = (next(it) for _ in range(4))
    kk_ref, ka_ref = next(it), next(it)
    r_out, lw_out, k_out, v_out, kk_out, b_out, g_out = (next(it) for _ in range(7))

    h = h_ref[0]
    rkv0 = rkv_ref[0]
    tm = h.shape[0]
    if shift_rows:
        first = pl.program_id(1) == 0
        row0 = lax.broadcasted_iota(jnp.int32, (tm, 1), 0) == 0
        h_edge = jnp.where(first, hlast_ref[0], hp8_ref[0, SUBLANES - 1:SUBLANES, :])
        r_edge = jnp.where(first, rlast_ref[0], rp8_ref[0, SUBLANES - 1:SUBLANES, :])
        hprev = jnp.where(row0, h_edge, pltpu.roll(h, 1, 0))
        rprev = jnp.where(row0, r_edge, pltpu.roll(rkv0, 1, 0))
    else:
        hprev = hprev_ref[0]
        rprev = rprev_ref[0]

    hb = h.astype(BF16)
    dhb = (hprev - h).astype(BF16)
    mu = mu_ref[0].astype(BF16)
    xw = hb + dhb * mu[0:1]
    xa = hb + dhb * mu[1:2]
    xg = hb + dhb * mu[2:3]

    z = w0_ref[0] + _dot(jnp.tanh(_dot(xw, dw1_ref[0])), dw2_ref[0])
    lw_out[0] = -math.exp(-0.5) * _sigmoid(z)

    a = _sigmoid(a0_ref[0] + _dot(_dot(xa, aw1_ref[0]), aw2_ref[0]))
    g_out[0] = _dot(_sigmoid(_dot(xg, gw1_ref[0])), gw2_ref[0])

    murkv = murkv_ref[0]
    DR = kk_ref.shape[-1]
    r0, k0, v0 = (rkv0[:, j * DR:(j + 1) * DR] for j in range(3))
    rp, kp, vp = (rprev[:, j * DR:(j + 1) * DR] for j in range(3))
    r_out[0] = r0 + (rp - r0) * murkv[0:1]
    kr = k0 + (kp - k0) * murkv[1:2]
    vr = v0 + (vp - v0) * murkv[2:3]
    if has_vres:
        xv = hb + dhb * vmu_ref[0].astype(BF16)
        vgate = _sigmoid(v0_ref[0] + _dot(_dot(xv, vw1_ref[0]), vw2_ref[0]))
        vr = vr + (vfirst_ref[0] - vr) * vgate
    v_out[0] = vr
    kk = kr * kk_ref[0]
    kkn = kk * lax.rsqrt(jnp.maximum(_head_sums(kk * kk), 1e-24))
    kk_out[0] = kkn
    b_out[0] = kkn * a
    k_out[0] = kr * (1.0 + (a - 1.0) * ka_ref[0])


def _rwkv_prep(h, h_prev, rkv0, rkv_prev, v_first, P, l, *, tm=256):
    B, T, D = h.shape
    DR = rkv0.shape[2] // 3
    tm = min(tm, T)
    shift_rows = h_prev.shape[1] == 1 and T > 1
    has_vres = v_first is not None

    def tile(C):
        return pl.BlockSpec((1, tm, C), lambda b, i: (b, i, 0))

    def prev8(C):
        return pl.BlockSpec((1, SUBLANES, C), lambda b, i: (b, jnp.maximum(i * (tm // SUBLANES) - 1, 0), 0))

    def seq_row(C):
        return pl.BlockSpec((1, 1, C), lambda b, i: (b, 0, 0))

    def layer(shape, ll=l):
        return pl.BlockSpec((1,) + shape, lambda b, i: (ll,) + (0,) * len(shape))

    args, specs = [h], [tile(D)]
    if shift_rows:
        args += [h, h_prev]
        specs += [prev8(D), seq_row(D)]
    else:
        args += [jnp.broadcast_to(h_prev, h.shape)]
        specs += [tile(D)]
    args.append(rkv0)
    specs.append(tile(3 * DR))
    if shift_rows:
        args += [rkv0, rkv_prev]
        specs += [prev8(3 * DR), seq_row(3 * DR)]
    else:
        args += [jnp.broadcast_to(rkv_prev, rkv0.shape)]
        specs += [tile(3 * DR)]
    if has_vres:
        args.append(v_first)
        specs.append(tile(DR))
    r1 = lambda a: a.reshape(a.shape[0], 1, a.shape[-1])
    for name in ('mu_wag', 'mu_rkv'):
        args.append(P[name]); specs.append(layer(P[name].shape[1:]))
    args.append(r1(P['decay_w0'])); specs.append(layer((1, DR)))
    for name in ('decay_w1', 'decay_w2'):
        args.append(P[name]); specs.append(layer(P[name].shape[1:]))
    args.append(r1(P['aaa_a0'])); specs.append(layer((1, DR)))
    for name in ('aaa_w1', 'aaa_w2', 'gate_w1', 'gate_w2'):
        args.append(P[name]); specs.append(layer(P[name].shape[1:]))
    if has_vres:
        args.append(r1(P['vres_mu'])); specs.append(layer((1, D), l - 1))
        args.append(r1(P['vres_v0'])); specs.append(layer((1, DR), l - 1))
        for name in ('vres_w1', 'vres_w2'):
            args.append(P[name]); specs.append(layer(P[name].shape[1:], l - 1))
    args.append(r1(P['k_k'])); specs.append(layer((1, DR)))
    args.append(r1(P['k_a'])); specs.append(layer((1, DR)))
    nbytes = 2 * tm * 4 * (2 * D + 7 * DR + 7 * DR + DR) + 8 * tm * D * 4 + 4 * D * 512 * 4
    return pl.pallas_call(
        functools.partial(_prep_kernel, shift_rows=shift_rows, has_vres=has_vres),
        grid=(B, T // tm),
        in_specs=specs,
        out_specs=[tile(DR) for _ in range(7)],
        out_shape=[jax.ShapeDtypeStruct((B, T, DR), F32) for _ in range(7)],
        compiler_params=_params(("arbitrary", "arbitrary"), nbytes),
        name="rwkv_prep",
    )(*args)


def _prefix_sum_rows(x):
    n = x.shape[0]
    row = lax.broadcasted_iota(jnp.int32, (n, 1), 0)
    s = 1
    while s < n:
        x = x + jnp.where(row >= s, pltpu.roll(x, s, 0), 0.0)
        s *= 2
    return x


def _unit_lower_inverse(a_strict, blk):
    n = a_strict.shape[0]
    ti = lax.broadcasted_iota(jnp.int32, (n, n), 0)
    si = lax.broadcasted_iota(jnp.int32, (n, n), 1)

    def lower_left(s):
        return ((ti // (2 * s)) == (si // (2 * s))) & ((ti % (2 * s)) >= s) & ((si % (2 * s)) < s)

    d = (ti == si).astype(F32) + jnp.where(lower_left(1), a_strict, 0.0)
    s = 2
    while s < blk:
        t = _dot(d, jnp.where(lower_left(s), a_strict, 0.0))
        yield
        d = d + _dot(t, d)
        yield
        s *= 2
    return d


def _round_robin(generators):
    results = [None] * len(generators)
    live = list(range(len(generators)))
    while live:
        for u in list(live):
            try:
                next(generators[u])
            except StopIteration as done:
                results[u] = done.value
                live.remove(u)
    return results


def _pair_chunk_terms(r, lw, k, v, kk, b, low, tri):
    C = r.shape[0]
    cum = _prefix_sum_rows(lw)
    g_in = jnp.exp(cum)
    g_inv = jnp.exp(-cum)
    g_end = g_in[C - 1:C]
    a_t = -kk * jnp.exp(cum - lw)
    r_t = r * g_in
    b_t = b * g_inv
    k_t = k * g_inv
    bg = b_t * g_end
    kg = k_t * g_end
    zc = jnp.zeros((C, LANES), F32)
    h0 = lambda x: jnp.where(low, x, 0.0)
    h1 = lambda x: jnp.where(low, 0.0, x)
    v0, v1 = h0(v), h1(v)
    ar = jnp.concatenate([a_t, r_t], axis=0)
    m0 = jnp.where(tri, _dot_nt(h0(ar), jnp.concatenate([b_t, k_t], axis=0)), 0.0)
    m1 = jnp.where(tri, _dot_nt(h1(ar), jnp.concatenate([k_t, b_t], axis=0)), 0.0)
    yield
    top0, bot0, top1, bot1 = m0[:C], m0[C:], m1[:C], m1[C:]
    stack2 = lambda x0, x1: jnp.concatenate([jnp.concatenate([x0, zc], axis=0),
                                             jnp.concatenate([zc, x1], axis=0)], axis=1)
    akv = _dot(stack2(top0, top1), jnp.concatenate([zc, v0, v1, zc], axis=0))
    a_sw = pltpu.roll(a_t, HEAD_DIM, 1)
    tinv = yield from _unit_lower_inverse(jnp.concatenate([h0(top0), h1(top1)], axis=0), C)
    x = _dot(tinv, akv + jnp.concatenate([h1(a_sw), h0(a_sw)], axis=0))
    yield
    z = jnp.concatenate([x[:C], v0, v1, x[C:]], axis=0)
    e = _dot(stack2(bot0, bot1), z)
    gh = _dot_tn(z, jnp.concatenate([h0(bg), h0(kg), h1(kg), h1(bg)], axis=0))
    yield
    y0 = jnp.where(low, e[:C], e[C:])
    r_eff = r_t + pltpu.roll(jnp.where(low, e[C:], e[:C]), HEAD_DIM, 1)
    h_mat = jnp.concatenate([h0(gh[:C]), h1(gh[C:])], axis=0)
    g_mat = jnp.concatenate([h0(gh[C:]), h1(gh[:C])], axis=0)
    return r_eff, y0, g_mat, h_mat, g_end


def _scan_kernel(r_ref, lw_ref, k_ref, v_ref, kk_ref, b_ref, g_ref, lng_ref, lnb_ref, rk_ref,
                 o_ref, s_ref, reff_s, y0_s, gm_s, hm_s, ge_s, *, chunk, unroll):
    C = chunk
    NS, T = r_ref.shape[:2]
    NC = T // C
    E = HEAD_DIM
    low = lax.broadcasted_iota(jnp.int32, (1, LANES), 1) < E
    ti = lax.broadcasted_iota(jnp.int32, (2 * C, 2 * C), 0)
    si = lax.broadcasted_iota(jnp.int32, (2 * C, 2 * C), 1)
    tri = (si % C) <= jnp.where(ti < C, ti - 1, ti - C)

    def phase1(i, _):
        sq = i // (NC // unroll)
        c0 = (i - sq * (NC // unroll)) * unroll
        cs = [c0 + u for u in range(unroll)]
        rows = [pl.ds(pl.multiple_of(c * C, C), C) for c in cs]
        loaded = [[ref[sq, rw, :] for ref in (r_ref, lw_ref, k_ref, v_ref, kk_ref, b_ref)] for rw in rows]
        terms = _round_robin([_pair_chunk_terms(*args, low, tri) for args in loaded])
        for c, rw, (r_eff, y0, g_mat, h_mat, g_end) in zip(cs, rows, terms):
            reff_s[sq, rw, :] = r_eff
            y0_s[sq, rw, :] = y0
            gm_s[sq, c] = g_mat
            hm_s[sq, c] = h_mat
            ge_s[sq, c] = jnp.broadcast_to(g_end, (SUBLANES, LANES))
        return 0

    lax.fori_loop(0, NS * (NC // unroll), phase1, 0)

    def phase2(c, states):
        rows = pl.ds(pl.multiple_of(c * C, C), C)
        ys = [_dot_nt(reff_s[sq, rows, :], S) + y0_s[sq, rows, :] for sq, S in enumerate(states)]
        new = tuple(S * ge_s[sq, c][0:1] + _dot(S, gm_s[sq, c]) + hm_s[sq, c] for sq, S in enumerate(states))
        for sq, y in enumerate(ys):
            y0_s[sq, rows, :] = y
        return new

    states = lax.fori_loop(0, NC, phase2, tuple(jnp.zeros((LANES, LANES), F32) for _ in range(NS)))
    for sq, S in enumerate(states):
        s_ref[sq, 0] = S[:E, :E]
        s_ref[sq, 1] = S[E:, E:]

    def finish(y, r, k, v, g):
        mu = _head_sums(y) * (1.0 / E)
        yield
        yc = y - mu
        var = _head_sums(yc * yc) * (1.0 / E)
        bonus = _head_sums(r * k * rk_ref[0])
        yield
        yn = yc * lax.rsqrt(var + GN_EPS) * lng_ref[0] + lnb_ref[0]
        return ((yn + bonus * v) * g).astype(o_ref.dtype)

    def phase3(i, _):
        sq = i // (NC // unroll)
        c0 = (i - sq * (NC // unroll)) * unroll
        rows = [pl.ds(pl.multiple_of((c0 + u) * C, C), C) for u in range(unroll)]
        loaded = [[y0_s[sq, rw, :]] + [ref[sq, rw, :] for ref in (r_ref, k_ref, v_ref, g_ref)] for rw in rows]
        outs = _round_robin([finish(*args) for args in loaded])
        for rw, out in zip(rows, outs):
            o_ref[sq, rw, :] = out
        return 0

    lax.fori_loop(0, NS * (NC // unroll), phase3, 0)


def _rwkv_scan(r, lw, k, v, kk, b, g, lnx_g, lnx_b, r_k, l):
    B, T, DR = r.shape
    H = DR // HEAD_DIM
    L = lnx_g.shape[0]
    C = min(SCAN_CHUNK, T)
    nc = T // C
    unroll = math.gcd(SCAN_UNROLL, nc)
    ns = math.gcd(SCAN_SEQS, B)
    assert T % C == 0
    seq = pl.BlockSpec((ns, T, LANES), lambda p, b_: (b_, 0, p))
    par = pl.BlockSpec((1, 1, LANES), lambda p, b_: (l, 0, p))
    scratch = [pltpu.VMEM((ns, T, LANES), F32), pltpu.VMEM((ns, T, LANES), F32),
               pltpu.VMEM((ns, nc, LANES, LANES), F32), pltpu.VMEM((ns, nc, LANES, LANES), F32),
               pltpu.VMEM((ns, nc, SUBLANES, LANES), F32)]
    nbytes = ns * (2 * 8 * T * LANES * 4 + 2 * T * LANES * 4 + 2 * nc * LANES * LANES * 4 + nc * SUBLANES * LANES * 4)
    return pl.pallas_call(
        functools.partial(_scan_kernel, chunk=C, unroll=unroll),
        grid=(DR // LANES, B // ns),
        in_specs=[seq] * 7 + [par] * 3,
        out_specs=[seq, pl.BlockSpec((ns, 2, HEAD_DIM, HEAD_DIM), lambda p, b_: (b_, p, 0, 0))],
        out_shape=[jax.ShapeDtypeStruct((B, T, DR), BF16), jax.ShapeDtypeStruct((B, H, HEAD_DIM, HEAD_DIM), F32)],
        scratch_shapes=scratch,
        compiler_params=_params(("arbitrary", "arbitrary"), nbytes),
        name="rwkv_scan",
    )(r, lw, k, v, kk, b, g, lnx_g.reshape(L, 1, DR), lnx_b.reshape(L, 1, DR), r_k.reshape(L, 1, DR))


def _step_kernel(s_ref, r_ref, lw_ref, k_ref, v_ref, kk_ref, b_ref, g_ref, lng_ref, lnb_ref, rk_ref,
                 o_ref, so_ref, *, n_heads):
    E = HEAD_DIM
    eye = (lax.broadcasted_iota(jnp.int32, (E, E), 0) == lax.broadcasted_iota(jnp.int32, (E, E), 1)).astype(F32)

    def head(hh):
        sl = slice(hh * E, (hh + 1) * E)
        S = s_ref[0, 0, hh]
        r, k, v, kk, b, g = (ref[0, :, sl] for ref in (r_ref, k_ref, v_ref, kk_ref, b_ref, g_ref))
        w = jnp.exp(lw_ref[0, :, sl])
        v_col = jnp.sum(eye * v, axis=-1, keepdims=True)
        sa = jnp.sum(S * (-kk), axis=-1, keepdims=True)
        bonus = jnp.sum(r * k * rk_ref[0, :, sl], axis=-1, keepdims=True)
        yield
        S = S * w + sa * b + v_col * k
        y_col = jnp.sum(S * r, axis=-1, keepdims=True)
        yield
        y = jnp.sum(eye * y_col, axis=0, keepdims=True)
        mu = jnp.mean(y, axis=-1, keepdims=True)
        yield
        yc = y - mu
        var = jnp.mean(yc * yc, axis=-1, keepdims=True)
        yield
        yn = yc * lax.rsqrt(var + GN_EPS) * lng_ref[0, :, sl] + lnb_ref[0, :, sl]
        return S, (yn + bonus * v) * g

    results = _round_robin([head(hh) for hh in range(n_heads)])
    for hh, (S, _) in enumerate(results):
        so_ref[0, hh] = S
    o_ref[0] = jnp.concatenate([o for _, o in results], axis=-1).astype(o_ref.dtype)


def _rwkv_step(state, r, lw, k, v, kk, b, g, lnx_g, lnx_b, r_k, l):
    DB, H = state.shape[1:3]
    DR = H * HEAD_DIM
    L = lnx_g.shape[0]
    vec = pl.BlockSpec((1, 1, DR), lambda b_: (b_, 0, 0))
    par = pl.BlockSpec((1, 1, DR), lambda b_: (l, 0, 0))
    st = pl.BlockSpec((1, H, HEAD_DIM, HEAD_DIM), lambda b_: (b_, 0, 0, 0))
    return pl.pallas_call(
        functools.partial(_step_kernel, n_heads=H),
        grid=(DB,),
        in_specs=[pl.BlockSpec((1, 1, H, HEAD_DIM, HEAD_DIM), lambda b_: (l, b_, 0, 0, 0))] + [vec] * 7 + [par] * 3,
        out_specs=[vec, st],
        out_shape=[jax.ShapeDtypeStruct((DB, 1, DR), BF16), jax.ShapeDtypeStruct(state.shape[1:], F32)],
        compiler_params=_params(("arbitrary",), 4 * H * HEAD_DIM * LANES * 4),
        name="rwkv_step",
    )(state, r, lw, k, v, kk, b, g, lnx_g.reshape(L, 1, DR), lnx_b.reshape(L, 1, DR), r_k.reshape(L, 1, DR))


def _layer(P, l, xp, xs, mod, base, vf_p, vf_s, k_stack, v_stack, cache_k, cache_v, state_wkv, h_last_s):
    B, T, D = xp.shape
    DB = xs.shape[1]
    DA = P['rel_bias'].shape[1] * HEAD_DIM
    n_rkv = P['w_in'].shape[2] - 3 * DA
    w_in = P['w_in']
    rows_p, rows_s = (DB, None), (0, DB)
    h, hb = _norm_mod(xp, P['norm1_g'], l, mod, rows_p, 0, [F32, BF16])
    hs, hbs = _norm_mod(xs, P['norm1_g'], l, mod, rows_s, 0, [F32, BF16])
    q, q_s = _proj(hb, hbs, w_in, l, 0, DA, head_g=P['q_norm_g'], scale=ATT_SCALE)
    k_stack, k_s = _proj(hb, hbs, w_in, l, DA, DA, head_g=P['k_norm_g'], stack=k_stack)
    v_stack, v_s = _proj(hb, hbs, w_in, l, 2 * DA, DA, stack=v_stack)
    both = jnp.concatenate([hbs, h_last_s[None].astype(BF16)], axis=1)
    rkv0, rkv2_s = _proj(hb, both, w_in, l, 3 * DA, n_rkv)

    att = _attention_prompt(q, k_stack, v_stack, l, base)
    r, lw, kr, vr, kk, b, g = _rwkv_prep(h, jnp.zeros((B, 1, D), F32), rkv0, jnp.zeros((B, 1, n_rkv), F32), vf_p, P, l)
    if vf_p is None:
        vf_p = vr
    rw, state_p = _rwkv_scan(r, lw, kr, vr, kk, b, g, P['lnx_g'], P['lnx_b'], P['r_k'], l)

    per_seq = lambda t: t.reshape(DB, 1, t.shape[-1])
    att_s = _attention_sample(per_seq(q_s), per_seq(k_s), per_seq(v_s), cache_k, cache_v, l, P['rel_bias'])
    r, lw, kr, vr, kk, b, g = _rwkv_prep(hs, h_last_s[None], rkv2_s[:, :DB], rkv2_s[:, DB:], vf_s, P, l)
    if vf_s is None:
        vf_s = vr
    rw_s, state_s = _rwkv_step(state_wkv, *(per_seq(t) for t in (r, lw, kr, vr, kk, b, g)),
                                P['lnx_g'], P['lnx_b'], P['r_k'], l)

    xp, xs = _proj_resid([att, rw], [att_s.reshape(1, DB, DA), rw_s.reshape(1, DB, DA)], P['w_out'], l,
                         xp, xs, mod, DB, 2, tm=1024, tn=1024)
    (h2,) = _norm_mod(xp, P['norm2_g'], l, mod, rows_p, 3, [BF16])
    (h2s,) = _norm_mod(xs, P['norm2_g'], l, mod, rows_s, 3, [BF16])
    act, act_s = _proj_swiglu(h2, h2s, P['w_gu'], l)
    xp, xs = _proj_resid([act], [act_s], P['w_down'], l, xp, xs, mod, DB, 5, tm=512, tn=512)
    return xp, xs, vf_p, vf_s, k_stack, v_stack, state_p, h[:, -1], k_s[0], v_s[0], state_s, hs[0]


def kernel(x_prompt, x_sample, c_prompt, c_sample, cache_k, cache_v, state_wkv, state_shift, rel_bias, ada_w, ada_b, norm1_g, norm2_g, w_in, q_norm_g, k_norm_g, mu_wag, mu_rkv, decay_w0, decay_w1, decay_w2, aaa_a0, aaa_w1, aaa_w2, gate_w1, gate_w2, vres_mu, vres_v0, vres_w1, vres_w2, k_k, k_a, r_k, lnx_g, lnx_b, w_out, w_gu, w_down):
    P = dict(rel_bias=rel_bias, norm1_g=norm1_g, norm2_g=norm2_g, w_in=w_in, q_norm_g=q_norm_g,
             k_norm_g=k_norm_g, mu_wag=mu_wag, mu_rkv=mu_rkv, decay_w0=decay_w0, decay_w1=decay_w1,
             decay_w2=decay_w2, aaa_a0=aaa_a0, aaa_w1=aaa_w1, aaa_w2=aaa_w2, gate_w1=gate_w1,
             gate_w2=gate_w2, vres_mu=vres_mu, vres_v0=vres_v0, vres_w1=vres_w1, vres_w2=vres_w2,
             k_k=k_k, k_a=k_a, r_k=r_k.reshape(r_k.shape[0], -1), lnx_g=lnx_g, lnx_b=lnx_b,
             w_out=w_out, w_gu=w_gu, w_down=w_down)
    L = ada_w.shape[0]
    B, T, D = x_prompt.shape
    DB = x_sample.shape[0]
    H_ATT = rel_bias.shape[1]
    assert x_sample.shape[1] == 1 and T <= W_MAX

    rows = -(-(B + DB) // SUBLANES) * SUBLANES
    c_all = jnp.concatenate([c_sample, c_prompt, jnp.zeros((rows - B - DB, D), F32)], axis=0)
    mod = _ada(c_all, ada_w, ada_b).reshape(L, rows, 1, 6 * D)
    base = _band_bias_rows(rel_bias)

    k_stack = jnp.zeros((L, B, T, H_ATT * HEAD_DIM), F32)
    v_stack = jnp.zeros((L, B, T, H_ATT * HEAD_DIM), F32)
    xp, xs, vf_p, vf_s = x_prompt, x_sample.reshape(1, DB, D), None, None
    ps, ph, sk, sv, ss, sh = [], [], [], [], [], []
    for l in range(L):
        (xp, xs, vf_p, vf_s, k_stack, v_stack, state_p, shift_p, k_s, v_s, state_s, shift_s) = _layer(
            P, l, xp, xs, mod, base, vf_p, vf_s, k_stack, v_stack, cache_k, cache_v, state_wkv, state_shift[l])
        ps.append(state_p)
        ph.append(shift_p)
        sk.append(k_s.reshape(DB, 1, H_ATT, HEAD_DIM))
        sv.append(v_s.reshape(DB, 1, H_ATT, HEAD_DIM))
        ss.append(state_s)
        sh.append(shift_s)

    return (xp, xs.reshape(DB, 1, D), k_stack.reshape(L, B, T, H_ATT, HEAD_DIM),
            v_stack.reshape(L, B, T, H_ATT, HEAD_DIM), jnp.stack(ps), jnp.stack(ph),
            jnp.stack(sk), jnp.stack(sv), jnp.stack(ss), jnp.stack(sh))
```

```python
import functools
import math

import jax
import jax.numpy as jnp
from jax import lax
from jax.experimental import pallas as pl
from jax.experimental.pallas import tpu as pltpu

F32 = jnp.float32
BF16 = jnp.bfloat16

HEAD_DIM = 64
BRANCHES = ((128, 1), (512, 4), (2048, 16))
W_MAX = 2048
NUM_BUCKETS = 32
MAX_DISTANCE = W_MAX
ATT_SCALE = HEAD_DIM ** -0.5
RMS_EPS = 1e-6
GN_EPS = 64e-5
NEG = -1e30

LANES = 128
SUBLANES = 8
Q_BLOCK = 128
SCAN_CHUNK = 64
SCAN_SEQS = 2
SCAN_UNROLL = 16
SATT_PAIRS = 4
ATT_UNROLL = 4
VMEM_CAP = 56 * 1024 * 1024

assert all(w // dil == Q_BLOCK for w, dil in BRANCHES) and 2 * HEAD_DIM == LANES


def _vmem(nbytes):
    return int(min(VMEM_CAP, nbytes * 1.3 + (6 << 20)))


def _params(sem, nbytes):
    return pltpu.CompilerParams(dimension_semantics=sem, vmem_limit_bytes=_vmem(nbytes))


def _dot(a, b):
    return jnp.dot(a.astype(BF16), b.astype(BF16), preferred_element_type=F32)


def _dot_nt(a, b):
    return lax.dot_general(a.astype(BF16), b.astype(BF16), (((1,), (1,)), ((), ())), preferred_element_type=F32)


def _dot_tn(a, b):
    return lax.dot_general(a.astype(BF16), b.astype(BF16), (((0,), (0,)), ((), ())), preferred_element_type=F32)


def _sigmoid(x):
    return 1.0 / (1.0 + jnp.exp(-x))


def _head_sums(x):
    r = lax.broadcasted_iota(jnp.int32, (LANES, LANES), 0) // HEAD_DIM
    c = lax.broadcasted_iota(jnp.int32, (LANES, LANES), 1) // HEAD_DIM
    bd = (r == c).astype(BF16)
    cols = [_dot(x[:, j * LANES:(j + 1) * LANES], bd) for j in range(x.shape[1] // LANES)]
    return cols[0] if len(cols) == 1 else jnp.concatenate(cols, axis=-1)


def _ada_kernel(c_ref, w_ref, b_ref, o_ref):
    c = c_ref[...]
    s = c * _sigmoid(c)
    o_ref[0] = _dot(s, w_ref[0]) + b_ref[0]


def _ada(c_all, ada_w, ada_b, tn=1024):
    L, D, N = ada_w.shape
    R = c_all.shape[0]
    return pl.pallas_call(
        _ada_kernel,
        grid=(L, N // tn),
        in_specs=[pl.BlockSpec((R, D), lambda l, j: (0, 0)),
                  pl.BlockSpec((1, D, tn), lambda l, j: (l, 0, j)),
                  pl.BlockSpec((1, 1, tn), lambda l, j: (l, 0, j))],
        out_specs=pl.BlockSpec((1, R, tn), lambda l, j: (l, 0, j)),
        out_shape=jax.ShapeDtypeStruct((L, R, N), F32),
        compiler_params=_params(("arbitrary", "arbitrary"), 2 * D * tn * 4 + D * tn * 2),
        name="ada_mod",
    )(c_all, ada_w, ada_b.reshape(L, 1, N))


def _norm_kernel(x_ref, g_ref, sh_ref, sc_ref, *out_refs):
    x = x_ref[0]
    y = x * lax.rsqrt(jnp.mean(x * x, axis=-1, keepdims=True) + RMS_EPS) * g_ref[0]
    h = y * (1.0 + sc_ref[0, :, 0, :]) + sh_ref[0, :, 0, :]
    for o in out_refs:
        o[0] = h.astype(o.dtype)


def _mod_spec(l, rows, which, D):
    r0, n = rows
    if n is None:
        return pl.BlockSpec((1, 1, 1, D), lambda b, i: (l, r0 + b, 0, which))
    return pl.BlockSpec((1, n, 1, D), lambda b, i: (l, 0, 0, which))


def _norm_mod(x, g_stack, l, mod, rows, which, out_dtypes, tm=512):
    B, T, D = x.shape
    tm = min(tm, T)
    L = g_stack.shape[0]
    assert rows[1] is None or (B == 1 and rows[1] == T == tm)
    outs = pl.pallas_call(
        _norm_kernel,
        grid=(B, T // tm),
        in_specs=[pl.BlockSpec((1, tm, D), lambda b, i: (b, i, 0)),
                  pl.BlockSpec((1, 1, D), lambda b, i: (l, 0, 0)),
                  _mod_spec(l, rows, which, D), _mod_spec(l, rows, which + 1, D)],
        out_specs=[pl.BlockSpec((1, tm, D), lambda b, i: (b, i, 0)) for _ in out_dtypes],
        out_shape=[jax.ShapeDtypeStruct((B, T, D), dt) for dt in out_dtypes],
        compiler_params=_params(("arbitrary", "arbitrary"), 2 * tm * D * 4 * (2 + len(out_dtypes))),
        name="norm_mod",
    )(x, g_stack.reshape(L, 1, D), mod, mod)
    return outs


def _first_inner_step():
    return (pl.program_id(1) == 0) & (pl.program_id(2) == 0)


def _head_rmsnorm(acc, g):
    return acc * lax.rsqrt(_head_sums(acc * acc) * (1.0 / HEAD_DIM) + RMS_EPS) * g


def _rider_specs(R, k_sizes, tn):
    ins = [pl.BlockSpec((1, R, ks), lambda n, b, i: (0, 0, 0)) for ks in k_sizes]
    return ins, pl.BlockSpec((1, R, tn), lambda n, b, i: (0, 0, n))


def _mm_kernel(*refs, headnorm, scale, aliased):
    it = iter(refs)
    x_ref, w_ref = next(it), next(it)
    g_ref = next(it) if headnorm else None
    if aliased:
        next(it)
    xr_ref, o_ref, or_ref, wb_ref = next(it), next(it), next(it), next(it)

    def result(x):
        acc = jnp.dot(x, wb_ref[...], preferred_element_type=F32)
        if headnorm:
            acc = _head_rmsnorm(acc, g_ref[0])
            if scale != 1.0:
                acc = acc * scale
        return acc

    @pl.when(_first_inner_step())
    def _():
        wb_ref[...] = w_ref[0].astype(BF16)
        or_ref[0] = result(xr_ref[0])

    if len(o_ref.shape) == 4:
        o_ref[0, 0] = result(x_ref[0])
    else:
        o_ref[0] = result(x_ref[0])


def _proj(x, rider, w_stack, l, col0, ncols, *, head_g=None, scale=1.0, stack=None, tm=1024, tn=1024):
    B, T, K = x.shape
    R = rider.shape[1]
    tm = min(tm, T)
    tn = min(tn, ncols)
    assert T % tm == 0 and ncols % tn == 0 and col0 % tn == 0
    cb = col0 // tn
    in_specs = [pl.BlockSpec((1, tm, K), lambda n, b, i: (b, i, 0)),
                pl.BlockSpec((1, K, tn), lambda n, b, i: (l, 0, cb + n))]
    args = [x, w_stack]
    if head_g is not None:
        in_specs.append(pl.BlockSpec((1, 1, tn), lambda n, b, i: (l, 0, 0)))
        args.append(jnp.tile(head_g, (1, tn // HEAD_DIM)).reshape(head_g.shape[0], 1, tn))
    aliases = {}
    if stack is None:
        out_spec = pl.BlockSpec((1, tm, tn), lambda n, b, i: (b, i, n))
        out_shape = jax.ShapeDtypeStruct((B, T, ncols), F32)
    else:
        out_spec = pl.BlockSpec((1, 1, tm, tn), lambda n, b, i: (l, b, i, n))
        out_shape = jax.ShapeDtypeStruct(stack.shape, F32)
        in_specs.append(pl.BlockSpec(memory_space=pl.ANY))
        args.append(stack)
        aliases = {len(args) - 1: 0}
    r_ins, r_out = _rider_specs(R, (K,), tn)
    nbytes = 2 * (tm * K * 2 + K * tn * 4 + tm * tn * 4) + K * tn * 2 + 2 * tm * tn * 4
    return pl.pallas_call(
        functools.partial(_mm_kernel, headnorm=head_g is not None, scale=scale, aliased=bool(aliases)),
        grid=(ncols // tn, B, T // tm),
        in_specs=in_specs + r_ins,
        out_specs=[out_spec, r_out],
        out_shape=[out_shape, jax.ShapeDtypeStruct((1, R, ncols), F32)],
        scratch_shapes=[pltpu.VMEM((K, tn), BF16)],
        input_output_aliases=aliases,
        compiler_params=_params(("arbitrary",) * 3, nbytes),
        name="proj",
    )(*args, rider)


def _swiglu_kernel(x_ref, wg_ref, wu_ref, xr_ref, o_ref, or_ref, wgb_ref, wub_ref):
    def result(x):
        gate = jnp.dot(x, wgb_ref[...], preferred_element_type=F32)
        up = jnp.dot(x, wub_ref[...], preferred_element_type=F32)
        return (gate * _sigmoid(gate) * up).astype(o_ref.dtype)

    @pl.when(_first_inner_step())
    def _():
        wgb_ref[...] = wg_ref[0].astype(BF16)
        wub_ref[...] = wu_ref[0].astype(BF16)
        or_ref[0] = result(xr_ref[0])

    o_ref[0] = result(x_ref[0])


def _proj_swiglu(x, rider, w_gu, l, *, tm=1024, tn=512):
    B, T, K = x.shape
    R = rider.shape[1]
    F = w_gu.shape[2] // 2
    tm = min(tm, T)
    assert T % tm == 0 and F % tn == 0
    nb = F // tn
    r_ins, r_out = _rider_specs(R, (K,), tn)
    nbytes = 2 * (tm * K * 2 + 2 * K * tn * 4 + tm * tn * 2) + 2 * K * tn * 2 + 3 * tm * tn * 4
    return pl.pallas_call(
        _swiglu_kernel,
        grid=(nb, B, T // tm),
        in_specs=[pl.BlockSpec((1, tm, K), lambda n, b, i: (b, i, 0)),
                  pl.BlockSpec((1, K, tn), lambda n, b, i: (l, 0, n)),
                  pl.BlockSpec((1, K, tn), lambda n, b, i: (l, 0, nb + n))] + r_ins,
        out_specs=[pl.BlockSpec((1, tm, tn), lambda n, b, i: (b, i, n)), r_out],
        out_shape=[jax.ShapeDtypeStruct((B, T, F), BF16), jax.ShapeDtypeStruct((1, R, F), BF16)],
        scratch_shapes=[pltpu.VMEM((K, tn), BF16), pltpu.VMEM((K, tn), BF16)],
        compiler_params=_params(("arbitrary",) * 3, nbytes),
        name="proj_swiglu",
    )(x, w_gu, w_gu, rider)


def _resid_kernel(*refs, k_sizes):
    n_x = len(k_sizes)
    x_refs, refs = refs[:n_x], refs[n_x:]
    w_ref, res_ref, gate_ref = refs[:3]
    xr_refs, refs = refs[3:3 + n_x], refs[3 + n_x:]
    resr_ref, gater_ref, o_ref, or_ref, wb_ref = refs

    def result(xs, res, gate):
        acc = None
        k0 = 0
        for x_ref, ks in zip(xs, k_sizes):
            part = jnp.dot(x_ref[0], wb_ref[k0:k0 + ks, :], preferred_element_type=F32)
            acc = part if acc is None else acc + part
            k0 += ks
        return res[0] + gate[0, :, 0, :] * acc

    @pl.when(_first_inner_step())
    def _():
        wb_ref[...] = w_ref[0].astype(BF16)
        or_ref[0] = result(xr_refs, resr_ref, gater_ref)

    o_ref[0] = result(x_refs, res_ref, gate_ref)


def _proj_resid(xs, rider_xs, w_stack, l, resid, rider_resid, mod, row0, which, *, tm=512, tn=512):
    B, T, N = resid.shape
    R = rider_resid.shape[1]
    k_sizes = tuple(x.shape[2] for x in xs)
    K = sum(k_sizes)
    tm = min(tm, T)
    assert T % tm == 0 and N % tn == 0
    c0 = which * (N // tn)
    in_specs = [pl.BlockSpec((1, tm, ks), lambda n, b, i: (b, i, 0)) for ks in k_sizes]
    in_specs += [pl.BlockSpec((1, K, tn), lambda n, b, i: (l, 0, n)),
                 pl.BlockSpec((1, tm, tn), lambda n, b, i: (b, i, n)),
                 pl.BlockSpec((1, 1, 1, tn), lambda n, b, i: (l, row0 + b, 0, c0 + n))]
    r_ins, r_out = _rider_specs(R, k_sizes, tn)
    r_gate = pl.BlockSpec((1, R, 1, tn), lambda n, b, i: (l, 0, 0, c0 + n))
    nbytes = 2 * (tm * K * 2 + K * tn * 4 + 2 * tm * tn * 4) + K * tn * 2 + 2 * tm * tn * 4
    return pl.pallas_call(
        functools.partial(_resid_kernel, k_sizes=k_sizes),
        grid=(N // tn, B, T // tm),
        in_specs=in_specs + r_ins + [r_out, r_gate],
        out_specs=[pl.BlockSpec((1, tm, tn), lambda n, b, i: (b, i, n)), r_out],
        out_shape=[jax.ShapeDtypeStruct((B, T, N), F32), jax.ShapeDtypeStruct((1, R, N), F32)],
        scratch_shapes=[pltpu.VMEM((K, tn), BF16)],
        compiler_params=_params(("arbitrary",) * 3, nbytes),
        name="proj_resid",
    )(*xs, w_stack, resid, mod, *rider_xs, rider_resid, mod)


def _rel_bucket(dist):
    max_exact = NUM_BUCKETS // 2
    d = jnp.maximum(dist, 0)
    df = jnp.maximum(d, 1).astype(F32)
    large = max_exact + (jnp.log(df / max_exact) / math.log(MAX_DISTANCE / max_exact)
                         * (NUM_BUCKETS - max_exact)).astype(jnp.int32)
    return jnp.where(d < max_exact, d, jnp.minimum(large, NUM_BUCKETS - 1))


def _band_bias_rows(rel_bias):
    dsub = Q_BLOCK - jnp.arange(2 * Q_BLOCK)
    rows = []
    for (w, dil) in BRANCHES:
        valid = (dsub >= 0) & (dsub <= w // dil)
        bias = rel_bias[_rel_bucket(jnp.maximum(dsub, 0) * dil)].astype(F32).T
        rows.append(jnp.where(valid[None], bias, NEG))
    return jnp.stack(rows, axis=1)


def _attn_kernel(q_ref, k_ref, v_ref, base_ref, o_ref, qs, ks, vs, os_, ms_, ds_, stage):
    T = q_ref.shape[2]
    QB = Q_BLOCK
    low = lax.broadcasted_iota(jnp.int32, (1, LANES), 1) < HEAD_DIM
    own = (low, jnp.logical_not(low))
    prev_cols = lax.broadcasted_iota(jnp.int32, (1, 2 * QB), 1) < QB

    assert len(BRANCHES) == 3 and BRANCHES[0][1] == 1
    prev_dil = 1
    for bi, (_, dil) in enumerate(BRANCHES):
        L, Lp, ratio = T // dil, T // prev_dil, dil // prev_dil
        ks[bi, 0:QB, :] = jnp.zeros((QB, LANES), BF16)
        vs[bi, 0:QB, :] = jnp.zeros((QB, LANES), BF16)
        for r in range(dil):
            rows = pl.ds((r % prev_dil) * Lp + r // prev_dil, L, stride=ratio)
            if bi < 2:
                q, k, v = (ref[0, 0, rows, :] for ref in (q_ref, k_ref, v_ref))
            else:
                q, k, v = (stage[a, rows, :] for a in range(3))
            if bi == 1:
                for a, x in enumerate((q, k, v)):
                    stage[a, r * L:(r + 1) * L, :] = x
            for hh in range(2):
                qs[hh, bi, r * L:(r + 1) * L, :] = jnp.where(own[hh], q, 0.0).astype(BF16)
            ks[bi, QB + r * L:QB + (r + 1) * L, :] = k.astype(BF16)
            vs[bi, QB + r * L:QB + (r + 1) * L, :] = v.astype(BF16)
        prev_dil = dil

    for bi, (_, dil) in enumerate(BRANCHES):
        nb = T // dil // QB
        tiles = [pltpu.roll(jnp.broadcast_to(base_ref[hh, bi:bi + 1, :], (QB, 2 * QB)), 0, 1, stride=1, stride_axis=0)
                 for hh in range(2)]
        first_only = nb == 1
        if first_only:
            tiles = [t[:, QB:] for t in tiles]
        k0, kn = (QB, QB) if first_only else (0, 2 * QB)

        def one_block(g, q0, q1, kw, vw, nb=nb, tiles=tiles, first_only=first_only):
            ss = [lax.dot_general(q, kw, (((1,), (1,)), ((), ())), preferred_element_type=F32) + tiles[hh]
                  for hh, q in enumerate((q0, q1))]
            if not first_only:
                no_prev = jnp.logical_and(g % nb == 0, prev_cols)
                ss = [jnp.where(no_prev, NEG, s) for s in ss]
            yield
            ms = [jnp.max(s, axis=-1, keepdims=True) for s in ss]
            ps = [jnp.exp(s - m) for s, m in zip(ss, ms)]
            dens = [jnp.sum(p, axis=-1, keepdims=True) for p in ps]
            os2 = [jnp.dot(p.astype(BF16), vw, preferred_element_type=F32) for p in ps]
            yield
            pair = lambda x0, x1: jnp.where(low, jnp.broadcast_to(x0, (QB, LANES)), jnp.broadcast_to(x1, (QB, LANES)))
            return pair(*os2), pair(*ms), pair(*dens)

        def blocks(i, _, bi=bi, dil=dil, nb=nb, k0=k0, kn=kn):
            gs = [i * ATT_UNROLL + u for u in range(ATT_UNROLL)]
            ats = [pl.multiple_of(g * QB, QB) for g in gs]
            loaded = [(qs[0, bi, pl.ds(at, QB), :], qs[1, bi, pl.ds(at, QB), :],
                       ks[bi, pl.ds(at + k0, kn), :], vs[bi, pl.ds(at + k0, kn), :]) for at in ats]
            results = _round_robin([one_block(g, *ld) for g, ld in zip(gs, loaded)])
            for g, (acc, m, den) in zip(gs, results):
                r = g // nb
                tok = pl.ds((g - r * nb) * (QB * dil) + r, QB, stride=dil)
                os_[bi, tok, :] = acc
                ms_[bi, tok, :] = m
                ds_[bi, tok, :] = den
            return 0

        lax.fori_loop(0, T // QB // ATT_UNROLL, blocks, 0)

    def merge(i, _):
        rows = pl.ds(pl.multiple_of(i * QB, QB), QB)
        maxes = [ms_[bi, rows, :] for bi in range(len(BRANCHES))]
        m = functools.reduce(jnp.maximum, maxes)
        ws = [jnp.exp(x - m) for x in maxes]
        num = sum(w * os_[bi, rows, :] for bi, w in enumerate(ws))
        den = sum(w * ds_[bi, rows, :] for bi, w in enumerate(ws))
        o_ref[0, rows, :] = (num / den).astype(o_ref.dtype)
        return 0

    lax.fori_loop(0, T // QB, merge, 0)


def _attention_prompt(q, k_stack, v_stack, l, base):
    B, T, DA = q.shape
    nbr = len(BRANCHES)
    assert T % (Q_BLOCK * max(d for _, d in BRANCHES)) == 0
    qkv = pl.BlockSpec((1, 1, T, LANES), lambda p, b: (l, b, 0, p))
    scratch = [pltpu.VMEM((2, nbr, T, LANES), BF16), pltpu.VMEM((nbr, T + Q_BLOCK, LANES), BF16),
               pltpu.VMEM((nbr, T + Q_BLOCK, LANES), BF16), pltpu.VMEM((nbr, T, LANES), F32),
               pltpu.VMEM((nbr, T, LANES), F32), pltpu.VMEM((nbr, T, LANES), F32), pltpu.VMEM((3, T, LANES), F32)]
    nbytes = (2 * 3 * T * LANES * 4 + 4 * nbr * (T + Q_BLOCK) * LANES * 2 + 3 * nbr * T * LANES * 4
              + 3 * T * LANES * 4 + 2 * T * LANES * 2)
    return pl.pallas_call(
        _attn_kernel,
        grid=(DA // LANES, B),
        in_specs=[pl.BlockSpec((1, 1, T, LANES), lambda p, b: (0, b, 0, p)), qkv, qkv,
                  pl.BlockSpec((2, nbr, 2 * Q_BLOCK), lambda p, b: (p, 0, 0))],
        out_specs=pl.BlockSpec((1, T, LANES), lambda p, b: (b, 0, p)),
        out_shape=jax.ShapeDtypeStruct((B, T, DA), BF16),
        scratch_shapes=scratch,
        compiler_params=_params(("arbitrary", "arbitrary"), nbytes),
        name="attn_prompt",
    )(q[None], k_stack, v_stack, base)


def _distance_logits(rel_bias, nd):
    d = jnp.arange(nd)
    mult = sum(((d % dil == 0) & (d // dil <= w // dil)).astype(F32) for (w, dil) in BRANCHES)
    bias = rel_bias[_rel_bucket(d)].astype(F32).T
    return jnp.where(mult > 0, bias + jnp.log(jnp.maximum(mult, 1.0)), NEG)


def _sattn_kernel(q_ref, kn_ref, vn_ref, kt_ref, vt_ref, tab_ref, o_ref):
    row = lax.broadcasted_iota(jnp.int32, (SUBLANES, LANES), 0)
    low = lax.broadcasted_iota(jnp.int32, (SUBLANES, LANES), 1) < HEAD_DIM
    own = ((row == 0) & low) | ((row == 1) & jnp.logical_not(low))
    W = kt_ref.shape[3]

    def pair(pp):
        cols = slice(pp * LANES, (pp + 1) * LANES)
        q = jnp.where(own, q_ref[0, :, cols], 0.0).astype(BF16)
        kn = kn_ref[0, :, cols].astype(BF16).astype(F32)
        vn = vn_ref[0, :, cols].astype(BF16).astype(F32)
        tab = tab_ref[pp]
        s_self = jnp.sum(q.astype(F32) * kn, axis=-1, keepdims=True) + tab[:, W:W + 1]
        s = _dot(q, kt_ref[0, 0, cols, :]) + tab[:, :W]
        yield
        m = jnp.maximum(s_self, jnp.max(s, axis=-1, keepdims=True))
        p_self = jnp.exp(s_self - m)
        p = jnp.exp(s - m)
        den = p_self + jnp.sum(p, axis=-1, keepdims=True)
        o = (p_self * vn + _dot_nt(p, vt_ref[0, 0, cols, :])) / den
        yield
        return jnp.where(low[0:1], o[0:1], o[1:2])

    outs = _round_robin([pair(pp) for pp in range(q_ref.shape[2] // LANES)])
    o_ref[0] = jnp.concatenate(outs, axis=-1).astype(o_ref.dtype)


def _attention_sample(q, k_new, v_new, cache_k, cache_v, l, rel_bias):
    L, DB, W, H, E = cache_k.shape
    DA = H * E
    assert W >= max(w for w, _ in BRANCHES)
    tab = _distance_logits(rel_bias, W + 1)[:, ::-1].reshape(H // 2, 2, W + 1)
    tab = jnp.concatenate([tab, jnp.zeros((H // 2, SUBLANES - 2, W + 1), F32)], axis=1)
    kt = jnp.transpose(cache_k, (0, 1, 3, 4, 2)).reshape(L, DB, DA, W)
    vt = jnp.transpose(cache_v, (0, 1, 3, 4, 2)).reshape(L, DB, DA, W)
    npairs = math.gcd(SATT_PAIRS, H // 2)
    cw = npairs * LANES
    vec = pl.BlockSpec((1, 1, cw), lambda p, b: (b, 0, p))
    buf = pl.BlockSpec((1, 1, cw, W), lambda p, b: (l, b, p, 0))
    return pl.pallas_call(
        _sattn_kernel,
        grid=(DA // cw, DB),
        in_specs=[vec, vec, vec, buf, buf, pl.BlockSpec((npairs, SUBLANES, W + 1), lambda p, b: (p, 0, 0))],
        out_specs=vec,
        out_shape=jax.ShapeDtypeStruct((DB, 1, DA), BF16),
        compiler_params=_params(("arbitrary", "arbitrary"), 2 * 2 * cw * W * 4 + 4 * npairs * SUBLANES * W * 4),
        name="attn_sample",
    )(q, k_new, v_new, kt, vt, tab)


def _prep_kernel(*refs, shift_rows, has_vres):
    it = iter(refs)
    h_ref = next(it)
    if shift_rows:
        hp8_ref, hlast_ref = next(it), next(it)
    else:
        hprev_ref = next(it)
    rkv_ref = next(it)
    if shift_rows:
        rp8_ref, rlast_ref = next(it), next(it)
    else:
        rprev_ref = next(it)
    if has_vres:
        vfirst_ref = next(it)
    mu_ref, murkv_ref, w0_ref, dw1_ref, dw2_ref, a0_ref, aw1_ref, aw2_ref, gw1_ref, gw2_ref = (next(it) for _ in range(10))
    if has_vres:
        vmu_ref, v0_ref, vw1_ref, vw2_ref = (next(it) for _ in range(4))
    kk_ref, ka_ref = next(it), next(it)
    r_out, lw_out, k_out, v_out, kk_out, b_out, g_out = (next(it) for _ in range(7))

    h = h_ref[0]
    rkv0 = rkv_ref[0]
    tm = h.shape[0]
    if shift_rows:
        first = pl.program_id(1) == 0
        row0 = lax.broadcasted_iota(jnp.int32, (tm, 1), 0) == 0
        h_edge = jnp.where(first, hlast_ref[0], hp8_ref[0, SUBLANES - 1:SUBLANES, :])
        r_edge = jnp.where(first, rlast_ref[0], rp8_ref[0, SUBLANES - 1:SUBLANES, :])
        hprev = jnp.where(row0, h_edge, pltpu.roll(h, 1, 0))
        rprev = jnp.where(row0, r_edge, pltpu.roll(rkv0, 1, 0))
    else:
        hprev = hprev_ref[0]
        rprev = rprev_ref[0]

    hb = h.astype(BF16)
    dhb = (hprev - h).astype(BF16)
    mu = mu_ref[0].astype(BF16)
    xw = hb + dhb * mu[0:1]
    xa = hb + dhb * mu[1:2]
    xg = hb + dhb * mu[2:3]

    z = w0_ref[0] + _dot(jnp.tanh(_dot(xw, dw1_ref[0])), dw2_ref[0])
    lw_out[0] = -math.exp(-0.5) * _sigmoid(z)

    a = _sigmoid(a0_ref[0] + _dot(_dot(xa, aw1_ref[0]), aw2_ref[0]))
    g_out[0] = _dot(_sigmoid(_dot(xg, gw1_ref[0])), gw2_ref[0])

    murkv = murkv_ref[0]
    DR = kk_ref.shape[-1]
    r0, k0, v0 = (rkv0[:, j * DR:(j + 1) * DR] for j in range(3))
    rp, kp, vp = (rprev[:, j * DR:(j + 1) * DR] for j in range(3))
    r_out[0] = r0 + (rp - r0) * murkv[0:1]
    kr = k0 + (kp - k0) * murkv[1:2]
    vr = v0 + (vp - v0) * murkv[2:3]
    if has_vres:
        xv = hb + dhb * vmu_ref[0].astype(BF16)
        vgate = _sigmoid(v0_ref[0] + _dot(_dot(xv, vw1_ref[0]), vw2_ref[0]))
        vr = vr + (vfirst_ref[0] - vr) * vgate
    v_out[0] = vr
    kk = kr * kk_ref[0]
    kkn = kk * lax.rsqrt(jnp.maximum(_head_sums(kk * kk), 1e-24))
    kk_out[0] = kkn
    b_out[0] = kkn * a
    k_out[0] = kr * (1.0 + (a - 1.0) * ka_ref[0])


def _rwkv_prep(h, h_prev, rkv0, rkv_prev, v_first, P, l, *, tm=256):
    B, T, D = h.shape
    DR = rkv0.shape[2] // 3
    tm = min(tm, T)
    shift_rows = h_prev.shape[1] == 1 and T > 1
    has_vres = v_first is not None

    def tile(C):
        return pl.BlockSpec((1, tm, C), lambda b, i: (b, i, 0))

    def prev8(C):
        return pl.BlockSpec((1, SUBLANES, C), lambda b, i: (b, jnp.maximum(i * (tm // SUBLANES) - 1, 0), 0))

    def seq_row(C):
        return pl.BlockSpec((1, 1, C), lambda b, i: (b, 0, 0))

    def layer(shape, ll=l):
        return pl.BlockSpec((1,) + shape, lambda b, i: (ll,) + (0,) * len(shape))

    args, specs = [h], [tile(D)]
    if shift_rows:
        args += [h, h_prev]
        specs += [prev8(D), seq_row(D)]
    else:
        args += [jnp.broadcast_to(h_prev, h.shape)]
        specs += [tile(D)]
    args.append(rkv0)
    specs.append(tile(3 * DR))
    if shift_rows:
        args += [rkv0, rkv_prev]
        specs += [prev8(3 * DR), seq_row(3 * DR)]
    else:
        args += [jnp.broadcast_to(rkv_prev, rkv0.shape)]
        specs += [tile(3 * DR)]
    if has_vres:
        args.append(v_first)
        specs.append(tile(DR))
    r1 = lambda a: a.reshape(a.shape[0], 1, a.shape[-1])
    for name in ('mu_wag', 'mu_rkv'):
        args.append(P[name]); specs.append(layer(P[name].shape[1:]))
    args.append(r1(P['decay_w0'])); specs.append(layer((1, DR)))
    for name in ('decay_w1', 'decay_w2'):
        args.append(P[name]); specs.append(layer(P[name].shape[1:]))
    args.append(r1(P['aaa_a0'])); specs.append(layer((1, DR)))
    for name in ('aaa_w1', 'aaa_w2', 'gate_w1', 'gate_w2'):
        args.append(P[name]); specs.append(layer(P[name].shape[1:]))
    if has_vres:
        args.append(r1(P['vres_mu'])); specs.append(layer((1, D), l - 1))
        args.append(r1(P['vres_v0'])); specs.append(layer((1, DR), l - 1))
        for name in ('vres_w1', 'vres_w2'):
            args.append(P[name]); specs.append(layer(P[name].shape[1:], l - 1))
    args.append(r1(P['k_k'])); specs.append(layer((1, DR)))
    args.append(r1(P['k_a'])); specs.append(layer((1, DR)))
    nbytes = 2 * tm * 4 * (2 * D + 7 * DR + 7 * DR + DR) + 8 * tm * D * 4 + 4 * D * 512 * 4
    return pl.pallas_call(
        functools.partial(_prep_kernel, shift_rows=shift_rows, has_vres=has_vres),
        grid=(B, T // tm),
        in_specs=specs,
        out_specs=[tile(DR) for _ in range(7)],
        out_shape=[jax.ShapeDtypeStruct((B, T, DR), F32) for _ in range(7)],
        compiler_params=_params(("arbitrary", "arbitrary"), nbytes),
        name="rwkv_prep",
    )(*args)


def _prefix_sum_rows(x):
    n = x.shape[0]
    row = lax.broadcasted_iota(jnp.int32, (n, 1), 0)
    s = 1
    while s < n:
        x = x + jnp.where(row >= s, pltpu.roll(x, s, 0), 0.0)
        s *= 2
    return x


def _unit_lower_inverse(a_strict, blk):
    n = a_strict.shape[0]
    ti = lax.broadcasted_iota(jnp.int32, (n, n), 0)
    si = lax.broadcasted_iota(jnp.int32, (n, n), 1)

    def lower_left(s):
        return ((ti // (2 * s)) == (si // (2 * s))) & ((ti % (2 * s)) >= s) & ((si % (2 * s)) < s)

    d = (ti == si).astype(F32) + jnp.where(lower_left(1), a_strict, 0.0)
    s = 2
    while s < blk:
        t = _dot(d, jnp.where(lower_left(s), a_strict, 0.0))
        yield
        d = d + _dot(t, d)
        yield
        s *= 2
    return d


def _round_robin(generators):
    results = [None] * len(generators)
    live = list(range(len(generators)))
    while live:
        for u in list(live):
            try:
                next(generators[u])
            except StopIteration as done:
                results[u] = done.value
                live.remove(u)
    return results


def _pair_chunk_terms(r, lw, k, v, kk, b, low, tri):
    C = r.shape[0]
    cum = _prefix_sum_rows(lw)
    g_in = jnp.exp(cum)
    g_inv = jnp.exp(-cum)
    g_end = g_in[C - 1:C]
    a_t = -kk * jnp.exp(cum - lw)
    r_t = r * g_in
    b_t = b * g_inv
    k_t = k * g_inv
    bg = b_t * g_end
    kg = k_t * g_end
    zc = jnp.zeros((C, LANES), F32)
    h0 = lambda x: jnp.where(low, x, 0.0)
    h1 = lambda x: jnp.where(low, 0.0, x)
    v0, v1 = h0(v), h1(v)
    ar = jnp.concatenate([a_t, r_t], axis=0)
    m0 = jnp.where(tri, _dot_nt(h0(ar), jnp.concatenate([b_t, k_t], axis=0)), 0.0)
    m1 = jnp.where(tri, _dot_nt(h1(ar), jnp.concatenate([k_t, b_t], axis=0)), 0.0)
    yield
    top0, bot0, top1, bot1 = m0[:C], m0[C:], m1[:C], m1[C:]
    stack2 = lambda x0, x1: jnp.concatenate([jnp.concatenate([x0, zc], axis=0),
                                             jnp.concatenate([zc, x1], axis=0)], axis=1)
    akv = _dot(stack2(top0, top1), jnp.concatenate([zc, v0, v1, zc], axis=0))
    a_sw = pltpu.roll(a_t, HEAD_DIM, 1)
    tinv = yield from _unit_lower_inverse(jnp.concatenate([h0(top0), h1(top1)], axis=0), C)
    x = _dot(tinv, akv + jnp.concatenate([h1(a_sw), h0(a_sw)], axis=0))
    yield
    z = jnp.concatenate([x[:C], v0, v1, x[C:]], axis=0)
    e = _dot(stack2(bot0, bot1), z)
    gh = _dot_tn(z, jnp.concatenate([h0(bg), h0(kg), h1(kg), h1(bg)], axis=0))
    yield
    y0 = jnp.where(low, e[:C], e[C:])
    r_eff = r_t + pltpu.roll(jnp.where(low, e[C:], e[:C]), HEAD_DIM, 1)
    h_mat = jnp.concatenate([h0(gh[:C]), h1(gh[C:])], axis=0)
    g_mat = jnp.concatenate([h0(gh[C:]), h1(gh[:C])], axis=0)
    return r_eff, y0, g_mat, h_mat, g_end


def _scan_kernel(r_ref, lw_ref, k_ref, v_ref, kk_ref, b_ref, g_ref, lng_ref, lnb_ref, rk_ref,
                 o_ref, s_ref, reff_s, y0_s, gm_s, hm_s, ge_s, *, chunk, unroll):
    C = chunk
    NS, T = r_ref.shape[:2]
    NC = T // C
    E = HEAD_DIM
    low = lax.broadcasted_iota(jnp.int32, (1, LANES), 1) < E
    ti = lax.broadcasted_iota(jnp.int32, (2 * C, 2 * C), 0)
    si = lax.broadcasted_iota(jnp.int32, (2 * C, 2 * C), 1)
    tri = (si % C) <= jnp.where(ti < C, ti - 1, ti - C)

    def phase1(i, _):
        sq = i // (NC // unroll)
        c0 = (i - sq * (NC // unroll)) * unroll
        cs = [c0 + u for u in range(unroll)]
        rows = [pl.ds(pl.multiple_of(c * C, C), C) for c in cs]
        loaded = [[ref[sq, rw, :] for ref in (r_ref, lw_ref, k_ref, v_ref, kk_ref, b_ref)] for rw in rows]
        terms = _round_robin([_pair_chunk_terms(*args, low, tri) for args in loaded])
        for c, rw, (r_eff, y0, g_mat, h_mat, g_end) in zip(cs, rows, terms):
            reff_s[sq, rw, :] = r_eff
            y0_s[sq, rw, :] = y0
            gm_s[sq, c] = g_mat
            hm_s[sq, c] = h_mat
            ge_s[sq, c] = jnp.broadcast_to(g_end, (SUBLANES, LANES))
        return 0

    lax.fori_loop(0, NS * (NC // unroll), phase1, 0)

    def phase2(c, states):
        rows = pl.ds(pl.multiple_of(c * C, C), C)
        ys = [_dot_nt(reff_s[sq, rows, :], S) + y0_s[sq, rows, :] for sq, S in enumerate(states)]
        new = tuple(S * ge_s[sq, c][0:1] + _dot(S, gm_s[sq, c]) + hm_s[sq, c] for sq, S in enumerate(states))
        for sq, y in enumerate(ys):
            y0_s[sq, rows, :] = y
        return new

    states = lax.fori_loop(0, NC, phase2, tuple(jnp.zeros((LANES, LANES), F32) for _ in range(NS)))
    for sq, S in enumerate(states):
        s_ref[sq, 0] = S[:E, :E]
        s_ref[sq, 1] = S[E:, E:]

    def finish(y, r, k, v, g):
        mu = _head_sums(y) * (1.0 / E)
        yield
        yc = y - mu
        var = _head_sums(yc * yc) * (1.0 / E)
        bonus = _head_sums(r * k * rk_ref[0])
        yield
        yn = yc * lax.rsqrt(var + GN_EPS) * lng_ref[0] + lnb_ref[0]
        return ((yn + bonus * v) * g).astype(o_ref.dtype)

    def phase3(i, _):
        sq = i // (NC // unroll)
        c0 = (i - sq * (NC // unroll)) * unroll
        rows = [pl.ds(pl.multiple_of((c0 + u) * C, C), C) for u in range(unroll)]
        loaded = [[y0_s[sq, rw, :]] + [ref[sq, rw, :] for ref in (r_ref, k_ref, v_ref, g_ref)] for rw in rows]
        outs = _round_robin([finish(*args) for args in loaded])
        for rw, out in zip(rows, outs):
            o_ref[sq, rw, :] = out
        return 0

    lax.fori_loop(0, NS * (NC // unroll), phase3, 0)


def _rwkv_scan(r, lw, k, v, kk, b, g, lnx_g, lnx_b, r_k, l):
    B, T, DR = r.shape
    H = DR // HEAD_DIM
    L = lnx_g.shape[0]
    C = min(SCAN_CHUNK, T)
    nc = T // C
    unroll = math.gcd(SCAN_UNROLL, nc)
    ns = math.gcd(SCAN_SEQS, B)
    assert T % C == 0
    seq = pl.BlockSpec((ns, T, LANES), lambda p, b_: (b_, 0, p))
    par = pl.BlockSpec((1, 1, LANES), lambda p, b_: (l, 0, p))
    scratch = [pltpu.VMEM((ns, T, LANES), F32), pltpu.VMEM((ns, T, LANES), F32),
               pltpu.VMEM((ns, nc, LANES, LANES), F32), pltpu.VMEM((ns, nc, LANES, LANES), F32),
               pltpu.VMEM((ns, nc, SUBLANES, LANES), F32)]
    nbytes = ns * (2 * 8 * T * LANES * 4 + 2 * T * LANES * 4 + 2 * nc * LANES * LANES * 4 + nc * SUBLANES * LANES * 4)
    return pl.pallas_call(
        functools.partial(_scan_kernel, chunk=C, unroll=unroll),
        grid=(DR // LANES, B // ns),
        in_specs=[seq] * 7 + [par] * 3,
        out_specs=[seq, pl.BlockSpec((ns, 2, HEAD_DIM, HEAD_DIM), lambda p, b_: (b_, p, 0, 0))],
        out_shape=[jax.ShapeDtypeStruct((B, T, DR), BF16), jax.ShapeDtypeStruct((B, H, HEAD_DIM, HEAD_DIM), F32)],
        scratch_shapes=scratch,
        compiler_params=_params(("arbitrary", "arbitrary"), nbytes),
        name="rwkv_scan",
    )(r, lw, k, v, kk, b, g, lnx_g.reshape(L, 1, DR), lnx_b.reshape(L, 1, DR), r_k.reshape(L, 1, DR))


def _step_kernel(s_ref, r_ref, lw_ref, k_ref, v_ref, kk_ref, b_ref, g_ref, lng_ref, lnb_ref, rk_ref,
                 o_ref, so_ref, *, n_heads):
    E = HEAD_DIM
    eye = (lax.broadcasted_iota(jnp.int32, (E, E), 0) == lax.broadcasted_iota(jnp.int32, (E, E), 1)).astype(F32)

    def head(hh):
        sl = slice(hh * E, (hh + 1) * E)
        S = s_ref[0, 0, hh]
        r, k, v, kk, b, g = (ref[0, :, sl] for ref in (r_ref, k_ref, v_ref, kk_ref, b_ref, g_ref))
        w = jnp.exp(lw_ref[0, :, sl])
        v_col = jnp.sum(eye * v, axis=-1, keepdims=True)
        sa = jnp.sum(S * (-kk), axis=-1, keepdims=True)
        bonus = jnp.sum(r * k * rk_ref[0, :, sl], axis=-1, keepdims=True)
        yield
        S = S * w + sa * b + v_col * k
        y_col = jnp.sum(S * r, axis=-1, keepdims=True)
        yield
        y = jnp.sum(eye * y_col, axis=0, keepdims=True)
        mu = jnp.mean(y, axis=-1, keepdims=True)
        yield
        yc = y - mu
        var = jnp.mean(yc * yc, axis=-1, keepdims=True)
        yield
        yn = yc * lax.rsqrt(var + GN_EPS) * lng_ref[0, :, sl] + lnb_ref[0, :, sl]
        return S, (yn + bonus * v) * g

    results = _round_robin([head(hh) for hh in range(n_heads)])
    for hh, (S, _) in enumerate(results):
        so_ref[0, hh] = S
    o_ref[0] = jnp.concatenate([o for _, o in results], axis=-1).astype(o_ref.dtype)


def _rwkv_step(state, r, lw, k, v, kk, b, g, lnx_g, lnx_b, r_k, l):
    DB, H = state.shape[1:3]
    DR = H * HEAD_DIM
    L = lnx_g.shape[0]
    vec = pl.BlockSpec((1, 1, DR), lambda b_: (b_, 0, 0))
    par = pl.BlockSpec((1, 1, DR), lambda b_: (l, 0, 0))
    st = pl.BlockSpec((1, H, HEAD_DIM, HEAD_DIM), lambda b_: (b_, 0, 0, 0))
    return pl.pallas_call(
        functools.partial(_step_kernel, n_heads=H),
        grid=(DB,),
        in_specs=[pl.BlockSpec((1, 1, H, HEAD_DIM, HEAD_DIM), lambda b_: (l, b_, 0, 0, 0))] + [vec] * 7 + [par] * 3,
        out_specs=[vec, st],
        out_shape=[jax.ShapeDtypeStruct((DB, 1, DR), BF16), jax.ShapeDtypeStruct(state.shape[1:], F32)],
        compiler_params=_params(("arbitrary",), 4 * H * HEAD_DIM * LANES * 4),
        name="rwkv_step",
    )(state, r, lw, k, v, kk, b, g, lnx_g.reshape(L, 1, DR), lnx_b.reshape(L, 1, DR), r_k.reshape(L, 1, DR))


def _layer(P, l, xp, xs, mod, base, vf_p, vf_s, k_stack, v_stack, cache_k, cache_v, state_wkv, h_last_s):
    B, T, D = xp.shape
    DB = xs.shape[1]
    DA = P['rel_bias'].shape[1] * HEAD_DIM
    n_rkv = P['w_in'].shape[2] - 3 * DA
    w_in = P['w_in']
    rows_p, rows_s = (DB, None), (0, DB)
    h, hb = _norm_mod(xp, P['norm1_g'], l, mod, rows_p, 0, [F32, BF16])
    hs, hbs = _norm_mod(xs, P['norm1_g'], l, mod, rows_s, 0, [F32, BF16])
    q, q_s = _proj(hb, hbs, w_in, l, 0, DA, head_g=P['q_norm_g'], scale=ATT_SCALE)
    k_stack, k_s = _proj(hb, hbs, w_in, l, DA, DA, head_g=P['k_norm_g'], stack=k_stack)
    v_stack, v_s = _proj(hb, hbs, w_in, l, 2 * DA, DA, stack=v_stack)
    both = jnp.concatenate([hbs, h_last_s[None].astype(BF16)], axis=1)
    rkv0, rkv2_s = _proj(hb, both, w_in, l, 3 * DA, n_rkv)

    att = _attention_prompt(q, k_stack, v_stack, l, base)
    r, lw, kr, vr, kk, b, g = _rwkv_prep(h, jnp.zeros((B, 1, D), F32), rkv0, jnp.zeros((B, 1, n_rkv), F32), vf_p, P, l)
    if vf_p is None:
        vf_p = vr
    rw, state_p = _rwkv_scan(r, lw, kr, vr, kk, b, g, P['lnx_g'], P['lnx_b'], P['r_k'], l)

    per_seq = lambda t: t.reshape(DB, 1, t.shape[-1])
    att_s = _attention_sample(per_seq(q_s), per_seq(k_s), per_seq(v_s), cache_k, cache_v, l, P['rel_bias'])
    r, lw, kr, vr, kk, b, g = _rwkv_prep(hs, h_last_s[None], rkv2_s[:, :DB], rkv2_s[:, DB:], vf_s, P, l)
    if vf_s is None:
        vf_s = vr
    rw_s, state_s = _rwkv_step(state_wkv, *(per_seq(t) for t in (r, lw, kr, vr, kk, b, g)),
                                P['lnx_g'], P['lnx_b'], P['r_k'], l)

    xp, xs = _proj_resid([att, rw], [att_s.reshape(1, DB, DA), rw_s.reshape(1, DB, DA)], P['w_out'], l,
                         xp, xs, mod, DB, 2, tm=1024, tn=1024)
    (h2,) = _norm_mod(xp, P['norm2_g'], l, mod, rows_p, 3, [BF16])
    (h2s,) = _norm_mod(xs, P['norm2_g'], l, mod, rows_s, 3, [BF16])
    act, act_s = _proj_swiglu(h2, h2s, P['w_gu'], l)
    xp, xs = _proj_resid([act], [act_s], P['w_down'], l, xp, xs, mod, DB, 5, tm=512, tn=512)
    return xp, xs, vf_p, vf_s, k_stack, v_stack, state_p, h[:, -1], k_s[0], v_s[0], state_s, hs[0]


def kernel(x_prompt, x_sample, c_prompt, c_sample, cache_k, cache_v, state_wkv, state_shift, rel_bias, ada_w, ada_b, norm1_g, norm2_g, w_in, q_norm_g, k_norm_g, mu_wag, mu_rkv, decay_w0, decay_w1, decay_w2, aaa_a0, aaa_w1, aaa_w2, gate_w1, gate_w2, vres_mu, vres_v0, vres_w1, vres_w2, k_k, k_a, r_k, lnx_g, lnx_b, w_out, w_gu, w_down):
    P = dict(rel_bias=rel_bias, norm1_g=norm1_g, norm2_g=norm2_g, w_in=w_in, q_norm_g=q_norm_g,
             k_norm_g=k_norm_g, mu_wag=mu_wag, mu_rkv=mu_rkv, decay_w0=decay_w0, decay_w1=decay_w1,
             decay_w2=decay_w2, aaa_a0=aaa_a0, aaa_w1=aaa_w1, aaa_w2=aaa_w2, gate_w1=gate_w1,
             gate_w2=gate_w2, vres_mu=vres_mu, vres_v0=vres_v0, vres_w1=vres_w1, vres_w2=vres_w2,
             k_k=k_k, k_a=k_a, r_k=r_k.reshape(r_k.shape[0], -1), lnx_g=lnx_g, lnx_b=lnx_b,
             w_out=w_out, w_gu=w_gu, w_down=w_down)
    L = ada_w.shape[0]
    B, T, D = x_prompt.shape
    DB = x_sample.shape[0]
    H_ATT = rel_bias.shape[1]
    assert x_sample.shape[1] == 1 and T <= W_MAX

    rows = -(-(B + DB) // SUBLANES) * SUBLANES
    c_all = jnp.concatenate([c_sample, c_prompt, jnp.zeros((rows - B - DB, D), F32)], axis=0)
    mod = _ada(c_all, ada_w, ada_b).reshape(L, rows, 1, 6 * D)
    base = _band_bias_rows(rel_bias)

    k_stack = jnp.zeros((L, B, T, H_ATT * HEAD_DIM), F32)
    v_stack = jnp.zeros((L, B, T, H_ATT * HEAD_DIM), F32)
    xp, xs, vf_p, vf_s = x_prompt, x_sample.reshape(1, DB, D), None, None
    ps, ph, sk, sv, ss, sh = [], [], [], [], [], []
    for l in range(L):
        (xp, xs, vf_p, vf_s, k_stack, v_stack, state_p, shift_p, k_s, v_s, state_s, shift_s) = _layer(
            P, l, xp, xs, mod, base, vf_p, vf_s, k_stack, v_stack, cache_k, cache_v, state_wkv, state_shift[l])
        ps.append(state_p)
        ph.append(shift_p)
        sk.append(k_s.reshape(DB, 1, H_ATT, HEAD_DIM))
        sv.append(v_s.reshape(DB, 1, H_ATT, HEAD_DIM))
        ss.append(state_s)
        sh.append(shift_s)

    return (xp, xs.reshape(DB, 1, D), k_stack.reshape(L, B, T, H_ATT, HEAD_DIM),
            v_stack.reshape(L, B, T, H_ATT, HEAD_DIM), jnp.stack(ps), jnp.stack(ph),
            jnp.stack(sk), jnp.stack(sv), jnp.stack(ss), jnp.stack(sh))
```

```python
import functools
import math

import jax
import jax.numpy as jnp
from jax import lax
from jax.experimental import pallas as pl
from jax.experimental.pallas import tpu as pltpu

F32 = jnp.float32
BF16 = jnp.bfloat16

HEAD_DIM = 64
BRANCHES = ((128, 1), (512, 4), (2048, 16))
W_MAX = 2048
NUM_BUCKETS = 32
MAX_DISTANCE = W_MAX
ATT_SCALE = HEAD_DIM ** -0.5
RMS_EPS = 1e-6
GN_EPS = 64e-5
NEG = -1e30

LANES = 128
SUBLANES = 8
Q_BLOCK = 128
SCAN_CHUNK = 64
SCAN_SEQS = 2
SCAN_UNROLL = 16
SATT_PAIRS = 4
ATT_UNROLL = 4
VMEM_CAP = 56 * 1024 * 1024

assert all(w // dil == Q_BLOCK for w, dil in BRANCHES) and 2 * HEAD_DIM == LANES


def _vmem(nbytes):
    return int(min(VMEM_CAP, nbytes * 1.3 + (6 << 20)))


def _params(sem, nbytes):
    return pltpu.CompilerParams(dimension_semantics=sem, vmem_limit_bytes=_vmem(nbytes))


def _dot(a, b):
    return jnp.dot(a.astype(BF16), b.astype(BF16), preferred_element_type=F32)


def _dot_nt(a, b):
    return lax.dot_general(a.astype(BF16), b.astype(BF16), (((1,), (1,)), ((), ())), preferred_element_type=F32)


def _dot_tn(a, b):
    return lax.dot_general(a.astype(BF16), b.astype(BF16), (((0,), (0,)), ((), ())), preferred_element_type=F32)


def _sigmoid(x):
    return 1.0 / (1.0 + jnp.exp(-x))


def _head_sums(x):
    r = lax.broadcasted_iota(jnp.int32, (LANES, LANES), 0) // HEAD_DIM
    c = lax.broadcasted_iota(jnp.int32, (LANES, LANES), 1) // HEAD_DIM
    bd = (r == c).astype(BF16)
    cols = [_dot(x[:, j * LANES:(j + 1) * LANES], bd) for j in range(x.shape[1] // LANES)]
    return cols[0] if len(cols) == 1 else jnp.concatenate(cols, axis=-1)


def _ada_kernel(c_ref, w_ref, b_ref, o_ref):
    c = c_ref[...]
    s = c * _sigmoid(c)
    o_ref[0] = _dot(s, w_ref[0]) + b_ref[0]


def _ada(c_all, ada_w, ada_b, tn=1024):
    L, D, N = ada_w.shape
    R = c_all.shape[0]
    return pl.pallas_call(
        _ada_kernel,
        grid=(L, N // tn),
        in_specs=[pl.BlockSpec((R, D), lambda l, j: (0, 0)),
                  pl.BlockSpec((1, D, tn), lambda l, j: (l, 0, j)),
                  pl.BlockSpec((1, 1, tn), lambda l, j: (l, 0, j))],
        out_specs=pl.BlockSpec((1, R, tn), lambda l, j: (l, 0, j)),
        out_shape=jax.ShapeDtypeStruct((L, R, N), F32),
        compiler_params=_params(("arbitrary", "arbitrary"), 2 * D * tn * 4 + D * tn * 2),
        name="ada_mod",
    )(c_all, ada_w, ada_b.reshape(L, 1, N))


def _norm_kernel(x_ref, g_ref, sh_ref, sc_ref, *out_refs):
    x = x_ref[0]
    y = x * lax.rsqrt(jnp.mean(x * x, axis=-1, keepdims=True) + RMS_EPS) * g_ref[0]
    h = y * (1.0 + sc_ref[0, :, 0, :]) + sh_ref[0, :, 0, :]
    for o in out_refs:
        o[0] = h.astype(o.dtype)


def _mod_spec(l, rows, which, D):
    r0, n = rows
    if n is None:
        return pl.BlockSpec((1, 1, 1, D), lambda b, i: (l, r0 + b, 0, which))
    return pl.BlockSpec((1, n, 1, D), lambda b, i: (l, 0, 0, which))


def _norm_mod(x, g_stack, l, mod, rows, which, out_dtypes, tm=1024):
    B, T, D = x.shape
    tm = min(tm, T)
    L = g_stack.shape[0]
    assert rows[1] is None or (B == 1 and rows[1] == T == tm)
    outs = pl.pallas_call(
        _norm_kernel,
        grid=(B, T // tm),
        in_specs=[pl.BlockSpec((1, tm, D), lambda b, i: (b, i, 0)),
                  pl.BlockSpec((1, 1, D), lambda b, i: (l, 0, 0)),
                  _mod_spec(l, rows, which, D), _mod_spec(l, rows, which + 1, D)],
        out_specs=[pl.BlockSpec((1, tm, D), lambda b, i: (b, i, 0)) for _ in out_dtypes],
        out_shape=[jax.ShapeDtypeStruct((B, T, D), dt) for dt in out_dtypes],
        compiler_params=_params(("arbitrary", "arbitrary"), 2 * tm * D * 4 * (2 + len(out_dtypes))),
        name="norm_mod",
    )(x, g_stack.reshape(L, 1, D), mod, mod)
    return outs


def _first_inner_step():
    return (pl.program_id(1) == 0) & (pl.program_id(2) == 0)


def _head_rmsnorm(acc, g):
    return acc * lax.rsqrt(_head_sums(acc * acc) * (1.0 / HEAD_DIM) + RMS_EPS) * g


def _rider_specs(R, k_sizes, tn):
    ins = [pl.BlockSpec((1, R, ks), lambda n, b, i: (0, 0, 0)) for ks in k_sizes]
    return ins, pl.BlockSpec((1, R, tn), lambda n, b, i: (0, 0, n))


def _mm_kernel(*refs, headnorm, scale, aliased):
    it = iter(refs)
    x_ref, w_ref = next(it), next(it)
    g_ref = next(it) if headnorm else None
    if aliased:
        next(it)
    xr_ref, o_ref, or_ref, wb_ref = next(it), next(it), next(it), next(it)

    def result(x):
        acc = jnp.dot(x, wb_ref[...], preferred_element_type=F32)
        if headnorm:
            acc = _head_rmsnorm(acc, g_ref[0])
            if scale != 1.0:
                acc = acc * scale
        return acc

    @pl.when(_first_inner_step())
    def _():
        wb_ref[...] = w_ref[0].astype(BF16)
        or_ref[0] = result(xr_ref[0])

    if len(o_ref.shape) == 4:
        o_ref[0, 0] = result(x_ref[0])
    else:
        o_ref[0] = result(x_ref[0])


def _proj(x, rider, w_stack, l, col0, ncols, *, head_g=None, scale=1.0, stack=None, tm=1024, tn=1024):
    B, T, K = x.shape
    R = rider.shape[1]
    tm = min(tm, T)
    tn = min(tn, ncols)
    assert T % tm == 0 and ncols % tn == 0 and col0 % tn == 0
    cb = col0 // tn
    in_specs = [pl.BlockSpec((1, tm, K), lambda n, b, i: (b, i, 0)),
                pl.BlockSpec((1, K, tn), lambda n, b, i: (l, 0, cb + n))]
    args = [x, w_stack]
    if head_g is not None:
        in_specs.append(pl.BlockSpec((1, 1, tn), lambda n, b, i: (l, 0, 0)))
        args.append(jnp.tile(head_g, (1, tn // HEAD_DIM)).reshape(head_g.shape[0], 1, tn))
    aliases = {}
    if stack is None:
        out_spec = pl.BlockSpec((1, tm, tn), lambda n, b, i: (b, i, n))
        out_shape = jax.ShapeDtypeStruct((B, T, ncols), F32)
    else:
        out_spec = pl.BlockSpec((1, 1, tm, tn), lambda n, b, i: (l, b, i, n))
        out_shape = jax.ShapeDtypeStruct(stack.shape, F32)
        in_specs.append(pl.BlockSpec(memory_space=pl.ANY))
        args.append(stack)
        aliases = {len(args) - 1: 0}
    r_ins, r_out = _rider_specs(R, (K,), tn)
    nbytes = 2 * (tm * K * 2 + K * tn * 4 + tm * tn * 4) + K * tn * 2 + 2 * tm * tn * 4
    return pl.pallas_call(
        functools.partial(_mm_kernel, headnorm=head_g is not None, scale=scale, aliased=bool(aliases)),
        grid=(ncols // tn, B, T // tm),
        in_specs=in_specs + r_ins,
        out_specs=[out_spec, r_out],
        out_shape=[out_shape, jax.ShapeDtypeStruct((1, R, ncols), F32)],
        scratch_shapes=[pltpu.VMEM((K, tn), BF16)],
        input_output_aliases=aliases,
        compiler_params=_params(("arbitrary",) * 3, nbytes),
        name="proj",
    )(*args, rider)


def _swiglu_kernel(x_ref, wg_ref, wu_ref, xr_ref, o_ref, or_ref, wgb_ref, wub_ref):
    def result(x):
        gate = jnp.dot(x, wgb_ref[...], preferred_element_type=F32)
        up = jnp.dot(x, wub_ref[...], preferred_element_type=F32)
        return (gate * _sigmoid(gate) * up).astype(o_ref.dtype)

    @pl.when(_first_inner_step())
    def _():
        wgb_ref[...] = wg_ref[0].astype(BF16)
        wub_ref[...] = wu_ref[0].astype(BF16)
        or_ref[0] = result(xr_ref[0])

    o_ref[0] = result(x_ref[0])


def _proj_swiglu(x, rider, w_gu, l, *, tm=1024, tn=512):
    B, T, K = x.shape
    R = rider.shape[1]
    F = w_gu.shape[2] // 2
    tm = min(tm, T)
    assert T % tm == 0 and F % tn == 0
    nb = F // tn
    r_ins, r_out = _rider_specs(R, (K,), tn)
    nbytes = 2 * (tm * K * 2 + 2 * K * tn * 4 + tm * tn * 2) + 2 * K * tn * 2 + 3 * tm * tn * 4
    return pl.pallas_call(
        _swiglu_kernel,
        grid=(nb, B, T // tm),
        in_specs=[pl.BlockSpec((1, tm, K), lambda n, b, i: (b, i, 0)),
                  pl.BlockSpec((1, K, tn), lambda n, b, i: (l, 0, n)),
                  pl.BlockSpec((1, K, tn), lambda n, b, i: (l, 0, nb + n))] + r_ins,
        out_specs=[pl.BlockSpec((1, tm, tn), lambda n, b, i: (b, i, n)), r_out],
        out_shape=[jax.ShapeDtypeStruct((B, T, F), BF16), jax.ShapeDtypeStruct((1, R, F), BF16)],
        scratch_shapes=[pltpu.VMEM((K, tn), BF16), pltpu.VMEM((K, tn), BF16)],
        compiler_params=_params(("arbitrary",) * 3, nbytes),
        name="proj_swiglu",
    )(x, w_gu, w_gu, rider)


def _resid_kernel(*refs, k_sizes):
    n_x = len(k_sizes)
    x_refs, refs = refs[:n_x], refs[n_x:]
    w_ref, res_ref, gate_ref = refs[:3]
    xr_refs, refs = refs[3:3 + n_x], refs[3 + n_x:]
    resr_ref, gater_ref, o_ref, or_ref, wb_ref = refs

    def result(xs, res, gate):
        acc = None
        k0 = 0
        for x_ref, ks in zip(xs, k_sizes):
            part = jnp.dot(x_ref[0], wb_ref[k0:k0 + ks, :], preferred_element_type=F32)
            acc = part if acc is None else acc + part
            k0 += ks
        return res[0] + gate[0, :, 0, :] * acc

    @pl.when(_first_inner_step())
    def _():
        wb_ref[...] = w_ref[0].astype(BF16)
        or_ref[0] = result(xr_refs, resr_ref, gater_ref)

    o_ref[0] = result(x_refs, res_ref, gate_ref)


def _proj_resid(xs, rider_xs, w_stack, l, resid, rider_resid, mod, row0, which, *, tm=512, tn=512):
    B, T, N = resid.shape
    R = rider_resid.shape[1]
    k_sizes = tuple(x.shape[2] for x in xs)
    K = sum(k_sizes)
    tm = min(tm, T)
    assert T % tm == 0 and N % tn == 0
    c0 = which * (N // tn)
    in_specs = [pl.BlockSpec((1, tm, ks), lambda n, b, i: (b, i, 0)) for ks in k_sizes]
    in_specs += [pl.BlockSpec((1, K, tn), lambda n, b, i: (l, 0, n)),
                 pl.BlockSpec((1, tm, tn), lambda n, b, i: (b, i, n)),
                 pl.BlockSpec((1, 1, 1, tn), lambda n, b, i: (l, row0 + b, 0, c0 + n))]
    r_ins, r_out = _rider_specs(R, k_sizes, tn)
    r_gate = pl.BlockSpec((1, R, 1, tn), lambda n, b, i: (l, 0, 0, c0 + n))
    nbytes = 2 * (tm * K * 2 + K * tn * 4 + 2 * tm * tn * 4) + K * tn * 2 + 2 * tm * tn * 4
    return pl.pallas_call(
        functools.partial(_resid_kernel, k_sizes=k_sizes),
        grid=(N // tn, B, T // tm),
        in_specs=in_specs + r_ins + [r_out, r_gate],
        out_specs=[pl.BlockSpec((1, tm, tn), lambda n, b, i: (b, i, n)), r_out],
        out_shape=[jax.ShapeDtypeStruct((B, T, N), F32), jax.ShapeDtypeStruct((1, R, N), F32)],
        scratch_shapes=[pltpu.VMEM((K, tn), BF16)],
        compiler_params=_params(("arbitrary",) * 3, nbytes),
        name="proj_resid",
    )(*xs, w_stack, resid, mod, *rider_xs, rider_resid, mod)


def _rel_bucket(dist):
    max_exact = NUM_BUCKETS // 2
    d = jnp.maximum(dist, 0)
    df = jnp.maximum(d, 1).astype(F32)
    large = max_exact + (jnp.log(df / max_exact) / math.log(MAX_DISTANCE / max_exact)
                         * (NUM_BUCKETS - max_exact)).astype(jnp.int32)
    return jnp.where(d < max_exact, d, jnp.minimum(large, NUM_BUCKETS - 1))


def _band_bias_rows(rel_bias):
    dsub = Q_BLOCK - jnp.arange(2 * Q_BLOCK)
    rows = []
    for (w, dil) in BRANCHES:
        valid = (dsub >= 0) & (dsub <= w // dil)
        bias = rel_bias[_rel_bucket(jnp.maximum(dsub, 0) * dil)].astype(F32).T
        rows.append(jnp.where(valid[None], bias, NEG))
    return jnp.stack(rows, axis=1)


def _attn_kernel(q_ref, k_ref, v_ref, base_ref, o_ref, qs, ks, vs, os_, ms_, ds_, stage):
    T = q_ref.shape[2]
    QB = Q_BLOCK
    low = lax.broadcasted_iota(jnp.int32, (1, LANES), 1) < HEAD_DIM
    own = (low, jnp.logical_not(low))
    prev_cols = lax.broadcasted_iota(jnp.int32, (1, 2 * QB), 1) < QB

    assert len(BRANCHES) == 3 and BRANCHES[0][1] == 1
    prev_dil = 1
    for bi, (_, dil) in enumerate(BRANCHES):
        L, Lp, ratio = T // dil, T // prev_dil, dil // prev_dil
        ks[bi, 0:QB, :] = jnp.zeros((QB, LANES), BF16)
        vs[bi, 0:QB, :] = jnp.zeros((QB, LANES), BF16)
        for r in range(dil):
            rows = pl.ds((r % prev_dil) * Lp + r // prev_dil, L, stride=ratio)
            if bi < 2:
                q, k, v = (ref[0, 0, rows, :] for ref in (q_ref, k_ref, v_ref))
            else:
                q, k, v = (stage[a, rows, :] for a in range(3))
            if bi == 1:
                for a, x in enumerate((q, k, v)):
                    stage[a, r * L:(r + 1) * L, :] = x
            for hh in range(2):
                qs[hh, bi, r * L:(r + 1) * L, :] = jnp.where(own[hh], q, 0.0).astype(BF16)
            ks[bi, QB + r * L:QB + (r + 1) * L, :] = k.astype(BF16)
            vs[bi, QB + r * L:QB + (r + 1) * L, :] = v.astype(BF16)
        prev_dil = dil

    for bi, (_, dil) in enumerate(BRANCHES):
        nb = T // dil // QB
        tiles = [pltpu.roll(jnp.broadcast_to(base_ref[hh, bi:bi + 1, :], (QB, 2 * QB)), 0, 1, stride=1, stride_axis=0)
                 for hh in range(2)]
        first_only = nb == 1
        if first_only:
            tiles = [t[:, QB:] for t in tiles]
        k0, kn = (QB, QB) if first_only else (0, 2 * QB)

        def one_block(g, q0, q1, kw, vw, nb=nb, tiles=tiles, first_only=first_only):
            ss = [lax.dot_general(q, kw, (((1,), (1,)), ((), ())), preferred_element_type=F32) + tiles[hh]
                  for hh, q in enumerate((q0, q1))]
            if not first_only:
                no_prev = jnp.logical_and(g % nb == 0, prev_cols)
                ss = [jnp.where(no_prev, NEG, s) for s in ss]
            yield
            ms = [jnp.max(s, axis=-1, keepdims=True) for s in ss]
            ps = [jnp.exp(s - m) for s, m in zip(ss, ms)]
            dens = [jnp.sum(p, axis=-1, keepdims=True) for p in ps]
            os2 = [jnp.dot(p.astype(BF16), vw, preferred_element_type=F32) for p in ps]
            yield
            pair = lambda x0, x1: jnp.where(low, jnp.broadcast_to(x0, (QB, LANES)), jnp.broadcast_to(x1, (QB, LANES)))
            return pair(*os2), pair(*ms), pair(*dens)

        def blocks(i, _, bi=bi, dil=dil, nb=nb, k0=k0, kn=kn):
            gs = [i * ATT_UNROLL + u for u in range(ATT_UNROLL)]
            ats = [pl.multiple_of(g * QB, QB) for g in gs]
            loaded = [(qs[0, bi, pl.ds(at, QB), :], qs[1, bi, pl.ds(at, QB), :],
                       ks[bi, pl.ds(at + k0, kn), :], vs[bi, pl.ds(at + k0, kn), :]) for at in ats]
            results = _round_robin([one_block(g, *ld) for g, ld in zip(gs, loaded)])
            for g, (acc, m, den) in zip(gs, results):
                r = g // nb
                tok = pl.ds((g - r * nb) * (QB * dil) + r, QB, stride=dil)
                os_[bi, tok, :] = acc
                ms_[bi, tok, :] = m
                ds_[bi, tok, :] = den
            return 0

        lax.fori_loop(0, T // QB // ATT_UNROLL, blocks, 0)

    def merge(i, _):
        rows = pl.ds(pl.multiple_of(i * QB, QB), QB)
        maxes = [ms_[bi, rows, :] for bi in range(len(BRANCHES))]
        m = functools.reduce(jnp.maximum, maxes)
        ws = [jnp.exp(x - m) for x in maxes]
        num = sum(w * os_[bi, rows, :] for bi, w in enumerate(ws))
        den = sum(w * ds_[bi, rows, :] for bi, w in enumerate(ws))
        o_ref[0, rows, :] = (num / den).astype(o_ref.dtype)
        return 0

    lax.fori_loop(0, T // QB, merge, 0)


def _attention_prompt(q, k_stack, v_stack, l, base):
    B, T, DA = q.shape
    nbr = len(BRANCHES)
    assert T % (Q_BLOCK * max(d for _, d in BRANCHES)) == 0
    qkv = pl.BlockSpec((1, 1, T, LANES), lambda p, b: (l, b, 0, p))
    scratch = [pltpu.VMEM((2, nbr, T, LANES), BF16), pltpu.VMEM((nbr, T + Q_BLOCK, LANES), BF16),
               pltpu.VMEM((nbr, T + Q_BLOCK, LANES), BF16), pltpu.VMEM((nbr, T, LANES), F32),
               pltpu.VMEM((nbr, T, LANES), F32), pltpu.VMEM((nbr, T, LANES), F32), pltpu.VMEM((3, T, LANES), F32)]
    nbytes = (2 * 3 * T * LANES * 4 + 4 * nbr * (T + Q_BLOCK) * LANES * 2 + 3 * nbr * T * LANES * 4
              + 3 * T * LANES * 4 + 2 * T * LANES * 2)
    return pl.pallas_call(
        _attn_kernel,
        grid=(DA // LANES, B),
        in_specs=[pl.BlockSpec((1, 1, T, LANES), lambda p, b: (0, b, 0, p)), qkv, qkv,
                  pl.BlockSpec((2, nbr, 2 * Q_BLOCK), lambda p, b: (p, 0, 0))],
        out_specs=pl.BlockSpec((1, T, LANES), lambda p, b: (b, 0, p)),
        out_shape=jax.ShapeDtypeStruct((B, T, DA), BF16),
        scratch_shapes=scratch,
        compiler_params=_params(("arbitrary", "arbitrary"), nbytes),
        name="attn_prompt",
    )(q[None], k_stack, v_stack, base)


def _distance_logits(rel_bias, nd):
    d = jnp.arange(nd)
    mult = sum(((d % dil == 0) & (d // dil <= w // dil)).astype(F32) for (w, dil) in BRANCHES)
    bias = rel_bias[_rel_bucket(d)].astype(F32).T
    return jnp.where(mult > 0, bias + jnp.log(jnp.maximum(mult, 1.0)), NEG)


def _sattn_kernel(q_ref, kn_ref, vn_ref, kt_ref, vt_ref, tab_ref, o_ref):
    row = lax.broadcasted_iota(jnp.int32, (SUBLANES, LANES), 0)
    low = lax.broadcasted_iota(jnp.int32, (SUBLANES, LANES), 1) < HEAD_DIM
    own = ((row == 0) & low) | ((row == 1) & jnp.logical_not(low))
    W = kt_ref.shape[3]

    def pair(pp):
        cols = slice(pp * LANES, (pp + 1) * LANES)
        q = jnp.where(own, q_ref[0, :, cols], 0.0).astype(BF16)
        kn = kn_ref[0, :, cols].astype(BF16).astype(F32)
        vn = vn_ref[0, :, cols].astype(BF16).astype(F32)
        tab = tab_ref[pp]
        s_self = jnp.sum(q.astype(F32) * kn, axis=-1, keepdims=True) + tab[:, W:W + 1]
        s = _dot(q, kt_ref[0, 0, cols, :]) + tab[:, :W]
        yield
        m = jnp.maximum(s_self, jnp.max(s, axis=-1, keepdims=True))
        p_self = jnp.exp(s_self - m)
        p = jnp.exp(s - m)
        den = p_self + jnp.sum(p, axis=-1, keepdims=True)
        o = (p_self * vn + _dot_nt(p, vt_ref[0, 0, cols, :])) / den
        yield
        return jnp.where(low[0:1], o[0:1], o[1:2])

    outs = _round_robin([pair(pp) for pp in range(q_ref.shape[2] // LANES)])
    o_ref[0] = jnp.concatenate(outs, axis=-1).astype(o_ref.dtype)


def _attention_sample(q, k_new, v_new, cache_k, cache_v, l, rel_bias):
    L, DB, W, H, E = cache_k.shape
    DA = H * E
    assert W >= max(w for w, _ in BRANCHES)
    tab = _distance_logits(rel_bias, W + 1)[:, ::-1].reshape(H // 2, 2, W + 1)
    tab = jnp.concatenate([tab, jnp.zeros((H // 2, SUBLANES - 2, W + 1), F32)], axis=1)
    kt = jnp.transpose(cache_k, (0, 1, 3, 4, 2)).reshape(L, DB, DA, W)
    vt = jnp.transpose(cache_v, (0, 1, 3, 4, 2)).reshape(L, DB, DA, W)
    npairs = math.gcd(SATT_PAIRS, H // 2)
    cw = npairs * LANES
    vec = pl.BlockSpec((1, 1, cw), lambda p, b: (b, 0, p))
    buf = pl.BlockSpec((1, 1, cw, W), lambda p, b: (l, b, p, 0))
    return pl.pallas_call(
        _sattn_kernel,
        grid=(DA // cw, DB),
        in_specs=[vec, vec, vec, buf, buf, pl.BlockSpec((npairs, SUBLANES, W + 1), lambda p, b: (p, 0, 0))],
        out_specs=vec,
        out_shape=jax.ShapeDtypeStruct((DB, 1, DA), BF16),
        compiler_params=_params(("arbitrary", "arbitrary"), 2 * 2 * cw * W * 4 + 4 * npairs * SUBLANES * W * 4),
        name="attn_sample",
    )(q, k_new, v_new, kt, vt, tab)


def _prep_kernel(*refs, shift_rows, has_vres):
    it = iter(refs)
    h_ref = next(it)
    if shift_rows:
        hp8_ref, hlast_ref = next(it), next(it)
    else:
        hprev_ref = next(it)
    rkv_ref = next(it)
    if shift_rows:
        rp8_ref, rlast_ref = next(it), next(it)
    else:
        rprev_ref = next(it)
    if has_vres:
        vfirst_ref = next(it)
    mu_ref, murkv_ref, w0_ref, dw1_ref, dw2_ref, a0_ref, aw1_ref, aw2_ref, gw1_ref, gw2_ref = (next(it) for _ in range(10))
    if has_vres:
        vmu_ref, v0_ref, vw1_ref, vw2_ref = (next(it) for _ in range(4))
    kk_ref, ka_ref = next(it), next(it)
    r_out, lw_out, k_out, v_out, kk_out, b_out, g_out = (next(it) for _ in range(7))

    h = h_ref[0]
    rkv0 = rkv_ref[0]
    tm = h.shape[0]
    if shift_rows:
        first = pl.program_id(1) == 0
        row0 = lax.broadcasted_iota(jnp.int32, (tm, 1), 0) == 0
        h_edge = jnp.where(first, hlast_ref[0], hp8_ref[0, SUBLANES - 1:SUBLANES, :])
        r_edge = jnp.where(first, rlast_ref[0], rp8_ref[0, SUBLANES - 1:SUBLANES, :])
        hprev = jnp.where(row0, h_edge, pltpu.roll(h, 1, 0))
        rprev = jnp.where(row0, r_edge, pltpu.roll(rkv0, 1, 0))
    else:
        hprev = hprev_ref[0]
        rprev = rprev_ref[0]

    hb = h.astype(BF16)
    dhb = (hprev - h).astype(BF16)
    mu = mu_ref[0].astype(BF16)
    xw = hb + dhb * mu[0:1]
    xa = hb + dhb * mu[1:2]
    xg = hb + dhb * mu[2:3]

    z = w0_ref[0] + _dot(jnp.tanh(_dot(xw, dw1_ref[0])), dw2_ref[0])
    lw_out[0] = -math.exp(-0.5) * _sigmoid(z)

    a = _sigmoid(a0_ref[0] + _dot(_dot(xa, aw1_ref[0]), aw2_ref[0]))
    g_out[0] = _dot(_sigmoid(_dot(xg, gw1_ref[0])), gw2_ref[0])

    murkv = murkv_ref[0]
    DR = kk_ref.shape[-1]
    r0, k0, v0 = (rkv0[:, j * DR:(j + 1) * DR] for j in range(3))
    rp, kp, vp = (rprev[:, j * DR:(j + 1) * DR] for j in range(3))
    r_out[0] = r0 + (rp - r0) * murkv[0:1]
    kr = k0 + (kp - k0) * murkv[1:2]
    vr = v0 + (vp - v0) * murkv[2:3]
    if has_vres:
        xv = hb + dhb * vmu_ref[0].astype(BF16)
        vgate = _sigmoid(v0_ref[0] + _dot(_dot(xv, vw1_ref[0]), vw2_ref[0]))
        vr = vr + (vfirst_ref[0] - vr) * vgate
    v_out[0] = vr
    kk = kr * kk_ref[0]
    kkn = kk * lax.rsqrt(jnp.maximum(_head_sums(kk * kk), 1e-24))
    kk_out[0] = kkn
    b_out[0] = kkn * a
    k_out[0] = kr * (1.0 + (a - 1.0) * ka_ref[0])


def _rwkv_prep(h, h_prev, rkv0, rkv_prev, v_first, P, l, *, tm=256):
    B, T, D = h.shape
    DR = rkv0.shape[2] // 3
    tm = min(tm, T)
    shift_rows = h_prev.shape[1] == 1 and T > 1
    has_vres = v_first is not None

    def tile(C):
        return pl.BlockSpec((1, tm, C), lambda b, i: (b, i, 0))

    def prev8(C):
        return pl.BlockSpec((1, SUBLANES, C), lambda b, i: (b, jnp.maximum(i * (tm // SUBLANES) - 1, 0), 0))

    def seq_row(C):
        return pl.BlockSpec((1, 1, C), lambda b, i: (b, 0, 0))

    def layer(shape, ll=l):
        return pl.BlockSpec((1,) + shape, lambda b, i: (ll,) + (0,) * len(shape))

    args, specs = [h], [tile(D)]
    if shift_rows:
        args += [h, h_prev]
        specs += [prev8(D), seq_row(D)]
    else:
        args += [jnp.broadcast_to(h_prev, h.shape)]
        specs += [tile(D)]
    args.append(rkv0)
    specs.append(tile(3 * DR))
    if shift_rows:
        args += [rkv0, rkv_prev]
        specs += [prev8(3 * DR), seq_row(3 * DR)]
    else:
        args += [jnp.broadcast_to(rkv_prev, rkv0.shape)]
        specs += [tile(3 * DR)]
    if has_vres:
        args.append(v_first)
        specs.append(tile(DR))
    r1 = lambda a: a.reshape(a.shape[0], 1, a.shape[-1])
    for name in ('mu_wag', 'mu_rkv'):
        args.append(P[name]); specs.append(layer(P[name].shape[1:]))
    args.append(r1(P['decay_w0'])); specs.append(layer((1, DR)))
    for name in ('decay_w1', 'decay_w2'):
        args.append(P[name]); specs.append(layer(P[name].shape[1:]))
    args.append(r1(P['aaa_a0'])); specs.append(layer((1, DR)))
    for name in ('aaa_w1', 'aaa_w2', 'gate_w1', 'gate_w2'):
        args.append(P[name]); specs.append(layer(P[name].shape[1:]))
    if has_vres:
        args.append(r1(P['vres_mu'])); specs.append(layer((1, D), l - 1))
        args.append(r1(P['vres_v0'])); specs.append(layer((1, DR), l - 1))
        for name in ('vres_w1', 'vres_w2'):
            args.append(P[name]); specs.append(layer(P[name].shape[1:], l - 1))
    args.append(r1(P['k_k'])); specs.append(layer((1, DR)))
    args.append(r1(P['k_a'])); specs.append(layer((1, DR)))
    nbytes = 2 * tm * 4 * (2 * D + 7 * DR + 7 * DR + DR) + 8 * tm * D * 4 + 4 * D * 512 * 4
    return pl.pallas_call(
        functools.partial(_prep_kernel, shift_rows=shift_rows, has_vres=has_vres),
        grid=(B, T // tm),
        in_specs=specs,
        out_specs=[tile(DR) for _ in range(7)],
        out_shape=[jax.ShapeDtypeStruct((B, T, DR), F32) for _ in range(7)],
        compiler_params=_params(("arbitrary", "arbitrary"), nbytes),
        name="rwkv_prep",
    )(*args)


def _prefix_sum_rows(x):
    n = x.shape[0]
    row = lax.broadcasted_iota(jnp.int32, (n, 1), 0)
    s = 1
    while s < n:
        x = x + jnp.where(row >= s, pltpu.roll(x, s, 0), 0.0)
        s *= 2
    return x


def _unit_lower_inverse(a_strict, blk):
    n = a_strict.shape[0]
    ti = lax.broadcasted_iota(jnp.int32, (n, n), 0)
    si = lax.broadcasted_iota(jnp.int32, (n, n), 1)

    def lower_left(s):
        return ((ti // (2 * s)) == (si // (2 * s))) & ((ti % (2 * s)) >= s) & ((si % (2 * s)) < s)

    d = (ti == si).astype(F32) + jnp.where(lower_left(1), a_strict, 0.0)
    s = 2
    while s < blk:
        t = _dot(d, jnp.where(lower_left(s), a_strict, 0.0))
        yield
        d = d + _dot(t, d)
        yield
        s *= 2
    return d


def _round_robin(generators):
    results = [None] * len(generators)
    live = list(range(len(generators)))
    while live:
        for u in list(live):
            try:
                next(generators[u])
            except StopIteration as done:
                results[u] = done.value
                live.remove(u)
    return results


def _pair_chunk_terms(r, lw, k, v, kk, b, low, tri):
    C = r.shape[0]
    cum = _prefix_sum_rows(lw)
    g_in = jnp.exp(cum)
    g_inv = jnp.exp(-cum)
    g_end = g_in[C - 1:C]
    a_t = -kk * jnp.exp(cum - lw)
    r_t = r * g_in
    b_t = b * g_inv
    k_t = k * g_inv
    bg = b_t * g_end
    kg = k_t * g_end
    zc = jnp.zeros((C, LANES), F32)
    h0 = lambda x: jnp.where(low, x, 0.0)
    h1 = lambda x: jnp.where(low, 0.0, x)
    v0, v1 = h0(v), h1(v)
    ar = jnp.concatenate([a_t, r_t], axis=0)
    m0 = jnp.where(tri, _dot_nt(h0(ar), jnp.concatenate([b_t, k_t], axis=0)), 0.0)
    m1 = jnp.where(tri, _dot_nt(h1(ar), jnp.concatenate([k_t, b_t], axis=0)), 0.0)
    yield
    top0, bot0, top1, bot1 = m0[:C], m0[C:], m1[:C], m1[C:]
    stack2 = lambda x0, x1: jnp.concatenate([jnp.concatenate([x0, zc], axis=0),
                                             jnp.concatenate([zc, x1], axis=0)], axis=1)
    akv = _dot(stack2(top0, top1), jnp.concatenate([zc, v0, v1, zc], axis=0))
    a_sw = pltpu.roll(a_t, HEAD_DIM, 1)
    tinv = yield from _unit_lower_inverse(jnp.concatenate([h0(top0), h1(top1)], axis=0), C)
    x = _dot(tinv, akv + jnp.concatenate([h1(a_sw), h0(a_sw)], axis=0))
    yield
    z = jnp.concatenate([x[:C], v0, v1, x[C:]], axis=0)
    e = _dot(stack2(bot0, bot1), z)
    gh = _dot_tn(z, jnp.concatenate([h0(bg), h0(kg), h1(kg), h1(bg)], axis=0))
    yield
    y0 = jnp.where(low, e[:C], e[C:])
    r_eff = r_t + pltpu.roll(jnp.where(low, e[C:], e[:C]), HEAD_DIM, 1)
    h_mat = jnp.concatenate([h0(gh[:C]), h1(gh[C:])], axis=0)
    g_mat = jnp.concatenate([h0(gh[C:]), h1(gh[:C])], axis=0)
    return r_eff, y0, g_mat, h_mat, g_end


def _scan_kernel(r_ref, lw_ref, k_ref, v_ref, kk_ref, b_ref, g_ref, lng_ref, lnb_ref, rk_ref,
                 o_ref, s_ref, reff_s, y0_s, gm_s, hm_s, ge_s, *, chunk, unroll):
    C = chunk
    NS, T = r_ref.shape[:2]
    NC = T // C
    E = HEAD_DIM
    low = lax.broadcasted_iota(jnp.int32, (1, LANES), 1) < E
    ti = lax.broadcasted_iota(jnp.int32, (2 * C, 2 * C), 0)
    si = lax.broadcasted_iota(jnp.int32, (2 * C, 2 * C), 1)
    tri = (si % C) <= jnp.where(ti < C, ti - 1, ti - C)

    def phase1(i, _):
        sq = i // (NC // unroll)
        c0 = (i - sq * (NC // unroll)) * unroll
        cs = [c0 + u for u in range(unroll)]
        rows = [pl.ds(pl.multiple_of(c * C, C), C) for c in cs]
        loaded = [[ref[sq, rw, :] for ref in (r_ref, lw_ref, k_ref, v_ref, kk_ref, b_ref)] for rw in rows]
        terms = _round_robin([_pair_chunk_terms(*args, low, tri) for args in loaded])
        for c, rw, (r_eff, y0, g_mat, h_mat, g_end) in zip(cs, rows, terms):
            reff_s[sq, rw, :] = r_eff
            y0_s[sq, rw, :] = y0
            gm_s[sq, c] = g_mat
            hm_s[sq, c] = h_mat
            ge_s[sq, c] = jnp.broadcast_to(g_end, (SUBLANES, LANES))
        return 0

    lax.fori_loop(0, NS * (NC // unroll), phase1, 0)

    def phase2(c, states):
        rows = pl.ds(pl.multiple_of(c * C, C), C)
        ys = [_dot_nt(reff_s[sq, rows, :], S) + y0_s[sq, rows, :] for sq, S in enumerate(states)]
        new = tuple(S * ge_s[sq, c][0:1] + _dot(S, gm_s[sq, c]) + hm_s[sq, c] for sq, S in enumerate(states))
        for sq, y in enumerate(ys):
            y0_s[sq, rows, :] = y
        return new

    states = lax.fori_loop(0, NC, phase2, tuple(jnp.zeros((LANES, LANES), F32) for _ in range(NS)))
    for sq, S in enumerate(states):
        s_ref[sq, 0] = S[:E, :E]
        s_ref[sq, 1] = S[E:, E:]

    def finish(y, r, k, v, g):
        mu = _head_sums(y) * (1.0 / E)
        yield
        yc = y - mu
        var = _head_sums(yc * yc) * (1.0 / E)
        bonus = _head_sums(r * k * rk_ref[0])
        yield
        yn = yc * lax.rsqrt(var + GN_EPS) * lng_ref[0] + lnb_ref[0]
        return ((yn + bonus * v) * g).astype(o_ref.dtype)

    def phase3(i, _):
        sq = i // (NC // unroll)
        c0 = (i - sq * (NC // unroll)) * unroll
        rows = [pl.ds(pl.multiple_of((c0 + u) * C, C), C) for u in range(unroll)]
        loaded = [[y0_s[sq, rw, :]] + [ref[sq, rw, :] for ref in (r_ref, k_ref, v_ref, g_ref)] for rw in rows]
        outs = _round_robin([finish(*args) for args in loaded])
        for rw, out in zip(rows, outs):
            o_ref[sq, rw, :] = out
        return 0

    lax.fori_loop(0, NS * (NC // unroll), phase3, 0)


def _rwkv_scan(r, lw, k, v, kk, b, g, lnx_g, lnx_b, r_k, l):
    B, T, DR = r.shape
    H = DR // HEAD_DIM
    L = lnx_g.shape[0]
    C = min(SCAN_CHUNK, T)
    nc = T // C
    unroll = math.gcd(SCAN_UNROLL, nc)
    ns = math.gcd(SCAN_SEQS, B)
    assert T % C == 0
    seq = pl.BlockSpec((ns, T, LANES), lambda p, b_: (b_, 0, p))
    par = pl.BlockSpec((1, 1, LANES), lambda p, b_: (l, 0, p))
    scratch = [pltpu.VMEM((ns, T, LANES), F32), pltpu.VMEM((ns, T, LANES), F32),
               pltpu.VMEM((ns, nc, LANES, LANES), F32), pltpu.VMEM((ns, nc, LANES, LANES), F32),
               pltpu.VMEM((ns, nc, SUBLANES, LANES), F32)]
    nbytes = ns * (2 * 8 * T * LANES * 4 + 2 * T * LANES * 4 + 2 * nc * LANES * LANES * 4 + nc * SUBLANES * LANES * 4)
    return pl.pallas_call(
        functools.partial(_scan_kernel, chunk=C, unroll=unroll),
        grid=(DR // LANES, B // ns),
        in_specs=[seq] * 7 + [par] * 3,
        out_specs=[seq, pl.BlockSpec((ns, 2, HEAD_DIM, HEAD_DIM), lambda p, b_: (b_, p, 0, 0))],
        out_shape=[jax.ShapeDtypeStruct((B, T, DR), BF16), jax.ShapeDtypeStruct((B, H, HEAD_DIM, HEAD_DIM), F32)],
        scratch_shapes=scratch,
        compiler_params=_params(("arbitrary", "arbitrary"), nbytes),
        name="rwkv_scan",
    )(r, lw, k, v, kk, b, g, lnx_g.reshape(L, 1, DR), lnx_b.reshape(L, 1, DR), r_k.reshape(L, 1, DR))


def _step_kernel(s_ref, r_ref, lw_ref, k_ref, v_ref, kk_ref, b_ref, g_ref, lng_ref, lnb_ref, rk_ref,
                 o_ref, so_ref, *, n_heads):
    E = HEAD_DIM
    eye = (lax.broadcasted_iota(jnp.int32, (E, E), 0) == lax.broadcasted_iota(jnp.int32, (E, E), 1)).astype(F32)

    def head(hh):
        sl = slice(hh * E, (hh + 1) * E)
        S = s_ref[0, 0, hh]
        r, k, v, kk, b, g = (ref[0, :, sl] for ref in (r_ref, k_ref, v_ref, kk_ref, b_ref, g_ref))
        w = jnp.exp(lw_ref[0, :, sl])
        v_col = jnp.sum(eye * v, axis=-1, keepdims=True)
        sa = jnp.sum(S * (-kk), axis=-1, keepdims=True)
        bonus = jnp.sum(r * k * rk_ref[0, :, sl], axis=-1, keepdims=True)
        yield
        S = S * w + sa * b + v_col * k
        y_col = jnp.sum(S * r, axis=-1, keepdims=True)
        yield
        y = jnp.sum(eye * y_col, axis=0, keepdims=True)
        mu = jnp.mean(y, axis=-1, keepdims=True)
        yield
        yc = y - mu
        var = jnp.mean(yc * yc, axis=-1, keepdims=True)
        yield
        yn = yc * lax.rsqrt(var + GN_EPS) * lng_ref[0, :, sl] + lnb_ref[0, :, sl]
        return S, (yn + bonus * v) * g

    results = _round_robin([head(hh) for hh in range(n_heads)])
    for hh, (S, _) in enumerate(results):
        so_ref[0, hh] = S
    o_ref[0] = jnp.concatenate([o for _, o in results], axis=-1).astype(o_ref.dtype)


def _rwkv_step(state, r, lw, k, v, kk, b, g, lnx_g, lnx_b, r_k, l):
    DB, H = state.shape[1:3]
    DR = H * HEAD_DIM
    L = lnx_g.shape[0]
    vec = pl.BlockSpec((1, 1, DR), lambda b_: (b_, 0, 0))
    par = pl.BlockSpec((1, 1, DR), lambda b_: (l, 0, 0))
    st = pl.BlockSpec((1, H, HEAD_DIM, HEAD_DIM), lambda b_: (b_, 0, 0, 0))
    return pl.pallas_call(
        functools.partial(_step_kernel, n_heads=H),
        grid=(DB,),
        in_specs=[pl.BlockSpec((1, 1, H, HEAD_DIM, HEAD_DIM), lambda b_: (l, b_, 0, 0, 0))] + [vec] * 7 + [par] * 3,
        out_specs=[vec, st],
        out_shape=[jax.ShapeDtypeStruct((DB, 1, DR), BF16), jax.ShapeDtypeStruct(state.shape[1:], F32)],
        compiler_params=_params(("arbitrary",), 4 * H * HEAD_DIM * LANES * 4),
        name="rwkv_step",
    )(state, r, lw, k, v, kk, b, g, lnx_g.reshape(L, 1, DR), lnx_b.reshape(L, 1, DR), r_k.reshape(L, 1, DR))


def _layer(P, l, xp, xs, mod, base, vf_p, vf_s, k_stack, v_stack, cache_k, cache_v, state_wkv, h_last_s):
    B, T, D = xp.shape
    DB = xs.shape[1]
    DA = P['rel_bias'].shape[1] * HEAD_DIM
    n_rkv = P['w_in'].shape[2] - 3 * DA
    w_in = P['w_in']
    rows_p, rows_s = (DB, None), (0, DB)
    h, hb = _norm_mod(xp, P['norm1_g'], l, mod, rows_p, 0, [F32, BF16])
    hs, hbs = _norm_mod(xs, P['norm1_g'], l, mod, rows_s, 0, [F32, BF16])
    q, q_s = _proj(hb, hbs, w_in, l, 0, DA, head_g=P['q_norm_g'], scale=ATT_SCALE)
    k_stack, k_s = _proj(hb, hbs, w_in, l, DA, DA, head_g=P['k_norm_g'], stack=k_stack)
    v_stack, v_s = _proj(hb, hbs, w_in, l, 2 * DA, DA, stack=v_stack)
    both = jnp.concatenate([hbs, h_last_s[None].astype(BF16)], axis=1)
    rkv0, rkv2_s = _proj(hb, both, w_in, l, 3 * DA, n_rkv)

    att = _attention_prompt(q, k_stack, v_stack, l, base)
    r, lw, kr, vr, kk, b, g = _rwkv_prep(h, jnp.zeros((B, 1, D), F32), rkv0, jnp.zeros((B, 1, n_rkv), F32), vf_p, P, l)
    if vf_p is None:
        vf_p = vr
    rw, state_p = _rwkv_scan(r, lw, kr, vr, kk, b, g, P['lnx_g'], P['lnx_b'], P['r_k'], l)

    per_seq = lambda t: t.reshape(DB, 1, t.shape[-1])
    att_s = _attention_sample(per_seq(q_s), per_seq(k_s), per_seq(v_s), cache_k, cache_v, l, P['rel_bias'])
    r, lw, kr, vr, kk, b, g = _rwkv_prep(hs, h_last_s[None], rkv2_s[:, :DB], rkv2_s[:, DB:], vf_s, P, l)
    if vf_s is None:
        vf_s = vr
    rw_s, state_s = _rwkv_step(state_wkv, *(per_seq(t) for t in (r, lw, kr, vr, kk, b, g)),
                                P['lnx_g'], P['lnx_b'], P['r_k'], l)

    xp, xs = _proj_resid([att, rw], [att_s.reshape(1, DB, DA), rw_s.reshape(1, DB, DA)], P['w_out'], l,
                         xp, xs, mod, DB, 2, tm=1024, tn=1024)
    (h2,) = _norm_mod(xp, P['norm2_g'], l, mod, rows_p, 3, [BF16])
    (h2s,) = _norm_mod(xs, P['norm2_g'], l, mod, rows_s, 3, [BF16])
    act, act_s = _proj_swiglu(h2, h2s, P['w_gu'], l)
    xp, xs = _proj_resid([act], [act_s], P['w_down'], l, xp, xs, mod, DB, 5, tm=512, tn=512)
    return xp, xs, vf_p, vf_s, k_stack, v_stack, state_p, h[:, -1], k_s[0], v_s[0], state_s, hs[0]


def kernel(x_prompt, x_sample, c_prompt, c_sample, cache_k, cache_v, state_wkv, state_shift, rel_bias, ada_w, ada_b, norm1_g, norm2_g, w_in, q_norm_g, k_norm_g, mu_wag, mu_rkv, decay_w0, decay_w1, decay_w2, aaa_a0, aaa_w1, aaa_w2, gate_w1, gate_w2, vres_mu, vres_v0, vres_w1, vres_w2, k_k, k_a, r_k, lnx_g, lnx_b, w_out, w_gu, w_down):
    P = dict(rel_bias=rel_bias, norm1_g=norm1_g, norm2_g=norm2_g, w_in=w_in, q_norm_g=q_norm_g,
             k_norm_g=k_norm_g, mu_wag=mu_wag, mu_rkv=mu_rkv, decay_w0=decay_w0, decay_w1=decay_w1,
             decay_w2=decay_w2, aaa_a0=aaa_a0, aaa_w1=aaa_w1, aaa_w2=aaa_w2, gate_w1=gate_w1,
             gate_w2=gate_w2, vres_mu=vres_mu, vres_v0=vres_v0, vres_w1=vres_w1, vres_w2=vres_w2,
             k_k=k_k, k_a=k_a, r_k=r_k.reshape(r_k.shape[0], -1), lnx_g=lnx_g, lnx_b=lnx_b,
             w_out=w_out, w_gu=w_gu, w_down=w_down)
    L = ada_w.shape[0]
    B, T, D = x_prompt.shape
    DB = x_sample.shape[0]
    H_ATT = rel_bias.shape[1]
    assert x_sample.shape[1] == 1 and T <= W_MAX

    rows = -(-(B + DB) // SUBLANES) * SUBLANES
    c_all = jnp.concatenate([c_sample, c_prompt, jnp.zeros((rows - B - DB, D), F32)], axis=0)
    mod = _ada(c_all, ada_w, ada_b).reshape(L, rows, 1, 6 * D)
    base = _band_bias_rows(rel_bias)

    k_stack = jnp.zeros((L, B, T, H_ATT * HEAD_DIM), F32)
    v_stack = jnp.zeros((L, B, T, H_ATT * HEAD_DIM), F32)
    xp, xs, vf_p, vf_s = x_prompt, x_sample.reshape(1, DB, D), None, None
    ps, ph, sk, sv, ss, sh = [], [], [], [], [], []
    for l in range(L):
        (xp, xs, vf_p, vf_s, k_stack, v_stack, state_p, shift_p, k_s, v_s, state_s, shift_s) = _layer(
            P, l, xp, xs, mod, base, vf_p, vf_s, k_stack, v_stack, cache_k, cache_v, state_wkv, state_shift[l])
        ps.append(state_p)
        ph.append(shift_p)
        sk.append(k_s.reshape(DB, 1, H_ATT, HEAD_DIM))
        sv.append(v_s.reshape(DB, 1, H_ATT, HEAD_DIM))
        ss.append(state_s)
        sh.append(shift_s)

    return (xp, xs.reshape(DB, 1, D), k_stack.reshape(L, B, T, H_ATT, HEAD_DIM),
            v_stack.reshape(L, B, T, H_ATT, HEAD_DIM), jnp.stack(ps), jnp.stack(ph),
            jnp.stack(sk), jnp.stack(sv), jnp.stack(ss), jnp.stack(sh))
```

```python
import functools
import math

import jax
import jax.numpy as jnp
from jax import lax
from jax.experimental import pallas as pl
from jax.experimental.pallas import tpu as pltpu

F32 = jnp.float32
BF16 = jnp.bfloat16

HEAD_DIM = 64
BRANCHES = ((128, 1), (512, 4), (2048, 16))
W_MAX = 2048
NUM_BUCKETS = 32
MAX_DISTANCE = W_MAX
ATT_SCALE = HEAD_DIM ** -0.5
RMS_EPS = 1e-6
GN_EPS = 64e-5
NEG = -1e30

LANES = 128
SUBLANES = 8
Q_BLOCK = 128
SCAN_CHUNK = 64
SCAN_SEQS = 2
SCAN_UNROLL = 16
SATT_PAIRS = 4
ATT_UNROLL = 4
VMEM_CAP = 56 * 1024 * 1024

assert all(w // dil == Q_BLOCK for w, dil in BRANCHES) and 2 * HEAD_DIM == LANES


def _vmem(nbytes):
    return int(min(VMEM_CAP, nbytes * 1.3 + (6 << 20)))


def _params(sem, nbytes):
    return pltpu.CompilerParams(dimension_semantics=sem, vmem_limit_bytes=_vmem(nbytes))


def _dot(a, b):
    return jnp.dot(a.astype(BF16), b.astype(BF16), preferred_element_type=F32)


def _dot_nt(a, b):
    return lax.dot_general(a.astype(BF16), b.astype(BF16), (((1,), (1,)), ((), ())), preferred_element_type=F32)


def _dot_tn(a, b):
    return lax.dot_general(a.astype(BF16), b.astype(BF16), (((0,), (0,)), ((), ())), preferred_element_type=F32)


def _sigmoid(x):
    return 1.0 / (1.0 + jnp.exp(-x))


def _head_sums(x):
    r = lax.broadcasted_iota(jnp.int32, (LANES, LANES), 0) // HEAD_DIM
    c = lax.broadcasted_iota(jnp.int32, (LANES, LANES), 1) // HEAD_DIM
    bd = (r == c).astype(BF16)
    cols = [_dot(x[:, j * LANES:(j + 1) * LANES], bd) for j in range(x.shape[1] // LANES)]
    return cols[0] if len(cols) == 1 else jnp.concatenate(cols, axis=-1)


def _ada_kernel(c_ref, w_ref, b_ref, o_ref):
    c = c_ref[...]
    s = c * _sigmoid(c)
    o_ref[0] = _dot(s, w_ref[0]) + b_ref[0]


def _ada(c_all, ada_w, ada_b, tn=1024):
    L, D, N = ada_w.shape
    R = c_all.shape[0]
    return pl.pallas_call(
        _ada_kernel,
        grid=(L, N // tn),
        in_specs=[pl.BlockSpec((R, D), lambda l, j: (0, 0)),
                  pl.BlockSpec((1, D, tn), lambda l, j: (l, 0, j)),
                  pl.BlockSpec((1, 1, tn), lambda l, j: (l, 0, j))],
        out_specs=pl.BlockSpec((1, R, tn), lambda l, j: (l, 0, j)),
        out_shape=jax.ShapeDtypeStruct((L, R, N), F32),
        compiler_params=_params(("arbitrary", "arbitrary"), 2 * D * tn * 4 + D * tn * 2),
        name="ada_mod",
    )(c_all, ada_w, ada_b.reshape(L, 1, N))


def _norm_kernel(x_ref, g_ref, sh_ref, sc_ref, *out_refs):
    x = x_ref[0]
    y = x * lax.rsqrt(jnp.mean(x * x, axis=-1, keepdims=True) + RMS_EPS) * g_ref[0]
    h = y * (1.0 + sc_ref[0, :, 0, :]) + sh_ref[0, :, 0, :]
    for o in out_refs:
        o[0] = h.astype(o.dtype)


def _mod_spec(l, rows, which, D):
    r0, n = rows
    if n is None:
        return pl.BlockSpec((1, 1, 1, D), lambda b, i: (l, r0 + b, 0, which))
    return pl.BlockSpec((1, n, 1, D), lambda b, i: (l, 0, 0, which))


def _norm_mod(x, g_stack, l, mod, rows, which, out_dtypes, tm=1024):
    B, T, D = x.shape
    tm = min(tm, T)
    L = g_stack.shape[0]
    assert rows[1] is None or (B == 1 and rows[1] == T == tm)
    outs = pl.pallas_call(
        _norm_kernel,
        grid=(B, T // tm),
        in_specs=[pl.BlockSpec((1, tm, D), lambda b, i: (b, i, 0)),
                  pl.BlockSpec((1, 1, D), lambda b, i: (l, 0, 0)),
                  _mod_spec(l, rows, which, D), _mod_spec(l, rows, which + 1, D)],
        out_specs=[pl.BlockSpec((1, tm, D), lambda b, i: (b, i, 0)) for _ in out_dtypes],
        out_shape=[jax.ShapeDtypeStruct((B, T, D), dt) for dt in out_dtypes],
        compiler_params=_params(("arbitrary", "arbitrary"), 2 * tm * D * 4 * (2 + len(out_dtypes))),
        name="norm_mod",
    )(x, g_stack.reshape(L, 1, D), mod, mod)
    return outs


def _first_inner_step():
    return (pl.program_id(1) == 0) & (pl.program_id(2) == 0)


def _head_rmsnorm(acc, g):
    return acc * lax.rsqrt(_head_sums(acc * acc) * (1.0 / HEAD_DIM) + RMS_EPS) * g


def _rider_specs(R, k_sizes, tn):
    ins = [pl.BlockSpec((1, R, ks), lambda n, b, i: (0, 0, 0)) for ks in k_sizes]
    return ins, pl.BlockSpec((1, R, tn), lambda n, b, i: (0, 0, n))


def _mm_kernel(*refs, headnorm, scale, aliased):
    it = iter(refs)
    x_ref, w_ref = next(it), next(it)
    g_ref = next(it) if headnorm else None
    if aliased:
        next(it)
    xr_ref, o_ref, or_ref, wb_ref = next(it), next(it), next(it), next(it)

    def result(x):
        acc = jnp.dot(x, wb_ref[...], preferred_element_type=F32)
        if headnorm:
            acc = _head_rmsnorm(acc, g_ref[0])
            if scale != 1.0:
                acc = acc * scale
        return acc

    @pl.when(_first_inner_step())
    def _():
        wb_ref[...] = w_ref[0].astype(BF16)
        or_ref[0] = result(xr_ref[0])

    if len(o_ref.shape) == 4:
        o_ref[0, 0] = result(x_ref[0])
    else:
        o_ref[0] = result(x_ref[0])


def _proj(x, rider, w_stack, l, col0, ncols, *, head_g=None, scale=1.0, stack=None, tm=1024, tn=1024):
    B, T, K = x.shape
    R = rider.shape[1]
    tm = min(tm, T)
    tn = min(tn, ncols)
    assert T % tm == 0 and ncols % tn == 0 and col0 % tn == 0
    cb = col0 // tn
    in_specs = [pl.BlockSpec((1, tm, K), lambda n, b, i: (b, i, 0)),
                pl.BlockSpec((1, K, tn), lambda n, b, i: (l, 0, cb + n))]
    args = [x, w_stack]
    if head_g is not None:
        in_specs.append(pl.BlockSpec((1, 1, tn), lambda n, b, i: (l, 0, 0)))
        args.append(jnp.tile(head_g, (1, tn // HEAD_DIM)).reshape(head_g.shape[0], 1, tn))
    aliases = {}
    if stack is None:
        out_spec = pl.BlockSpec((1, tm, tn), lambda n, b, i: (b, i, n))
        out_shape = jax.ShapeDtypeStruct((B, T, ncols), F32)
    else:
        out_spec = pl.BlockSpec((1, 1, tm, tn), lambda n, b, i: (l, b, i, n))
        out_shape = jax.ShapeDtypeStruct(stack.shape, F32)
        in_specs.append(pl.BlockSpec(memory_space=pl.ANY))
        args.append(stack)
        aliases = {len(args) - 1: 0}
    r_ins, r_out = _rider_specs(R, (K,), tn)
    nbytes = 2 * (tm * K * 2 + K * tn * 4 + tm * tn * 4) + K * tn * 2 + 2 * tm * tn * 4
    return pl.pallas_call(
        functools.partial(_mm_kernel, headnorm=head_g is not None, scale=scale, aliased=bool(aliases)),
        grid=(ncols // tn, B, T // tm),
        in_specs=in_specs + r_ins,
        out_specs=[out_spec, r_out],
        out_shape=[out_shape, jax.ShapeDtypeStruct((1, R, ncols), F32)],
        scratch_shapes=[pltpu.VMEM((K, tn), BF16)],
        input_output_aliases=aliases,
        compiler_params=_params(("arbitrary",) * 3, nbytes),
        name="proj",
    )(*args, rider)


def _swiglu_kernel(x_ref, wg_ref, wu_ref, xr_ref, o_ref, or_ref, wgb_ref, wub_ref):
    def result(x):
        gate = jnp.dot(x, wgb_ref[...], preferred_element_type=F32)
        up = jnp.dot(x, wub_ref[...], preferred_element_type=F32)
        return (gate * _sigmoid(gate) * up).astype(o_ref.dtype)

    @pl.when(_first_inner_step())
    def _():
        wgb_ref[...] = wg_ref[0].astype(BF16)
        wub_ref[...] = wu_ref[0].astype(BF16)
        or_ref[0] = result(xr_ref[0])

    o_ref[0] = result(x_ref[0])


def _proj_swiglu(x, rider, w_gu, l, *, tm=1024, tn=512):
    B, T, K = x.shape
    R = rider.shape[1]
    F = w_gu.shape[2] // 2
    tm = min(tm, T)
    assert T % tm == 0 and F % tn == 0
    nb = F // tn
    r_ins, r_out = _rider_specs(R, (K,), tn)
    nbytes = 2 * (tm * K * 2 + 2 * K * tn * 4 + tm * tn * 2) + 2 * K * tn * 2 + 3 * tm * tn * 4
    return pl.pallas_call(
        _swiglu_kernel,
        grid=(nb, B, T // tm),
        in_specs=[pl.BlockSpec((1, tm, K), lambda n, b, i: (b, i, 0)),
                  pl.BlockSpec((1, K, tn), lambda n, b, i: (l, 0, n)),
                  pl.BlockSpec((1, K, tn), lambda n, b, i: (l, 0, nb + n))] + r_ins,
        out_specs=[pl.BlockSpec((1, tm, tn), lambda n, b, i: (b, i, n)), r_out],
        out_shape=[jax.ShapeDtypeStruct((B, T, F), BF16), jax.ShapeDtypeStruct((1, R, F), BF16)],
        scratch_shapes=[pltpu.VMEM((K, tn), BF16), pltpu.VMEM((K, tn), BF16)],
        compiler_params=_params(("arbitrary",) * 3, nbytes),
        name="proj_swiglu",
    )(x, w_gu, w_gu, rider)


def _resid_kernel(*refs, k_sizes):
    n_x = len(k_sizes)
    x_refs, refs = refs[:n_x], refs[n_x:]
    w_ref, res_ref, gate_ref = refs[:3]
    xr_refs, refs = refs[3:3 + n_x], refs[3 + n_x:]
    resr_ref, gater_ref, o_ref, or_ref, wb_ref = refs

    def result(xs, res, gate):
        acc = None
        k0 = 0
        for x_ref, ks in zip(xs, k_sizes):
            part = jnp.dot(x_ref[0], wb_ref[k0:k0 + ks, :], preferred_element_type=F32)
            acc = part if acc is None else acc + part
            k0 += ks
        return res[0] + gate[0, :, 0, :] * acc

    @pl.when(_first_inner_step())
    def _():
        wb_ref[...] = w_ref[0].astype(BF16)
        or_ref[0] = result(xr_refs, resr_ref, gater_ref)

    o_ref[0] = result(x_refs, res_ref, gate_ref)


def _proj_resid(xs, rider_xs, w_stack, l, resid, rider_resid, mod, row0, which, *, tm=512, tn=512):
    B, T, N = resid.shape
    R = rider_resid.shape[1]
    k_sizes = tuple(x.shape[2] for x in xs)
    K = sum(k_sizes)
    tm = min(tm, T)
    assert T % tm == 0 and N % tn == 0
    c0 = which * (N // tn)
    in_specs = [pl.BlockSpec((1, tm, ks), lambda n, b, i: (b, i, 0)) for ks in k_sizes]
    in_specs += [pl.BlockSpec((1, K, tn), lambda n, b, i: (l, 0, n)),
                 pl.BlockSpec((1, tm, tn), lambda n, b, i: (b, i, n)),
                 pl.BlockSpec((1, 1, 1, tn), lambda n, b, i: (l, row0 + b, 0, c0 + n))]
    r_ins, r_out = _rider_specs(R, k_sizes, tn)
    r_gate = pl.BlockSpec((1, R, 1, tn), lambda n, b, i: (l, 0, 0, c0 + n))
    nbytes = 2 * (tm * K * 2 + K * tn * 4 + 2 * tm * tn * 4) + K * tn * 2 + 2 * tm * tn * 4
    return pl.pallas_call(
        functools.partial(_resid_kernel, k_sizes=k_sizes),
        grid=(N // tn, B, T // tm),
        in_specs=in_specs + r_ins + [r_out, r_gate],
        out_specs=[pl.BlockSpec((1, tm, tn), lambda n, b, i: (b, i, n)), r_out],
        out_shape=[jax.ShapeDtypeStruct((B, T, N), F32), jax.ShapeDtypeStruct((1, R, N), F32)],
        scratch_shapes=[pltpu.VMEM((K, tn), BF16)],
        compiler_params=_params(("arbitrary",) * 3, nbytes),
        name="proj_resid",
    )(*xs, w_stack, resid, mod, *rider_xs, rider_resid, mod)


def _resid_norm_kernel(*refs, k_sizes):
    n_x = len(k_sizes)
    x_refs, refs = refs[:n_x], refs[n_x:]
    w_ref, res_ref, g_ref, gate_ref, sh_ref, sc_ref = refs[:6]
    xr_refs, refs = refs[6:6 + n_x], refs[6 + n_x:]
    resr_ref, gater_ref, shr_ref, scr_ref, o_ref, h_ref, or_ref, hr_ref, wb_ref = refs

    def result(xs, res, gate, sh, sc):
        acc = None
        k0 = 0
        for x_ref, ks in zip(xs, k_sizes):
            part = jnp.dot(x_ref[0], wb_ref[k0:k0 + ks, :], preferred_element_type=F32)
            acc = part if acc is None else acc + part
            k0 += ks
        x = res[0] + gate[0, :, 0, :] * acc
        y = x * lax.rsqrt(jnp.mean(x * x, axis=-1, keepdims=True) + RMS_EPS) * g_ref[0]
        return x, (y * (1.0 + sc[0, :, 0, :]) + sh[0, :, 0, :]).astype(BF16)

    @pl.when((pl.program_id(0) == 0) & (pl.program_id(1) == 0))
    def _():
        wb_ref[...] = w_ref[0].astype(BF16)
        or_ref[0], hr_ref[0] = result(xr_refs, resr_ref, gater_ref, shr_ref, scr_ref)

    o_ref[0], h_ref[0] = result(x_refs, res_ref, gate_ref, sh_ref, sc_ref)


def _proj_resid_norm(xs, rider_xs, w_stack, l, resid, rider_resid, mod, row0, which, g_stack, *, tm=256):
    B, T, N = resid.shape
    R = rider_resid.shape[1]
    L = g_stack.shape[0]
    k_sizes = tuple(x.shape[2] for x in xs)
    K = sum(k_sizes)
    tm = min(tm, T)
    assert T % tm == 0
    row = lambda w: pl.BlockSpec((1, 1, 1, N), lambda b, i: (l, row0 + b, 0, w))
    rrow = lambda w: pl.BlockSpec((1, R, 1, N), lambda b, i: (l, 0, 0, w))
    in_specs = [pl.BlockSpec((1, tm, ks), lambda b, i: (b, i, 0)) for ks in k_sizes]
    in_specs += [pl.BlockSpec((1, K, N), lambda b, i: (l, 0, 0), pipeline_mode=pl.Buffered(1)),
                 pl.BlockSpec((1, tm, N), lambda b, i: (b, i, 0)),
                 pl.BlockSpec((1, 1, N), lambda b, i: (l, 0, 0)),
                 row(which), row(which + 1), row(which + 2)]
    in_specs += [pl.BlockSpec((1, R, ks), lambda b, i: (0, 0, 0)) for ks in k_sizes]
    rfull = pl.BlockSpec((1, R, N), lambda b, i: (0, 0, 0))
    in_specs += [rfull, rrow(which), rrow(which + 1), rrow(which + 2)]
    tile = pl.BlockSpec((1, tm, N), lambda b, i: (b, i, 0))
    nbytes = K * N * 4 + K * N * 2 + 2 * (tm * K * 2 + 2 * tm * N * 4 + tm * N * 2) + 2 * tm * N * 4
    return pl.pallas_call(
        functools.partial(_resid_norm_kernel, k_sizes=k_sizes),
        grid=(B, T // tm),
        in_specs=in_specs,
        out_specs=[tile, tile, rfull, rfull],
        out_shape=[jax.ShapeDtypeStruct((B, T, N), F32), jax.ShapeDtypeStruct((B, T, N), BF16),
                   jax.ShapeDtypeStruct((1, R, N), F32), jax.ShapeDtypeStruct((1, R, N), BF16)],
        scratch_shapes=[pltpu.VMEM((K, N), BF16)],
        compiler_params=_params(("arbitrary", "arbitrary"), nbytes),
        name="proj_resid_norm",
    )(*xs, w_stack, resid, g_stack.reshape(L, 1, N), mod, mod, mod, *rider_xs, rider_resid, mod, mod, mod)


def _rel_bucket(dist):
    max_exact = NUM_BUCKETS // 2
    d = jnp.maximum(dist, 0)
    df = jnp.maximum(d, 1).astype(F32)
    large = max_exact + (jnp.log(df / max_exact) / math.log(MAX_DISTANCE / max_exact)
                         * (NUM_BUCKETS - max_exact)).astype(jnp.int32)
    return jnp.where(d < max_exact, d, jnp.minimum(large, NUM_BUCKETS - 1))


def _band_bias_rows(rel_bias):
    dsub = Q_BLOCK - jnp.arange(2 * Q_BLOCK)
    rows = []
    for (w, dil) in BRANCHES:
        valid = (dsub >= 0) & (dsub <= w // dil)
        bias = rel_bias[_rel_bucket(jnp.maximum(dsub, 0) * dil)].astype(F32).T
        rows.append(jnp.where(valid[None], bias, NEG))
    return jnp.stack(rows, axis=1)


def _attn_kernel(q_ref, k_ref, v_ref, base_ref, o_ref, qs, ks, vs, os_, ms_, ds_, stage):
    T = q_ref.shape[2]
    QB = Q_BLOCK
    low = lax.broadcasted_iota(jnp.int32, (1, LANES), 1) < HEAD_DIM
    own = (low, jnp.logical_not(low))
    prev_cols = lax.broadcasted_iota(jnp.int32, (1, 2 * QB), 1) < QB

    assert len(BRANCHES) == 3 and BRANCHES[0][1] == 1
    prev_dil = 1
    for bi, (_, dil) in enumerate(BRANCHES):
        L, Lp, ratio = T // dil, T // prev_dil, dil // prev_dil
        ks[bi, 0:QB, :] = jnp.zeros((QB, LANES), BF16)
        vs[bi, 0:QB, :] = jnp.zeros((QB, LANES), BF16)
        for r in range(dil):
            rows = pl.ds((r % prev_dil) * Lp + r // prev_dil, L, stride=ratio)
            if bi < 2:
                q, k, v = (ref[0, 0, rows, :] for ref in (q_ref, k_ref, v_ref))
            else:
                q, k, v = (stage[a, rows, :] for a in range(3))
            if bi == 1:
                for a, x in enumerate((q, k, v)):
                    stage[a, r * L:(r + 1) * L, :] = x
            for hh in range(2):
                qs[hh, bi, r * L:(r + 1) * L, :] = jnp.where(own[hh], q, 0.0).astype(BF16)
            ks[bi, QB + r * L:QB + (r + 1) * L, :] = k.astype(BF16)
            vs[bi, QB + r * L:QB + (r + 1) * L, :] = v.astype(BF16)
        prev_dil = dil

    for bi, (_, dil) in enumerate(BRANCHES):
        nb = T // dil // QB
        tiles = [pltpu.roll(jnp.broadcast_to(base_ref[hh, bi:bi + 1, :], (QB, 2 * QB)), 0, 1, stride=1, stride_axis=0)
                 for hh in range(2)]
        first_only = nb == 1
        if first_only:
            tiles = [t[:, QB:] for t in tiles]
        k0, kn = (QB, QB) if first_only else (0, 2 * QB)

        def one_block(g, q0, q1, kw, vw, nb=nb, tiles=tiles, first_only=first_only):
            ss = [lax.dot_general(q, kw, (((1,), (1,)), ((), ())), preferred_element_type=F32) + tiles[hh]
                  for hh, q in enumerate((q0, q1))]
            if not first_only:
                no_prev = jnp.logical_and(g % nb == 0, prev_cols)
                ss = [jnp.where(no_prev, NEG, s) for s in ss]
            yield
            ms = [jnp.max(s, axis=-1, keepdims=True) for s in ss]
            ps = [jnp.exp(s - m) for s, m in zip(ss, ms)]
            dens = [jnp.sum(p, axis=-1, keepdims=True) for p in ps]
            os2 = [jnp.dot(p.astype(BF16), vw, preferred_element_type=F32) for p in ps]
            yield
            pair = lambda x0, x1: jnp.where(low, jnp.broadcast_to(x0, (QB, LANES)), jnp.broadcast_to(x1, (QB, LANES)))
            return pair(*os2), pair(*ms), pair(*dens)

        def blocks(i, _, bi=bi, dil=dil, nb=nb, k0=k0, kn=kn):
            gs = [i * ATT_UNROLL + u for u in range(ATT_UNROLL)]
            ats = [pl.multiple_of(g * QB, QB) for g in gs]
            loaded = [(qs[0, bi, pl.ds(at, QB), :], qs[1, bi, pl.ds(at, QB), :],
                       ks[bi, pl.ds(at + k0, kn), :], vs[bi, pl.ds(at + k0, kn), :]) for at in ats]
            results = _round_robin([one_block(g, *ld) for g, ld in zip(gs, loaded)])
            for g, (acc, m, den) in zip(gs, results):
                r = g // nb
                tok = pl.ds((g - r * nb) * (QB * dil) + r, QB, stride=dil)
                os_[bi, tok, :] = acc
                ms_[bi, tok, :] = m
                ds_[bi, tok, :] = den
            return 0

        lax.fori_loop(0, T // QB // ATT_UNROLL, blocks, 0)

    def merge(i, _):
        rows = pl.ds(pl.multiple_of(i * QB, QB), QB)
        maxes = [ms_[bi, rows, :] for bi in range(len(BRANCHES))]
        m = functools.reduce(jnp.maximum, maxes)
        ws = [jnp.exp(x - m) for x in maxes]
        num = sum(w * os_[bi, rows, :] for bi, w in enumerate(ws))
        den = sum(w * ds_[bi, rows, :] for bi, w in enumerate(ws))
        o_ref[0, rows, :] = (num / den).astype(o_ref.dtype)
        return 0

    lax.fori_loop(0, T // QB, merge, 0)


def _attention_prompt(q, k_stack, v_stack, l, base):
    B, T, DA = q.shape
    nbr = len(BRANCHES)
    assert T % (Q_BLOCK * max(d for _, d in BRANCHES)) == 0
    qkv = pl.BlockSpec((1, 1, T, LANES), lambda p, b: (l, b, 0, p))
    scratch = [pltpu.VMEM((2, nbr, T, LANES), BF16), pltpu.VMEM((nbr, T + Q_BLOCK, LANES), BF16),
               pltpu.VMEM((nbr, T + Q_BLOCK, LANES), BF16), pltpu.VMEM((nbr, T, LANES), F32),
               pltpu.VMEM((nbr, T, LANES), F32), pltpu.VMEM((nbr, T, LANES), F32), pltpu.VMEM((3, T, LANES), F32)]
    nbytes = (2 * 3 * T * LANES * 4 + 4 * nbr * (T + Q_BLOCK) * LANES * 2 + 3 * nbr * T * LANES * 4
              + 3 * T * LANES * 4 + 2 * T * LANES * 2)
    return pl.pallas_call(
        _attn_kernel,
        grid=(DA // LANES, B),
        in_specs=[pl.BlockSpec((1, 1, T, LANES), lambda p, b: (0, b, 0, p)), qkv, qkv,
                  pl.BlockSpec((2, nbr, 2 * Q_BLOCK), lambda p, b: (p, 0, 0))],
        out_specs=pl.BlockSpec((1, T, LANES), lambda p, b: (b, 0, p)),
        out_shape=jax.ShapeDtypeStruct((B, T, DA), BF16),
        scratch_shapes=scratch,
        compiler_params=_params(("arbitrary", "arbitrary"), nbytes),
        name="attn_prompt",
    )(q[None], k_stack, v_stack, base)


def _distance_logits(rel_bias, nd):
    d = jnp.arange(nd)
    mult = sum(((d % dil == 0) & (d // dil <= w // dil)).astype(F32) for (w, dil) in BRANCHES)
    bias = rel_bias[_rel_bucket(d)].astype(F32).T
    return jnp.where(mult > 0, bias + jnp.log(jnp.maximum(mult, 1.0)), NEG)


def _sattn_kernel(q_ref, kn_ref, vn_ref, kt_ref, vt_ref, tab_ref, o_ref):
    row = lax.broadcasted_iota(jnp.int32, (SUBLANES, LANES), 0)
    low = lax.broadcasted_iota(jnp.int32, (SUBLANES, LANES), 1) < HEAD_DIM
    own = ((row == 0) & low) | ((row == 1) & jnp.logical_not(low))
    W = kt_ref.shape[3]

    def pair(pp):
        cols = slice(pp * LANES, (pp + 1) * LANES)
        q = jnp.where(own, q_ref[0, :, cols], 0.0).astype(BF16)
        kn = kn_ref[0, :, cols].astype(BF16).astype(F32)
        vn = vn_ref[0, :, cols].astype(BF16).astype(F32)
        tab = tab_ref[pp]
        s_self = jnp.sum(q.astype(F32) * kn, axis=-1, keepdims=True) + tab[:, W:W + 1]
        s = _dot(q, kt_ref[0, 0, cols, :]) + tab[:, :W]
        yield
        m = jnp.maximum(s_self, jnp.max(s, axis=-1, keepdims=True))
        p_self = jnp.exp(s_self - m)
        p = jnp.exp(s - m)
        den = p_self + jnp.sum(p, axis=-1, keepdims=True)
        o = (p_self * vn + _dot_nt(p, vt_ref[0, 0, cols, :])) / den
        yield
        return jnp.where(low[0:1], o[0:1], o[1:2])

    outs = _round_robin([pair(pp) for pp in range(q_ref.shape[2] // LANES)])
    o_ref[0] = jnp.concatenate(outs, axis=-1).astype(o_ref.dtype)


def _attention_sample(q, k_new, v_new, cache_k, cache_v, l, rel_bias):
    L, DB, W, H, E = cache_k.shape
    DA = H * E
    assert W >= max(w for w, _ in BRANCHES)
    tab = _distance_logits(rel_bias, W + 1)[:, ::-1].reshape(H // 2, 2, W + 1)
    tab = jnp.concatenate([tab, jnp.zeros((H // 2, SUBLANES - 2, W + 1), F32)], axis=1)
    kt = jnp.transpose(cache_k, (0, 1, 3, 4, 2)).reshape(L, DB, DA, W)
    vt = jnp.transpose(cache_v, (0, 1, 3, 4, 2)).reshape(L, DB, DA, W)
    npairs = math.gcd(SATT_PAIRS, H // 2)
    cw = npairs * LANES
    vec = pl.BlockSpec((1, 1, cw), lambda p, b: (b, 0, p))
    buf = pl.BlockSpec((1, 1, cw, W), lambda p, b: (l, b, p, 0))
    return pl.pallas_call(
        _sattn_kernel,
        grid=(DA // cw, DB),
        in_specs=[vec, vec, vec, buf, buf, pl.BlockSpec((npairs, SUBLANES, W + 1), lambda p, b: (p, 0, 0))],
        out_specs=vec,
        out_shape=jax.ShapeDtypeStruct((DB, 1, DA), BF16),
        compiler_params=_params(("arbitrary", "arbitrary"), 2 * 2 * cw * W * 4 + 4 * npairs * SUBLANES * W * 4),
        name="attn_sample",
    )(q, k_new, v_new, kt, vt, tab)


def _prep_kernel(*refs, shift_rows, has_vres):
    it = iter(refs)
    h_ref = next(it)
    if shift_rows:
        hp8_ref, hlast_ref = next(it), next(it)
    else:
        hprev_ref = next(it)
    rkv_ref = next(it)
    if shift_rows:
        rp8_ref, rlast_ref = next(it), next(it)
    else:
        rprev_ref = next(it)
    if has_vres:
        vfirst_ref = next(it)
    mu_ref, murkv_ref, w0_ref, dw1_ref, dw2_ref, a0_ref, aw1_ref, aw2_ref, gw1_ref, gw2_ref = (next(it) for _ in range(10))
    if has_vres:
        vmu_ref, v0_ref, vw1_ref, vw2_ref = (next(it) for _ in range(4))
    kk_ref, ka_ref = next(it), next(it)
    r_out, lw_out, k_out, v_out, kk_out, b_out, g_out = (next(it) for _ in range(7))

    h = h_ref[0]
    rkv0 = rkv_ref[0]
    tm = h.shape[0]
    if shift_rows:
        first = pl.program_id(1) == 0
        row0 = lax.broadcasted_iota(jnp.int32, (tm, 1), 0) == 0
        h_edge = jnp.where(first, hlast_ref[0], hp8_ref[0, SUBLANES - 1:SUBLANES, :])
        r_edge = jnp.where(first, rlast_ref[0], rp8_ref[0, SUBLANES - 1:SUBLANES, :])
        hprev = jnp.where(row0, h_edge, pltpu.roll(h, 1, 0))
        rprev = jnp.where(row0, r_edge, pltpu.roll(rkv0, 1, 0))
    else:
        hprev = hprev_ref[0]
        rprev = rprev_ref[0]

    hb = h.astype(BF16)
    dhb = (hprev - h).astype(BF16)
    mu = mu_ref[0].astype(BF16)
    xw = hb + dhb * mu[0:1]
    xa = hb + dhb * mu[1:2]
    xg = hb + dhb * mu[2:3]

    z = w0_ref[0] + _dot(jnp.tanh(_dot(xw, dw1_ref[0])), dw2_ref[0])
    lw_out[0] = -math.exp(-0.5) * _sigmoid(z)

    a = _sigmoid(a0_ref[0] + _dot(_dot(xa, aw1_ref[0]), aw2_ref[0]))
    g_out[0] = _dot(_sigmoid(_dot(xg, gw1_ref[0])), gw2_ref[0])

    murkv = murkv_ref[0]
    DR = kk_ref.shape[-1]
    r0, k0, v0 = (rkv0[:, j * DR:(j + 1) * DR] for j in range(3))
    rp, kp, vp = (rprev[:, j * DR:(j + 1) * DR] for j in range(3))
    r_out[0] = r0 + (rp - r0) * murkv[0:1]
    kr = k0 + (kp - k0) * murkv[1:2]
    vr = v0 + (vp - v0) * murkv[2:3]
    if has_vres:
        xv = hb + dhb * vmu_ref[0].astype(BF16)
        vgate = _sigmoid(v0_ref[0] + _dot(_dot(xv, vw1_ref[0]), vw2_ref[0]))
        vr = vr + (vfirst_ref[0] - vr) * vgate
    v_out[0] = vr
    kk = kr * kk_ref[0]
    kkn = kk * lax.rsqrt(jnp.maximum(_head_sums(kk * kk), 1e-24))
    kk_out[0] = kkn
    b_out[0] = kkn * a
    k_out[0] = kr * (1.0 + (a - 1.0) * ka_ref[0])


def _rwkv_prep(h, h_prev, rkv0, rkv_prev, v_first, P, l, *, tm=256):
    B, T, D = h.shape
    DR = rkv0.shape[2] // 3
    tm = min(tm, T)
    shift_rows = h_prev.shape[1] == 1 and T > 1
    has_vres = v_first is not None

    def tile(C):
        return pl.BlockSpec((1, tm, C), lambda b, i: (b, i, 0))

    def prev8(C):
        return pl.BlockSpec((1, SUBLANES, C), lambda b, i: (b, jnp.maximum(i * (tm // SUBLANES) - 1, 0), 0))

    def seq_row(C):
        return pl.BlockSpec((1, 1, C), lambda b, i: (b, 0, 0))

    def layer(shape, ll=l):
        return pl.BlockSpec((1,) + shape, lambda b, i: (ll,) + (0,) * len(shape))

    args, specs = [h], [tile(D)]
    if shift_rows:
        args += [h, h_prev]
        specs += [prev8(D), seq_row(D)]
    else:
        args += [jnp.broadcast_to(h_prev, h.shape)]
        specs += [tile(D)]
    args.append(rkv0)
    specs.append(tile(3 * DR))
    if shift_rows:
        args += [rkv0, rkv_prev]
        specs += [prev8(3 * DR), seq_row(3 * DR)]
    else:
        args += [jnp.broadcast_to(rkv_prev, rkv0.shape)]
        specs += [tile(3 * DR)]
    if has_vres:
        args.append(v_first)
        specs.append(tile(DR))
    r1 = lambda a: a.reshape(a.shape[0], 1, a.shape[-1])
    for name in ('mu_wag', 'mu_rkv'):
        args.append(P[name]); specs.append(layer(P[name].shape[1:]))
    args.append(r1(P['decay_w0'])); specs.append(layer((1, DR)))
    for name in ('decay_w1', 'decay_w2'):
        args.append(P[name]); specs.append(layer(P[name].shape[1:]))
    args.append(r1(P['aaa_a0'])); specs.append(layer((1, DR)))
    for name in ('aaa_w1', 'aaa_w2', 'gate_w1', 'gate_w2'):
        args.append(P[name]); specs.append(layer(P[name].shape[1:]))
    if has_vres:
        args.append(r1(P['vres_mu'])); specs.append(layer((1, D), l - 1))
        args.append(r1(P['vres_v0'])); specs.append(layer((1, DR), l - 1))
        for name in ('vres_w1', 'vres_w2'):
            args.append(P[name]); specs.append(layer(P[name].shape[1:], l - 1))
    args.append(r1(P['k_k'])); specs.append(layer((1, DR)))
    args.append(r1(P['k_a'])); specs.append(layer((1, DR)))
    nbytes = 2 * tm * 4 * (2 * D + 7 * DR + 7 * DR + DR) + 8 * tm * D * 4 + 4 * D * 512 * 4
    return pl.pallas_call(
        functools.partial(_prep_kernel, shift_rows=shift_rows, has_vres=has_vres),
        grid=(B, T // tm),
        in_specs=specs,
        out_specs=[tile(DR) for _ in range(7)],
        out_shape=[jax.ShapeDtypeStruct((B, T, DR), F32) for _ in range(7)],
        compiler_params=_params(("arbitrary", "arbitrary"), nbytes),
        name="rwkv_prep",
    )(*args)


def _prefix_sum_rows(x):
    n = x.shape[0]
    row = lax.broadcasted_iota(jnp.int32, (n, 1), 0)
    s = 1
    while s < n:
        x = x + jnp.where(row >= s, pltpu.roll(x, s, 0), 0.0)
        s *= 2
    return x


def _unit_lower_inverse(a_strict, blk):
    n = a_strict.shape[0]
    ti = lax.broadcasted_iota(jnp.int32, (n, n), 0)
    si = lax.broadcasted_iota(jnp.int32, (n, n), 1)

    def lower_left(s):
        return ((ti // (2 * s)) == (si // (2 * s))) & ((ti % (2 * s)) >= s) & ((si % (2 * s)) < s)

    d = (ti == si).astype(F32) + jnp.where(lower_left(1), a_strict, 0.0)
    s = 2
    while s < blk:
        t = _dot(d, jnp.where(lower_left(s), a_strict, 0.0))
        yield
        d = d + _dot(t, d)
        yield
        s *= 2
    return d


def _round_robin(generators):
    results = [None] * len(generators)
    live = list(range(len(generators)))
    while live:
        for u in list(live):
            try:
                next(generators[u])
            except StopIteration as done:
                results[u] = done.value
                live.remove(u)
    return results


def _pair_chunk_terms(r, lw, k, v, kk, b, low, tri):
    C = r.shape[0]
    cum = _prefix_sum_rows(lw)
    g_in = jnp.exp(cum)
    g_inv = jnp.exp(-cum)
    g_end = g_in[C - 1:C]
    a_t = -kk * jnp.exp(cum - lw)
    r_t = r * g_in
    b_t = b * g_inv
    k_t = k * g_inv
    bg = b_t * g_end
    kg = k_t * g_end
    zc = jnp.zeros((C, LANES), F32)
    h0 = lambda x: jnp.where(low, x, 0.0)
    h1 = lambda x: jnp.where(low, 0.0, x)
    v0, v1 = h0(v), h1(v)
    ar = jnp.concatenate([a_t, r_t], axis=0)
    m0 = jnp.where(tri, _dot_nt(h0(ar), jnp.concatenate([b_t, k_t], axis=0)), 0.0)
    m1 = jnp.where(tri, _dot_nt(h1(ar), jnp.concatenate([k_t, b_t], axis=0)), 0.0)
    yield
    top0, bot0, top1, bot1 = m0[:C], m0[C:], m1[:C], m1[C:]
    stack2 = lambda x0, x1: jnp.concatenate([jnp.concatenate([x0, zc], axis=0),
                                             jnp.concatenate([zc, x1], axis=0)], axis=1)
    akv = _dot(stack2(top0, top1), jnp.concatenate([zc, v0, v1, zc], axis=0))
    a_sw = pltpu.roll(a_t, HEAD_DIM, 1)
    tinv = yield from _unit_lower_inverse(jnp.concatenate([h0(top0), h1(top1)], axis=0), C)
    x = _dot(tinv, akv + jnp.concatenate([h1(a_sw), h0(a_sw)], axis=0))
    yield
    z = jnp.concatenate([x[:C], v0, v1, x[C:]], axis=0)
    e = _dot(stack2(bot0, bot1), z)
    gh = _dot_tn(z, jnp.concatenate([h0(bg), h0(kg), h1(kg), h1(bg)], axis=0))
    yield
    y0 = jnp.where(low, e[:C], e[C:])
    r_eff = r_t + pltpu.roll(jnp.where(low, e[C:], e[:C]), HEAD_DIM, 1)
    h_mat = jnp.concatenate([h0(gh[:C]), h1(gh[C:])], axis=0)
    g_mat = jnp.concatenate([h0(gh[C:]), h1(gh[:C])], axis=0)
    return r_eff, y0, g_mat, h_mat, g_end


def _scan_kernel(r_ref, lw_ref, k_ref, v_ref, kk_ref, b_ref, g_ref, lng_ref, lnb_ref, rk_ref,
                 o_ref, s_ref, reff_s, y0_s, gm_s, hm_s, ge_s, *, chunk, unroll):
    C = chunk
    NS, T = r_ref.shape[:2]
    NC = T // C
    E = HEAD_DIM
    low = lax.broadcasted_iota(jnp.int32, (1, LANES), 1) < E
    ti = lax.broadcasted_iota(jnp.int32, (2 * C, 2 * C), 0)
    si = lax.broadcasted_iota(jnp.int32, (2 * C, 2 * C), 1)
    tri = (si % C) <= jnp.where(ti < C, ti - 1, ti - C)

    def phase1(i, _):
        sq = i // (NC // unroll)
        c0 = (i - sq * (NC // unroll)) * unroll
        cs = [c0 + u for u in range(unroll)]
        rows = [pl.ds(pl.multiple_of(c * C, C), C) for c in cs]
        loaded = [[ref[sq, rw, :] for ref in (r_ref, lw_ref, k_ref, v_ref, kk_ref, b_ref)] for rw in rows]
        terms = _round_robin([_pair_chunk_terms(*args, low, tri) for args in loaded])
        for c, rw, (r_eff, y0, g_mat, h_mat, g_end) in zip(cs, rows, terms):
            reff_s[sq, rw, :] = r_eff
            y0_s[sq, rw, :] = y0
            gm_s[sq, c] = g_mat
            hm_s[sq, c] = h_mat
            ge_s[sq, c] = jnp.broadcast_to(g_end, (SUBLANES, LANES))
        return 0

    lax.fori_loop(0, NS * (NC // unroll), phase1, 0)

    def phase2(c, states):
        rows = pl.ds(pl.multiple_of(c * C, C), C)
        ys = [_dot_nt(reff_s[sq, rows, :], S) + y0_s[sq, rows, :] for sq, S in enumerate(states)]
        new = tuple(S * ge_s[sq, c][0:1] + _dot(S, gm_s[sq, c]) + hm_s[sq, c] for sq, S in enumerate(states))
        for sq, y in enumerate(ys):
            y0_s[sq, rows, :] = y
        return new

    states = lax.fori_loop(0, NC, phase2, tuple(jnp.zeros((LANES, LANES), F32) for _ in range(NS)))
    for sq, S in enumerate(states):
        s_ref[sq, 0] = S[:E, :E]
        s_ref[sq, 1] = S[E:, E:]

    def finish(y, r, k, v, g):
        mu = _head_sums(y) * (1.0 / E)
        yield
        yc = y - mu
        var = _head_sums(yc * yc) * (1.0 / E)
        bonus = _head_sums(r * k * rk_ref[0])
        yield
        yn = yc * lax.rsqrt(var + GN_EPS) * lng_ref[0] + lnb_ref[0]
        return ((yn + bonus * v) * g).astype(o_ref.dtype)

    def phase3(i, _):
        sq = i // (NC // unroll)
        c0 = (i - sq * (NC // unroll)) * unroll
        rows = [pl.ds(pl.multiple_of((c0 + u) * C, C), C) for u in range(unroll)]
        loaded = [[y0_s[sq, rw, :]] + [ref[sq, rw, :] for ref in (r_ref, k_ref, v_ref, g_ref)] for rw in rows]
        outs = _round_robin([finish(*args) for args in loaded])
        for rw, out in zip(rows, outs):
            o_ref[sq, rw, :] = out
        return 0

    lax.fori_loop(0, NS * (NC // unroll), phase3, 0)


def _rwkv_scan(r, lw, k, v, kk, b, g, lnx_g, lnx_b, r_k, l):
    B, T, DR = r.shape
    H = DR // HEAD_DIM
    L = lnx_g.shape[0]
    C = min(SCAN_CHUNK, T)
    nc = T // C
    unroll = math.gcd(SCAN_UNROLL, nc)
    ns = math.gcd(SCAN_SEQS, B)
    assert T % C == 0
    seq = pl.BlockSpec((ns, T, LANES), lambda p, b_: (b_, 0, p))
    par = pl.BlockSpec((1, 1, LANES), lambda p, b_: (l, 0, p))
    scratch = [pltpu.VMEM((ns, T, LANES), F32), pltpu.VMEM((ns, T, LANES), F32),
               pltpu.VMEM((ns, nc, LANES, LANES), F32), pltpu.VMEM((ns, nc, LANES, LANES), F32),
               pltpu.VMEM((ns, nc, SUBLANES, LANES), F32)]
    nbytes = ns * (2 * 8 * T * LANES * 4 + 2 * T * LANES * 4 + 2 * nc * LANES * LANES * 4 + nc * SUBLANES * LANES * 4)
    return pl.pallas_call(
        functools.partial(_scan_kernel, chunk=C, unroll=unroll),
        grid=(DR // LANES, B // ns),
        in_specs=[seq] * 7 + [par] * 3,
        out_specs=[seq, pl.BlockSpec((ns, 2, HEAD_DIM, HEAD_DIM), lambda p, b_: (b_, p, 0, 0))],
        out_shape=[jax.ShapeDtypeStruct((B, T, DR), BF16), jax.ShapeDtypeStruct((B, H, HEAD_DIM, HEAD_DIM), F32)],
        scratch_shapes=scratch,
        compiler_params=_params(("arbitrary", "arbitrary"), nbytes),
        name="rwkv_scan",
    )(r, lw, k, v, kk, b, g, lnx_g.reshape(L, 1, DR), lnx_b.reshape(L, 1, DR), r_k.reshape(L, 1, DR))


def _step_kernel(s_ref, r_ref, lw_ref, k_ref, v_ref, kk_ref, b_ref, g_ref, lng_ref, lnb_ref, rk_ref,
                 o_ref, so_ref, *, n_heads):
    E = HEAD_DIM
    eye = (lax.broadcasted_iota(jnp.int32, (E, E), 0) == lax.broadcasted_iota(jnp.int32, (E, E), 1)).astype(F32)

    def head(hh):
        sl = slice(hh * E, (hh + 1) * E)
        S = s_ref[0, 0, hh]
        r, k, v, kk, b, g = (ref[0, :, sl] for ref in (r_ref, k_ref, v_ref, kk_ref, b_ref, g_ref))
        w = jnp.exp(lw_ref[0, :, sl])
        v_col = jnp.sum(eye * v, axis=-1, keepdims=True)
        sa = jnp.sum(S * (-kk), axis=-1, keepdims=True)
        bonus = jnp.sum(r * k * rk_ref[0, :, sl], axis=-1, keepdims=True)
        yield
        S = S * w + sa * b + v_col * k
        y_col = jnp.sum(S * r, axis=-1, keepdims=True)
        yield
        y = jnp.sum(eye * y_col, axis=0, keepdims=True)
        mu = jnp.mean(y, axis=-1, keepdims=True)
        yield
        yc = y - mu
        var = jnp.mean(yc * yc, axis=-1, keepdims=True)
        yield
        yn = yc * lax.rsqrt(var + GN_EPS) * lng_ref[0, :, sl] + lnb_ref[0, :, sl]
        return S, (yn + bonus * v) * g

    results = _round_robin([head(hh) for hh in range(n_heads)])
    for hh, (S, _) in enumerate(results):
        so_ref[0, hh] = S
    o_ref[0] = jnp.concatenate([o for _, o in results], axis=-1).astype(o_ref.dtype)


def _rwkv_step(state, r, lw, k, v, kk, b, g, lnx_g, lnx_b, r_k, l):
    DB, H = state.shape[1:3]
    DR = H * HEAD_DIM
    L = lnx_g.shape[0]
    vec = pl.BlockSpec((1, 1, DR), lambda b_: (b_, 0, 0))
    par = pl.BlockSpec((1, 1, DR), lambda b_: (l, 0, 0))
    st = pl.BlockSpec((1, H, HEAD_DIM, HEAD_DIM), lambda b_: (b_, 0, 0, 0))
    return pl.pallas_call(
        functools.partial(_step_kernel, n_heads=H),
        grid=(DB,),
        in_specs=[pl.BlockSpec((1, 1, H, HEAD_DIM, HEAD_DIM), lambda b_: (l, b_, 0, 0, 0))] + [vec] * 7 + [par] * 3,
        out_specs=[vec, st],
        out_shape=[jax.ShapeDtypeStruct((DB, 1, DR), BF16), jax.ShapeDtypeStruct(state.shape[1:], F32)],
        compiler_params=_params(("arbitrary",), 4 * H * HEAD_DIM * LANES * 4),
        name="rwkv_step",
    )(state, r, lw, k, v, kk, b, g, lnx_g.reshape(L, 1, DR), lnx_b.reshape(L, 1, DR), r_k.reshape(L, 1, DR))


def _layer(P, l, xp, xs, mod, base, vf_p, vf_s, k_stack, v_stack, cache_k, cache_v, state_wkv, h_last_s):
    B, T, D = xp.shape
    DB = xs.shape[1]
    DA = P['rel_bias'].shape[1] * HEAD_DIM
    n_rkv = P['w_in'].shape[2] - 3 * DA
    w_in = P['w_in']
    rows_p, rows_s = (DB, None), (0, DB)
    h, hb = _norm_mod(xp, P['norm1_g'], l, mod, rows_p, 0, [F32, BF16])
    hs, hbs = _norm_mod(xs, P['norm1_g'], l, mod, rows_s, 0, [F32, BF16])
    q, q_s = _proj(hb, hbs, w_in, l, 0, DA, head_g=P['q_norm_g'], scale=ATT_SCALE)
    k_stack, k_s = _proj(hb, hbs, w_in, l, DA, DA, head_g=P['k_norm_g'], stack=k_stack)
    v_stack, v_s = _proj(hb, hbs, w_in, l, 2 * DA, DA, stack=v_stack)
    both = jnp.concatenate([hbs, h_last_s[None].astype(BF16)], axis=1)
    rkv0, rkv2_s = _proj(hb, both, w_in, l, 3 * DA, n_rkv)

    att = _attention_prompt(q, k_stack, v_stack, l, base)
    r, lw, kr, vr, kk, b, g = _rwkv_prep(h, jnp.zeros((B, 1, D), F32), rkv0, jnp.zeros((B, 1, n_rkv), F32), vf_p, P, l)
    if vf_p is None:
        vf_p = vr
    rw, state_p = _rwkv_scan(r, lw, kr, vr, kk, b, g, P['lnx_g'], P['lnx_b'], P['r_k'], l)

    per_seq = lambda t: t.reshape(DB, 1, t.shape[-1])
    att_s = _attention_sample(per_seq(q_s), per_seq(k_s), per_seq(v_s), cache_k, cache_v, l, P['rel_bias'])
    r, lw, kr, vr, kk, b, g = _rwkv_prep(hs, h_last_s[None], rkv2_s[:, :DB], rkv2_s[:, DB:], vf_s, P, l)
    if vf_s is None:
        vf_s = vr
    rw_s, state_s = _rwkv_step(state_wkv, *(per_seq(t) for t in (r, lw, kr, vr, kk, b, g)),
                                P['lnx_g'], P['lnx_b'], P['r_k'], l)

    xp, h2, xs, h2s = _proj_resid_norm([att, rw], [att_s.reshape(1, DB, DA), rw_s.reshape(1, DB, DA)], P['w_out'], l,
                                       xp, xs, mod, DB, 2, P['norm2_g'])
    act, act_s = _proj_swiglu(h2, h2s, P['w_gu'], l)
    xp, xs = _proj_resid([act], [act_s], P['w_down'], l, xp, xs, mod, DB, 5, tm=512, tn=512)
    return xp, xs, vf_p, vf_s, k_stack, v_stack, state_p, h[:, -1], k_s[0], v_s[0], state_s, hs[0]


def kernel(x_prompt, x_sample, c_prompt, c_sample, cache_k, cache_v, state_wkv, state_shift, rel_bias, ada_w, ada_b, norm1_g, norm2_g, w_in, q_norm_g, k_norm_g, mu_wag, mu_rkv, decay_w0, decay_w1, decay_w2, aaa_a0, aaa_w1, aaa_w2, gate_w1, gate_w2, vres_mu, vres_v0, vres_w1, vres_w2, k_k, k_a, r_k, lnx_g, lnx_b, w_out, w_gu, w_down):
    P = dict(rel_bias=rel_bias, norm1_g=norm1_g, norm2_g=norm2_g, w_in=w_in, q_norm_g=q_norm_g,
             k_norm_g=k_norm_g, mu_wag=mu_wag, mu_rkv=mu_rkv, decay_w0=decay_w0, decay_w1=decay_w1,
             decay_w2=decay_w2, aaa_a0=aaa_a0, aaa_w1=aaa_w1, aaa_w2=aaa_w2, gate_w1=gate_w1,
             gate_w2=gate_w2, vres_mu=vres_mu, vres_v0=vres_v0, vres_w1=vres_w1, vres_w2=vres_w2,
             k_k=k_k, k_a=k_a, r_k=r_k.reshape(r_k.shape[0], -1), lnx_g=lnx_g, lnx_b=lnx_b,
             w_out=w_out, w_gu=w_gu, w_down=w_down)
    L = ada_w.shape[0]
    B, T, D = x_prompt.shape
    DB = x_sample.shape[0]
    H_ATT = rel_bias.shape[1]
    assert x_sample.shape[1] == 1 and T <= W_MAX

    rows = -(-(B + DB) // SUBLANES) * SUBLANES
    c_all = jnp.concatenate([c_sample, c_prompt, jnp.zeros((rows - B - DB, D), F32)], axis=0)
    mod = _ada(c_all, ada_w, ada_b).reshape(L, rows, 1, 6 * D)
    base = _band_bias_rows(rel_bias)

    k_stack = jnp.zeros((L, B, T, H_ATT * HEAD_DIM), F32)
    v_stack = jnp.zeros((L, B, T, H_ATT * HEAD_DIM), F32)
    xp, xs, vf_p, vf_s = x_prompt, x_sample.reshape(1, DB, D), None, None
    ps, ph, sk, sv, ss, sh = [], [], [], [], [], []
    for l in range(L):
        (xp, xs, vf_p, vf_s, k_stack, v_stack, state_p, shift_p, k_s, v_s, state_s, shift_s) = _layer(
            P, l, xp, xs, mod, base, vf_p, vf_s, k_stack, v_stack, cache_k, cache_v, state_wkv, state_shift[l])
        ps.append(state_p)
        ph.append(shift_p)
        sk.append(k_s.reshape(DB, 1, H_ATT, HEAD_DIM))
        sv.append(v_s.reshape(DB, 1, H_ATT, HEAD_DIM))
        ss.append(state_s)
        sh.append(shift_s)

    return (xp, xs.reshape(DB, 1, D), k_stack.reshape(L, B, T, H_ATT, HEAD_DIM),
            v_stack.reshape(L, B, T, H_ATT, HEAD_DIM), jnp.stack(ps), jnp.stack(ph),
            jnp.stack(sk), jnp.stack(sv), jnp.stack(ss), jnp.stack(sh))
```

```python
import functools
import math

import jax
import jax.numpy as jnp
from jax import lax
from jax.experimental import pallas as pl
from jax.experimental.pallas import tpu as pltpu

F32 = jnp.float32
BF16 = jnp.bfloat16

HEAD_DIM = 64
BRANCHES = ((128, 1), (512, 4), (2048, 16))
W_MAX = 2048
NUM_BUCKETS = 32
MAX_DISTANCE = W_MAX
ATT_SCALE = HEAD_DIM ** -0.5
RMS_EPS = 1e-6
GN_EPS = 64e-5
NEG = -1e30

LANES = 128
SUBLANES = 8
Q_BLOCK = 128
SCAN_CHUNK = 64
SCAN_SEQS = 2
SCAN_UNROLL = 16
SATT_PAIRS = 4
ATT_UNROLL = 4
VMEM_CAP = 56 * 1024 * 1024

assert all(w // dil == Q_BLOCK for w, dil in BRANCHES) and 2 * HEAD_DIM == LANES


def _vmem(nbytes):
    return int(min(VMEM_CAP, nbytes * 1.3 + (6 << 20)))


def _params(sem, nbytes):
    return pltpu.CompilerParams(dimension_semantics=sem, vmem_limit_bytes=_vmem(nbytes))


def _dot(a, b):
    return jnp.dot(a.astype(BF16), b.astype(BF16), preferred_element_type=F32)


def _dot_nt(a, b):
    return lax.dot_general(a.astype(BF16), b.astype(BF16), (((1,), (1,)), ((), ())), preferred_element_type=F32)


def _dot_tn(a, b):
    return lax.dot_general(a.astype(BF16), b.astype(BF16), (((0,), (0,)), ((), ())), preferred_element_type=F32)


def _sigmoid(x):
    return 1.0 / (1.0 + jnp.exp(-x))


def _head_sums(x):
    r = lax.broadcasted_iota(jnp.int32, (LANES, LANES), 0) // HEAD_DIM
    c = lax.broadcasted_iota(jnp.int32, (LANES, LANES), 1) // HEAD_DIM
    bd = (r == c).astype(BF16)
    cols = [_dot(x[:, j * LANES:(j + 1) * LANES], bd) for j in range(x.shape[1] // LANES)]
    return cols[0] if len(cols) == 1 else jnp.concatenate(cols, axis=-1)


def _ada_kernel(c_ref, w_ref, b_ref, o_ref):
    c = c_ref[...]
    s = c * _sigmoid(c)
    o_ref[0] = _dot(s, w_ref[0]) + b_ref[0]


def _ada(c_all, ada_w, ada_b, tn=1024):
    L, D, N = ada_w.shape
    R = c_all.shape[0]
    return pl.pallas_call(
        _ada_kernel,
        grid=(L, N // tn),
        in_specs=[pl.BlockSpec((R, D), lambda l, j: (0, 0)),
                  pl.BlockSpec((1, D, tn), lambda l, j: (l, 0, j)),
                  pl.BlockSpec((1, 1, tn), lambda l, j: (l, 0, j))],
        out_specs=pl.BlockSpec((1, R, tn), lambda l, j: (l, 0, j)),
        out_shape=jax.ShapeDtypeStruct((L, R, N), F32),
        compiler_params=_params(("arbitrary", "arbitrary"), 2 * D * tn * 4 + D * tn * 2),
        name="ada_mod",
    )(c_all, ada_w, ada_b.reshape(L, 1, N))


def _norm_kernel(x_ref, g_ref, sh_ref, sc_ref, *out_refs):
    x = x_ref[0]
    y = x * lax.rsqrt(jnp.mean(x * x, axis=-1, keepdims=True) + RMS_EPS) * g_ref[0]
    h = y * (1.0 + sc_ref[0, :, 0, :]) + sh_ref[0, :, 0, :]
    for o in out_refs:
        o[0] = h.astype(o.dtype)


def _mod_spec(l, rows, which, D):
    r0, n = rows
    if n is None:
        return pl.BlockSpec((1, 1, 1, D), lambda b, i: (l, r0 + b, 0, which))
    return pl.BlockSpec((1, n, 1, D), lambda b, i: (l, 0, 0, which))


def _norm_mod(x, g_stack, l, mod, rows, which, out_dtypes, tm=1024):
    B, T, D = x.shape
    tm = min(tm, T)
    L = g_stack.shape[0]
    assert rows[1] is None or (B == 1 and rows[1] == T == tm)
    outs = pl.pallas_call(
        _norm_kernel,
        grid=(B, T // tm),
        in_specs=[pl.BlockSpec((1, tm, D), lambda b, i: (b, i, 0)),
                  pl.BlockSpec((1, 1, D), lambda b, i: (l, 0, 0)),
                  _mod_spec(l, rows, which, D), _mod_spec(l, rows, which + 1, D)],
        out_specs=[pl.BlockSpec((1, tm, D), lambda b, i: (b, i, 0)) for _ in out_dtypes],
        out_shape=[jax.ShapeDtypeStruct((B, T, D), dt) for dt in out_dtypes],
        compiler_params=_params(("arbitrary", "arbitrary"), 2 * tm * D * 4 * (2 + len(out_dtypes))),
        name="norm_mod",
    )(x, g_stack.reshape(L, 1, D), mod, mod)
    return outs


def _first_inner_step():
    return (pl.program_id(1) == 0) & (pl.program_id(2) == 0)


def _head_rmsnorm(acc, g):
    return acc * lax.rsqrt(_head_sums(acc * acc) * (1.0 / HEAD_DIM) + RMS_EPS) * g


def _rider_specs(R, k_sizes, tn):
    ins = [pl.BlockSpec((1, R, ks), lambda n, b, i: (0, 0, 0)) for ks in k_sizes]
    return ins, pl.BlockSpec((1, R, tn), lambda n, b, i: (0, 0, n))


def _mm_kernel(*refs, headnorm, scale, aliased):
    it = iter(refs)
    x_ref, w_ref = next(it), next(it)
    g_ref = next(it) if headnorm else None
    if aliased:
        next(it)
    xr_ref, o_ref, or_ref, wb_ref = next(it), next(it), next(it), next(it)

    def result(x):
        acc = jnp.dot(x, wb_ref[...], preferred_element_type=F32)
        if headnorm:
            acc = _head_rmsnorm(acc, g_ref[0])
            if scale != 1.0:
                acc = acc * scale
        return acc

    @pl.when(_first_inner_step())
    def _():
        wb_ref[...] = w_ref[0].astype(BF16)
        or_ref[0] = result(xr_ref[0])

    if len(o_ref.shape) == 4:
        o_ref[0, 0] = result(x_ref[0])
    else:
        o_ref[0] = result(x_ref[0])


def _proj(x, rider, w_stack, l, col0, ncols, *, head_g=None, scale=1.0, stack=None, tm=1024, tn=1024):
    B, T, K = x.shape
    R = rider.shape[1]
    tm = min(tm, T)
    tn = min(tn, ncols)
    assert T % tm == 0 and ncols % tn == 0 and col0 % tn == 0
    cb = col0 // tn
    in_specs = [pl.BlockSpec((1, tm, K), lambda n, b, i: (b, i, 0)),
                pl.BlockSpec((1, K, tn), lambda n, b, i: (l, 0, cb + n))]
    args = [x, w_stack]
    if head_g is not None:
        in_specs.append(pl.BlockSpec((1, 1, tn), lambda n, b, i: (l, 0, 0)))
        args.append(jnp.tile(head_g, (1, tn // HEAD_DIM)).reshape(head_g.shape[0], 1, tn))
    aliases = {}
    if stack is None:
        out_spec = pl.BlockSpec((1, tm, tn), lambda n, b, i: (b, i, n))
        out_shape = jax.ShapeDtypeStruct((B, T, ncols), F32)
    else:
        out_spec = pl.BlockSpec((1, 1, tm, tn), lambda n, b, i: (l, b, i, n))
        out_shape = jax.ShapeDtypeStruct(stack.shape, F32)
        in_specs.append(pl.BlockSpec(memory_space=pl.ANY))
        args.append(stack)
        aliases = {len(args) - 1: 0}
    r_ins, r_out = _rider_specs(R, (K,), tn)
    nbytes = 2 * (tm * K * 2 + K * tn * 4 + tm * tn * 4) + K * tn * 2 + 2 * tm * tn * 4
    return pl.pallas_call(
        functools.partial(_mm_kernel, headnorm=head_g is not None, scale=scale, aliased=bool(aliases)),
        grid=(ncols // tn, B, T // tm),
        in_specs=in_specs + r_ins,
        out_specs=[out_spec, r_out],
        out_shape=[out_shape, jax.ShapeDtypeStruct((1, R, ncols), F32)],
        scratch_shapes=[pltpu.VMEM((K, tn), BF16)],
        input_output_aliases=aliases,
        compiler_params=_params(("arbitrary",) * 3, nbytes),
        name="proj",
    )(*args, rider)


def _swiglu_kernel(x_ref, wg_ref, wu_ref, xr_ref, o_ref, or_ref, wgb_ref, wub_ref):
    def result(x):
        gate = jnp.dot(x, wgb_ref[...], preferred_element_type=F32)
        up = jnp.dot(x, wub_ref[...], preferred_element_type=F32)
        return (gate * _sigmoid(gate) * up).astype(o_ref.dtype)

    @pl.when(_first_inner_step())
    def _():
        wgb_ref[...] = wg_ref[0].astype(BF16)
        wub_ref[...] = wu_ref[0].astype(BF16)
        or_ref[0] = result(xr_ref[0])

    o_ref[0] = result(x_ref[0])


def _proj_swiglu(x, rider, w_gu, l, *, tm=1024, tn=512):
    B, T, K = x.shape
    R = rider.shape[1]
    F = w_gu.shape[2] // 2
    tm = min(tm, T)
    assert T % tm == 0 and F % tn == 0
    nb = F // tn
    r_ins, r_out = _rider_specs(R, (K,), tn)
    nbytes = 2 * (tm * K * 2 + 2 * K * tn * 4 + tm * tn * 2) + 2 * K * tn * 2 + 3 * tm * tn * 4
    return pl.pallas_call(
        _swiglu_kernel,
        grid=(nb, B, T // tm),
        in_specs=[pl.BlockSpec((1, tm, K), lambda n, b, i: (b, i, 0)),
                  pl.BlockSpec((1, K, tn), lambda n, b, i: (l, 0, n)),
                  pl.BlockSpec((1, K, tn), lambda n, b, i: (l, 0, nb + n))] + r_ins,
        out_specs=[pl.BlockSpec((1, tm, tn), lambda n, b, i: (b, i, n)), r_out],
        out_shape=[jax.ShapeDtypeStruct((B, T, F), BF16), jax.ShapeDtypeStruct((1, R, F), BF16)],
        scratch_shapes=[pltpu.VMEM((K, tn), BF16), pltpu.VMEM((K, tn), BF16)],
        compiler_params=_params(("arbitrary",) * 3, nbytes),
        name="proj_swiglu",
    )(x, w_gu, w_gu, rider)


def _resid_kernel(*refs, k_sizes):
    n_x = len(k_sizes)
    x_refs, refs = refs[:n_x], refs[n_x:]
    w_ref, res_ref, gate_ref = refs[:3]
    xr_refs, refs = refs[3:3 + n_x], refs[3 + n_x:]
    resr_ref, gater_ref, o_ref, or_ref, wb_ref = refs

    def result(xs, res, gate):
        acc = None
        k0 = 0
        for x_ref, ks in zip(xs, k_sizes):
            part = jnp.dot(x_ref[0], wb_ref[k0:k0 + ks, :], preferred_element_type=F32)
            acc = part if acc is None else acc + part
            k0 += ks
        return res[0] + gate[0, :, 0, :] * acc

    @pl.when(_first_inner_step())
    def _():
        wb_ref[...] = w_ref[0].astype(BF16)
        or_ref[0] = result(xr_refs, resr_ref, gater_ref)

    o_ref[0] = result(x_refs, res_ref, gate_ref)


def _proj_resid(xs, rider_xs, w_stack, l, resid, rider_resid, mod, row0, which, *, tm=512, tn=512):
    B, T, N = resid.shape
    R = rider_resid.shape[1]
    k_sizes = tuple(x.shape[2] for x in xs)
    K = sum(k_sizes)
    tm = min(tm, T)
    assert T % tm == 0 and N % tn == 0
    c0 = which * (N // tn)
    in_specs = [pl.BlockSpec((1, tm, ks), lambda n, b, i: (b, i, 0)) for ks in k_sizes]
    in_specs += [pl.BlockSpec((1, K, tn), lambda n, b, i: (l, 0, n)),
                 pl.BlockSpec((1, tm, tn), lambda n, b, i: (b, i, n)),
                 pl.BlockSpec((1, 1, 1, tn), lambda n, b, i: (l, row0 + b, 0, c0 + n))]
    r_ins, r_out = _rider_specs(R, k_sizes, tn)
    r_gate = pl.BlockSpec((1, R, 1, tn), lambda n, b, i: (l, 0, 0, c0 + n))
    nbytes = 2 * (tm * K * 2 + K * tn * 4 + 2 * tm * tn * 4) + K * tn * 2 + 2 * tm * tn * 4
    return pl.pallas_call(
        functools.partial(_resid_kernel, k_sizes=k_sizes),
        grid=(N // tn, B, T // tm),
        in_specs=in_specs + r_ins + [r_out, r_gate],
        out_specs=[pl.BlockSpec((1, tm, tn), lambda n, b, i: (b, i, n)), r_out],
        out_shape=[jax.ShapeDtypeStruct((B, T, N), F32), jax.ShapeDtypeStruct((1, R, N), F32)],
        scratch_shapes=[pltpu.VMEM((K, tn), BF16)],
        compiler_params=_params(("arbitrary",) * 3, nbytes),
        name="proj_resid",
    )(*xs, w_stack, resid, mod, *rider_xs, rider_resid, mod)


def _resid_norm_kernel(*refs, k_sizes):
    n_x = len(k_sizes)
    x_refs, refs = refs[:n_x], refs[n_x:]
    w_ref, res_ref, g_ref, gate_ref, sh_ref, sc_ref = refs[:6]
    xr_refs, refs = refs[6:6 + n_x], refs[6 + n_x:]
    resr_ref, gater_ref, shr_ref, scr_ref, o_ref, h_ref, or_ref, hr_ref, wb_ref = refs

    def result(xs, res, gate, sh, sc):
        acc = None
        k0 = 0
        for x_ref, ks in zip(xs, k_sizes):
            part = jnp.dot(x_ref[0], wb_ref[k0:k0 + ks, :], preferred_element_type=F32)
            acc = part if acc is None else acc + part
            k0 += ks
        x = res[0] + gate[0, :, 0, :] * acc
        y = x * lax.rsqrt(jnp.mean(x * x, axis=-1, keepdims=True) + RMS_EPS) * g_ref[0]
        return x, (y * (1.0 + sc[0, :, 0, :]) + sh[0, :, 0, :]).astype(BF16)

    @pl.when((pl.program_id(0) == 0) & (pl.program_id(1) == 0))
    def _():
        wb_ref[...] = w_ref[0].astype(BF16)
        or_ref[0], hr_ref[0] = result(xr_refs, resr_ref, gater_ref, shr_ref, scr_ref)

    o_ref[0], h_ref[0] = result(x_refs, res_ref, gate_ref, sh_ref, sc_ref)


def _proj_resid_norm(xs, rider_xs, w_stack, l, resid, rider_resid, mod, row0, which, g_stack, *, tm=512):
    B, T, N = resid.shape
    R = rider_resid.shape[1]
    L = g_stack.shape[0]
    k_sizes = tuple(x.shape[2] for x in xs)
    K = sum(k_sizes)
    tm = min(tm, T)
    assert T % tm == 0
    row = lambda w: pl.BlockSpec((1, 1, 1, N), lambda b, i: (l, row0 + b, 0, w))
    rrow = lambda w: pl.BlockSpec((1, R, 1, N), lambda b, i: (l, 0, 0, w))
    in_specs = [pl.BlockSpec((1, tm, ks), lambda b, i: (b, i, 0)) for ks in k_sizes]
    in_specs += [pl.BlockSpec((1, K, N), lambda b, i: (l, 0, 0), pipeline_mode=pl.Buffered(1)),
                 pl.BlockSpec((1, tm, N), lambda b, i: (b, i, 0)),
                 pl.BlockSpec((1, 1, N), lambda b, i: (l, 0, 0)),
                 row(which), row(which + 1), row(which + 2)]
    in_specs += [pl.BlockSpec((1, R, ks), lambda b, i: (0, 0, 0)) for ks in k_sizes]
    rfull = pl.BlockSpec((1, R, N), lambda b, i: (0, 0, 0))
    in_specs += [rfull, rrow(which), rrow(which + 1), rrow(which + 2)]
    tile = pl.BlockSpec((1, tm, N), lambda b, i: (b, i, 0))
    nbytes = K * N * 4 + K * N * 2 + 2 * (tm * K * 2 + 2 * tm * N * 4 + tm * N * 2) + 2 * tm * N * 4
    return pl.pallas_call(
        functools.partial(_resid_norm_kernel, k_sizes=k_sizes),
        grid=(B, T // tm),
        in_specs=in_specs,
        out_specs=[tile, tile, rfull, rfull],
        out_shape=[jax.ShapeDtypeStruct((B, T, N), F32), jax.ShapeDtypeStruct((B, T, N), BF16),
                   jax.ShapeDtypeStruct((1, R, N), F32), jax.ShapeDtypeStruct((1, R, N), BF16)],
        scratch_shapes=[pltpu.VMEM((K, N), BF16)],
        compiler_params=_params(("arbitrary", "arbitrary"), nbytes),
        name="proj_resid_norm",
    )(*xs, w_stack, resid, g_stack.reshape(L, 1, N), mod, mod, mod, *rider_xs, rider_resid, mod, mod, mod)


def _rel_bucket(dist):
    max_exact = NUM_BUCKETS // 2
    d = jnp.maximum(dist, 0)
    df = jnp.maximum(d, 1).astype(F32)
    large = max_exact + (jnp.log(df / max_exact) / math.log(MAX_DISTANCE / max_exact)
                         * (NUM_BUCKETS - max_exact)).astype(jnp.int32)
    return jnp.where(d < max_exact, d, jnp.minimum(large, NUM_BUCKETS - 1))


def _band_bias_rows(rel_bias):
    dsub = Q_BLOCK - jnp.arange(2 * Q_BLOCK)
    rows = []
    for (w, dil) in BRANCHES:
        valid = (dsub >= 0) & (dsub <= w // dil)
        bias = rel_bias[_rel_bucket(jnp.maximum(dsub, 0) * dil)].astype(F32).T
        rows.append(jnp.where(valid[None], bias, NEG))
    return jnp.stack(rows, axis=1)


def _attn_kernel(q_ref, k_ref, v_ref, base_ref, o_ref, qs, ks, vs, os_, ms_, ds_, stage):
    T = q_ref.shape[2]
    QB = Q_BLOCK
    low = lax.broadcasted_iota(jnp.int32, (1, LANES), 1) < HEAD_DIM
    own = (low, jnp.logical_not(low))
    prev_cols = lax.broadcasted_iota(jnp.int32, (1, 2 * QB), 1) < QB

    assert len(BRANCHES) == 3 and BRANCHES[0][1] == 1
    prev_dil = 1
    for bi, (_, dil) in enumerate(BRANCHES):
        L, Lp, ratio = T // dil, T // prev_dil, dil // prev_dil
        ks[bi, 0:QB, :] = jnp.zeros((QB, LANES), BF16)
        vs[bi, 0:QB, :] = jnp.zeros((QB, LANES), BF16)
        for r in range(dil):
            rows = pl.ds((r % prev_dil) * Lp + r // prev_dil, L, stride=ratio)
            if bi < 2:
                q, k, v = (ref[0, 0, rows, :] for ref in (q_ref, k_ref, v_ref))
            else:
                q, k, v = (stage[a, rows, :] for a in range(3))
            if bi == 1:
                for a, x in enumerate((q, k, v)):
                    stage[a, r * L:(r + 1) * L, :] = x
            for hh in range(2):
                qs[hh, bi, r * L:(r + 1) * L, :] = jnp.where(own[hh], q, 0.0).astype(BF16)
            ks[bi, QB + r * L:QB + (r + 1) * L, :] = k.astype(BF16)
            vs[bi, QB + r * L:QB + (r + 1) * L, :] = v.astype(BF16)
        prev_dil = dil

    for bi, (_, dil) in enumerate(BRANCHES):
        nb = T // dil // QB
        tiles = [pltpu.roll(jnp.broadcast_to(base_ref[hh, bi:bi + 1, :], (QB, 2 * QB)), 0, 1, stride=1, stride_axis=0)
                 for hh in range(2)]
        first_only = nb == 1
        if first_only:
            tiles = [t[:, QB:] for t in tiles]
        k0, kn = (QB, QB) if first_only else (0, 2 * QB)

        def one_block(g, q0, q1, kw, vw, nb=nb, tiles=tiles, first_only=first_only):
            ss = [lax.dot_general(q, kw, (((1,), (1,)), ((), ())), preferred_element_type=F32) + tiles[hh]
                  for hh, q in enumerate((q0, q1))]
            if not first_only:
                no_prev = jnp.logical_and(g % nb == 0, prev_cols)
                ss = [jnp.where(no_prev, NEG, s) for s in ss]
            yield
            ms = [jnp.max(s, axis=-1, keepdims=True) for s in ss]
            ps = [jnp.exp(s - m) for s, m in zip(ss, ms)]
            dens = [jnp.sum(p, axis=-1, keepdims=True) for p in ps]
            os2 = [jnp.dot(p.astype(BF16), vw, preferred_element_type=F32) for p in ps]
            yield
            pair = lambda x0, x1: jnp.where(low, jnp.broadcast_to(x0, (QB, LANES)), jnp.broadcast_to(x1, (QB, LANES)))
            return pair(*os2), pair(*ms), pair(*dens)

        def blocks(i, _, bi=bi, dil=dil, nb=nb, k0=k0, kn=kn):
            gs = [i * ATT_UNROLL + u for u in range(ATT_UNROLL)]
            ats = [pl.multiple_of(g * QB, QB) for g in gs]
            loaded = [(qs[0, bi, pl.ds(at, QB), :], qs[1, bi, pl.ds(at, QB), :],
                       ks[bi, pl.ds(at + k0, kn), :], vs[bi, pl.ds(at + k0, kn), :]) for at in ats]
            results = _round_robin([one_block(g, *ld) for g, ld in zip(gs, loaded)])
            for g, (acc, m, den) in zip(gs, results):
                r = g // nb
                tok = pl.ds((g - r * nb) * (QB * dil) + r, QB, stride=dil)
                os_[bi, tok, :] = acc
                ms_[bi, tok, :] = m
                ds_[bi, tok, :] = den
            return 0

        lax.fori_loop(0, T // QB // ATT_UNROLL, blocks, 0)

    def merge(i, _):
        rows = pl.ds(pl.multiple_of(i * QB, QB), QB)
        maxes = [ms_[bi, rows, :] for bi in range(len(BRANCHES))]
        m = functools.reduce(jnp.maximum, maxes)
        ws = [jnp.exp(x - m) for x in maxes]
        num = sum(w * os_[bi, rows, :] for bi, w in enumerate(ws))
        den = sum(w * ds_[bi, rows, :] for bi, w in enumerate(ws))
        o_ref[0, rows, :] = (num / den).astype(o_ref.dtype)
        return 0

    lax.fori_loop(0, T // QB, merge, 0)


def _attention_prompt(q, k_stack, v_stack, l, base):
    B, T, DA = q.shape
    nbr = len(BRANCHES)
    assert T % (Q_BLOCK * max(d for _, d in BRANCHES)) == 0
    qkv = pl.BlockSpec((1, 1, T, LANES), lambda p, b: (l, b, 0, p))
    scratch = [pltpu.VMEM((2, nbr, T, LANES), BF16), pltpu.VMEM((nbr, T + Q_BLOCK, LANES), BF16),
               pltpu.VMEM((nbr, T + Q_BLOCK, LANES), BF16), pltpu.VMEM((nbr, T, LANES), F32),
               pltpu.VMEM((nbr, T, LANES), F32), pltpu.VMEM((nbr, T, LANES), F32), pltpu.VMEM((3, T, LANES), F32)]
    nbytes = (2 * 3 * T * LANES * 4 + 4 * nbr * (T + Q_BLOCK) * LANES * 2 + 3 * nbr * T * LANES * 4
              + 3 * T * LANES * 4 + 2 * T * LANES * 2)
    return pl.pallas_call(
        _attn_kernel,
        grid=(DA // LANES, B),
        in_specs=[pl.BlockSpec((1, 1, T, LANES), lambda p, b: (0, b, 0, p)), qkv, qkv,
                  pl.BlockSpec((2, nbr, 2 * Q_BLOCK), lambda p, b: (p, 0, 0))],
        out_specs=pl.BlockSpec((1, T, LANES), lambda p, b: (b, 0, p)),
        out_shape=jax.ShapeDtypeStruct((B, T, DA), BF16),
        scratch_shapes=scratch,
        compiler_params=_params(("arbitrary", "arbitrary"), nbytes),
        name="attn_prompt",
    )(q[None], k_stack, v_stack, base)


def _distance_logits(rel_bias, nd):
    d = jnp.arange(nd)
    mult = sum(((d % dil == 0) & (d // dil <= w // dil)).astype(F32) for (w, dil) in BRANCHES)
    bias = rel_bias[_rel_bucket(d)].astype(F32).T
    return jnp.where(mult > 0, bias + jnp.log(jnp.maximum(mult, 1.0)), NEG)


def _sattn_kernel(q_ref, kn_ref, vn_ref, kt_ref, vt_ref, tab_ref, o_ref):
    row = lax.broadcasted_iota(jnp.int32, (SUBLANES, LANES), 0)
    low = lax.broadcasted_iota(jnp.int32, (SUBLANES, LANES), 1) < HEAD_DIM
    own = ((row == 0) & low) | ((row == 1) & jnp.logical_not(low))
    W = kt_ref.shape[3]

    def pair(pp):
        cols = slice(pp * LANES, (pp + 1) * LANES)
        q = jnp.where(own, q_ref[0, :, cols], 0.0).astype(BF16)
        kn = kn_ref[0, :, cols].astype(BF16).astype(F32)
        vn = vn_ref[0, :, cols].astype(BF16).astype(F32)
        tab = tab_ref[pp]
        s_self = jnp.sum(q.astype(F32) * kn, axis=-1, keepdims=True) + tab[:, W:W + 1]
        s = _dot(q, kt_ref[0, 0, cols, :]) + tab[:, :W]
        yield
        m = jnp.maximum(s_self, jnp.max(s, axis=-1, keepdims=True))
        p_self = jnp.exp(s_self - m)
        p = jnp.exp(s - m)
        den = p_self + jnp.sum(p, axis=-1, keepdims=True)
        o = (p_self * vn + _dot_nt(p, vt_ref[0, 0, cols, :])) / den
        yield
        return jnp.where(low[0:1], o[0:1], o[1:2])

    outs = _round_robin([pair(pp) for pp in range(q_ref.shape[2] // LANES)])
    o_ref[0] = jnp.concatenate(outs, axis=-1).astype(o_ref.dtype)


def _attention_sample(q, k_new, v_new, cache_k, cache_v, l, rel_bias):
    L, DB, W, H, E = cache_k.shape
    DA = H * E
    assert W >= max(w for w, _ in BRANCHES)
    tab = _distance_logits(rel_bias, W + 1)[:, ::-1].reshape(H // 2, 2, W + 1)
    tab = jnp.concatenate([tab, jnp.zeros((H // 2, SUBLANES - 2, W + 1), F32)], axis=1)
    kt = jnp.transpose(cache_k, (0, 1, 3, 4, 2)).reshape(L, DB, DA, W)
    vt = jnp.transpose(cache_v, (0, 1, 3, 4, 2)).reshape(L, DB, DA, W)
    npairs = math.gcd(SATT_PAIRS, H // 2)
    cw = npairs * LANES
    vec = pl.BlockSpec((1, 1, cw), lambda p, b: (b, 0, p))
    buf = pl.BlockSpec((1, 1, cw, W), lambda p, b: (l, b, p, 0))
    return pl.pallas_call(
        _sattn_kernel,
        grid=(DA // cw, DB),
        in_specs=[vec, vec, vec, buf, buf, pl.BlockSpec((npairs, SUBLANES, W + 1), lambda p, b: (p, 0, 0))],
        out_specs=vec,
        out_shape=jax.ShapeDtypeStruct((DB, 1, DA), BF16),
        compiler_params=_params(("arbitrary", "arbitrary"), 2 * 2 * cw * W * 4 + 4 * npairs * SUBLANES * W * 4),
        name="attn_sample",
    )(q, k_new, v_new, kt, vt, tab)


def _prep_kernel(*refs, shift_rows, has_vres):
    it = iter(refs)
    h_ref = next(it)
    if shift_rows:
        hp8_ref, hlast_ref = next(it), next(it)
    else:
        hprev_ref = next(it)
    rkv_ref = next(it)
    if shift_rows:
        rp8_ref, rlast_ref = next(it), next(it)
    else:
        rprev_ref = next(it)
    if has_vres:
        vfirst_ref = next(it)
    mu_ref, murkv_ref, w0_ref, dw1_ref, dw2_ref, a0_ref, aw1_ref, aw2_ref, gw1_ref, gw2_ref = (next(it) for _ in range(10))
    if has_vres:
        vmu_ref, v0_ref, vw1_ref, vw2_ref = (next(it) for _ in range(4))
    kk_ref, ka_ref = next(it), next(it)
    r_out, lw_out, k_out, v_out, kk_out, b_out, g_out = (next(it) for _ in range(7))

    h = h_ref[0]
    rkv0 = rkv_ref[0]
    tm = h.shape[0]
    if shift_rows:
        first = pl.program_id(1) == 0
        row0 = lax.broadcasted_iota(jnp.int32, (tm, 1), 0) == 0
        h_edge = jnp.where(first, hlast_ref[0], hp8_ref[0, SUBLANES - 1:SUBLANES, :])
        r_edge = jnp.where(first, rlast_ref[0], rp8_ref[0, SUBLANES - 1:SUBLANES, :])
        hprev = jnp.where(row0, h_edge, pltpu.roll(h, 1, 0))
        rprev = jnp.where(row0, r_edge, pltpu.roll(rkv0, 1, 0))
    else:
        hprev = hprev_ref[0]
        rprev = rprev_ref[0]

    hb = h.astype(BF16)
    dhb = (hprev - h).astype(BF16)
    mu = mu_ref[0].astype(BF16)
    xw = hb + dhb * mu[0:1]
    xa = hb + dhb * mu[1:2]
    xg = hb + dhb * mu[2:3]

    z = w0_ref[0] + _dot(jnp.tanh(_dot(xw, dw1_ref[0])), dw2_ref[0])
    lw_out[0] = -math.exp(-0.5) * _sigmoid(z)

    a = _sigmoid(a0_ref[0] + _dot(_dot(xa, aw1_ref[0]), aw2_ref[0]))
    g_out[0] = _dot(_sigmoid(_dot(xg, gw1_ref[0])), gw2_ref[0])

    murkv = murkv_ref[0]
    DR = kk_ref.shape[-1]
    r0, k0, v0 = (rkv0[:, j * DR:(j + 1) * DR] for j in range(3))
    rp, kp, vp = (rprev[:, j * DR:(j + 1) * DR] for j in range(3))
    r_out[0] = r0 + (rp - r0) * murkv[0:1]
    kr = k0 + (kp - k0) * murkv[1:2]
    vr = v0 + (vp - v0) * murkv[2:3]
    if has_vres:
        xv = hb + dhb * vmu_ref[0].astype(BF16)
        vgate = _sigmoid(v0_ref[0] + _dot(_dot(xv, vw1_ref[0]), vw2_ref[0]))
        vr = vr + (vfirst_ref[0] - vr) * vgate
    v_out[0] = vr
    kk = kr * kk_ref[0]
    kkn = kk * lax.rsqrt(jnp.maximum(_head_sums(kk * kk), 1e-24))
    kk_out[0] = kkn
    b_out[0] = kkn * a
    k_out[0] = kr * (1.0 + (a - 1.0) * ka_ref[0])


def _rwkv_prep(h, h_prev, rkv0, rkv_prev, v_first, P, l, *, tm=256):
    B, T, D = h.shape
    DR = rkv0.shape[2] // 3
    tm = min(tm, T)
    shift_rows = h_prev.shape[1] == 1 and T > 1
    has_vres = v_first is not None

    def tile(C):
        return pl.BlockSpec((1, tm, C), lambda b, i: (b, i, 0))

    def prev8(C):
        return pl.BlockSpec((1, SUBLANES, C), lambda b, i: (b, jnp.maximum(i * (tm // SUBLANES) - 1, 0), 0))

    def seq_row(C):
        return pl.BlockSpec((1, 1, C), lambda b, i: (b, 0, 0))

    def layer(shape, ll=l):
        return pl.BlockSpec((1,) + shape, lambda b, i: (ll,) + (0,) * len(shape))

    args, specs = [h], [tile(D)]
    if shift_rows:
        args += [h, h_prev]
        specs += [prev8(D), seq_row(D)]
    else:
        args += [jnp.broadcast_to(h_prev, h.shape)]
        specs += [tile(D)]
    args.append(rkv0)
    specs.append(tile(3 * DR))
    if shift_rows:
        args += [rkv0, rkv_prev]
        specs += [prev8(3 * DR), seq_row(3 * DR)]
    else:
        args += [jnp.broadcast_to(rkv_prev, rkv0.shape)]
        specs += [tile(3 * DR)]
    if has_vres:
        args.append(v_first)
        specs.append(tile(DR))
    r1 = lambda a: a.reshape(a.shape[0], 1, a.shape[-1])
    for name in ('mu_wag', 'mu_rkv'):
        args.append(P[name]); specs.append(layer(P[name].shape[1:]))
    args.append(r1(P['decay_w0'])); specs.append(layer((1, DR)))
    for name in ('decay_w1', 'decay_w2'):
        args.append(P[name]); specs.append(layer(P[name].shape[1:]))
    args.append(r1(P['aaa_a0'])); specs.append(layer((1, DR)))
    for name in ('aaa_w1', 'aaa_w2', 'gate_w1', 'gate_w2'):
        args.append(P[name]); specs.append(layer(P[name].shape[1:]))
    if has_vres:
        args.append(r1(P['vres_mu'])); specs.append(layer((1, D), l - 1))
        args.append(r1(P['vres_v0'])); specs.append(layer((1, DR), l - 1))
        for name in ('vres_w1', 'vres_w2'):
            args.append(P[name]); specs.append(layer(P[name].shape[1:], l - 1))
    args.append(r1(P['k_k'])); specs.append(layer((1, DR)))
    args.append(r1(P['k_a'])); specs.append(layer((1, DR)))
    nbytes = 2 * tm * 4 * (2 * D + 7 * DR + 7 * DR + DR) + 8 * tm * D * 4 + 4 * D * 512 * 4
    return pl.pallas_call(
        functools.partial(_prep_kernel, shift_rows=shift_rows, has_vres=has_vres),
        grid=(B, T // tm),
        in_specs=specs,
        out_specs=[tile(DR) for _ in range(7)],
        out_shape=[jax.ShapeDtypeStruct((B, T, DR), F32) for _ in range(7)],
        compiler_params=_params(("arbitrary", "arbitrary"), nbytes),
        name="rwkv_prep",
    )(*args)


def _prefix_sum_rows(x):
    n = x.shape[0]
    row = lax.broadcasted_iota(jnp.int32, (n, 1), 0)
    s = 1
    while s < n:
        x = x + jnp.where(row >= s, pltpu.roll(x, s, 0), 0.0)
        s *= 2
    return x


def _unit_lower_inverse(a_strict, blk):
    n = a_strict.shape[0]
    ti = lax.broadcasted_iota(jnp.int32, (n, n), 0)
    si = lax.broadcasted_iota(jnp.int32, (n, n), 1)

    def lower_left(s):
        return ((ti // (2 * s)) == (si // (2 * s))) & ((ti % (2 * s)) >= s) & ((si % (2 * s)) < s)

    d = (ti == si).astype(F32) + jnp.where(lower_left(1), a_strict, 0.0)
    s = 2
    while s < blk:
        t = _dot(d, jnp.where(lower_left(s), a_strict, 0.0))
        yield
        d = d + _dot(t, d)
        yield
        s *= 2
    return d


def _round_robin(generators):
    results = [None] * len(generators)
    live = list(range(len(generators)))
    while live:
        for u in list(live):
            try:
                next(generators[u])
            except StopIteration as done:
                results[u] = done.value
                live.remove(u)
    return results


def _pair_chunk_terms(r, lw, k, v, kk, b, low, tri):
    C = r.shape[0]
    cum = _prefix_sum_rows(lw)
    g_in = jnp.exp(cum)
    g_inv = jnp.exp(-cum)
    g_end = g_in[C - 1:C]
    a_t = -kk * jnp.exp(cum - lw)
    r_t = r * g_in
    b_t = b * g_inv
    k_t = k * g_inv
    bg = b_t * g_end
    kg = k_t * g_end
    zc = jnp.zeros((C, LANES), F32)
    h0 = lambda x: jnp.where(low, x, 0.0)
    h1 = lambda x: jnp.where(low, 0.0, x)
    v0, v1 = h0(v), h1(v)
    ar = jnp.concatenate([a_t, r_t], axis=0)
    m0 = jnp.where(tri, _dot_nt(h0(ar), jnp.concatenate([b_t, k_t], axis=0)), 0.0)
    m1 = jnp.where(tri, _dot_nt(h1(ar), jnp.concatenate([k_t, b_t], axis=0)), 0.0)
    yield
    top0, bot0, top1, bot1 = m0[:C], m0[C:], m1[:C], m1[C:]
    stack2 = lambda x0, x1: jnp.concatenate([jnp.concatenate([x0, zc], axis=0),
                                             jnp.concatenate([zc, x1], axis=0)], axis=1)
    akv = _dot(stack2(top0, top1), jnp.concatenate([zc, v0, v1, zc], axis=0))
    a_sw = pltpu.roll(a_t, HEAD_DIM, 1)
    tinv = yield from _unit_lower_inverse(jnp.concatenate([h0(top0), h1(top1)], axis=0), C)
    x = _dot(tinv, akv + jnp.concatenate([h1(a_sw), h0(a_sw)], axis=0))
    yield
    z = jnp.concatenate([x[:C], v0, v1, x[C:]], axis=0)
    e = _dot(stack2(bot0, bot1), z)
    gh = _dot_tn(z, jnp.concatenate([h0(bg), h0(kg), h1(kg), h1(bg)], axis=0))
    yield
    y0 = jnp.where(low, e[:C], e[C:])
    r_eff = r_t + pltpu.roll(jnp.where(low, e[C:], e[:C]), HEAD_DIM, 1)
    h_mat = jnp.concatenate([h0(gh[:C]), h1(gh[C:])], axis=0)
    g_mat = jnp.concatenate([h0(gh[C:]), h1(gh[:C])], axis=0)
    return r_eff, y0, g_mat, h_mat, g_end


def _scan_kernel(r_ref, lw_ref, k_ref, v_ref, kk_ref, b_ref, g_ref, lng_ref, lnb_ref, rk_ref,
                 o_ref, s_ref, reff_s, y0_s, gm_s, hm_s, ge_s, *, chunk, unroll):
    C = chunk
    NS, T = r_ref.shape[:2]
    NC = T // C
    E = HEAD_DIM
    low = lax.broadcasted_iota(jnp.int32, (1, LANES), 1) < E
    ti = lax.broadcasted_iota(jnp.int32, (2 * C, 2 * C), 0)
    si = lax.broadcasted_iota(jnp.int32, (2 * C, 2 * C), 1)
    tri = (si % C) <= jnp.where(ti < C, ti - 1, ti - C)

    def phase1(i, _):
        sq = i // (NC // unroll)
        c0 = (i - sq * (NC // unroll)) * unroll
        cs = [c0 + u for u in range(unroll)]
        rows = [pl.ds(pl.multiple_of(c * C, C), C) for c in cs]
        loaded = [[ref[sq, rw, :] for ref in (r_ref, lw_ref, k_ref, v_ref, kk_ref, b_ref)] for rw in rows]
        terms = _round_robin([_pair_chunk_terms(*args, low, tri) for args in loaded])
        for c, rw, (r_eff, y0, g_mat, h_mat, g_end) in zip(cs, rows, terms):
            reff_s[sq, rw, :] = r_eff
            y0_s[sq, rw, :] = y0
            gm_s[sq, c] = g_mat
            hm_s[sq, c] = h_mat
            ge_s[sq, c] = jnp.broadcast_to(g_end, (SUBLANES, LANES))
        return 0

    lax.fori_loop(0, NS * (NC // unroll), phase1, 0)

    def phase2(c, states):
        rows = pl.ds(pl.multiple_of(c * C, C), C)
        ys = [_dot_nt(reff_s[sq, rows, :], S) + y0_s[sq, rows, :] for sq, S in enumerate(states)]
        new = tuple(S * ge_s[sq, c][0:1] + _dot(S, gm_s[sq, c]) + hm_s[sq, c] for sq, S in enumerate(states))
        for sq, y in enumerate(ys):
            y0_s[sq, rows, :] = y
        return new

    states = lax.fori_loop(0, NC, phase2, tuple(jnp.zeros((LANES, LANES), F32) for _ in range(NS)))
    for sq, S in enumerate(states):
        s_ref[sq, 0] = S[:E, :E]
        s_ref[sq, 1] = S[E:, E:]

    def finish(y, r, k, v, g):
        mu = _head_sums(y) * (1.0 / E)
        yield
        yc = y - mu
        var = _head_sums(yc * yc) * (1.0 / E)
        bonus = _head_sums(r * k * rk_ref[0])
        yield
        yn = yc * lax.rsqrt(var + GN_EPS) * lng_ref[0] + lnb_ref[0]
        return ((yn + bonus * v) * g).astype(o_ref.dtype)

    def phase3(i, _):
        sq = i // (NC // unroll)
        c0 = (i - sq * (NC // unroll)) * unroll
        rows = [pl.ds(pl.multiple_of((c0 + u) * C, C), C) for u in range(unroll)]
        loaded = [[y0_s[sq, rw, :]] + [ref[sq, rw, :] for ref in (r_ref, k_ref, v_ref, g_ref)] for rw in rows]
        outs = _round_robin([finish(*args) for args in loaded])
        for rw, out in zip(rows, outs):
            o_ref[sq, rw, :] = out
        return 0

    lax.fori_loop(0, NS * (NC // unroll), phase3, 0)


def _rwkv_scan(r, lw, k, v, kk, b, g, lnx_g, lnx_b, r_k, l):
    B, T, DR = r.shape
    H = DR // HEAD_DIM
    L = lnx_g.shape[0]
    C = min(SCAN_CHUNK, T)
    nc = T // C
    unroll = math.gcd(SCAN_UNROLL, nc)
    ns = math.gcd(SCAN_SEQS, B)
    assert T % C == 0
    seq = pl.BlockSpec((ns, T, LANES), lambda p, b_: (b_, 0, p))
    par = pl.BlockSpec((1, 1, LANES), lambda p, b_: (l, 0, p))
    scratch = [pltpu.VMEM((ns, T, LANES), F32), pltpu.VMEM((ns, T, LANES), F32),
               pltpu.VMEM((ns, nc, LANES, LANES), F32), pltpu.VMEM((ns, nc, LANES, LANES), F32),
               pltpu.VMEM((ns, nc, SUBLANES, LANES), F32)]
    nbytes = ns * (2 * 8 * T * LANES * 4 + 2 * T * LANES * 4 + 2 * nc * LANES * LANES * 4 + nc * SUBLANES * LANES * 4)
    return pl.pallas_call(
        functools.partial(_scan_kernel, chunk=C, unroll=unroll),
        grid=(DR // LANES, B // ns),
        in_specs=[seq] * 7 + [par] * 3,
        out_specs=[seq, pl.BlockSpec((ns, 2, HEAD_DIM, HEAD_DIM), lambda p, b_: (b_, p, 0, 0))],
        out_shape=[jax.ShapeDtypeStruct((B, T, DR), BF16), jax.ShapeDtypeStruct((B, H, HEAD_DIM, HEAD_DIM), F32)],
        scratch_shapes=scratch,
        compiler_params=_params(("arbitrary", "arbitrary"), nbytes),
        name="rwkv_scan",
    )(r, lw, k, v, kk, b, g, lnx_g.reshape(L, 1, DR), lnx_b.reshape(L, 1, DR), r_k.reshape(L, 1, DR))


def _step_kernel(s_ref, r_ref, lw_ref, k_ref, v_ref, kk_ref, b_ref, g_ref, lng_ref, lnb_ref, rk_ref,
                 o_ref, so_ref, *, n_heads):
    E = HEAD_DIM
    eye = (lax.broadcasted_iota(jnp.int32, (E, E), 0) == lax.broadcasted_iota(jnp.int32, (E, E), 1)).astype(F32)

    def head(hh):
        sl = slice(hh * E, (hh + 1) * E)
        S = s_ref[0, 0, hh]
        r, k, v, kk, b, g = (ref[0, :, sl] for ref in (r_ref, k_ref, v_ref, kk_ref, b_ref, g_ref))
        w = jnp.exp(lw_ref[0, :, sl])
        v_col = jnp.sum(eye * v, axis=-1, keepdims=True)
        sa = jnp.sum(S * (-kk), axis=-1, keepdims=True)
        bonus = jnp.sum(r * k * rk_ref[0, :, sl], axis=-1, keepdims=True)
        yield
        S = S * w + sa * b + v_col * k
        y_col = jnp.sum(S * r, axis=-1, keepdims=True)
        yield
        y = jnp.sum(eye * y_col, axis=0, keepdims=True)
        mu = jnp.mean(y, axis=-1, keepdims=True)
        yield
        yc = y - mu
        var = jnp.mean(yc * yc, axis=-1, keepdims=True)
        yield
        yn = yc * lax.rsqrt(var + GN_EPS) * lng_ref[0, :, sl] + lnb_ref[0, :, sl]
        return S, (yn + bonus * v) * g

    results = _round_robin([head(hh) for hh in range(n_heads)])
    for hh, (S, _) in enumerate(results):
        so_ref[0, hh] = S
    o_ref[0] = jnp.concatenate([o for _, o in results], axis=-1).astype(o_ref.dtype)


def _rwkv_step(state, r, lw, k, v, kk, b, g, lnx_g, lnx_b, r_k, l):
    DB, H = state.shape[1:3]
    DR = H * HEAD_DIM
    L = lnx_g.shape[0]
    vec = pl.BlockSpec((1, 1, DR), lambda b_: (b_, 0, 0))
    par = pl.BlockSpec((1, 1, DR), lambda b_: (l, 0, 0))
    st = pl.BlockSpec((1, H, HEAD_DIM, HEAD_DIM), lambda b_: (b_, 0, 0, 0))
    return pl.pallas_call(
        functools.partial(_step_kernel, n_heads=H),
        grid=(DB,),
        in_specs=[pl.BlockSpec((1, 1, H, HEAD_DIM, HEAD_DIM), lambda b_: (l, b_, 0, 0, 0))] + [vec] * 7 + [par] * 3,
        out_specs=[vec, st],
        out_shape=[jax.ShapeDtypeStruct((DB, 1, DR), BF16), jax.ShapeDtypeStruct(state.shape[1:], F32)],
        compiler_params=_params(("arbitrary",), 4 * H * HEAD_DIM * LANES * 4),
        name="rwkv_step",
    )(state, r, lw, k, v, kk, b, g, lnx_g.reshape(L, 1, DR), lnx_b.reshape(L, 1, DR), r_k.reshape(L, 1, DR))


def _layer(P, l, xp, xs, mod, base, vf_p, vf_s, k_stack, v_stack, cache_k, cache_v, state_wkv, h_last_s):
    B, T, D = xp.shape
    DB = xs.shape[1]
    DA = P['rel_bias'].shape[1] * HEAD_DIM
    n_rkv = P['w_in'].shape[2] - 3 * DA
    w_in = P['w_in']
    rows_p, rows_s = (DB, None), (0, DB)
    h, hb = _norm_mod(xp, P['norm1_g'], l, mod, rows_p, 0, [F32, BF16])
    hs, hbs = _norm_mod(xs, P['norm1_g'], l, mod, rows_s, 0, [F32, BF16])
    q, q_s = _proj(hb, hbs, w_in, l, 0, DA, head_g=P['q_norm_g'], scale=ATT_SCALE)
    k_stack, k_s = _proj(hb, hbs, w_in, l, DA, DA, head_g=P['k_norm_g'], stack=k_stack)
    v_stack, v_s = _proj(hb, hbs, w_in, l, 2 * DA, DA, stack=v_stack)
    both = jnp.concatenate([hbs, h_last_s[None].astype(BF16)], axis=1)
    rkv0, rkv2_s = _proj(hb, both, w_in, l, 3 * DA, n_rkv)

    att = _attention_prompt(q, k_stack, v_stack, l, base)
    r, lw, kr, vr, kk, b, g = _rwkv_prep(h, jnp.zeros((B, 1, D), F32), rkv0, jnp.zeros((B, 1, n_rkv), F32), vf_p, P, l)
    if vf_p is None:
        vf_p = vr
    rw, state_p = _rwkv_scan(r, lw, kr, vr, kk, b, g, P['lnx_g'], P['lnx_b'], P['r_k'], l)

    per_seq = lambda t: t.reshape(DB, 1, t.shape[-1])
    att_s = _attention_sample(per_seq(q_s), per_seq(k_s), per_seq(v_s), cache_k, cache_v, l, P['rel_bias'])
    r, lw, kr, vr, kk, b, g = _rwkv_prep(hs, h_last_s[None], rkv2_s[:, :DB], rkv2_s[:, DB:], vf_s, P, l)
    if vf_s is None:
        vf_s = vr
    rw_s, state_s = _rwkv_step(state_wkv, *(per_seq(t) for t in (r, lw, kr, vr, kk, b, g)),
                                P['lnx_g'], P['lnx_b'], P['r_k'], l)

    xp, h2, xs, h2s = _proj_resid_norm([att, rw], [att_s.reshape(1, DB, DA), rw_s.reshape(1, DB, DA)], P['w_out'], l,
                                       xp, xs, mod, DB, 2, P['norm2_g'])
    act, act_s = _proj_swiglu(h2, h2s, P['w_gu'], l)
    xp, xs = _proj_resid([act], [act_s], P['w_down'], l, xp, xs, mod, DB, 5, tm=512, tn=512)
    return xp, xs, vf_p, vf_s, k_stack, v_stack, state_p, h[:, -1], k_s[0], v_s[0], state_s, hs[0]


def kernel(x_prompt, x_sample, c_prompt, c_sample, cache_k, cache_v, state_wkv, state_shift, rel_bias, ada_w, ada_b, norm1_g, norm2_g, w_in, q_norm_g, k_norm_g, mu_wag, mu_rkv, decay_w0, decay_w1, decay_w2, aaa_a0, aaa_w1, aaa_w2, gate_w1, gate_w2, vres_mu, vres_v0, vres_w1, vres_w2, k_k, k_a, r_k, lnx_g, lnx_b, w_out, w_gu, w_down):
    P = dict(rel_bias=rel_bias, norm1_g=norm1_g, norm2_g=norm2_g, w_in=w_in, q_norm_g=q_norm_g,
             k_norm_g=k_norm_g, mu_wag=mu_wag, mu_rkv=mu_rkv, decay_w0=decay_w0, decay_w1=decay_w1,
             decay_w2=decay_w2, aaa_a0=aaa_a0, aaa_w1=aaa_w1, aaa_w2=aaa_w2, gate_w1=gate_w1,
             gate_w2=gate_w2, vres_mu=vres_mu, vres_v0=vres_v0, vres_w1=vres_w1, vres_w2=vres_w2,
             k_k=k_k, k_a=k_a, r_k=r_k.reshape(r_k.shape[0], -1), lnx_g=lnx_g, lnx_b=lnx_b,
             w_out=w_out, w_gu=w_gu, w_down=w_down)
    L = ada_w.shape[0]
    B, T, D = x_prompt.shape
    DB = x_sample.shape[0]
    H_ATT = rel_bias.shape[1]
    assert x_sample.shape[1] == 1 and T <= W_MAX

    rows = -(-(B + DB) // SUBLANES) * SUBLANES
    c_all = jnp.concatenate([c_sample, c_prompt, jnp.zeros((rows - B - DB, D), F32)], axis=0)
    mod = _ada(c_all, ada_w, ada_b).reshape(L, rows, 1, 6 * D)
    base = _band_bias_rows(rel_bias)

    k_stack = jnp.zeros((L, B, T, H_ATT * HEAD_DIM), F32)
    v_stack = jnp.zeros((L, B, T, H_ATT * HEAD_DIM), F32)
    xp, xs, vf_p, vf_s = x_prompt, x_sample.reshape(1, DB, D), None, None
    ps, ph, sk, sv, ss, sh = [], [], [], [], [], []
    for l in range(L):
        (xp, xs, vf_p, vf_s, k_stack, v_stack, state_p, shift_p, k_s, v_s, state_s, shift_s) = _layer(
            P, l, xp, xs, mod, base, vf_p, vf_s, k_stack, v_stack, cache_k, cache_v, state_wkv, state_shift[l])
        ps.append(state_p)
        ph.append(shift_p)
        sk.append(k_s.reshape(DB, 1, H_ATT, HEAD_DIM))
        sv.append(v_s.reshape(DB, 1, H_ATT, HEAD_DIM))
        ss.append(state_s)
        sh.append(shift_s)

    return (xp, xs.reshape(DB, 1, D), k_stack.reshape(L, B, T, H_ATT, HEAD_DIM),
            v_stack.reshape(L, B, T, H_ATT, HEAD_DIM), jnp.stack(ps), jnp.stack(ph),
            jnp.stack(sk), jnp.stack(sv), jnp.stack(ss), jnp.stack(sh))
```
